```python
import math
import jax, jax.numpy as jnp
from jax import lax
import numpy as np

D_MODEL = 1024
BATCH = 1
SEQ = 16384
DEPTH = 4
DEC_BATCH = 32
DEC_SEQ = 16
PAST_LEN = 2048

CHUNK = 64
Q_BLOCK = 128
POOL_WINDOWS = (2, 4, 8, 16)
POOL_GROUPS = 4
W_A = D_MODEL // 4
GA = W_A // POOL_GROUPS
POOL_STATE = max(POOL_WINDOWS) - 1
W_B = D_MODEL // 4
SCONV_K = 3
W_C = D_MODEL // 4
CCONV_K = 31
HEAD_DIM = 64
N_HEADS = D_MODEL // (4 * HEAD_DIM)
W_D = N_HEADS * 2 * HEAD_DIM
ROPE_THETA = 10000.0
N_BRANCH = 4
OFF_A = 0
OFF_B = OFF_A + W_A
OFF_C = OFF_B + 3 * W_B
OFF_D = OFF_C + 2 * W_C
OFF_G = OFF_D + 3 * W_D
IN_WIDTH = OFF_G + N_BRANCH * D_MODEL
D_FF = ((8 * D_MODEL // 3 + 127) // 128) * 128
EPS = 1e-6

kernel_name = 'hybrid_streaming_encoder_step'


def _rmsnorm(x, g):
    xf = x.astype(jnp.float32)
    r = lax.rsqrt(jnp.mean(xf * xf, axis=-1, keepdims=True) + EPS)
    return (xf * r * g.astype(jnp.float32)).astype(x.dtype)


def _layernorm(x, g, b):
    xf = x.astype(jnp.float32)
    mu = jnp.mean(xf, axis=-1, keepdims=True)
    var = jnp.mean(jnp.square(xf - mu), axis=-1, keepdims=True)
    y = (xf - mu) * lax.rsqrt(var + EPS) * g.astype(jnp.float32) + b.astype(jnp.float32)
    return y.astype(x.dtype)


def _swiglu(x, wg, wu, wd):
    return (jax.nn.silu(x @ wg) * (x @ wu)) @ wd


def _rope(x, pos):
    half = HEAD_DIM // 2
    inv_freq = ROPE_THETA ** (-jnp.arange(half, dtype=jnp.float32) / half)
    ang = pos.astype(jnp.float32)[:, None] * inv_freq[None, :]
    cos = jnp.cos(ang)[None, :, None, :]
    sin = jnp.sin(ang)[None, :, None, :]
    xf = x.astype(jnp.float32)
    x1, x2 = xf[..., :half], xf[..., half:]
    return jnp.concatenate([x1 * cos - x2 * sin, x2 * cos + x1 * sin], axis=-1).astype(x.dtype)


def _causal_dwconv(u, state, w):
    ext = jnp.concatenate([state, u], axis=1)
    y = lax.conv_general_dilated(ext, w[:, None, :].astype(u.dtype), window_strides=(1,), padding='VALID',
                                 dimension_numbers=('NWC', 'WIO', 'NWC'), feature_group_count=u.shape[-1])
    return y, ext[:, -(w.shape[0] - 1):]


def _pool_mixer(u, state, pos0, pool_w, pool_scale):
    B, T, _ = u.shape
    ext = jnp.concatenate([state, u], axis=1)
    cs = jnp.cumsum(ext.astype(jnp.float32), axis=1)
    cs = jnp.concatenate([jnp.zeros((B, 1, W_A), jnp.float32), cs], axis=1)
    end = cs[:, POOL_STATE + 1:]
    pos = pos0 + jnp.arange(T)
    means = []
    for g, win in enumerate(POOL_WINDOWS):
        sl = slice(g * GA, (g + 1) * GA)
        start = cs[:, POOL_STATE + 1 - win: POOL_STATE + 1 - win + T, sl]
        cnt = jnp.minimum(pos + 1, win).astype(jnp.float32)[None, :, None]
        means.append((end[..., sl] - start) / cnt)
    mean = jnp.concatenate(means, axis=-1)
    d = (mean - u.astype(jnp.float32)).astype(u.dtype).reshape(B, T, POOL_GROUPS, GA)
    y = jnp.einsum('btgc,gcd->btgd', d, pool_w).reshape(B, T, W_A) * pool_scale
    return y, ext[:, -POOL_STATE:]


def _diff_attn_core(q, k, v, lam, mask):
    s = jnp.einsum('bqhd,bkhd->bhqk', q, k).astype(jnp.float32) * (HEAD_DIM ** -0.5)
    if mask is not None:
        s = jnp.where(mask[None, None], s, -jnp.inf)
    p = jax.nn.softmax(s, axis=-1)
    B, _, Tq, Tk = p.shape
    p = p.reshape(B, N_HEADS, 2, Tq, Tk)
    a = (p[:, :, 0] - lam * p[:, :, 1]).astype(v.dtype)
    return jnp.einsum('bhqk,bkhe->bqhe', a, v)


def _diff_attn_prompt(q, k, v, lam):
    B, S = q.shape[0], q.shape[1]
    nb = S // Q_BLOCK
    qb = q.reshape(B, nb, Q_BLOCK, 2 * N_HEADS, HEAD_DIM).transpose(1, 0, 2, 3, 4)
    key_chunk = jnp.arange(S) // CHUNK

    def one(args):
        q_blk, i = args
        q_chunk = (i * Q_BLOCK + jnp.arange(Q_BLOCK)) // CHUNK
        mask = key_chunk[None, :] <= q_chunk[:, None]
        return _diff_attn_core(q_blk, k, v, lam, mask)

    o = lax.map(one, (qb, jnp.arange(nb)))
    return o.transpose(1, 0, 2, 3, 4).reshape(B, S, N_HEADS, 2 * HEAD_DIM)


def _layer(x, pos0, lam_init, pool_st, sconv_st, cconv_st, k_past, v_past, w):
    B, T, _ = x.shape
    x = x + 0.5 * _swiglu(_rmsnorm(x, w['g_ffn1']), w['w1_gate'], w['w1_up'], w['w1_down'])
    h = _rmsnorm(x, w['g_mix'])
    z = h @ w['w_in']
    pos = pos0 + jnp.arange(T)

    ya, pool_new = _pool_mixer(z[..., OFF_A:OFF_A + W_A], pool_st, pos0, w['pool_w'], w['pool_scale'])

    b_gate = z[..., OFF_B:OFF_B + W_B]
    c_gate = z[..., OFF_B + W_B:OFF_B + 2 * W_B]
    s_in = z[..., OFF_B + 2 * W_B:OFF_B + 3 * W_B]
    sconv_out, sconv_new = _causal_dwconv(c_gate * s_in, sconv_st, w['sconv_w'])
    yb = b_gate * sconv_out

    glu = z[..., OFF_C:OFF_C + W_C] * jax.nn.sigmoid(z[..., OFF_C + W_C:OFF_C + 2 * W_C])
    cconv_out, cconv_new = _causal_dwconv(glu, cconv_st, w['cconv_w'])
    yc = jax.nn.silu(_layernorm(cconv_out + w['cconv_b'], w['ln_g'], w['ln_b']))

    q = z[..., OFF_D:OFF_D + W_D].reshape(B, T, 2 * N_HEADS, HEAD_DIM)
    k = z[..., OFF_D + W_D:OFF_D + 2 * W_D].reshape(B, T, 2 * N_HEADS, HEAD_DIM)
    v = z[..., OFF_D + 2 * W_D:OFF_D + 3 * W_D].reshape(B, T, N_HEADS, 2 * HEAD_DIM)
    q = _rope(_rmsnorm(q, w['q_norm_g']), pos)
    k = _rope(_rmsnorm(k, w['k_norm_g']), pos)
    f32 = jnp.float32
    lam = (jnp.exp(jnp.sum(w['lam_q1'].astype(f32) * w['lam_k1'].astype(f32)))
           - jnp.exp(jnp.sum(w['lam_q2'].astype(f32) * w['lam_k2'].astype(f32))) + lam_init)
    if k_past is None:
        o = _diff_attn_prompt(q, k, v, lam)
    else:
        o = _diff_attn_core(q, jnp.concatenate([k_past, k], axis=1),
                            jnp.concatenate([v_past, v], axis=1), lam, None)
    yd = (_rmsnorm(o, w['subln_g']) * (1.0 - lam_init)).reshape(B, T, W_D)

    gates = jax.nn.sigmoid(z[..., OFF_G:]).reshape(B, T, N_BRANCH, D_MODEL)
    merged = (gates[:, :, 0] * (ya @ w['wp_a']) + gates[:, :, 1] * (yb @ w['wp_b'])
              + gates[:, :, 2] * (yc @ w['wp_c']) + gates[:, :, 3] * (yd @ w['wp_d']))
    x = x + merged @ w['w_out']
    x = x + 0.5 * _swiglu(_rmsnorm(x, w['g_ffn2']), w['w2_gate'], w['w2_up'], w['w2_down'])
    return x, k, v, pool_new, sconv_new, cconv_new


def setup_inputs(seed: int = 0) -> dict:
    key = jax.random.key(seed)
    ks = jax.random.split(key, 48)
    ctr = [0]

    def nxt():
        k = ks[ctr[0]]
        ctr[0] += 1
        return k

    def nrm(shape, scale):
        return jax.random.normal(nxt(), shape, jnp.float32) * scale

    def gain(shape):
        return 1.0 + 0.02 * jax.random.normal(nxt(), shape, jnp.float32)

    L = DEPTH
    return {
        'x_prompt': nrm((BATCH, SEQ, D_MODEL), 1.0),
        'x_sample': nrm((DEC_BATCH, DEC_SEQ, D_MODEL), 1.0),
        'cache_k': nrm((L, DEC_BATCH, PAST_LEN, 2 * N_HEADS, HEAD_DIM), 1.0),
        'cache_v': nrm((L, DEC_BATCH, PAST_LEN, N_HEADS, 2 * HEAD_DIM), 1.0),
        'state_pool': nrm((L, DEC_BATCH, POOL_STATE, W_A), 1.0),
        'state_sconv': nrm((L, DEC_BATCH, SCONV_K - 1, W_B), 1.0),
        'state_cconv': nrm((L, DEC_BATCH, CCONV_K - 1, W_C), 0.5),
        'g_ffn1': gain((L, D_MODEL)),
        'w1_gate': nrm((L, D_MODEL, D_FF), D_MODEL ** -0.5),
        'w1_up': nrm((L, D_MODEL, D_FF), D_MODEL ** -0.5),
        'w1_down': nrm((L, D_FF, D_MODEL), D_FF ** -0.5),
        'g_mix': gain((L, D_MODEL)),
        'w_in': nrm((L, D_MODEL, IN_WIDTH), D_MODEL ** -0.5),
        'pool_w': nrm((L, POOL_GROUPS, GA, GA), GA ** -0.5),
        'pool_scale': gain((L, W_A)),
        'sconv_w': nrm((L, SCONV_K, W_B), SCONV_K ** -0.5),
        'cconv_w': nrm((L, CCONV_K, W_C), CCONV_K ** -0.5),
        'cconv_b': nrm((L, W_C), 0.01),
        'ln_g': gain((L, W_C)),
        'ln_b': nrm((L, W_C), 0.01),
        'q_norm_g': gain((L, HEAD_DIM)),
        'k_norm_g': gain((L, HEAD_DIM)),
        'lam_q1': nrm((L, HEAD_DIM), 0.1),
        'lam_k1': nrm((L, HEAD_DIM), 0.1),
        'lam_q2': nrm((L, HEAD_DIM), 0.1),
        'lam_k2': nrm((L, HEAD_DIM), 0.1),
        'subln_g': gain((L, 2 * HEAD_DIM)),
        'wp_a': nrm((L, W_A, D_MODEL), W_A ** -0.5),
        'wp_b': nrm((L, W_B, D_MODEL), W_B ** -0.5),
        'wp_c': nrm((L, W_C, D_MODEL), W_C ** -0.5),
        'wp_d': nrm((L, W_D, D_MODEL), W_D ** -0.5),
        'w_out': nrm((L, D_MODEL, D_MODEL), D_MODEL ** -0.5),
        'g_ffn2': gain((L, D_MODEL)),
        'w2_gate': nrm((L, D_MODEL, D_FF), D_MODEL ** -0.5),
        'w2_up': nrm((L, D_MODEL, D_FF), D_MODEL ** -0.5),
        'w2_down': nrm((L, D_FF, D_MODEL), D_FF ** -0.5),
    }


def reference(x_prompt, x_sample, cache_k, cache_v, state_pool, state_sconv, state_cconv,
              g_ffn1, w1_gate, w1_up, w1_down, g_mix, w_in, pool_w, pool_scale, sconv_w,
              cconv_w, cconv_b, ln_g, ln_b, q_norm_g, k_norm_g, lam_q1, lam_k1, lam_q2, lam_k2,
              subln_g, wp_a, wp_b, wp_c, wp_d, w_out, g_ffn2, w2_gate, w2_up, w2_down):
    yp, ys = x_prompt, x_sample
    bp = x_prompt.shape[0]
    kp_l, vp_l, pp_l, sp_l, cp_l = [], [], [], [], []
    ks_l, vs_l, ps_l, ss_l, cs_l = [], [], [], [], []
    for l in range(DEPTH):
        w = {
            'g_ffn1': g_ffn1[l], 'w1_gate': w1_gate[l], 'w1_up': w1_up[l], 'w1_down': w1_down[l],
            'g_mix': g_mix[l], 'w_in': w_in[l], 'pool_w': pool_w[l], 'pool_scale': pool_scale[l],
            'sconv_w': sconv_w[l], 'cconv_w': cconv_w[l], 'cconv_b': cconv_b[l],
            'ln_g': ln_g[l], 'ln_b': ln_b[l], 'q_norm_g': q_norm_g[l], 'k_norm_g': k_norm_g[l],
            'lam_q1': lam_q1[l], 'lam_k1': lam_k1[l], 'lam_q2': lam_q2[l], 'lam_k2': lam_k2[l],
            'subln_g': subln_g[l], 'wp_a': wp_a[l], 'wp_b': wp_b[l], 'wp_c': wp_c[l], 'wp_d': wp_d[l],
            'w_out': w_out[l], 'g_ffn2': g_ffn2[l], 'w2_gate': w2_gate[l], 'w2_up': w2_up[l],
            'w2_down': w2_down[l],
        }
        lam_init = 0.8 - 0.6 * math.exp(-0.3 * l)
        yp, kp, vp, pp, sp, cp = _layer(
            yp, 0, lam_init,
            jnp.zeros((bp, POOL_STATE, W_A), yp.dtype),
            jnp.zeros((bp, SCONV_K - 1, W_B), yp.dtype),
            jnp.zeros((bp, CCONV_K - 1, W_C), yp.dtype),
            None, None, w)
        ys, kn, vn, pn, sn, cn = _layer(
            ys, PAST_LEN, lam_init, state_pool[l], state_sconv[l], state_cconv[l],
            cache_k[l], cache_v[l], w)
        kp_l.append(kp); vp_l.append(vp); pp_l.append(pp); sp_l.append(sp); cp_l.append(cp)
        ks_l.append(kn); vs_l.append(vn); ps_l.append(pn); ss_l.append(sn); cs_l.append(cn)
    return (yp, ys,
            jnp.stack(kp_l), jnp.stack(vp_l), jnp.stack(pp_l), jnp.stack(sp_l), jnp.stack(cp_l),
            jnp.stack(ks_l), jnp.stack(vs_l), jnp.stack(ps_l), jnp.stack(ss_l), jnp.stack(cs_l))
```

```python
import functools
import math
from typing import NamedTuple

import jax
import jax.numpy as jnp
from jax import lax
from jax.experimental import pallas as pl
from jax.experimental.pallas import tpu as pltpu

F32 = jnp.float32
BF16 = jnp.bfloat16

D_MODEL = 1024
DEPTH = 4
CHUNK = 64
POOL_WINDOWS = (2, 4, 8, 16)
W_A = 256
GA = 64
POOL_STATE = 15
W_B = 256
SCONV_K = 3
W_C = 256
CCONV_K = 31
HEAD_DIM = 64
N_HEADS = 4
W_D = 512
ROPE_THETA = 10000.0
D_FF = 2816
EPS = 1e-6
W_ABC = W_A + 3 * W_B + 2 * W_C
W_PROJ = W_ABC + 3 * W_D
W_Y = W_A + W_B + W_C

LANES = 128
HALO = 32
MIB = 1024 * 1024


class Cfg(NamedTuple):
    s_prompt: int
    n_dec: int
    t_dec: int
    past: int
    tm: int
    tq: int
    tmix: int

    @property
    def n_tok(self):
        return self.s_prompt + self.n_dec * self.t_dec


def _const_spec(shape, index):
    return pl.BlockSpec(shape, lambda *_: index, pipeline_mode=pl.Buffered(1))


def _rms(x, g):
    ms = jnp.mean(x * x, axis=-1, keepdims=True)
    return x * lax.rsqrt(ms + EPS) * g


def _dot(a, b):
    return jnp.dot(a, b, preferred_element_type=F32)


def _dot_t(a, b):
    return lax.dot_general(a, b, (((1,), (1,)), ((), ())), preferred_element_type=F32)


def _ffn_kernel(x_ref, g_ref, wg_ref, wu_ref, wd_ref, o_ref):
    x = x_ref[...]
    h = _rms(x, g_ref[...]).astype(BF16)
    a = _dot(h, wg_ref[...])
    u = _dot(h, wu_ref[...])
    act = (a * jax.nn.sigmoid(a) * u).astype(BF16)
    o_ref[...] = x + 0.5 * _dot(act, wd_ref[...])


def _ffn(cfg, l, x, g, wg, wu, wd):
    n, tm = cfg.n_tok, cfg.tm
    return pl.pallas_call(
        _ffn_kernel,
        grid=(n // tm,),
        in_specs=[
            pl.BlockSpec((tm, D_MODEL), lambda i: (i, 0)),
            _const_spec((1, D_MODEL), (0, 0)),
            _const_spec((None, D_MODEL, D_FF), (l, 0, 0)),
            _const_spec((None, D_MODEL, D_FF), (l, 0, 0)),
            _const_spec((None, D_FF, D_MODEL), (l, 0, 0)),
        ],
        out_specs=pl.BlockSpec((tm, D_MODEL), lambda i: (i, 0)),
        out_shape=jax.ShapeDtypeStruct((n, D_MODEL), F32),
        compiler_params=pltpu.CompilerParams(
            dimension_semantics=("arbitrary",), vmem_limit_bytes=52 * MIB),
        name="ffn",
    )(x, g, wg, wu, wd)


def _inproj_kernel(x_ref, g_ref, w_ref, qg_ref, kg_ref, cos_ref, sin_ref, ones_ref,
                   zabc_ref, qb_ref, k_ref, kb_ref, v_ref, vb_ref):
    h = _rms(x_ref[...], g_ref[...]).astype(BF16)
    zabc_ref[...] = _dot(h, w_ref[:, 0:W_ABC])
    cos = cos_ref[...]
    sin = sin_ref[...]
    ones = ones_ref[...]
    lane = lax.broadcasted_iota(jnp.int32, cos.shape, 1)
    first_half = (lane & (HEAD_DIM // 2)) == 0

    def norm_rope(z, g):
        ss = z * z
        hi = ss.astype(BF16)
        lo = (ss - hi.astype(F32)).astype(BF16)
        tot = _dot(hi, ones) + _dot(lo, ones)
        y = z * lax.rsqrt(tot * (1.0 / HEAD_DIM) + EPS) * g
        half = HEAD_DIM // 2
        partner = jnp.where(first_half, pltpu.roll(y, LANES - half, 1), pltpu.roll(y, half, 1))
        return y * cos + partner * sin

    for c in range(W_D // LANES):
        sl = slice(c * LANES, (c + 1) * LANES)
        zq = _dot(h, w_ref[:, W_ABC + c * LANES: W_ABC + (c + 1) * LANES])
        qb_ref[:, sl] = (norm_rope(zq, qg_ref[...]) * (HEAD_DIM ** -0.5)).astype(BF16)
        zk = _dot(h, w_ref[:, W_ABC + W_D + c * LANES: W_ABC + W_D + (c + 1) * LANES])
        k = norm_rope(zk, kg_ref[...])
        k_ref[:, sl] = k
        kb_ref[:, sl] = k.astype(BF16)
    v = _dot(h, w_ref[:, W_ABC + 2 * W_D: W_PROJ])
    v_ref[...] = v
    vb_ref[...] = v.astype(BF16)


def _inproj(cfg, l, x, g, w_in, qg, kg, cos_t, sin_t, ones_bd):
    n, tm = cfg.n_tok, cfg.tm
    row = lambda w: pl.BlockSpec((tm, w), lambda i: (i, 0))
    return pl.pallas_call(
        _inproj_kernel,
        grid=(n // tm,),
        in_specs=[
            row(D_MODEL),
            _const_spec((1, D_MODEL), (0, 0)),
            _const_spec((None, D_MODEL, W_PROJ), (l, 0, 0)),
            _const_spec((1, LANES), (0, 0)),
            _const_spec((1, LANES), (0, 0)),
            row(LANES),
            row(LANES),
            _const_spec((LANES, LANES), (0, 0)),
        ],
        out_specs=[row(W_ABC), row(W_D), row(W_D), row(W_D), row(W_D), row(W_D)],
        out_shape=[
            jax.ShapeDtypeStruct((n, W_ABC), F32),
            jax.ShapeDtypeStruct((n, W_D), BF16),
            jax.ShapeDtypeStruct((n, W_D), F32),
            jax.ShapeDtypeStruct((n, W_D), BF16),
            jax.ShapeDtypeStruct((n, W_D), F32),
            jax.ShapeDtypeStruct((n, W_D), BF16),
        ],
        compiler_params=pltpu.CompilerParams(
            dimension_semantics=("arbitrary",), vmem_limit_bytes=40 * MIB),
        name="inproj",
    )(x, g, w_in, qg, kg, cos_t, sin_t, ones_bd)


MIX_ROWS = 128


def _mixer_compute(t, pos0, z_ref, pw_ref, ps_ref, sw_ref, cw_ref, cb_ref, lg_ref, lb_ref,
                   y_ref, pool_o, sconv_o, cconv_o, ea, eb, ec):
    u = z_ref[:, 0:W_A]
    ea[HALO:HALO + t, :] = u
    eb[HALO:HALO + t, :] = z_ref[:, W_A + W_B:W_A + 2 * W_B] * z_ref[:, W_A + 2 * W_B:W_A + 3 * W_B]
    zc = z_ref[:, W_A + 3 * W_B:W_A + 3 * W_B + W_C]
    ec[HALO:HALO + t, :] = zc * jax.nn.sigmoid(z_ref[:, W_A + 3 * W_B + W_C:W_ABC])

    rows = min(t, MIX_ROWS)
    for r0 in range(0, t, rows):
        base = HALO + r0
        lane = lax.broadcasted_iota(jnp.int32, (rows, W_A), 1)
        cur = ea[base:base + rows, :]
        acc = cur
        sums = {}
        for j in range(1, max(POOL_WINDOWS)):
            acc = acc + ea[base - j:base - j + rows, :]
            if j + 1 in POOL_WINDOWS:
                sums[j + 1] = acc
        tot = sums[POOL_WINDOWS[-1]]
        win = jnp.full((rows, W_A), float(POOL_WINDOWS[-1]), F32)
        for gi in range(len(POOL_WINDOWS) - 2, -1, -1):
            in_group = lane < (gi + 1) * GA
            tot = jnp.where(in_group, sums[POOL_WINDOWS[gi]], tot)
            win = jnp.where(in_group, float(POOL_WINDOWS[gi]), win)
        if pos0 is None:
            cnt = win
        else:
            pos1 = (pos0 + r0 + 1 + lax.broadcasted_iota(jnp.int32, (rows, W_A), 0)).astype(F32)
            cnt = jnp.minimum(pos1, win)
        d = (tot / cnt - cur).astype(BF16)
        y_ref[r0:r0 + rows, 0:W_A] = _dot(d, pw_ref[...]) * ps_ref[...]

        conv = sw_ref[SCONV_K - 1:SCONV_K, :] * eb[base:base + rows, :]
        for j in range(SCONV_K - 1):
            off = base - (SCONV_K - 1) + j
            conv = conv + sw_ref[j:j + 1, :] * eb[off:off + rows, :]
        y_ref[r0:r0 + rows, W_A:W_A + W_B] = z_ref[r0:r0 + rows, W_A:W_A + W_B] * conv

        conv = cw_ref[CCONV_K - 1:CCONV_K, :] * ec[base:base + rows, :]
        for j in range(CCONV_K - 1):
            off = base - (CCONV_K - 1) + j
            conv = conv + cw_ref[j:j + 1, :] * ec[off:off + rows, :]
        conv = conv + cb_ref[...]
        mu = jnp.mean(conv, axis=-1, keepdims=True)
        cen = conv - mu
        var = jnp.mean(cen * cen, axis=-1, keepdims=True)
        ln = cen * lax.rsqrt(var + EPS) * lg_ref[...] + lb_ref[...]
        y_ref[r0:r0 + rows, W_A + W_B:W_Y] = ln * jax.nn.sigmoid(ln)

    pool_o[...] = ea[HALO + t - 16:HALO + t, :]
    sconv_o[...] = eb[HALO + t - 8:HALO + t, :]
    cconv_o[...] = ec[HALO + t - 32:HALO + t, :]


def _mixer_prompt_kernel(z_ref, pw_ref, ps_ref, sw_ref, cw_ref, cb_ref, lg_ref, lb_ref,
                         y_ref, pool_o, sconv_o, cconv_o, ea, eb, ec, *, t):
    i = pl.program_id(0)

    @pl.when(i == 0)
    def _():
        zeros = jnp.zeros((HALO, W_A), F32)
        ea[0:HALO, :] = zeros
        eb[0:HALO, :] = zeros
        ec[0:HALO, :] = zeros

    @pl.when(i > 0)
    def _():
        ea[0:HALO, :] = ea[t:t + HALO, :]
        eb[0:HALO, :] = eb[t:t + HALO, :]
        ec[0:HALO, :] = ec[t:t + HALO, :]

    _mixer_compute(t, i * t, z_ref, pw_ref, ps_ref, sw_ref, cw_ref, cb_ref, lg_ref, lb_ref,
                   y_ref, pool_o, sconv_o, cconv_o, ea, eb, ec)


def _mixer_sample_kernel(z_ref, sp_ref, ss_ref, sc_ref, pw_ref, ps_ref, sw_ref, cw_ref, cb_ref,
                         lg_ref, lb_ref, y_in_ref, y_ref, pool_o, sconv_o, cconv_o, ea, eb, ec, *, t):
    del y_in_ref
    ea[0:HALO, :] = sp_ref[...]
    eb[0:HALO, :] = ss_ref[...]
    ec[0:HALO, :] = sc_ref[...]
    _mixer_compute(t, None, z_ref, pw_ref, ps_ref, sw_ref, cw_ref, cb_ref, lg_ref, lb_ref,
                   y_ref, pool_o, sconv_o, cconv_o, ea, eb, ec)


def _mixer_weight_specs():
    return [
        _const_spec((W_A, W_A), (0, 0)),
        _const_spec((1, W_A), (0, 0)),
        _const_spec((SCONV_K, W_B), (0, 0)),
        _const_spec((CCONV_K, W_C), (0, 0)),
        _const_spec((1, W_C), (0, 0)),
        _const_spec((1, W_C), (0, 0)),
        _const_spec((1, W_C), (0, 0)),
    ]


def _mixer_scratch(t):
    return [pltpu.VMEM((HALO + t, W_A), F32), pltpu.VMEM((HALO + t, W_B), F32),
            pltpu.VMEM((HALO + t, W_C), F32)]


def _mixer_prompt(cfg, zabc, weights):
    t = cfg.tmix
    const_out = lambda r: pl.BlockSpec((r, W_A), lambda i: (0, 0))
    return pl.pallas_call(
        functools.partial(_mixer_prompt_kernel, t=t),
        grid=(cfg.s_prompt // t,),
        in_specs=[pl.BlockSpec((t, W_ABC), lambda i: (i, 0))] + _mixer_weight_specs(),
        out_specs=[pl.BlockSpec((t, W_Y), lambda i: (i, 0)), const_out(16), const_out(8), const_out(32)],
        out_shape=[
            jax.ShapeDtypeStruct((cfg.n_tok, W_Y), F32),
            jax.ShapeDtypeStruct((16, W_A), F32),
            jax.ShapeDtypeStruct((8, W_B), F32),
            jax.ShapeDtypeStruct((32, W_C), F32),
        ],
        scratch_shapes=_mixer_scratch(t),
        compiler_params=pltpu.CompilerParams(dimension_semantics=("arbitrary",)),
        name="mixer_prompt",
    )(zabc, *weights)


def _mixer_sample(cfg, zabc, st_pool, st_sconv, st_cconv, weights, y_abc):
    t, nb = cfg.t_dec, cfg.n_dec
    row0 = cfg.s_prompt // t
    state_spec = lambda: pl.BlockSpec((None, HALO, W_A), lambda b: (b, 0, 0))
    out_state = lambda r: pl.BlockSpec((None, r, W_A), lambda b: (b, 0, 0))
    return pl.pallas_call(
        functools.partial(_mixer_sample_kernel, t=t),
        grid=(nb,),
        in_specs=[pl.BlockSpec((t, W_ABC), lambda b: (row0 + b, 0)),
                  state_spec(), state_spec(), state_spec()]
                 + _mixer_weight_specs()
                 + [pl.BlockSpec(memory_space=pl.ANY)],
        out_specs=[pl.BlockSpec((t, W_Y), lambda b: (row0 + b, 0)),
                   out_state(16), out_state(8), out_state(32)],
        out_shape=[
            jax.ShapeDtypeStruct((cfg.n_tok, W_Y), F32),
            jax.ShapeDtypeStruct((nb, 16, W_A), F32),
            jax.ShapeDtypeStruct((nb, 8, W_B), F32),
            jax.ShapeDtypeStruct((nb, 32, W_C), F32),
        ],
        scratch_shapes=_mixer_scratch(t),
        input_output_aliases={11: 0},
        compiler_params=pltpu.CompilerParams(dimension_semantics=("arbitrary",)),
        name="mixer_sample",
    )(zabc, st_pool, st_sconv, st_cconv, *weights, y_abc)


def _lambda(lq1, lk1, lq2, lk2, lam_init):
    s1 = jnp.sum(lq1[...] * lk1[...], axis=-1, keepdims=True)
    s2 = jnp.sum(lq2[...] * lk2[...], axis=-1, keepdims=True)
    return jnp.exp(s1) - jnp.exp(s2) + lam_init


def _stack_maps(q):
    lane = lax.broadcasted_iota(jnp.int32, q.shape, 1)
    zero = jnp.zeros_like(q)
    return jnp.concatenate([jnp.where(lane < HEAD_DIM, q, zero), jnp.where(lane >= HEAD_DIM, q, zero)],
                           axis=0)


def _diff_out(acc, l, lam, sg, lam_init, t):
    o = acc[0:t] / l[0:t] - lam * (acc[t:2 * t] / l[t:2 * t])
    return _rms(o, sg) * (1.0 - lam_init)


def _attn_prompt_kernel(q_ref, k_ref, v_ref, lq1, lk1, lq2, lk2, sg_ref, o_ref,
                        q2_ref, m_ref, l_ref, acc_ref, *, tq, lam_init):
    qi = pl.program_id(1)
    q2_ref[...] = _stack_maps(q_ref[...])
    m_ref[...] = jnp.full(m_ref.shape, -jnp.inf, F32)
    l_ref[...] = jnp.zeros(l_ref.shape, F32)
    acc_ref[...] = jnp.zeros(acc_ref.shape, F32)

    def block(k0, diagonal):
        kb = k_ref[pl.ds(k0, tq), :]
        vb = v_ref[pl.ds(k0, tq), :]
        s = _dot_t(q2_ref[...], kb)
        if diagonal:
            row = lax.broadcasted_iota(jnp.int32, s.shape, 0)
            col = lax.broadcasted_iota(jnp.int32, s.shape, 1)
            q_chunk = (row & (tq - 1)) // CHUNK
            s = jnp.where(col // CHUNK <= q_chunk, s, -jnp.inf)
        m_prev = m_ref[...]
        m_new = jnp.maximum(m_prev, jnp.max(s, axis=-1, keepdims=True))
        alpha = jnp.exp(m_prev - m_new)
        p = jnp.exp(s - m_new)
        l_ref[...] = alpha * l_ref[...] + jnp.sum(p, axis=-1, keepdims=True)
        acc_ref[...] = alpha * acc_ref[...] + _dot(p.astype(BF16), vb)
        m_ref[...] = m_new

    def body(j, carry):
        block(pl.multiple_of(j * tq, tq), False)
        return carry

    lax.fori_loop(0, qi, body, 0)
    block(pl.multiple_of(qi * tq, tq), True)

    lam = _lambda(lq1, lk1, lq2, lk2, lam_init)
    o_ref[...] = _diff_out(acc_ref[...], l_ref[...], lam, sg_ref[...], lam_init, tq)


def _lam_specs():
    return [_const_spec((1, HEAD_DIM), (0, 0)) for _ in range(4)] + [_const_spec((1, LANES), (0, 0))]


def _attn_prompt(cfg, qb, kb, vb, lam_w, lam_init):
    s, tq = cfg.s_prompt, cfg.tq
    assert tq & (tq - 1) == 0 and tq % CHUNK == 0
    return pl.pallas_call(
        functools.partial(_attn_prompt_kernel, tq=tq, lam_init=lam_init),
        grid=(N_HEADS, s // tq),
        in_specs=[
            pl.BlockSpec((tq, LANES), lambda h, i: (i, h)),
            pl.BlockSpec((s, LANES), lambda h, i: (0, h)),
            pl.BlockSpec((s, LANES), lambda h, i: (0, h)),
        ] + _lam_specs(),
        out_specs=pl.BlockSpec((tq, LANES), lambda h, i: (i, h)),
        out_shape=jax.ShapeDtypeStruct((cfg.n_tok, W_D), F32),
        scratch_shapes=[
            pltpu.VMEM((2 * tq, LANES), BF16),
            pltpu.VMEM((2 * tq, 1), F32),
            pltpu.VMEM((2 * tq, 1), F32),
            pltpu.VMEM((2 * tq, LANES), F32),
        ],
        compiler_params=pltpu.CompilerParams(
            dimension_semantics=("arbitrary", "arbitrary"), vmem_limit_bytes=40 * MIB),
        name="attn_prompt",
    )(qb, kb, vb, *lam_w)


def _attn_sample_kernel(q_ref, kn_ref, vn_ref, kc_ref, vc_ref, lq1, lk1, lq2, lk2, sg_ref, yd_in_ref,
                        o_ref, *, t, lam_init):
    del yd_in_ref
    lam = _lambda(lq1, lk1, lq2, lk2, lam_init)
    for h in range(N_HEADS):
        sl = slice(h * LANES, (h + 1) * LANES)
        q2 = _stack_maps(q_ref[:, sl])
        s_past = _dot_t(q2, kc_ref[:, sl].astype(BF16))
        s_new = _dot_t(q2, kn_ref[:, sl])
        m = jnp.maximum(jnp.max(s_past, axis=-1, keepdims=True), jnp.max(s_new, axis=-1, keepdims=True))
        p_past = jnp.exp(s_past - m)
        p_new = jnp.exp(s_new - m)
        l = jnp.sum(p_past, axis=-1, keepdims=True) + jnp.sum(p_new, axis=-1, keepdims=True)
        acc = _dot(p_past.astype(BF16), vc_ref[:, sl].astype(BF16)) + _dot(p_new.astype(BF16), vn_ref[:, sl])
        o_ref[:, sl] = _diff_out(acc, l, lam, sg_ref[...], lam_init, t)


def _attn_sample(cfg, l, qb, kb, vb, cache_k, cache_v, lam_w, lam_init, yd):
    t, nb = cfg.t_dec, cfg.n_dec
    row0 = cfg.s_prompt // t
    new_rows = lambda: pl.BlockSpec((t, W_D), lambda b: (row0 + b, 0))
    cache = lambda: pl.BlockSpec((None, None, cfg.past, W_D), lambda b: (l, b, 0, 0))
    return pl.pallas_call(
        functools.partial(_attn_sample_kernel, t=t, lam_init=lam_init),
        grid=(nb,),
        in_specs=[new_rows(), new_rows(), new_rows(), cache(), cache()] + _lam_specs()
                 + [pl.BlockSpec(memory_space=pl.ANY)],
        out_specs=new_rows(),
        out_shape=jax.ShapeDtypeStruct((cfg.n_tok, W_D), F32),
        input_output_aliases={10: 0},
        compiler_params=pltpu.CompilerParams(
            dimension_semantics=("arbitrary",), vmem_limit_bytes=40 * MIB),
        name="attn_sample",
    )(qb, kb, vb, cache_k, cache_v, *lam_w, yd)


def _merge_kernel(x_ref, yabc_ref, yd_ref, g_ref, wg_ref, wpa_ref, wpb_ref, wpc_ref, wpd_ref, wo_ref, o_ref):
    x = x_ref[...]
    h = _rms(x, g_ref[...]).astype(BF16)
    branches = (
        (yabc_ref[:, 0:W_A], wpa_ref),
        (yabc_ref[:, W_A:W_A + W_B], wpb_ref),
        (yabc_ref[:, W_A + W_B:W_Y], wpc_ref),
        (yd_ref[...], wpd_ref),
    )
    merged = None
    for i, (y, wp_ref) in enumerate(branches):
        gate = jax.nn.sigmoid(_dot(h, wg_ref[:, i * D_MODEL:(i + 1) * D_MODEL]))
        term = gate * _dot(y.astype(BF16), wp_ref[...])
        merged = term if merged is None else merged + term
    o_ref[...] = x + _dot(merged.astype(BF16), wo_ref[...])


def _merge(cfg, l, x, y_abc, yd, g, w_gate, wpa, wpb, wpc, wpd, wo):
    n, tm = cfg.n_tok, cfg.tm
    row = lambda w: pl.BlockSpec((tm, w), lambda i: (i, 0))
    return pl.pallas_call(
        _merge_kernel,
        grid=(n // tm,),
        in_specs=[
            row(D_MODEL), row(W_Y), row(W_D),
            _const_spec((1, D_MODEL), (0, 0)),
            _const_spec((None, D_MODEL, 4 * D_MODEL), (l, 0, 0)),
            _const_spec((None, W_A, D_MODEL), (l, 0, 0)),
            _const_spec((None, W_B, D_MODEL), (l, 0, 0)),
            _const_spec((None, W_C, D_MODEL), (l, 0, 0)),
            _const_spec((None, W_D, D_MODEL), (l, 0, 0)),
            _const_spec((None, D_MODEL, D_MODEL), (l, 0, 0)),
        ],
        out_specs=row(D_MODEL),
        out_shape=jax.ShapeDtypeStruct((n, D_MODEL), F32),
        compiler_params=pltpu.CompilerParams(
            dimension_semantics=("arbitrary",), vmem_limit_bytes=48 * MIB),
        name="merge",
    )(x, y_abc, yd, g, w_gate, wpa, wpb, wpc, wpd, wo)


def _rope_tables(cfg):
    half = HEAD_DIM // 2
    inv_freq = ROPE_THETA ** (-jnp.arange(half, dtype=F32) / half)
    pos = jnp.concatenate([jnp.arange(cfg.s_prompt), jnp.tile(cfg.past + jnp.arange(cfg.t_dec), cfg.n_dec)])
    ang = pos.astype(F32)[:, None] * inv_freq[None, :]
    cos, sin = jnp.cos(ang), jnp.sin(ang)
    reps = LANES // HEAD_DIM
    cos_t = jnp.tile(jnp.concatenate([cos, cos], axis=1), (1, reps))
    sin_t = jnp.tile(jnp.concatenate([-sin, sin], axis=1), (1, reps))
    return cos_t, sin_t


def _pad_rows_top(a, rows):
    return jnp.pad(a, ((0, 0), (0, 0), (rows - a.shape[2], 0), (0, 0)))


def _forward(cfg, x_prompt, x_sample, cache_k, cache_v, state_pool, state_sconv, state_cconv,
             g_ffn1, w1_gate, w1_up, w1_down, g_mix, w_in, pool_w, pool_scale, sconv_w,
             cconv_w, cconv_b, ln_g, ln_b, q_norm_g, k_norm_g, lam_q1, lam_k1, lam_q2, lam_k2,
             subln_g, wp_a, wp_b, wp_c, wp_d, w_out, g_ffn2, w2_gate, w2_up, w2_down):
    depth = w_in.shape[0]
    sp, nb, td = cfg.s_prompt, cfg.n_dec, cfg.t_dec
    x = jnp.concatenate([x_prompt.reshape(sp, D_MODEL), x_sample.reshape(nb * td, D_MODEL)], axis=0)

    bf = lambda w: w.astype(BF16)
    w1g, w1u, w1d = bf(w1_gate), bf(w1_up), bf(w1_down)
    w2g, w2u, w2d = bf(w2_gate), bf(w2_up), bf(w2_down)
    w_proj, w_gate = bf(w_in[:, :, :W_PROJ]), bf(w_in[:, :, W_PROJ:])
    wpa, wpb, wpc, wpd, wo = bf(wp_a), bf(wp_b), bf(wp_c), bf(wp_d), bf(w_out)
    cos_t, sin_t = _rope_tables(cfg)
    ones_bd = jnp.kron(jnp.eye(LANES // HEAD_DIM, dtype=F32), jnp.ones((HEAD_DIM, HEAD_DIM), F32)).astype(BF16)
    eye_g = jnp.eye(len(POOL_WINDOWS), dtype=F32)
    ck = cache_k.reshape(depth, nb, cfg.past, W_D)
    cv = cache_v.reshape(depth, nb, cfg.past, W_D)
    st_pool = _pad_rows_top(state_pool, HALO)
    st_sconv = _pad_rows_top(state_sconv, HALO)
    st_cconv = _pad_rows_top(state_cconv, HALO)
    row = lambda a: a.reshape(1, -1)

    outs = [[] for _ in range(10)]
    for l in range(depth):
        lam_init = 0.8 - 0.6 * math.exp(-0.3 * l)
        x = _ffn(cfg, l, x, row(g_ffn1[l]), w1g, w1u, w1d)
        qg = row(jnp.tile(q_norm_g[l], LANES // HEAD_DIM))
        kg = row(jnp.tile(k_norm_g[l], LANES // HEAD_DIM))
        zabc, qb, k, kb, v, vb = _inproj(cfg, l, x, row(g_mix[l]), w_proj, qg, kg, cos_t, sin_t, ones_bd)

        pw_bd = (eye_g[:, None, :, None] * pool_w[l][:, :, None, :]).reshape(W_A, W_A).astype(BF16)
        mix_w = (pw_bd, row(pool_scale[l]), sconv_w[l], cconv_w[l], row(cconv_b[l]), row(ln_g[l]), row(ln_b[l]))
        y_abc, pool_p, sconv_p, cconv_p = _mixer_prompt(cfg, zabc, mix_w)
        y_abc, pool_s, sconv_s, cconv_s = _mixer_sample(cfg, zabc, st_pool[l], st_sconv[l], st_cconv[l],
                                                        mix_w, y_abc)

        lam_w = (row(lam_q1[l]), row(lam_k1[l]), row(lam_q2[l]), row(lam_k2[l]), row(subln_g[l]))
        yd = _attn_prompt(cfg, qb, kb, vb, lam_w, lam_init)
        yd = _attn_sample(cfg, l, qb, kb, vb, ck, cv, lam_w, lam_init, yd)

        x = _merge(cfg, l, x, y_abc, yd, row(g_mix[l]), w_gate, wpa, wpb, wpc, wpd, wo)
        x = _ffn(cfg, l, x, row(g_ffn2[l]), w2g, w2u, w2d)

        outs[0].append(k[:sp].reshape(1, sp, 2 * N_HEADS, HEAD_DIM))
        outs[1].append(v[:sp].reshape(1, sp, N_HEADS, 2 * HEAD_DIM))
        outs[2].append(pool_p[None, 16 - POOL_STATE:])
        outs[3].append(sconv_p[None, 8 - (SCONV_K - 1):])
        outs[4].append(cconv_p[None, 32 - (CCONV_K - 1):])
        outs[5].append(k[sp:].reshape(nb, td, 2 * N_HEADS, HEAD_DIM))
        outs[6].append(v[sp:].reshape(nb, td, N_HEADS, 2 * HEAD_DIM))
        outs[7].append(pool_s[:, 16 - POOL_STATE:])
        outs[8].append(sconv_s[:, 8 - (SCONV_K - 1):])
        outs[9].append(cconv_s[:, 32 - (CCONV_K - 1):])

    y_prompt = x[:sp].reshape(1, sp, D_MODEL)
    y_sample = x[sp:].reshape(nb, td, D_MODEL)
    return (y_prompt, y_sample) + tuple(jnp.stack(o) for o in outs)


def kernel(x_prompt, x_sample, cache_k, cache_v, state_pool, state_sconv, state_cconv, g_ffn1, w1_gate, w1_up, w1_down, g_mix, w_in, pool_w, pool_scale, sconv_w, cconv_w, cconv_b, ln_g, ln_b, q_norm_g, k_norm_g, lam_q1, lam_k1, lam_q2, lam_k2, subln_g, wp_a, wp_b, wp_c, wp_d, w_out, g_ffn2, w2_gate, w2_up, w2_down):
    assert x_prompt.shape[0] == 1
    cfg = Cfg(s_prompt=x_prompt.shape[1], n_dec=x_sample.shape[0], t_dec=x_sample.shape[1],
              past=cache_k.shape[2], tm=512, tq=256, tmix=512)
    return _forward(cfg, x_prompt, x_sample, cache_k, cache_v, state_pool, state_sconv, state_cconv,
                    g_ffn1, w1_gate, w1_up, w1_down, g_mix, w_in, pool_w, pool_scale, sconv_w,
                    cconv_w, cconv_b, ln_g, ln_b, q_norm_g, k_norm_g, lam_q1, lam_k1, lam_q2, lam_k2,
                    subln_g, wp_a, wp_b, wp_c, wp_d, w_out, g_ffn2, w2_gate, w2_up, w2_down)
```

```python
import functools
import math
from typing import NamedTuple

import jax
import jax.numpy as jnp
from jax import lax
from jax.experimental import pallas as pl
from jax.experimental.pallas import tpu as pltpu

F32 = jnp.float32
BF16 = jnp.bfloat16

D_MODEL = 1024
DEPTH = 4
CHUNK = 64
POOL_WINDOWS = (2, 4, 8, 16)
W_A = 256
GA = 64
POOL_STATE = 15
W_B = 256
SCONV_K = 3
W_C = 256
CCONV_K = 31
HEAD_DIM = 64
N_HEADS = 4
W_D = 512
ROPE_THETA = 10000.0
D_FF = 2816
EPS = 1e-6
W_ABC = W_A + 3 * W_B + 2 * W_C
W_PROJ = W_ABC + 3 * W_D
W_Y = W_A + W_B + W_C

LANES = 128
HALO = 32
MIB = 1024 * 1024


class Cfg(NamedTuple):
    s_prompt: int
    n_dec: int
    t_dec: int
    past: int
    tm: int
    tq: int
    tmix: int

    @property
    def n_tok(self):
        return self.s_prompt + self.n_dec * self.t_dec


def _const_spec(shape, index):
    return pl.BlockSpec(shape, lambda *_: index, pipeline_mode=pl.Buffered(1))


def _rms(x, g):
    ms = jnp.mean(x * x, axis=-1, keepdims=True)
    return x * lax.rsqrt(ms + EPS) * g


def _dot(a, b):
    return jnp.dot(a, b, preferred_element_type=F32)


def _dot_t(a, b):
    return lax.dot_general(a, b, (((1,), (1,)), ((), ())), preferred_element_type=F32)


def _ffn_kernel(x_ref, g_ref, wg_ref, wu_ref, wd_ref, o_ref):
    x = x_ref[...]
    h = _rms(x, g_ref[...]).astype(BF16)
    a = _dot(h, wg_ref[...])
    u = _dot(h, wu_ref[...])
    act = (a * jax.nn.sigmoid(a) * u).astype(BF16)
    o_ref[...] = x + 0.5 * _dot(act, wd_ref[...])


def _ffn(cfg, l, x, g, wg, wu, wd):
    n, tm = cfg.n_tok, cfg.tm
    return pl.pallas_call(
        _ffn_kernel,
        grid=(n // tm,),
        in_specs=[
            pl.BlockSpec((tm, D_MODEL), lambda i: (i, 0)),
            _const_spec((1, D_MODEL), (0, 0)),
            _const_spec((None, D_MODEL, D_FF), (l, 0, 0)),
            _const_spec((None, D_MODEL, D_FF), (l, 0, 0)),
            _const_spec((None, D_FF, D_MODEL), (l, 0, 0)),
        ],
        out_specs=pl.BlockSpec((tm, D_MODEL), lambda i: (i, 0)),
        out_shape=jax.ShapeDtypeStruct((n, D_MODEL), F32),
        compiler_params=pltpu.CompilerParams(
            dimension_semantics=("arbitrary",), vmem_limit_bytes=52 * MIB),
        name="ffn",
    )(x, g, wg, wu, wd)


def _inproj_kernel(x_ref, g_ref, w_ref, qg_ref, kg_ref, cos_ref, sin_ref, ones_ref,
                   zabc_ref, qb_ref, k_ref, kb_ref, v_ref, vb_ref, qt_ref, vt_ref):
    h = _rms(x_ref[...], g_ref[...]).astype(BF16)
    zabc_ref[...] = _dot(h, w_ref[:, 0:W_ABC])
    cos = cos_ref[...]
    sin = sin_ref[...]
    ones = ones_ref[...]
    lane = lax.broadcasted_iota(jnp.int32, cos.shape, 1)
    first_half = (lane & (HEAD_DIM // 2)) == 0

    def norm_rope(z, g):
        ss = z * z
        hi = ss.astype(BF16)
        lo = (ss - hi.astype(F32)).astype(BF16)
        tot = _dot(hi, ones) + _dot(lo, ones)
        y = z * lax.rsqrt(tot * (1.0 / HEAD_DIM) + EPS) * g
        half = HEAD_DIM // 2
        partner = jnp.where(first_half, pltpu.roll(y, LANES - half, 1), pltpu.roll(y, half, 1))
        return y * cos + partner * sin

    for c in range(W_D // LANES):
        sl = slice(c * LANES, (c + 1) * LANES)
        zq = _dot(h, w_ref[:, W_ABC + c * LANES: W_ABC + (c + 1) * LANES])
        q = norm_rope(zq, qg_ref[...]) * (HEAD_DIM ** -0.5)
        qb_ref[:, sl] = q.astype(BF16)
        qt_ref[c] = q.T.astype(BF16)
        zk = _dot(h, w_ref[:, W_ABC + W_D + c * LANES: W_ABC + W_D + (c + 1) * LANES])
        k = norm_rope(zk, kg_ref[...])
        k_ref[:, sl] = k
        kb_ref[:, sl] = k.astype(BF16)
        v = _dot(h, w_ref[:, W_ABC + 2 * W_D + c * LANES: W_ABC + 2 * W_D + (c + 1) * LANES])
        v_ref[:, sl] = v
        vb_ref[:, sl] = v.astype(BF16)
        vt_ref[c] = v.T.astype(BF16)


def _inproj(cfg, l, x, g, w_in, qg, kg, cos_t, sin_t, ones_bd):
    n, tm = cfg.n_tok, cfg.tm
    row = lambda w: pl.BlockSpec((tm, w), lambda i: (i, 0))
    return pl.pallas_call(
        _inproj_kernel,
        grid=(n // tm,),
        in_specs=[
            row(D_MODEL),
            _const_spec((1, D_MODEL), (0, 0)),
            _const_spec((None, D_MODEL, W_PROJ), (l, 0, 0)),
            _const_spec((1, LANES), (0, 0)),
            _const_spec((1, LANES), (0, 0)),
            row(LANES),
            row(LANES),
            _const_spec((LANES, LANES), (0, 0)),
        ],
        out_specs=[row(W_ABC), row(W_D), row(W_D), row(W_D), row(W_D), row(W_D),
                   pl.BlockSpec((N_HEADS, None, LANES, tm), lambda i: (0, i, 0, 0)),
                   pl.BlockSpec((N_HEADS, None, LANES, tm), lambda i: (0, i, 0, 0))],
        out_shape=[
            jax.ShapeDtypeStruct((n, W_ABC), F32),
            jax.ShapeDtypeStruct((n, W_D), BF16),
            jax.ShapeDtypeStruct((n, W_D), F32),
            jax.ShapeDtypeStruct((n, W_D), BF16),
            jax.ShapeDtypeStruct((n, W_D), F32),
            jax.ShapeDtypeStruct((n, W_D), BF16),
            jax.ShapeDtypeStruct((N_HEADS, n // tm, LANES, tm), BF16),
            jax.ShapeDtypeStruct((N_HEADS, n // tm, LANES, tm), BF16),
        ],
        compiler_params=pltpu.CompilerParams(
            dimension_semantics=("arbitrary",), vmem_limit_bytes=40 * MIB),
        name="inproj",
    )(x, g, w_in, qg, kg, cos_t, sin_t, ones_bd)


MIX_ROWS = 128


def _mixer_compute(t, pos0, z_ref, pw_ref, ps_ref, sw_ref, cw_ref, cb_ref, lg_ref, lb_ref,
                   y_ref, pool_o, sconv_o, cconv_o, ea, eb, ec):
    u = z_ref[:, 0:W_A]
    ea[HALO:HALO + t, :] = u
    eb[HALO:HALO + t, :] = z_ref[:, W_A + W_B:W_A + 2 * W_B] * z_ref[:, W_A + 2 * W_B:W_A + 3 * W_B]
    zc = z_ref[:, W_A + 3 * W_B:W_A + 3 * W_B + W_C]
    ec[HALO:HALO + t, :] = zc * jax.nn.sigmoid(z_ref[:, W_A + 3 * W_B + W_C:W_ABC])

    rows = min(t, MIX_ROWS)
    for r0 in range(0, t, rows):
        base = HALO + r0
        lane = lax.broadcasted_iota(jnp.int32, (rows, W_A), 1)
        cur = ea[base:base + rows, :]
        acc = cur
        sums = {}
        for j in range(1, max(POOL_WINDOWS)):
            acc = acc + ea[base - j:base - j + rows, :]
            if j + 1 in POOL_WINDOWS:
                sums[j + 1] = acc
        tot = sums[POOL_WINDOWS[-1]]
        win = jnp.full((rows, W_A), float(POOL_WINDOWS[-1]), F32)
        for gi in range(len(POOL_WINDOWS) - 2, -1, -1):
            in_group = lane < (gi + 1) * GA
            tot = jnp.where(in_group, sums[POOL_WINDOWS[gi]], tot)
            win = jnp.where(in_group, float(POOL_WINDOWS[gi]), win)
        if pos0 is None:
            cnt = win
        else:
            pos1 = (pos0 + r0 + 1 + lax.broadcasted_iota(jnp.int32, (rows, W_A), 0)).astype(F32)
            cnt = jnp.minimum(pos1, win)
        d = (tot / cnt - cur).astype(BF16)
        y_ref[r0:r0 + rows, 0:W_A] = _dot(d, pw_ref[...]) * ps_ref[...]

        conv = sw_ref[SCONV_K - 1:SCONV_K, :] * eb[base:base + rows, :]
        for j in range(SCONV_K - 1):
            off = base - (SCONV_K - 1) + j
            conv = conv + sw_ref[j:j + 1, :] * eb[off:off + rows, :]
        y_ref[r0:r0 + rows, W_A:W_A + W_B] = z_ref[r0:r0 + rows, W_A:W_A + W_B] * conv

        conv = cw_ref[CCONV_K - 1:CCONV_K, :] * ec[base:base + rows, :]
        for j in range(CCONV_K - 1):
            off = base - (CCONV_K - 1) + j
            conv = conv + cw_ref[j:j + 1, :] * ec[off:off + rows, :]
        conv = conv + cb_ref[...]
        mu = jnp.mean(conv, axis=-1, keepdims=True)
        cen = conv - mu
        var = jnp.mean(cen * cen, axis=-1, keepdims=True)
        ln = cen * lax.rsqrt(var + EPS) * lg_ref[...] + lb_ref[...]
        y_ref[r0:r0 + rows, W_A + W_B:W_Y] = ln * jax.nn.sigmoid(ln)

    pool_o[...] = ea[HALO + t - 16:HALO + t, :]
    sconv_o[...] = eb[HALO + t - 8:HALO + t, :]
    cconv_o[...] = ec[HALO + t - 32:HALO + t, :]


def _mixer_prompt_kernel(z_ref, pw_ref, ps_ref, sw_ref, cw_ref, cb_ref, lg_ref, lb_ref,
                         y_ref, pool_o, sconv_o, cconv_o, ea, eb, ec, *, t):
    i = pl.program_id(0)

    @pl.when(i == 0)
    def _():
        zeros = jnp.zeros((HALO, W_A), F32)
        ea[0:HALO, :] = zeros
        eb[0:HALO, :] = zeros
        ec[0:HALO, :] = zeros

    @pl.when(i > 0)
    def _():
        ea[0:HALO, :] = ea[t:t + HALO, :]
        eb[0:HALO, :] = eb[t:t + HALO, :]
        ec[0:HALO, :] = ec[t:t + HALO, :]

    _mixer_compute(t, i * t, z_ref, pw_ref, ps_ref, sw_ref, cw_ref, cb_ref, lg_ref, lb_ref,
                   y_ref, pool_o, sconv_o, cconv_o, ea, eb, ec)


def _mixer_sample_kernel(z_ref, sp_ref, ss_ref, sc_ref, pw_ref, ps_ref, sw_ref, cw_ref, cb_ref,
                         lg_ref, lb_ref, y_in_ref, y_ref, pool_o, sconv_o, cconv_o, ea, eb, ec, *, t):
    del y_in_ref
    ea[0:HALO, :] = sp_ref[...]
    eb[0:HALO, :] = ss_ref[...]
    ec[0:HALO, :] = sc_ref[...]
    _mixer_compute(t, None, z_ref, pw_ref, ps_ref, sw_ref, cw_ref, cb_ref, lg_ref, lb_ref,
                   y_ref, pool_o, sconv_o, cconv_o, ea, eb, ec)


def _mixer_weight_specs():
    return [
        _const_spec((W_A, W_A), (0, 0)),
        _const_spec((1, W_A), (0, 0)),
        _const_spec((SCONV_K, W_B), (0, 0)),
        _const_spec((CCONV_K, W_C), (0, 0)),
        _const_spec((1, W_C), (0, 0)),
        _const_spec((1, W_C), (0, 0)),
        _const_spec((1, W_C), (0, 0)),
    ]


def _mixer_scratch(t):
    return [pltpu.VMEM((HALO + t, W_A), F32), pltpu.VMEM((HALO + t, W_B), F32),
            pltpu.VMEM((HALO + t, W_C), F32)]


def _mixer_prompt(cfg, zabc, weights):
    t = cfg.tmix
    const_out = lambda r: pl.BlockSpec((r, W_A), lambda i: (0, 0))
    return pl.pallas_call(
        functools.partial(_mixer_prompt_kernel, t=t),
        grid=(cfg.s_prompt // t,),
        in_specs=[pl.BlockSpec((t, W_ABC), lambda i: (i, 0))] + _mixer_weight_specs(),
        out_specs=[pl.BlockSpec((t, W_Y), lambda i: (i, 0)), const_out(16), const_out(8), const_out(32)],
        out_shape=[
            jax.ShapeDtypeStruct((cfg.n_tok, W_Y), F32),
            jax.ShapeDtypeStruct((16, W_A), F32),
            jax.ShapeDtypeStruct((8, W_B), F32),
            jax.ShapeDtypeStruct((32, W_C), F32),
        ],
        scratch_shapes=_mixer_scratch(t),
        compiler_params=pltpu.CompilerParams(dimension_semantics=("arbitrary",)),
        name="mixer_prompt",
    )(zabc, *weights)


def _mixer_sample(cfg, zabc, st_pool, st_sconv, st_cconv, weights, y_abc):
    t, nb = cfg.t_dec, cfg.n_dec
    row0 = cfg.s_prompt // t
    state_spec = lambda: pl.BlockSpec((None, HALO, W_A), lambda b: (b, 0, 0))
    out_state = lambda r: pl.BlockSpec((None, r, W_A), lambda b: (b, 0, 0))
    return pl.pallas_call(
        functools.partial(_mixer_sample_kernel, t=t),
        grid=(nb,),
        in_specs=[pl.BlockSpec((t, W_ABC), lambda b: (row0 + b, 0)),
                  state_spec(), state_spec(), state_spec()]
                 + _mixer_weight_specs()
                 + [pl.BlockSpec(memory_space=pl.ANY)],
        out_specs=[pl.BlockSpec((t, W_Y), lambda b: (row0 + b, 0)),
                   out_state(16), out_state(8), out_state(32)],
        out_shape=[
            jax.ShapeDtypeStruct((cfg.n_tok, W_Y), F32),
            jax.ShapeDtypeStruct((nb, 16, W_A), F32),
            jax.ShapeDtypeStruct((nb, 8, W_B), F32),
            jax.ShapeDtypeStruct((nb, 32, W_C), F32),
        ],
        scratch_shapes=_mixer_scratch(t),
        input_output_aliases={11: 0},
        compiler_params=pltpu.CompilerParams(dimension_semantics=("arbitrary",)),
        name="mixer_sample",
    )(zabc, st_pool, st_sconv, st_cconv, *weights, y_abc)


def _lambda(lq1, lk1, lq2, lk2, lam_init):
    s1 = jnp.sum(lq1[...] * lk1[...], axis=-1, keepdims=True)
    s2 = jnp.sum(lq2[...] * lk2[...], axis=-1, keepdims=True)
    return jnp.exp(s1) - jnp.exp(s2) + lam_init


def _stack_maps(q):
    lane = lax.broadcasted_iota(jnp.int32, q.shape, 1)
    zero = jnp.zeros_like(q)
    return jnp.concatenate([jnp.where(lane < HEAD_DIM, q, zero), jnp.where(lane >= HEAD_DIM, q, zero)],
                           axis=0)


def _diff_out(acc, l, lam, sg, lam_init, t):
    o = acc[0:t] / l[0:t] - lam * (acc[t:2 * t] / l[t:2 * t])
    return _rms(o, sg) * (1.0 - lam_init)


def _attn_prompt_kernel(qt_ref, k_ref, vt_ref, lq1, lk1, lq2, lk2, sg_ref, o_ref,
                        q2t_ref, m_ref, l_ref, acc_ref, *, tq, lam_init):
    qi = pl.program_id(1)
    qt = qt_ref[...]
    chan = lax.broadcasted_iota(jnp.int32, qt.shape, 0)
    zero = jnp.zeros_like(qt)
    q2t_ref[:, 0:tq] = jnp.where(chan < HEAD_DIM, qt, zero)
    q2t_ref[:, tq:2 * tq] = jnp.where(chan >= HEAD_DIM, qt, zero)
    m_ref[...] = jnp.full(m_ref.shape, -jnp.inf, F32)
    l_ref[...] = jnp.zeros(l_ref.shape, F32)
    acc_ref[...] = jnp.zeros(acc_ref.shape, F32)

    def block(j, diagonal):
        kb = k_ref[pl.ds(pl.multiple_of(j * tq, tq), tq), :]
        s = _dot(kb, q2t_ref[...])
        if diagonal:
            key = lax.broadcasted_iota(jnp.int32, s.shape, 0)
            col = lax.broadcasted_iota(jnp.int32, s.shape, 1)
            s = jnp.where(key // CHUNK <= (col & (tq - 1)) // CHUNK, s, -jnp.inf)
        m_prev = m_ref[...]
        m_new = jnp.maximum(m_prev, jnp.max(s, axis=0, keepdims=True))
        alpha = jnp.exp(m_prev - m_new)
        p = jnp.exp(s - m_new)
        l_ref[...] = alpha * l_ref[...] + jnp.sum(p, axis=0, keepdims=True)
        acc_ref[...] = alpha * acc_ref[...] + _dot(vt_ref[j], p.astype(BF16))
        m_ref[...] = m_new

    def body(j, carry):
        block(j, False)
        return carry

    lax.fori_loop(0, qi, body, 0)
    block(qi, True)

    lam = _lambda(lq1, lk1, lq2, lk2, lam_init)
    acc = acc_ref[...]
    l = l_ref[...]
    ot = acc[:, 0:tq] / l[:, 0:tq] - lam * (acc[:, tq:2 * tq] / l[:, tq:2 * tq])
    o_ref[...] = _rms(ot.T, sg_ref[...]) * (1.0 - lam_init)


def _lam_specs():
    return [_const_spec((1, HEAD_DIM), (0, 0)) for _ in range(4)] + [_const_spec((1, LANES), (0, 0))]


def _attn_prompt(cfg, qt, kb, vt, lam_w, lam_init):
    s, tq = cfg.s_prompt, cfg.tq
    assert tq & (tq - 1) == 0 and tq % CHUNK == 0 and tq == cfg.tm
    return pl.pallas_call(
        functools.partial(_attn_prompt_kernel, tq=tq, lam_init=lam_init),
        grid=(N_HEADS, s // tq),
        in_specs=[
            pl.BlockSpec((None, None, LANES, tq), lambda h, i: (h, i, 0, 0)),
            pl.BlockSpec((s, LANES), lambda h, i: (0, h)),
            pl.BlockSpec((None, s // tq, LANES, tq), lambda h, i: (h, 0, 0, 0)),
        ] + _lam_specs(),
        out_specs=pl.BlockSpec((tq, LANES), lambda h, i: (i, h)),
        out_shape=jax.ShapeDtypeStruct((cfg.n_tok, W_D), F32),
        scratch_shapes=[
            pltpu.VMEM((LANES, 2 * tq), BF16),
            pltpu.VMEM((1, 2 * tq), F32),
            pltpu.VMEM((1, 2 * tq), F32),
            pltpu.VMEM((LANES, 2 * tq), F32),
        ],
        compiler_params=pltpu.CompilerParams(
            dimension_semantics=("arbitrary", "arbitrary"), vmem_limit_bytes=48 * MIB),
        name="attn_prompt",
    )(qt, kb, vt, *lam_w)


def _attn_sample_kernel(q_ref, kn_ref, vn_ref, kc_ref, vc_ref, lq1, lk1, lq2, lk2, sg_ref, yd_in_ref,
                        o_ref, *, t, lam_init):
    del yd_in_ref
    lam = _lambda(lq1, lk1, lq2, lk2, lam_init)
    for h in range(N_HEADS):
        sl = slice(h * LANES, (h + 1) * LANES)
        q2 = _stack_maps(q_ref[:, sl])
        s_past = _dot_t(q2, kc_ref[:, sl].astype(BF16))
        s_new = _dot_t(q2, kn_ref[:, sl])
        m = jnp.maximum(jnp.max(s_past, axis=-1, keepdims=True), jnp.max(s_new, axis=-1, keepdims=True))
        p_past = jnp.exp(s_past - m)
        p_new = jnp.exp(s_new - m)
        l = jnp.sum(p_past, axis=-1, keepdims=True) + jnp.sum(p_new, axis=-1, keepdims=True)
        acc = _dot(p_past.astype(BF16), vc_ref[:, sl].astype(BF16)) + _dot(p_new.astype(BF16), vn_ref[:, sl])
        o_ref[:, sl] = _diff_out(acc, l, lam, sg_ref[...], lam_init, t)


def _attn_sample(cfg, l, qb, kb, vb, cache_k, cache_v, lam_w, lam_init, yd):
    t, nb = cfg.t_dec, cfg.n_dec
    row0 = cfg.s_prompt // t
    new_rows = lambda: pl.BlockSpec((t, W_D), lambda b: (row0 + b, 0))
    cache = lambda: pl.BlockSpec((None, None, cfg.past, W_D), lambda b: (l, b, 0, 0))
    return pl.pallas_call(
        functools.partial(_attn_sample_kernel, t=t, lam_init=lam_init),
        grid=(nb,),
        in_specs=[new_rows(), new_rows(), new_rows(), cache(), cache()] + _lam_specs()
                 + [pl.BlockSpec(memory_space=pl.ANY)],
        out_specs=new_rows(),
        out_shape=jax.ShapeDtypeStruct((cfg.n_tok, W_D), F32),
        input_output_aliases={10: 0},
        compiler_params=pltpu.CompilerParams(
            dimension_semantics=("arbitrary",), vmem_limit_bytes=40 * MIB),
        name="attn_sample",
    )(qb, kb, vb, cache_k, cache_v, *lam_w, yd)


def _merge_kernel(x_ref, yabc_ref, yd_ref, g_ref, wg_ref, wpa_ref, wpb_ref, wpc_ref, wpd_ref, wo_ref, o_ref):
    x = x_ref[...]
    h = _rms(x, g_ref[...]).astype(BF16)
    branches = (
        (yabc_ref[:, 0:W_A], wpa_ref),
        (yabc_ref[:, W_A:W_A + W_B], wpb_ref),
        (yabc_ref[:, W_A + W_B:W_Y], wpc_ref),
        (yd_ref[...], wpd_ref),
    )
    merged = None
    for i, (y, wp_ref) in enumerate(branches):
        gate = jax.nn.sigmoid(_dot(h, wg_ref[:, i * D_MODEL:(i + 1) * D_MODEL]))
        term = gate * _dot(y.astype(BF16), wp_ref[...])
        merged = term if merged is None else merged + term
    o_ref[...] = x + _dot(merged.astype(BF16), wo_ref[...])


def _merge(cfg, l, x, y_abc, yd, g, w_gate, wpa, wpb, wpc, wpd, wo):
    n, tm = cfg.n_tok, cfg.tm
    row = lambda w: pl.BlockSpec((tm, w), lambda i: (i, 0))
    return pl.pallas_call(
        _merge_kernel,
        grid=(n // tm,),
        in_specs=[
            row(D_MODEL), row(W_Y), row(W_D),
            _const_spec((1, D_MODEL), (0, 0)),
            _const_spec((None, D_MODEL, 4 * D_MODEL), (l, 0, 0)),
            _const_spec((None, W_A, D_MODEL), (l, 0, 0)),
            _const_spec((None, W_B, D_MODEL), (l, 0, 0)),
            _const_spec((None, W_C, D_MODEL), (l, 0, 0)),
            _const_spec((None, W_D, D_MODEL), (l, 0, 0)),
            _const_spec((None, D_MODEL, D_MODEL), (l, 0, 0)),
        ],
        out_specs=row(D_MODEL),
        out_shape=jax.ShapeDtypeStruct((n, D_MODEL), F32),
        compiler_params=pltpu.CompilerParams(
            dimension_semantics=("arbitrary",), vmem_limit_bytes=48 * MIB),
        name="merge",
    )(x, y_abc, yd, g, w_gate, wpa, wpb, wpc, wpd, wo)


def _rope_tables(cfg):
    half = HEAD_DIM // 2
    inv_freq = ROPE_THETA ** (-jnp.arange(half, dtype=F32) / half)
    pos = jnp.concatenate([jnp.arange(cfg.s_prompt), jnp.tile(cfg.past + jnp.arange(cfg.t_dec), cfg.n_dec)])
    ang = pos.astype(F32)[:, None] * inv_freq[None, :]
    cos, sin = jnp.cos(ang), jnp.sin(ang)
    reps = LANES // HEAD_DIM
    cos_t = jnp.tile(jnp.concatenate([cos, cos], axis=1), (1, reps))
    sin_t = jnp.tile(jnp.concatenate([-sin, sin], axis=1), (1, reps))
    return cos_t, sin_t


def _pad_rows_top(a, rows):
    return jnp.pad(a, ((0, 0), (0, 0), (rows - a.shape[2], 0), (0, 0)))


def _forward(cfg, x_prompt, x_sample, cache_k, cache_v, state_pool, state_sconv, state_cconv,
             g_ffn1, w1_gate, w1_up, w1_down, g_mix, w_in, pool_w, pool_scale, sconv_w,
             cconv_w, cconv_b, ln_g, ln_b, q_norm_g, k_norm_g, lam_q1, lam_k1, lam_q2, lam_k2,
             subln_g, wp_a, wp_b, wp_c, wp_d, w_out, g_ffn2, w2_gate, w2_up, w2_down):
    depth = w_in.shape[0]
    sp, nb, td = cfg.s_prompt, cfg.n_dec, cfg.t_dec
    x = jnp.concatenate([x_prompt.reshape(sp, D_MODEL), x_sample.reshape(nb * td, D_MODEL)], axis=0)

    bf = lambda w: w.astype(BF16)
    w1g, w1u, w1d = bf(w1_gate), bf(w1_up), bf(w1_down)
    w2g, w2u, w2d = bf(w2_gate), bf(w2_up), bf(w2_down)
    w_proj, w_gate = bf(w_in[:, :, :W_PROJ]), bf(w_in[:, :, W_PROJ:])
    wpa, wpb, wpc, wpd, wo = bf(wp_a), bf(wp_b), bf(wp_c), bf(wp_d), bf(w_out)
    cos_t, sin_t = _rope_tables(cfg)
    ones_bd = jnp.kron(jnp.eye(LANES // HEAD_DIM, dtype=F32), jnp.ones((HEAD_DIM, HEAD_DIM), F32)).astype(BF16)
    eye_g = jnp.eye(len(POOL_WINDOWS), dtype=F32)
    ck = cache_k.reshape(depth, nb, cfg.past, W_D)
    cv = cache_v.reshape(depth, nb, cfg.past, W_D)
    st_pool = _pad_rows_top(state_pool, HALO)
    st_sconv = _pad_rows_top(state_sconv, HALO)
    st_cconv = _pad_rows_top(state_cconv, HALO)
    row = lambda a: a.reshape(1, -1)

    outs = [[] for _ in range(10)]
    for l in range(depth):
        lam_init = 0.8 - 0.6 * math.exp(-0.3 * l)
        x = _ffn(cfg, l, x, row(g_ffn1[l]), w1g, w1u, w1d)
        qg = row(jnp.tile(q_norm_g[l], LANES // HEAD_DIM))
        kg = row(jnp.tile(k_norm_g[l], LANES // HEAD_DIM))
        zabc, qb, k, kb, v, vb, qt, vt = _inproj(cfg, l, x, row(g_mix[l]), w_proj, qg, kg, cos_t, sin_t, ones_bd)

        pw_bd = (eye_g[:, None, :, None] * pool_w[l][:, :, None, :]).reshape(W_A, W_A).astype(BF16)
        mix_w = (pw_bd, row(pool_scale[l]), sconv_w[l], cconv_w[l], row(cconv_b[l]), row(ln_g[l]), row(ln_b[l]))
        y_abc, pool_p, sconv_p, cconv_p = _mixer_prompt(cfg, zabc, mix_w)
        y_abc, pool_s, sconv_s, cconv_s = _mixer_sample(cfg, zabc, st_pool[l], st_sconv[l], st_cconv[l],
                                                        mix_w, y_abc)

        lam_w = (row(lam_q1[l]), row(lam_k1[l]), row(lam_q2[l]), row(lam_k2[l]), row(subln_g[l]))
        yd = _attn_prompt(cfg, qt, kb, vt, lam_w, lam_init)
        yd = _attn_sample(cfg, l, qb, kb, vb, ck, cv, lam_w, lam_init, yd)

        x = _merge(cfg, l, x, y_abc, yd, row(g_mix[l]), w_gate, wpa, wpb, wpc, wpd, wo)
        x = _ffn(cfg, l, x, row(g_ffn2[l]), w2g, w2u, w2d)

        outs[0].append(k[:sp].reshape(1, sp, 2 * N_HEADS, HEAD_DIM))
        outs[1].append(v[:sp].reshape(1, sp, N_HEADS, 2 * HEAD_DIM))
        outs[2].append(pool_p[None, 16 - POOL_STATE:])
        outs[3].append(sconv_p[None, 8 - (SCONV_K - 1):])
        outs[4].append(cconv_p[None, 32 - (CCONV_K - 1):])
        outs[5].append(k[sp:].reshape(nb, td, 2 * N_HEADS, HEAD_DIM))
        outs[6].append(v[sp:].reshape(nb, td, N_HEADS, 2 * HEAD_DIM))
        outs[7].append(pool_s[:, 16 - POOL_STATE:])
        outs[8].append(sconv_s[:, 8 - (SCONV_K - 1):])
        outs[9].append(cconv_s[:, 32 - (CCONV_K - 1):])

    y_prompt = x[:sp].reshape(1, sp, D_MODEL)
    y_sample = x[sp:].reshape(nb, td, D_MODEL)
    return (y_prompt, y_sample) + tuple(jnp.stack(o) for o in outs)


def kernel(x_prompt, x_sample, cache_k, cache_v, state_pool, state_sconv, state_cconv, g_ffn1, w1_gate, w1_up, w1_down, g_mix, w_in, pool_w, pool_scale, sconv_w, cconv_w, cconv_b, ln_g, ln_b, q_norm_g, k_norm_g, lam_q1, lam_k1, lam_q2, lam_k2, subln_g, wp_a, wp_b, wp_c, wp_d, w_out, g_ffn2, w2_gate, w2_up, w2_down):
    assert x_prompt.shape[0] == 1
    cfg = Cfg(s_prompt=x_prompt.shape[1], n_dec=x_sample.shape[0], t_dec=x_sample.shape[1],
              past=cache_k.shape[2], tm=512, tq=512, tmix=512)
    return _forward(cfg, x_prompt, x_sample, cache_k, cache_v, state_pool, state_sconv, state_cconv,
                    g_ffn1, w1_gate, w1_up, w1_down, g_mix, w_in, pool_w, pool_scale, sconv_w,
                    cconv_w, cconv_b, ln_g, ln_b, q_norm_g, k_norm_g, lam_q1, lam_k1, lam_q2, lam_k2,
                    subln_g, wp_a, wp_b, wp_c, wp_d, w_out, g_ffn2, w2_gate, w2_up, w2_down)
```

```python
import functools
import math
from typing import NamedTuple

import jax
import jax.numpy as jnp
from jax import lax
from jax.experimental import pallas as pl
from jax.experimental.pallas import tpu as pltpu

F32 = jnp.float32
BF16 = jnp.bfloat16

D_MODEL = 1024
DEPTH = 4
CHUNK = 64
POOL_WINDOWS = (2, 4, 8, 16)
W_A = 256
GA = 64
POOL_STATE = 15
W_B = 256
SCONV_K = 3
W_C = 256
CCONV_K = 31
HEAD_DIM = 64
N_HEADS = 4
W_D = 512
ROPE_THETA = 10000.0
D_FF = 2816
EPS = 1e-6
W_ABC = W_A + 3 * W_B + 2 * W_C
W_PROJ = W_ABC + 3 * W_D
W_Y = W_A + W_B + W_C

LANES = 128
VT_ROWS = LANES + 16
LOG2E = 1.4426950408889634
HALO = 32
MIB = 1024 * 1024


class Cfg(NamedTuple):
    s_prompt: int
    n_dec: int
    t_dec: int
    past: int
    tm: int
    tq: int
    tmix: int

    @property
    def n_tok(self):
        return self.s_prompt + self.n_dec * self.t_dec


def _const_spec(shape, index):
    return pl.BlockSpec(shape, lambda *_: index, pipeline_mode=pl.Buffered(1))


def _rms(x, g):
    ms = jnp.mean(x * x, axis=-1, keepdims=True)
    return x * lax.rsqrt(ms + EPS) * g


def _dot(a, b):
    return jnp.dot(a, b, preferred_element_type=F32)


def _dot_t(a, b):
    return lax.dot_general(a, b, (((1,), (1,)), ((), ())), preferred_element_type=F32)


def _ffn_kernel(x_ref, g_ref, wg_ref, wu_ref, wd_ref, o_ref):
    x = x_ref[...]
    h = _rms(x, g_ref[...]).astype(BF16)
    a = _dot(h, wg_ref[...])
    u = _dot(h, wu_ref[...])
    act = (a * jax.nn.sigmoid(a) * u).astype(BF16)
    o_ref[...] = x + 0.5 * _dot(act, wd_ref[...])


def _ffn(cfg, l, x, g, wg, wu, wd):
    n, tm = cfg.n_tok, cfg.tm
    return pl.pallas_call(
        _ffn_kernel,
        grid=(n // tm,),
        in_specs=[
            pl.BlockSpec((tm, D_MODEL), lambda i: (i, 0)),
            _const_spec((1, D_MODEL), (0, 0)),
            _const_spec((None, D_MODEL, D_FF), (l, 0, 0)),
            _const_spec((None, D_MODEL, D_FF), (l, 0, 0)),
            _const_spec((None, D_FF, D_MODEL), (l, 0, 0)),
        ],
        out_specs=pl.BlockSpec((tm, D_MODEL), lambda i: (i, 0)),
        out_shape=jax.ShapeDtypeStruct((n, D_MODEL), F32),
        compiler_params=pltpu.CompilerParams(
            dimension_semantics=("arbitrary",), vmem_limit_bytes=52 * MIB),
        name="ffn",
    )(x, g, wg, wu, wd)


def _inproj_kernel(x_ref, g_ref, w_ref, qg_ref, kg_ref, cos_ref, sin_ref, ones_ref,
                   zabc_ref, qb_ref, k_ref, kb_ref, v_ref, vb_ref, qt_ref, vt_ref):
    h = _rms(x_ref[...], g_ref[...]).astype(BF16)
    zabc_ref[...] = _dot(h, w_ref[:, 0:W_ABC])
    cos = cos_ref[...]
    sin = sin_ref[...]
    ones = ones_ref[...]
    lane = lax.broadcasted_iota(jnp.int32, cos.shape, 1)
    first_half = (lane & (HEAD_DIM // 2)) == 0

    def norm_rope(z, g):
        ss = z * z
        hi = ss.astype(BF16)
        lo = (ss - hi.astype(F32)).astype(BF16)
        tot = _dot(hi, ones) + _dot(lo, ones)
        y = z * lax.rsqrt(tot * (1.0 / HEAD_DIM) + EPS) * g
        half = HEAD_DIM // 2
        partner = jnp.where(first_half, pltpu.roll(y, LANES - half, 1), pltpu.roll(y, half, 1))
        return y * cos + partner * sin

    for c in range(W_D // LANES):
        sl = slice(c * LANES, (c + 1) * LANES)
        zq = _dot(h, w_ref[:, W_ABC + c * LANES: W_ABC + (c + 1) * LANES])
        q = norm_rope(zq, qg_ref[...]) * (HEAD_DIM ** -0.5)
        qb_ref[:, sl] = q.astype(BF16)
        qt_ref[c] = (q * LOG2E).T.astype(BF16)
        zk = _dot(h, w_ref[:, W_ABC + W_D + c * LANES: W_ABC + W_D + (c + 1) * LANES])
        k = norm_rope(zk, kg_ref[...])
        k_ref[:, sl] = k
        kb_ref[:, sl] = k.astype(BF16)
        v = _dot(h, w_ref[:, W_ABC + 2 * W_D + c * LANES: W_ABC + 2 * W_D + (c + 1) * LANES])
        v_ref[:, sl] = v
        vb_ref[:, sl] = v.astype(BF16)
        vt_ref[c, 0:LANES, :] = v.T.astype(BF16)
        vt_ref[c, LANES:VT_ROWS, :] = jnp.ones((VT_ROWS - LANES, v.shape[0]), BF16)


def _inproj(cfg, l, x, g, w_in, qg, kg, cos_t, sin_t, ones_bd):
    n, tm = cfg.n_tok, cfg.tm
    row = lambda w: pl.BlockSpec((tm, w), lambda i: (i, 0))
    return pl.pallas_call(
        _inproj_kernel,
        grid=(n // tm,),
        in_specs=[
            row(D_MODEL),
            _const_spec((1, D_MODEL), (0, 0)),
            _const_spec((None, D_MODEL, W_PROJ), (l, 0, 0)),
            _const_spec((1, LANES), (0, 0)),
            _const_spec((1, LANES), (0, 0)),
            row(LANES),
            row(LANES),
            _const_spec((LANES, LANES), (0, 0)),
        ],
        out_specs=[row(W_ABC), row(W_D), row(W_D), row(W_D), row(W_D), row(W_D),
                   pl.BlockSpec((N_HEADS, None, LANES, tm), lambda i: (0, i, 0, 0)),
                   pl.BlockSpec((N_HEADS, None, VT_ROWS, tm), lambda i: (0, i, 0, 0))],
        out_shape=[
            jax.ShapeDtypeStruct((n, W_ABC), F32),
            jax.ShapeDtypeStruct((n, W_D), BF16),
            jax.ShapeDtypeStruct((n, W_D), F32),
            jax.ShapeDtypeStruct((n, W_D), BF16),
            jax.ShapeDtypeStruct((n, W_D), F32),
            jax.ShapeDtypeStruct((n, W_D), BF16),
            jax.ShapeDtypeStruct((N_HEADS, n // tm, LANES, tm), BF16),
            jax.ShapeDtypeStruct((N_HEADS, n // tm, VT_ROWS, tm), BF16),
        ],
        compiler_params=pltpu.CompilerParams(
            dimension_semantics=("arbitrary",), vmem_limit_bytes=40 * MIB),
        name="inproj",
    )(x, g, w_in, qg, kg, cos_t, sin_t, ones_bd)


MIX_ROWS = 128


def _mixer_compute(t, pos0, z_ref, pw_ref, ps_ref, sw_ref, cw_ref, cb_ref, lg_ref, lb_ref,
                   y_ref, pool_o, sconv_o, cconv_o, ea, eb, ec):
    u = z_ref[:, 0:W_A]
    ea[HALO:HALO + t, :] = u
    eb[HALO:HALO + t, :] = z_ref[:, W_A + W_B:W_A + 2 * W_B] * z_ref[:, W_A + 2 * W_B:W_A + 3 * W_B]
    zc = z_ref[:, W_A + 3 * W_B:W_A + 3 * W_B + W_C]
    ec[HALO:HALO + t, :] = zc * jax.nn.sigmoid(z_ref[:, W_A + 3 * W_B + W_C:W_ABC])

    rows = min(t, MIX_ROWS)
    for r0 in range(0, t, rows):
        base = HALO + r0
        lane = lax.broadcasted_iota(jnp.int32, (rows, W_A), 1)
        cur = ea[base:base + rows, :]
        acc = cur
        sums = {}
        for j in range(1, max(POOL_WINDOWS)):
            acc = acc + ea[base - j:base - j + rows, :]
            if j + 1 in POOL_WINDOWS:
                sums[j + 1] = acc
        tot = sums[POOL_WINDOWS[-1]]
        win = jnp.full((rows, W_A), float(POOL_WINDOWS[-1]), F32)
        for gi in range(len(POOL_WINDOWS) - 2, -1, -1):
            in_group = lane < (gi + 1) * GA
            tot = jnp.where(in_group, sums[POOL_WINDOWS[gi]], tot)
            win = jnp.where(in_group, float(POOL_WINDOWS[gi]), win)
        if pos0 is None:
            cnt = win
        else:
            pos1 = (pos0 + r0 + 1 + lax.broadcasted_iota(jnp.int32, (rows, W_A), 0)).astype(F32)
            cnt = jnp.minimum(pos1, win)
        d = (tot / cnt - cur).astype(BF16)
        y_ref[r0:r0 + rows, 0:W_A] = _dot(d, pw_ref[...]) * ps_ref[...]

        conv = sw_ref[SCONV_K - 1:SCONV_K, :] * eb[base:base + rows, :]
        for j in range(SCONV_K - 1):
            off = base - (SCONV_K - 1) + j
            conv = conv + sw_ref[j:j + 1, :] * eb[off:off + rows, :]
        y_ref[r0:r0 + rows, W_A:W_A + W_B] = z_ref[r0:r0 + rows, W_A:W_A + W_B] * conv

        conv = cw_ref[CCONV_K - 1:CCONV_K, :] * ec[base:base + rows, :]
        for j in range(CCONV_K - 1):
            off = base - (CCONV_K - 1) + j
            conv = conv + cw_ref[j:j + 1, :] * ec[off:off + rows, :]
        conv = conv + cb_ref[...]
        mu = jnp.mean(conv, axis=-1, keepdims=True)
        cen = conv - mu
        var = jnp.mean(cen * cen, axis=-1, keepdims=True)
        ln = cen * lax.rsqrt(var + EPS) * lg_ref[...] + lb_ref[...]
        y_ref[r0:r0 + rows, W_A + W_B:W_Y] = ln * jax.nn.sigmoid(ln)

    pool_o[...] = ea[HALO + t - 16:HALO + t, :]
    sconv_o[...] = eb[HALO + t - 8:HALO + t, :]
    cconv_o[...] = ec[HALO + t - 32:HALO + t, :]


def _mixer_prompt_kernel(z_ref, pw_ref, ps_ref, sw_ref, cw_ref, cb_ref, lg_ref, lb_ref,
                         y_ref, pool_o, sconv_o, cconv_o, ea, eb, ec, *, t):
    i = pl.program_id(0)

    @pl.when(i == 0)
    def _():
        zeros = jnp.zeros((HALO, W_A), F32)
        ea[0:HALO, :] = zeros
        eb[0:HALO, :] = zeros
        ec[0:HALO, :] = zeros

    @pl.when(i > 0)
    def _():
        ea[0:HALO, :] = ea[t:t + HALO, :]
        eb[0:HALO, :] = eb[t:t + HALO, :]
        ec[0:HALO, :] = ec[t:t + HALO, :]

    _mixer_compute(t, i * t, z_ref, pw_ref, ps_ref, sw_ref, cw_ref, cb_ref, lg_ref, lb_ref,
                   y_ref, pool_o, sconv_o, cconv_o, ea, eb, ec)


def _mixer_sample_kernel(z_ref, sp_ref, ss_ref, sc_ref, pw_ref, ps_ref, sw_ref, cw_ref, cb_ref,
                         lg_ref, lb_ref, y_in_ref, y_ref, pool_o, sconv_o, cconv_o, ea, eb, ec, *, t):
    del y_in_ref
    ea[0:HALO, :] = sp_ref[...]
    eb[0:HALO, :] = ss_ref[...]
    ec[0:HALO, :] = sc_ref[...]
    _mixer_compute(t, None, z_ref, pw_ref, ps_ref, sw_ref, cw_ref, cb_ref, lg_ref, lb_ref,
                   y_ref, pool_o, sconv_o, cconv_o, ea, eb, ec)


def _mixer_weight_specs():
    return [
        _const_spec((W_A, W_A), (0, 0)),
        _const_spec((1, W_A), (0, 0)),
        _const_spec((SCONV_K, W_B), (0, 0)),
        _const_spec((CCONV_K, W_C), (0, 0)),
        _const_spec((1, W_C), (0, 0)),
        _const_spec((1, W_C), (0, 0)),
        _const_spec((1, W_C), (0, 0)),
    ]


def _mixer_scratch(t):
    return [pltpu.VMEM((HALO + t, W_A), F32), pltpu.VMEM((HALO + t, W_B), F32),
            pltpu.VMEM((HALO + t, W_C), F32)]


def _mixer_prompt(cfg, zabc, weights):
    t = cfg.tmix
    const_out = lambda r: pl.BlockSpec((r, W_A), lambda i: (0, 0))
    return pl.pallas_call(
        functools.partial(_mixer_prompt_kernel, t=t),
        grid=(cfg.s_prompt // t,),
        in_specs=[pl.BlockSpec((t, W_ABC), lambda i: (i, 0))] + _mixer_weight_specs(),
        out_specs=[pl.BlockSpec((t, W_Y), lambda i: (i, 0)), const_out(16), const_out(8), const_out(32)],
        out_shape=[
            jax.ShapeDtypeStruct((cfg.n_tok, W_Y), F32),
            jax.ShapeDtypeStruct((16, W_A), F32),
            jax.ShapeDtypeStruct((8, W_B), F32),
            jax.ShapeDtypeStruct((32, W_C), F32),
        ],
        scratch_shapes=_mixer_scratch(t),
        compiler_params=pltpu.CompilerParams(dimension_semantics=("arbitrary",)),
        name="mixer_prompt",
    )(zabc, *weights)


def _mixer_sample(cfg, zabc, st_pool, st_sconv, st_cconv, weights, y_abc):
    t, nb = cfg.t_dec, cfg.n_dec
    row0 = cfg.s_prompt // t
    state_spec = lambda: pl.BlockSpec((None, HALO, W_A), lambda b: (b, 0, 0))
    out_state = lambda r: pl.BlockSpec((None, r, W_A), lambda b: (b, 0, 0))
    return pl.pallas_call(
        functools.partial(_mixer_sample_kernel, t=t),
        grid=(nb,),
        in_specs=[pl.BlockSpec((t, W_ABC), lambda b: (row0 + b, 0)),
                  state_spec(), state_spec(), state_spec()]
                 + _mixer_weight_specs()
                 + [pl.BlockSpec(memory_space=pl.ANY)],
        out_specs=[pl.BlockSpec((t, W_Y), lambda b: (row0 + b, 0)),
                   out_state(16), out_state(8), out_state(32)],
        out_shape=[
            jax.ShapeDtypeStruct((cfg.n_tok, W_Y), F32),
            jax.ShapeDtypeStruct((nb, 16, W_A), F32),
            jax.ShapeDtypeStruct((nb, 8, W_B), F32),
            jax.ShapeDtypeStruct((nb, 32, W_C), F32),
        ],
        scratch_shapes=_mixer_scratch(t),
        input_output_aliases={11: 0},
        compiler_params=pltpu.CompilerParams(dimension_semantics=("arbitrary",)),
        name="mixer_sample",
    )(zabc, st_pool, st_sconv, st_cconv, *weights, y_abc)


def _lambda(lq1, lk1, lq2, lk2, lam_init):
    s1 = jnp.sum(lq1[...] * lk1[...], axis=-1, keepdims=True)
    s2 = jnp.sum(lq2[...] * lk2[...], axis=-1, keepdims=True)
    return jnp.exp(s1) - jnp.exp(s2) + lam_init


def _stack_maps(q):
    lane = lax.broadcasted_iota(jnp.int32, q.shape, 1)
    zero = jnp.zeros_like(q)
    return jnp.concatenate([jnp.where(lane < HEAD_DIM, q, zero), jnp.where(lane >= HEAD_DIM, q, zero)],
                           axis=0)


def _diff_out(acc, l, lam, sg, lam_init, t):
    o = acc[0:t] / l[0:t] - lam * (acc[t:2 * t] / l[t:2 * t])
    return _rms(o, sg) * (1.0 - lam_init)


def _attn_prompt_kernel(qt_ref, k_ref, vt_ref, lq1, lk1, lq2, lk2, sg_ref, o_ref,
                        q2t_ref, s0_ref, s1_ref, m_ref, acc_ref, *, tq, lam_init):
    qi = pl.program_id(1)
    qt = qt_ref[...]
    chan = lax.broadcasted_iota(jnp.int32, qt.shape, 0)
    zero = jnp.zeros_like(qt)
    q2t_ref[:, 0:tq] = jnp.where(chan < HEAD_DIM, qt, zero)
    q2t_ref[:, tq:2 * tq] = jnp.where(chan >= HEAD_DIM, qt, zero)
    m_ref[...] = jnp.full(m_ref.shape, -jnp.inf, F32)
    acc_ref[...] = jnp.zeros(acc_ref.shape, F32)

    def scores(j, s_ref):
        kb = k_ref[pl.ds(pl.multiple_of(j * tq, tq), tq), :]
        s_ref[...] = _dot(kb, q2t_ref[...])

    def consume(j, s_ref, diagonal):
        s = s_ref[...]
        if diagonal:
            key = lax.broadcasted_iota(jnp.int32, s.shape, 0)
            col = lax.broadcasted_iota(jnp.int32, s.shape, 1)
            s = jnp.where(key // CHUNK <= (col & (tq - 1)) // CHUNK, s, -jnp.inf)
        m_prev = m_ref[...]
        m_new = jnp.maximum(m_prev, jnp.max(s, axis=0, keepdims=True))
        alpha = jnp.exp2(m_prev - m_new)
        p = jnp.exp2(s - m_new).astype(BF16)
        acc_ref[...] = alpha * acc_ref[...] + _dot(vt_ref[j], p)
        m_ref[...] = m_new

    scores(0, s0_ref)

    def pair(i, carry):
        j = 2 * i
        scores(j + 1, s1_ref)
        consume(j, s0_ref, False)
        scores(j + 2, s0_ref)
        consume(j + 1, s1_ref, False)
        return carry

    lax.fori_loop(0, qi // 2, pair, 0)
    odd = (qi & 1) == 1

    @pl.when(odd)
    def _():
        scores(qi, s1_ref)
        consume(qi - 1, s0_ref, False)
        consume(qi, s1_ref, True)

    @pl.when(jnp.logical_not(odd))
    def _():
        consume(qi, s0_ref, True)

    lam = _lambda(lq1, lk1, lq2, lk2, lam_init)
    acc = acc_ref[0:LANES, :]
    l = acc_ref[LANES:LANES + 1, :]
    ot = acc[:, 0:tq] / l[:, 0:tq] - lam * (acc[:, tq:2 * tq] / l[:, tq:2 * tq])
    o_ref[...] = _rms(ot.T, sg_ref[...]) * (1.0 - lam_init)


def _lam_specs():
    return [_const_spec((1, HEAD_DIM), (0, 0)) for _ in range(4)] + [_const_spec((1, LANES), (0, 0))]


def _attn_prompt(cfg, qt, kb, vt, lam_w, lam_init):
    s, tq = cfg.s_prompt, cfg.tq
    assert tq & (tq - 1) == 0 and tq % CHUNK == 0 and tq == cfg.tm
    return pl.pallas_call(
        functools.partial(_attn_prompt_kernel, tq=tq, lam_init=lam_init),
        grid=(N_HEADS, s // tq),
        in_specs=[
            pl.BlockSpec((None, None, LANES, tq), lambda h, i: (h, i, 0, 0)),
            pl.BlockSpec((s, LANES), lambda h, i: (0, h)),
            pl.BlockSpec((None, s // tq, VT_ROWS, tq), lambda h, i: (h, 0, 0, 0)),
        ] + _lam_specs(),
        out_specs=pl.BlockSpec((tq, LANES), lambda h, i: (i, h)),
        out_shape=jax.ShapeDtypeStruct((cfg.n_tok, W_D), F32),
        scratch_shapes=[
            pltpu.VMEM((LANES, 2 * tq), BF16),
            pltpu.VMEM((tq, 2 * tq), F32),
            pltpu.VMEM((tq, 2 * tq), F32),
            pltpu.VMEM((1, 2 * tq), F32),
            pltpu.VMEM((VT_ROWS, 2 * tq), F32),
        ],
        compiler_params=pltpu.CompilerParams(
            dimension_semantics=("arbitrary", "arbitrary"), vmem_limit_bytes=48 * MIB),
        name="attn_prompt",
    )(qt, kb, vt, *lam_w)


def _attn_sample_kernel(q_ref, kn_ref, vn_ref, kc_ref, vc_ref, lq1, lk1, lq2, lk2, sg_ref, yd_in_ref,
                        o_ref, *, t, lam_init):
    del yd_in_ref
    lam = _lambda(lq1, lk1, lq2, lk2, lam_init)
    past = vc_ref.shape[0] // N_HEADS
    for h in range(N_HEADS):
        sl = slice(h * LANES, (h + 1) * LANES)
        q2 = _stack_maps(q_ref[:, sl])
        k_past = jnp.concatenate([kc_ref[pl.ds(2 * h, past, stride=2 * N_HEADS), :],
                                  kc_ref[pl.ds(2 * h + 1, past, stride=2 * N_HEADS), :]], axis=1)
        v_past = vc_ref[pl.ds(h, past, stride=N_HEADS), :]
        s_past = _dot_t(q2, k_past.astype(BF16))
        s_new = _dot_t(q2, kn_ref[:, sl])
        m = jnp.maximum(jnp.max(s_past, axis=-1, keepdims=True), jnp.max(s_new, axis=-1, keepdims=True))
        p_past = jnp.exp(s_past - m)
        p_new = jnp.exp(s_new - m)
        l = jnp.sum(p_past, axis=-1, keepdims=True) + jnp.sum(p_new, axis=-1, keepdims=True)
        acc = _dot(p_past.astype(BF16), v_past.astype(BF16)) + _dot(p_new.astype(BF16), vn_ref[:, sl])
        o_ref[:, sl] = _diff_out(acc, l, lam, sg_ref[...], lam_init, t)


def _attn_sample(cfg, l, qb, kb, vb, cache_k, cache_v, lam_w, lam_init, yd):
    t, nb = cfg.t_dec, cfg.n_dec
    row0 = cfg.s_prompt // t
    new_rows = lambda: pl.BlockSpec((t, W_D), lambda b: (row0 + b, 0))
    cache = lambda a: pl.BlockSpec((None, None) + a.shape[2:], lambda b: (l, b, 0, 0))
    return pl.pallas_call(
        functools.partial(_attn_sample_kernel, t=t, lam_init=lam_init),
        grid=(nb,),
        in_specs=[new_rows(), new_rows(), new_rows(), cache(cache_k), cache(cache_v)] + _lam_specs()
                 + [pl.BlockSpec(memory_space=pl.ANY)],
        out_specs=new_rows(),
        out_shape=jax.ShapeDtypeStruct((cfg.n_tok, W_D), F32),
        input_output_aliases={10: 0},
        compiler_params=pltpu.CompilerParams(
            dimension_semantics=("arbitrary",), vmem_limit_bytes=40 * MIB),
        name="attn_sample",
    )(qb, kb, vb, cache_k, cache_v, *lam_w, yd)


def _merge_kernel(x_ref, yabc_ref, yd_ref, g_ref, wg_ref, wpa_ref, wpb_ref, wpc_ref, wpd_ref, wo_ref, o_ref):
    x = x_ref[...]
    h = _rms(x, g_ref[...]).astype(BF16)
    branches = (
        (yabc_ref[:, 0:W_A], wpa_ref),
        (yabc_ref[:, W_A:W_A + W_B], wpb_ref),
        (yabc_ref[:, W_A + W_B:W_Y], wpc_ref),
        (yd_ref[...], wpd_ref),
    )
    merged = None
    for i, (y, wp_ref) in enumerate(branches):
        gate = jax.nn.sigmoid(_dot(h, wg_ref[:, i * D_MODEL:(i + 1) * D_MODEL]))
        term = gate * _dot(y.astype(BF16), wp_ref[...])
        merged = term if merged is None else merged + term
    o_ref[...] = x + _dot(merged.astype(BF16), wo_ref[...])


def _merge(cfg, l, x, y_abc, yd, g, w_gate, wpa, wpb, wpc, wpd, wo):
    n, tm = cfg.n_tok, cfg.tm
    row = lambda w: pl.BlockSpec((tm, w), lambda i: (i, 0))
    return pl.pallas_call(
        _merge_kernel,
        grid=(n // tm,),
        in_specs=[
            row(D_MODEL), row(W_Y), row(W_D),
            _const_spec((1, D_MODEL), (0, 0)),
            _const_spec((None, D_MODEL, 4 * D_MODEL), (l, 0, 0)),
            _const_spec((None, W_A, D_MODEL), (l, 0, 0)),
            _const_spec((None, W_B, D_MODEL), (l, 0, 0)),
            _const_spec((None, W_C, D_MODEL), (l, 0, 0)),
            _const_spec((None, W_D, D_MODEL), (l, 0, 0)),
            _const_spec((None, D_MODEL, D_MODEL), (l, 0, 0)),
        ],
        out_specs=row(D_MODEL),
        out_shape=jax.ShapeDtypeStruct((n, D_MODEL), F32),
        compiler_params=pltpu.CompilerParams(
            dimension_semantics=("arbitrary",), vmem_limit_bytes=48 * MIB),
        name="merge",
    )(x, y_abc, yd, g, w_gate, wpa, wpb, wpc, wpd, wo)


def _rope_tables(cfg):
    half = HEAD_DIM // 2
    inv_freq = ROPE_THETA ** (-jnp.arange(half, dtype=F32) / half)
    pos = jnp.concatenate([jnp.arange(cfg.s_prompt), jnp.tile(cfg.past + jnp.arange(cfg.t_dec), cfg.n_dec)])
    ang = pos.astype(F32)[:, None] * inv_freq[None, :]
    cos, sin = jnp.cos(ang), jnp.sin(ang)
    reps = LANES // HEAD_DIM
    cos_t = jnp.tile(jnp.concatenate([cos, cos], axis=1), (1, reps))
    sin_t = jnp.tile(jnp.concatenate([-sin, sin], axis=1), (1, reps))
    return cos_t, sin_t


def _pad_rows_top(a, rows):
    return jnp.pad(a, ((0, 0), (0, 0), (rows - a.shape[2], 0), (0, 0)))


def _forward(cfg, x_prompt, x_sample, cache_k, cache_v, state_pool, state_sconv, state_cconv,
             g_ffn1, w1_gate, w1_up, w1_down, g_mix, w_in, pool_w, pool_scale, sconv_w,
             cconv_w, cconv_b, ln_g, ln_b, q_norm_g, k_norm_g, lam_q1, lam_k1, lam_q2, lam_k2,
             subln_g, wp_a, wp_b, wp_c, wp_d, w_out, g_ffn2, w2_gate, w2_up, w2_down):
    depth = w_in.shape[0]
    sp, nb, td = cfg.s_prompt, cfg.n_dec, cfg.t_dec
    x = jnp.concatenate([x_prompt.reshape(sp, D_MODEL), x_sample.reshape(nb * td, D_MODEL)], axis=0)

    bf = lambda w: w.astype(BF16)
    w1g, w1u, w1d = bf(w1_gate), bf(w1_up), bf(w1_down)
    w2g, w2u, w2d = bf(w2_gate), bf(w2_up), bf(w2_down)
    w_proj, w_gate = bf(w_in[:, :, :W_PROJ]), bf(w_in[:, :, W_PROJ:])
    wpa, wpb, wpc, wpd, wo = bf(wp_a), bf(wp_b), bf(wp_c), bf(wp_d), bf(w_out)
    cos_t, sin_t = _rope_tables(cfg)
    ones_bd = jnp.kron(jnp.eye(LANES // HEAD_DIM, dtype=F32), jnp.ones((HEAD_DIM, HEAD_DIM), F32)).astype(BF16)
    eye_g = jnp.eye(len(POOL_WINDOWS), dtype=F32)
    ck = cache_k.reshape(depth, nb, cfg.past * 2 * N_HEADS, HEAD_DIM)
    cv = cache_v.reshape(depth, nb, cfg.past * N_HEADS, 2 * HEAD_DIM)
    st_pool = _pad_rows_top(state_pool, HALO)
    st_sconv = _pad_rows_top(state_sconv, HALO)
    st_cconv = _pad_rows_top(state_cconv, HALO)
    row = lambda a: a.reshape(1, -1)

    outs = [[] for _ in range(10)]
    for l in range(depth):
        lam_init = 0.8 - 0.6 * math.exp(-0.3 * l)
        x = _ffn(cfg, l, x, row(g_ffn1[l]), w1g, w1u, w1d)
        qg = row(jnp.tile(q_norm_g[l], LANES // HEAD_DIM))
        kg = row(jnp.tile(k_norm_g[l], LANES // HEAD_DIM))
        zabc, qb, k, kb, v, vb, qt, vt = _inproj(cfg, l, x, row(g_mix[l]), w_proj, qg, kg, cos_t, sin_t, ones_bd)

        pw_bd = (eye_g[:, None, :, None] * pool_w[l][:, :, None, :]).reshape(W_A, W_A).astype(BF16)
        mix_w = (pw_bd, row(pool_scale[l]), sconv_w[l], cconv_w[l], row(cconv_b[l]), row(ln_g[l]), row(ln_b[l]))
        y_abc, pool_p, sconv_p, cconv_p = _mixer_prompt(cfg, zabc, mix_w)
        y_abc, pool_s, sconv_s, cconv_s = _mixer_sample(cfg, zabc, st_pool[l], st_sconv[l], st_cconv[l],
                                                        mix_w, y_abc)

        lam_w = (row(lam_q1[l]), row(lam_k1[l]), row(lam_q2[l]), row(lam_k2[l]), row(subln_g[l]))
        yd = _attn_prompt(cfg, qt, kb, vt, lam_w, lam_init)
        yd = _attn_sample(cfg, l, qb, kb, vb, ck, cv, lam_w, lam_init, yd)

        x = _merge(cfg, l, x, y_abc, yd, row(g_mix[l]), w_gate, wpa, wpb, wpc, wpd, wo)
        x = _ffn(cfg, l, x, row(g_ffn2[l]), w2g, w2u, w2d)

        outs[0].append(k[:sp].reshape(1, sp, 2 * N_HEADS, HEAD_DIM))
        outs[1].append(v[:sp].reshape(1, sp, N_HEADS, 2 * HEAD_DIM))
        outs[2].append(pool_p[None, 16 - POOL_STATE:])
        outs[3].append(sconv_p[None, 8 - (SCONV_K - 1):])
        outs[4].append(cconv_p[None, 32 - (CCONV_K - 1):])
        outs[5].append(k[sp:].reshape(nb, td, 2 * N_HEADS, HEAD_DIM))
        outs[6].append(v[sp:].reshape(nb, td, N_HEADS, 2 * HEAD_DIM))
        outs[7].append(pool_s[:, 16 - POOL_STATE:])
        outs[8].append(sconv_s[:, 8 - (SCONV_K - 1):])
        outs[9].append(cconv_s[:, 32 - (CCONV_K - 1):])

    y_prompt = x[:sp].reshape(1, sp, D_MODEL)
    y_sample = x[sp:].reshape(nb, td, D_MODEL)
    return (y_prompt, y_sample) + tuple(jnp.stack(o) for o in outs)


def kernel(x_prompt, x_sample, cache_k, cache_v, state_pool, state_sconv, state_cconv, g_ffn1, w1_gate, w1_up, w1_down, g_mix, w_in, pool_w, pool_scale, sconv_w, cconv_w, cconv_b, ln_g, ln_b, q_norm_g, k_norm_g, lam_q1, lam_k1, lam_q2, lam_k2, subln_g, wp_a, wp_b, wp_c, wp_d, w_out, g_ffn2, w2_gate, w2_up, w2_down):
    assert x_prompt.shape[0] == 1
    cfg = Cfg(s_prompt=x_prompt.shape[1], n_dec=x_sample.shape[0], t_dec=x_sample.shape[1],
              past=cache_k.shape[2], tm=512, tq=512, tmix=512)
    return _forward(cfg, x_prompt, x_sample, cache_k, cache_v, state_pool, state_sconv, state_cconv,
                    g_ffn1, w1_gate, w1_up, w1_down, g_mix, w_in, pool_w, pool_scale, sconv_w,
                    cconv_w, cconv_b, ln_g, ln_b, q_norm_g, k_norm_g, lam_q1, lam_k1, lam_q2, lam_k2,
                    subln_g, wp_a, wp_b, wp_c, wp_d, w_out, g_ffn2, w2_gate, w2_up, w2_down)
```

```python
import functools
import math
from typing import NamedTuple

import jax
import jax.numpy as jnp
from jax import lax
from jax.experimental import pallas as pl
from jax.experimental.pallas import tpu as pltpu

F32 = jnp.float32
BF16 = jnp.bfloat16

D_MODEL = 1024
DEPTH = 4
CHUNK = 64
POOL_WINDOWS = (2, 4, 8, 16)
W_A = 256
GA = 64
POOL_STATE = 15
W_B = 256
SCONV_K = 3
W_C = 256
CCONV_K = 31
HEAD_DIM = 64
N_HEADS = 4
W_D = 512
ROPE_THETA = 10000.0
D_FF = 2816
EPS = 1e-6
W_ABC = W_A + 3 * W_B + 2 * W_C
W_PROJ = W_ABC + 3 * W_D
W_Y = W_A + W_B + W_C

LANES = 128
VT_ROWS = LANES + 16
LOG2E = 1.4426950408889634
HALO = 32
MIB = 1024 * 1024


class Cfg(NamedTuple):
    s_prompt: int
    n_dec: int
    t_dec: int
    past: int
    tm: int
    tq: int
    tmix: int

    @property
    def n_tok(self):
        return self.s_prompt + self.n_dec * self.t_dec


def _const_spec(shape, index):
    return pl.BlockSpec(shape, lambda *_: index, pipeline_mode=pl.Buffered(1))


def _rms(x, g):
    ms = jnp.mean(x * x, axis=-1, keepdims=True)
    return x * lax.rsqrt(ms + EPS) * g


def _dot(a, b):
    return jnp.dot(a, b, preferred_element_type=F32)


def _dot_t(a, b):
    return lax.dot_general(a, b, (((1,), (1,)), ((), ())), preferred_element_type=F32)


def _ffn_kernel(x_ref, g_ref, wg_ref, wu_ref, wd_ref, o_ref):
    x = x_ref[...]
    h = _rms(x, g_ref[...]).astype(BF16)
    a = _dot(h, wg_ref[...])
    u = _dot(h, wu_ref[...])
    act = (a * jax.nn.sigmoid(a) * u).astype(BF16)
    o_ref[...] = x + 0.5 * _dot(act, wd_ref[...])


def _ffn(cfg, l, x, g, wg, wu, wd):
    n, tm = cfg.n_tok, cfg.tm
    return pl.pallas_call(
        _ffn_kernel,
        grid=(n // tm,),
        in_specs=[
            pl.BlockSpec((tm, D_MODEL), lambda i: (i, 0)),
            _const_spec((1, D_MODEL), (0, 0)),
            _const_spec((None, D_MODEL, D_FF), (l, 0, 0)),
            _const_spec((None, D_MODEL, D_FF), (l, 0, 0)),
            _const_spec((None, D_FF, D_MODEL), (l, 0, 0)),
        ],
        out_specs=pl.BlockSpec((tm, D_MODEL), lambda i: (i, 0)),
        out_shape=jax.ShapeDtypeStruct((n, D_MODEL), F32),
        compiler_params=pltpu.CompilerParams(
            dimension_semantics=("arbitrary",), vmem_limit_bytes=52 * MIB),
        name="ffn",
    )(x, g, wg, wu, wd)


def _inproj_kernel(x_ref, g_ref, w_ref, qg_ref, kg_ref, cos_ref, sin_ref, ones_ref, *rest, n_prompt_tiles):
    zabc_ref, qb_ref, kb_ref, vb_ref, qt_ref, vt_ref, kt_ref, ks_ref, vp_ref, vs_ref = rest[-10:]
    is_prompt = pl.program_id(0) < n_prompt_tiles
    h = _rms(x_ref[...], g_ref[...]).astype(BF16)
    zabc_ref[...] = _dot(h, w_ref[:, 0:W_ABC])
    cos = cos_ref[...]
    sin = sin_ref[...]
    ones = ones_ref[...]
    lane = lax.broadcasted_iota(jnp.int32, cos.shape, 1)
    first_half = (lane & (HEAD_DIM // 2)) == 0

    def norm_rope(z, g):
        ss = z * z
        hi = ss.astype(BF16)
        lo = (ss - hi.astype(F32)).astype(BF16)
        tot = _dot(hi, ones) + _dot(lo, ones)
        y = z * lax.rsqrt(tot * (1.0 / HEAD_DIM) + EPS) * g
        half = HEAD_DIM // 2
        partner = jnp.where(first_half, pltpu.roll(y, LANES - half, 1), pltpu.roll(y, half, 1))
        return y * cos + partner * sin

    for c in range(W_D // LANES):
        sl = slice(c * LANES, (c + 1) * LANES)
        zq = _dot(h, w_ref[:, W_ABC + c * LANES: W_ABC + (c + 1) * LANES])
        q = norm_rope(zq, qg_ref[...]) * (HEAD_DIM ** -0.5)
        qb_ref[:, sl] = q.astype(BF16)
        qt_ref[c] = (q * LOG2E).T.astype(BF16)
        zk = _dot(h, w_ref[:, W_ABC + W_D + c * LANES: W_ABC + W_D + (c + 1) * LANES])
        k = norm_rope(zk, kg_ref[...])
        kb_ref[:, sl] = k.astype(BF16)
        v = _dot(h, w_ref[:, W_ABC + 2 * W_D + c * LANES: W_ABC + 2 * W_D + (c + 1) * LANES])
        vb_ref[:, sl] = v.astype(BF16)
        vt_ref[c, 0:LANES, :] = v.T.astype(BF16)
        vt_ref[c, LANES:VT_ROWS, :] = jnp.ones((VT_ROWS - LANES, v.shape[0]), BF16)

        @pl.when(is_prompt)
        def _():
            kt = k.T
            kt_ref[2 * c] = kt[0:HEAD_DIM]
            kt_ref[2 * c + 1] = kt[HEAD_DIM:LANES]
            vp_ref[:, sl] = v

        @pl.when(jnp.logical_not(is_prompt))
        def _():
            ks_ref[:, sl] = k
            vs_ref[:, sl] = v


def _inproj(cfg, l, depth, x, g, w_in, qg, kg, cos_t, sin_t, ones_bd, caches):
    n, tm, sp = cfg.n_tok, cfg.tm, cfg.s_prompt
    n_dec = n - sp
    assert sp % tm == 0 and n_dec % tm == 0
    npt = sp // tm
    row = lambda w: pl.BlockSpec((tm, w), lambda i: (i, 0))
    prompt_rows = pl.BlockSpec((None, tm, W_D), lambda i: (l, jnp.minimum(i, npt - 1), 0))
    sample_rows = pl.BlockSpec((None, tm, W_D), lambda i: (l, jnp.maximum(i - npt, 0), 0))
    n_in = 8
    aliased = [] if caches is None else list(caches)
    return pl.pallas_call(
        functools.partial(_inproj_kernel, n_prompt_tiles=npt),
        grid=(n // tm,),
        in_specs=[
            row(D_MODEL),
            _const_spec((1, D_MODEL), (0, 0)),
            _const_spec((None, D_MODEL, W_PROJ), (l, 0, 0)),
            _const_spec((1, LANES), (0, 0)),
            _const_spec((1, LANES), (0, 0)),
            row(LANES),
            row(LANES),
            _const_spec((LANES, LANES), (0, 0)),
        ] + [pl.BlockSpec(memory_space=pl.ANY) for _ in aliased],
        out_specs=[row(W_ABC), row(W_D), row(W_D), row(W_D),
                   pl.BlockSpec((N_HEADS, None, LANES, tm), lambda i: (0, i, 0, 0)),
                   pl.BlockSpec((N_HEADS, None, VT_ROWS, tm), lambda i: (0, i, 0, 0)),
                   pl.BlockSpec((None, 2 * N_HEADS, HEAD_DIM, tm), lambda i: (l, 0, 0, jnp.minimum(i, npt - 1))),
                   sample_rows, prompt_rows, sample_rows],
        out_shape=[
            jax.ShapeDtypeStruct((n, W_ABC), F32),
            jax.ShapeDtypeStruct((n, W_D), BF16),
            jax.ShapeDtypeStruct((n, W_D), BF16),
            jax.ShapeDtypeStruct((n, W_D), BF16),
            jax.ShapeDtypeStruct((N_HEADS, n // tm, LANES, tm), BF16),
            jax.ShapeDtypeStruct((N_HEADS, n // tm, VT_ROWS, tm), BF16),
            jax.ShapeDtypeStruct((depth, 2 * N_HEADS, HEAD_DIM, sp), F32),
            jax.ShapeDtypeStruct((depth, n_dec, W_D), F32),
            jax.ShapeDtypeStruct((depth, sp, W_D), F32),
            jax.ShapeDtypeStruct((depth, n_dec, W_D), F32),
        ],
        input_output_aliases={n_in + j: 6 + j for j in range(len(aliased))},
        compiler_params=pltpu.CompilerParams(
            dimension_semantics=("arbitrary",), vmem_limit_bytes=44 * MIB),
        name="inproj",
    )(x, g, w_in, qg, kg, cos_t, sin_t, ones_bd, *aliased)


MIX_ROWS = 128


def _mixer_compute(t, pos0, z_ref, pw_ref, ps_ref, sw_ref, cw_ref, cb_ref, lg_ref, lb_ref,
                   y_ref, pool_o, sconv_o, cconv_o, ea, eb, ec):
    u = z_ref[:, 0:W_A]
    ea[HALO:HALO + t, :] = u
    eb[HALO:HALO + t, :] = z_ref[:, W_A + W_B:W_A + 2 * W_B] * z_ref[:, W_A + 2 * W_B:W_A + 3 * W_B]
    zc = z_ref[:, W_A + 3 * W_B:W_A + 3 * W_B + W_C]
    ec[HALO:HALO + t, :] = zc * jax.nn.sigmoid(z_ref[:, W_A + 3 * W_B + W_C:W_ABC])

    rows = min(t, MIX_ROWS)
    for r0 in range(0, t, rows):
        base = HALO + r0
        lane = lax.broadcasted_iota(jnp.int32, (rows, W_A), 1)
        cur = ea[base:base + rows, :]
        acc = cur
        sums = {}
        for j in range(1, max(POOL_WINDOWS)):
            acc = acc + ea[base - j:base - j + rows, :]
            if j + 1 in POOL_WINDOWS:
                sums[j + 1] = acc
        tot = sums[POOL_WINDOWS[-1]]
        win = jnp.full((rows, W_A), float(POOL_WINDOWS[-1]), F32)
        for gi in range(len(POOL_WINDOWS) - 2, -1, -1):
            in_group = lane < (gi + 1) * GA
            tot = jnp.where(in_group, sums[POOL_WINDOWS[gi]], tot)
            win = jnp.where(in_group, float(POOL_WINDOWS[gi]), win)
        if pos0 is None:
            cnt = win
        else:
            pos1 = (pos0 + r0 + 1 + lax.broadcasted_iota(jnp.int32, (rows, W_A), 0)).astype(F32)
            cnt = jnp.minimum(pos1, win)
        d = (tot / cnt - cur).astype(BF16)
        y_ref[r0:r0 + rows, 0:W_A] = _dot(d, pw_ref[...]) * ps_ref[...]

        conv = sw_ref[SCONV_K - 1:SCONV_K, :] * eb[base:base + rows, :]
        for j in range(SCONV_K - 1):
            off = base - (SCONV_K - 1) + j
            conv = conv + sw_ref[j:j + 1, :] * eb[off:off + rows, :]
        y_ref[r0:r0 + rows, W_A:W_A + W_B] = z_ref[r0:r0 + rows, W_A:W_A + W_B] * conv

        conv = cw_ref[CCONV_K - 1:CCONV_K, :] * ec[base:base + rows, :]
        for j in range(CCONV_K - 1):
            off = base - (CCONV_K - 1) + j
            conv = conv + cw_ref[j:j + 1, :] * ec[off:off + rows, :]
        conv = conv + cb_ref[...]
        mu = jnp.mean(conv, axis=-1, keepdims=True)
        cen = conv - mu
        var = jnp.mean(cen * cen, axis=-1, keepdims=True)
        ln = cen * lax.rsqrt(var + EPS) * lg_ref[...] + lb_ref[...]
        y_ref[r0:r0 + rows, W_A + W_B:W_Y] = ln * jax.nn.sigmoid(ln)

    pool_o[...] = ea[HALO + t - 16:HALO + t, :]
    sconv_o[...] = eb[HALO + t - 8:HALO + t, :]
    cconv_o[...] = ec[HALO + t - 32:HALO + t, :]


def _mixer_prompt_kernel(z_ref, pw_ref, ps_ref, sw_ref, cw_ref, cb_ref, lg_ref, lb_ref,
                         y_ref, pool_o, sconv_o, cconv_o, ea, eb, ec, *, t):
    i = pl.program_id(0)

    @pl.when(i == 0)
    def _():
        zeros = jnp.zeros((HALO, W_A), F32)
        ea[0:HALO, :] = zeros
        eb[0:HALO, :] = zeros
        ec[0:HALO, :] = zeros

    @pl.when(i > 0)
    def _():
        ea[0:HALO, :] = ea[t:t + HALO, :]
        eb[0:HALO, :] = eb[t:t + HALO, :]
        ec[0:HALO, :] = ec[t:t + HALO, :]

    _mixer_compute(t, i * t, z_ref, pw_ref, ps_ref, sw_ref, cw_ref, cb_ref, lg_ref, lb_ref,
                   y_ref, pool_o, sconv_o, cconv_o, ea, eb, ec)


def _mixer_sample_kernel(z_ref, sp_ref, ss_ref, sc_ref, pw_ref, ps_ref, sw_ref, cw_ref, cb_ref,
                         lg_ref, lb_ref, y_in_ref, y_ref, pool_o, sconv_o, cconv_o, ea, eb, ec, *, t):
    del y_in_ref
    ea[0:HALO, :] = sp_ref[...]
    eb[0:HALO, :] = ss_ref[...]
    ec[0:HALO, :] = sc_ref[...]
    _mixer_compute(t, None, z_ref, pw_ref, ps_ref, sw_ref, cw_ref, cb_ref, lg_ref, lb_ref,
                   y_ref, pool_o, sconv_o, cconv_o, ea, eb, ec)


def _mixer_weight_specs():
    return [
        _const_spec((W_A, W_A), (0, 0)),
        _const_spec((1, W_A), (0, 0)),
        _const_spec((SCONV_K, W_B), (0, 0)),
        _const_spec((CCONV_K, W_C), (0, 0)),
        _const_spec((1, W_C), (0, 0)),
        _const_spec((1, W_C), (0, 0)),
        _const_spec((1, W_C), (0, 0)),
    ]


def _mixer_scratch(t):
    return [pltpu.VMEM((HALO + t, W_A), F32), pltpu.VMEM((HALO + t, W_B), F32),
            pltpu.VMEM((HALO + t, W_C), F32)]


def _mixer_prompt(cfg, zabc, weights):
    t = cfg.tmix
    const_out = lambda r: pl.BlockSpec((r, W_A), lambda i: (0, 0))
    return pl.pallas_call(
        functools.partial(_mixer_prompt_kernel, t=t),
        grid=(cfg.s_prompt // t,),
        in_specs=[pl.BlockSpec((t, W_ABC), lambda i: (i, 0))] + _mixer_weight_specs(),
        out_specs=[pl.BlockSpec((t, W_Y), lambda i: (i, 0)), const_out(16), const_out(8), const_out(32)],
        out_shape=[
            jax.ShapeDtypeStruct((cfg.n_tok, W_Y), F32),
            jax.ShapeDtypeStruct((16, W_A), F32),
            jax.ShapeDtypeStruct((8, W_B), F32),
            jax.ShapeDtypeStruct((32, W_C), F32),
        ],
        scratch_shapes=_mixer_scratch(t),
        compiler_params=pltpu.CompilerParams(dimension_semantics=("arbitrary",)),
        name="mixer_prompt",
    )(zabc, *weights)


def _mixer_sample(cfg, zabc, st_pool, st_sconv, st_cconv, weights, y_abc):
    t, nb = cfg.t_dec, cfg.n_dec
    row0 = cfg.s_prompt // t
    state_spec = lambda: pl.BlockSpec((None, HALO, W_A), lambda b: (b, 0, 0))
    out_state = lambda r: pl.BlockSpec((None, r, W_A), lambda b: (b, 0, 0))
    return pl.pallas_call(
        functools.partial(_mixer_sample_kernel, t=t),
        grid=(nb,),
        in_specs=[pl.BlockSpec((t, W_ABC), lambda b: (row0 + b, 0)),
                  state_spec(), state_spec(), state_spec()]
                 + _mixer_weight_specs()
                 + [pl.BlockSpec(memory_space=pl.ANY)],
        out_specs=[pl.BlockSpec((t, W_Y), lambda b: (row0 + b, 0)),
                   out_state(16), out_state(8), out_state(32)],
        out_shape=[
            jax.ShapeDtypeStruct((cfg.n_tok, W_Y), F32),
            jax.ShapeDtypeStruct((nb, 16, W_A), F32),
            jax.ShapeDtypeStruct((nb, 8, W_B), F32),
            jax.ShapeDtypeStruct((nb, 32, W_C), F32),
        ],
        scratch_shapes=_mixer_scratch(t),
        input_output_aliases={11: 0},
        compiler_params=pltpu.CompilerParams(dimension_semantics=("arbitrary",)),
        name="mixer_sample",
    )(zabc, st_pool, st_sconv, st_cconv, *weights, y_abc)


def _lambda(lq1, lk1, lq2, lk2, lam_init):
    s1 = jnp.sum(lq1[...] * lk1[...], axis=-1, keepdims=True)
    s2 = jnp.sum(lq2[...] * lk2[...], axis=-1, keepdims=True)
    return jnp.exp(s1) - jnp.exp(s2) + lam_init


def _stack_maps(q):
    lane = lax.broadcasted_iota(jnp.int32, q.shape, 1)
    zero = jnp.zeros_like(q)
    return jnp.concatenate([jnp.where(lane < HEAD_DIM, q, zero), jnp.where(lane >= HEAD_DIM, q, zero)],
                           axis=0)


def _diff_out(acc, l, lam, sg, lam_init, t):
    o = acc[0:t] / l[0:t] - lam * (acc[t:2 * t] / l[t:2 * t])
    return _rms(o, sg) * (1.0 - lam_init)


def _attn_prompt_kernel(qt_ref, k_ref, vt_ref, lq1, lk1, lq2, lk2, sg_ref, o_ref,
                        q2t_ref, s0_ref, s1_ref, m_ref, acc_ref, *, tq, lam_init):
    qi = pl.program_id(1)
    qt = qt_ref[...]
    chan = lax.broadcasted_iota(jnp.int32, qt.shape, 0)
    zero = jnp.zeros_like(qt)
    q2t_ref[:, 0:tq] = jnp.where(chan < HEAD_DIM, qt, zero)
    q2t_ref[:, tq:2 * tq] = jnp.where(chan >= HEAD_DIM, qt, zero)
    m_ref[...] = jnp.full(m_ref.shape, -jnp.inf, F32)
    acc_ref[...] = jnp.zeros(acc_ref.shape, F32)

    def scores(j, s_ref):
        kb = k_ref[pl.ds(pl.multiple_of(j * tq, tq), tq), :]
        s_ref[...] = _dot(kb, q2t_ref[...])

    def consume(j, s_ref, diagonal):
        s = s_ref[...]
        if diagonal:
            key = lax.broadcasted_iota(jnp.int32, s.shape, 0)
            col = lax.broadcasted_iota(jnp.int32, s.shape, 1)
            s = jnp.where(key // CHUNK <= (col & (tq - 1)) // CHUNK, s, -jnp.inf)
        m_prev = m_ref[...]
        m_new = jnp.maximum(m_prev, jnp.max(s, axis=0, keepdims=True))
        alpha = jnp.exp2(m_prev - m_new)
        p = jnp.exp2(s - m_new).astype(BF16)
        acc_ref[...] = alpha * acc_ref[...] + _dot(vt_ref[j], p)
        m_ref[...] = m_new

    scores(0, s0_ref)

    def pair(i, carry):
        j = 2 * i
        scores(j + 1, s1_ref)
        consume(j, s0_ref, False)
        scores(j + 2, s0_ref)
        consume(j + 1, s1_ref, False)
        return carry

    lax.fori_loop(0, qi // 2, pair, 0)
    odd = (qi & 1) == 1

    @pl.when(odd)
    def _():
        scores(qi, s1_ref)
        consume(qi - 1, s0_ref, False)
        consume(qi, s1_ref, True)

    @pl.when(jnp.logical_not(odd))
    def _():
        consume(qi, s0_ref, True)

    lam = _lambda(lq1, lk1, lq2, lk2, lam_init)
    acc = acc_ref[0:LANES, :]
    l = acc_ref[LANES:LANES + 1, :]
    ot = acc[:, 0:tq] / l[:, 0:tq] - lam * (acc[:, tq:2 * tq] / l[:, tq:2 * tq])
    o_ref[...] = _rms(ot.T, sg_ref[...]) * (1.0 - lam_init)


def _lam_specs():
    return [_const_spec((1, HEAD_DIM), (0, 0)) for _ in range(4)] + [_const_spec((1, LANES), (0, 0))]


def _attn_prompt(cfg, qt, kb, vt, lam_w, lam_init):
    s, tq = cfg.s_prompt, cfg.tq
    assert tq & (tq - 1) == 0 and tq % CHUNK == 0 and tq == cfg.tm
    return pl.pallas_call(
        functools.partial(_attn_prompt_kernel, tq=tq, lam_init=lam_init),
        grid=(N_HEADS, s // tq),
        in_specs=[
            pl.BlockSpec((None, None, LANES, tq), lambda h, i: (h, i, 0, 0)),
            pl.BlockSpec((s, LANES), lambda h, i: (0, h)),
            pl.BlockSpec((None, s // tq, VT_ROWS, tq), lambda h, i: (h, 0, 0, 0)),
        ] + _lam_specs(),
        out_specs=pl.BlockSpec((tq, LANES), lambda h, i: (i, h)),
        out_shape=jax.ShapeDtypeStruct((cfg.n_tok, W_D), F32),
        scratch_shapes=[
            pltpu.VMEM((LANES, 2 * tq), BF16),
            pltpu.VMEM((tq, 2 * tq), F32),
            pltpu.VMEM((tq, 2 * tq), F32),
            pltpu.VMEM((1, 2 * tq), F32),
            pltpu.VMEM((VT_ROWS, 2 * tq), F32),
        ],
        compiler_params=pltpu.CompilerParams(
            dimension_semantics=("arbitrary", "arbitrary"), vmem_limit_bytes=48 * MIB),
        name="attn_prompt",
    )(qt, kb, vt, *lam_w)


def _attn_sample_kernel(q_ref, kn_ref, vn_ref, kc_ref, vc_ref, lq1, lk1, lq2, lk2, sg_ref, yd_in_ref,
                        o_ref, *, t, lam_init):
    del yd_in_ref
    lam = _lambda(lq1, lk1, lq2, lk2, lam_init)
    past = vc_ref.shape[0] // N_HEADS
    for h in range(N_HEADS):
        sl = slice(h * LANES, (h + 1) * LANES)
        s_past, s_new = [], []
        for c in range(2):
            ch = slice(h * LANES + c * HEAD_DIM, h * LANES + (c + 1) * HEAD_DIM)
            q = q_ref[:, ch]
            s_past.append(_dot(q, kc_ref[2 * h + c].astype(BF16)))
            s_new.append(_dot_t(q, kn_ref[:, ch]))
        s_past = jnp.concatenate(s_past, axis=0)
        s_new = jnp.concatenate(s_new, axis=0)
        v_past = vc_ref[pl.ds(h, past, stride=N_HEADS), :]
        m = jnp.maximum(jnp.max(s_past, axis=-1, keepdims=True), jnp.max(s_new, axis=-1, keepdims=True))
        p_past = jnp.exp(s_past - m)
        p_new = jnp.exp(s_new - m)
        l = jnp.sum(p_past, axis=-1, keepdims=True) + jnp.sum(p_new, axis=-1, keepdims=True)
        acc = _dot(p_past.astype(BF16), v_past.astype(BF16)) + _dot(p_new.astype(BF16), vn_ref[:, sl])
        o_ref[:, sl] = _diff_out(acc, l, lam, sg_ref[...], lam_init, t)


def _attn_sample(cfg, l, qb, kb, vb, cache_k, cache_v, lam_w, lam_init, yd):
    t, nb = cfg.t_dec, cfg.n_dec
    row0 = cfg.s_prompt // t
    new_rows = lambda: pl.BlockSpec((t, W_D), lambda b: (row0 + b, 0))
    cache = lambda a: pl.BlockSpec((None, None) + a.shape[2:], lambda b: (l, b) + (0,) * (a.ndim - 2))
    return pl.pallas_call(
        functools.partial(_attn_sample_kernel, t=t, lam_init=lam_init),
        grid=(nb,),
        in_specs=[new_rows(), new_rows(), new_rows(), cache(cache_k), cache(cache_v)] + _lam_specs()
                 + [pl.BlockSpec(memory_space=pl.ANY)],
        out_specs=new_rows(),
        out_shape=jax.ShapeDtypeStruct((cfg.n_tok, W_D), F32),
        input_output_aliases={10: 0},
        compiler_params=pltpu.CompilerParams(
            dimension_semantics=("arbitrary",), vmem_limit_bytes=40 * MIB),
        name="attn_sample",
    )(qb, kb, vb, cache_k, cache_v, *lam_w, yd)


def _merge_kernel(x_ref, yabc_ref, yd_ref, g_ref, wg_ref, wpa_ref, wpb_ref, wpc_ref, wpd_ref, wo_ref, o_ref):
    x = x_ref[...]
    h = _rms(x, g_ref[...]).astype(BF16)
    branches = (
        (yabc_ref[:, 0:W_A], wpa_ref),
        (yabc_ref[:, W_A:W_A + W_B], wpb_ref),
        (yabc_ref[:, W_A + W_B:W_Y], wpc_ref),
        (yd_ref[...], wpd_ref),
    )
    merged = None
    for i, (y, wp_ref) in enumerate(branches):
        gate = jax.nn.sigmoid(_dot(h, wg_ref[:, i * D_MODEL:(i + 1) * D_MODEL]))
        term = gate * _dot(y.astype(BF16), wp_ref[...])
        merged = term if merged is None else merged + term
    o_ref[...] = x + _dot(merged.astype(BF16), wo_ref[...])


def _merge(cfg, l, x, y_abc, yd, g, w_gate, wpa, wpb, wpc, wpd, wo):
    n, tm = cfg.n_tok, cfg.tm
    row = lambda w: pl.BlockSpec((tm, w), lambda i: (i, 0))
    return pl.pallas_call(
        _merge_kernel,
        grid=(n // tm,),
        in_specs=[
            row(D_MODEL), row(W_Y), row(W_D),
            _const_spec((1, D_MODEL), (0, 0)),
            _const_spec((None, D_MODEL, 4 * D_MODEL), (l, 0, 0)),
            _const_spec((None, W_A, D_MODEL), (l, 0, 0)),
            _const_spec((None, W_B, D_MODEL), (l, 0, 0)),
            _const_spec((None, W_C, D_MODEL), (l, 0, 0)),
            _const_spec((None, W_D, D_MODEL), (l, 0, 0)),
            _const_spec((None, D_MODEL, D_MODEL), (l, 0, 0)),
        ],
        out_specs=row(D_MODEL),
        out_shape=jax.ShapeDtypeStruct((n, D_MODEL), F32),
        compiler_params=pltpu.CompilerParams(
            dimension_semantics=("arbitrary",), vmem_limit_bytes=48 * MIB),
        name="merge",
    )(x, y_abc, yd, g, w_gate, wpa, wpb, wpc, wpd, wo)


def _rope_tables(cfg):
    half = HEAD_DIM // 2
    inv_freq = ROPE_THETA ** (-jnp.arange(half, dtype=F32) / half)
    pos = jnp.concatenate([jnp.arange(cfg.s_prompt), jnp.tile(cfg.past + jnp.arange(cfg.t_dec), cfg.n_dec)])
    ang = pos.astype(F32)[:, None] * inv_freq[None, :]
    cos, sin = jnp.cos(ang), jnp.sin(ang)
    reps = LANES // HEAD_DIM
    cos_t = jnp.tile(jnp.concatenate([cos, cos], axis=1), (1, reps))
    sin_t = jnp.tile(jnp.concatenate([-sin, sin], axis=1), (1, reps))
    return cos_t, sin_t


def _pad_rows_top(a, rows):
    return jnp.pad(a, ((0, 0), (0, 0), (rows - a.shape[2], 0), (0, 0)))


def _forward(cfg, x_prompt, x_sample, cache_k, cache_v, state_pool, state_sconv, state_cconv,
             g_ffn1, w1_gate, w1_up, w1_down, g_mix, w_in, pool_w, pool_scale, sconv_w,
             cconv_w, cconv_b, ln_g, ln_b, q_norm_g, k_norm_g, lam_q1, lam_k1, lam_q2, lam_k2,
             subln_g, wp_a, wp_b, wp_c, wp_d, w_out, g_ffn2, w2_gate, w2_up, w2_down):
    depth = w_in.shape[0]
    sp, nb, td = cfg.s_prompt, cfg.n_dec, cfg.t_dec
    x = jnp.concatenate([x_prompt.reshape(sp, D_MODEL), x_sample.reshape(nb * td, D_MODEL)], axis=0)

    bf = lambda w: w.astype(BF16)
    w1g, w1u, w1d = bf(w1_gate), bf(w1_up), bf(w1_down)
    w2g, w2u, w2d = bf(w2_gate), bf(w2_up), bf(w2_down)
    w_proj, w_gate = bf(w_in[:, :, :W_PROJ]), bf(w_in[:, :, W_PROJ:])
    wpa, wpb, wpc, wpd, wo = bf(wp_a), bf(wp_b), bf(wp_c), bf(wp_d), bf(w_out)
    cos_t, sin_t = _rope_tables(cfg)
    ones_bd = jnp.kron(jnp.eye(LANES // HEAD_DIM, dtype=F32), jnp.ones((HEAD_DIM, HEAD_DIM), F32)).astype(BF16)
    eye_g = jnp.eye(len(POOL_WINDOWS), dtype=F32)
    ck = jnp.transpose(cache_k, (0, 1, 3, 4, 2))
    cv = cache_v.reshape(depth, nb, cfg.past * N_HEADS, 2 * HEAD_DIM)
    st_pool = _pad_rows_top(state_pool, HALO)
    st_sconv = _pad_rows_top(state_sconv, HALO)
    st_cconv = _pad_rows_top(state_cconv, HALO)
    row = lambda a: a.reshape(1, -1)

    outs = [[] for _ in range(6)]
    caches = None
    for l in range(depth):
        lam_init = 0.8 - 0.6 * math.exp(-0.3 * l)
        x = _ffn(cfg, l, x, row(g_ffn1[l]), w1g, w1u, w1d)
        qg = row(jnp.tile(q_norm_g[l], LANES // HEAD_DIM))
        kg = row(jnp.tile(k_norm_g[l], LANES // HEAD_DIM))
        zabc, qb, kb, vb, qt, vt, *caches = _inproj(cfg, l, depth, x, row(g_mix[l]), w_proj, qg, kg, cos_t, sin_t,
                                                    ones_bd, caches)

        pw_bd = (eye_g[:, None, :, None] * pool_w[l][:, :, None, :]).reshape(W_A, W_A).astype(BF16)
        mix_w = (pw_bd, row(pool_scale[l]), sconv_w[l], cconv_w[l], row(cconv_b[l]), row(ln_g[l]), row(ln_b[l]))
        y_abc, pool_p, sconv_p, cconv_p = _mixer_prompt(cfg, zabc, mix_w)
        y_abc, pool_s, sconv_s, cconv_s = _mixer_sample(cfg, zabc, st_pool[l], st_sconv[l], st_cconv[l],
                                                        mix_w, y_abc)

        lam_w = (row(lam_q1[l]), row(lam_k1[l]), row(lam_q2[l]), row(lam_k2[l]), row(subln_g[l]))
        yd = _attn_prompt(cfg, qt, kb, vt, lam_w, lam_init)
        yd = _attn_sample(cfg, l, qb, kb, vb, ck, cv, lam_w, lam_init, yd)

        x = _merge(cfg, l, x, y_abc, yd, row(g_mix[l]), w_gate, wpa, wpb, wpc, wpd, wo)
        x = _ffn(cfg, l, x, row(g_ffn2[l]), w2g, w2u, w2d)

        outs[0].append(pool_p[None, 16 - POOL_STATE:])
        outs[1].append(sconv_p[None, 8 - (SCONV_K - 1):])
        outs[2].append(cconv_p[None, 32 - (CCONV_K - 1):])
        outs[3].append(pool_s[:, 16 - POOL_STATE:])
        outs[4].append(sconv_s[:, 8 - (SCONV_K - 1):])
        outs[5].append(cconv_s[:, 32 - (CCONV_K - 1):])

    kt_all, ks_all, vp_all, vs_all = caches
    y_prompt = x[:sp].reshape(1, sp, D_MODEL)
    y_sample = x[sp:].reshape(nb, td, D_MODEL)
    k_prompt = jnp.transpose(kt_all.reshape(depth, 1, 2 * N_HEADS, HEAD_DIM, sp), (0, 1, 4, 2, 3))
    v_prompt = vp_all.reshape(depth, 1, sp, N_HEADS, 2 * HEAD_DIM)
    k_sample = ks_all.reshape(depth, nb, td, 2 * N_HEADS, HEAD_DIM)
    v_sample = vs_all.reshape(depth, nb, td, N_HEADS, 2 * HEAD_DIM)
    st = [jnp.stack(o) for o in outs]
    return (y_prompt, y_sample, k_prompt, v_prompt, st[0], st[1], st[2], k_sample, v_sample, st[3], st[4], st[5])


def kernel(x_prompt, x_sample, cache_k, cache_v, state_pool, state_sconv, state_cconv, g_ffn1, w1_gate, w1_up, w1_down, g_mix, w_in, pool_w, pool_scale, sconv_w, cconv_w, cconv_b, ln_g, ln_b, q_norm_g, k_norm_g, lam_q1, lam_k1, lam_q2, lam_k2, subln_g, wp_a, wp_b, wp_c, wp_d, w_out, g_ffn2, w2_gate, w2_up, w2_down):
    assert x_prompt.shape[0] == 1
    cfg = Cfg(s_prompt=x_prompt.shape[1], n_dec=x_sample.shape[0], t_dec=x_sample.shape[1],
              past=cache_k.shape[2], tm=512, tq=512, tmix=512)
    return _forward(cfg, x_prompt, x_sample, cache_k, cache_v, state_pool, state_sconv, state_cconv,
                    g_ffn1, w1_gate, w1_up, w1_down, g_mix, w_in, pool_w, pool_scale, sconv_w,
                    cconv_w, cconv_b, ln_g, ln_b, q_norm_g, k_norm_g, lam_q1, lam_k1, lam_q2, lam_k2,
                    subln_g, wp_a, wp_b, wp_c, wp_d, w_out, g_ffn2, w2_gate, w2_up, w2_down)
```

```python
import functools
import math
from typing import NamedTuple

import jax
import jax.numpy as jnp
from jax import lax
from jax.experimental import pallas as pl
from jax.experimental.pallas import tpu as pltpu

F32 = jnp.float32
BF16 = jnp.bfloat16

D_MODEL = 1024
DEPTH = 4
CHUNK = 64
POOL_WINDOWS = (2, 4, 8, 16)
W_A = 256
GA = 64
POOL_STATE = 15
W_B = 256
SCONV_K = 3
W_C = 256
CCONV_K = 31
HEAD_DIM = 64
N_HEADS = 4
W_D = 512
ROPE_THETA = 10000.0
D_FF = 2816
EPS = 1e-6
W_ABC = W_A + 3 * W_B + 2 * W_C
W_PROJ = W_ABC + 3 * W_D
W_Y = W_A + W_B + W_C

LANES = 128
NORM_W = 256
VT_ROWS = LANES + 16
LOG2E = 1.4426950408889634
HALO = 32
MIB = 1024 * 1024


class Cfg(NamedTuple):
    s_prompt: int
    n_dec: int
    t_dec: int
    past: int
    tm: int
    tmix: int

    @property
    def n_tok(self):
        return self.s_prompt + self.n_dec * self.t_dec


def _const_spec(shape, index):
    return pl.BlockSpec(shape, lambda *_: index, pipeline_mode=pl.Buffered(1))


def _rms(x, g):
    ms = jnp.mean(x * x, axis=-1, keepdims=True)
    return x * lax.rsqrt(ms + EPS) * g


def _dot(a, b):
    return jnp.dot(a, b, preferred_element_type=F32)


def _dot_t(a, b):
    return lax.dot_general(a, b, (((1,), (1,)), ((), ())), preferred_element_type=F32)


def _ffn_kernel(x_ref, g_ref, wg_ref, wu_ref, wd_ref, o_ref):
    x = x_ref[...]
    h = _rms(x, g_ref[...]).astype(BF16)
    a = _dot(h, wg_ref[...])
    u = _dot(h, wu_ref[...])
    act = (a * jax.nn.sigmoid(a) * u).astype(BF16)
    o_ref[...] = x + 0.5 * _dot(act, wd_ref[...])


def _ffn(cfg, l, x, g, wg, wu, wd):
    n, tm = cfg.n_tok, cfg.tm
    return pl.pallas_call(
        _ffn_kernel,
        grid=(n // tm,),
        in_specs=[
            pl.BlockSpec((tm, D_MODEL), lambda i: (i, 0)),
            _const_spec((1, D_MODEL), (0, 0)),
            _const_spec((None, D_MODEL, D_FF), (l, 0, 0)),
            _const_spec((None, D_MODEL, D_FF), (l, 0, 0)),
            _const_spec((None, D_FF, D_MODEL), (l, 0, 0)),
        ],
        out_specs=pl.BlockSpec((tm, D_MODEL), lambda i: (i, 0)),
        out_shape=jax.ShapeDtypeStruct((n, D_MODEL), F32),
        compiler_params=pltpu.CompilerParams(
            dimension_semantics=("arbitrary",), vmem_limit_bytes=52 * MIB),
        name="ffn",
    )(x, g, wg, wu, wd)


def _inproj_kernel(x_ref, g_ref, w_ref, qg_ref, kg_ref, cos_ref, sin_ref, ones_ref, *rest, n_prompt_tiles):
    zabc_ref, qb_ref, kb_ref, vb_ref, qt_ref, vt_ref, kt_ref, ks_ref, vp_ref, vs_ref, zqkv_ref = rest[-11:]
    is_prompt = pl.program_id(0) < n_prompt_tiles
    h = _rms(x_ref[...], g_ref[...]).astype(BF16)
    zabc_ref[...] = _dot(h, w_ref[:, 0:W_ABC])
    cos = cos_ref[...]
    sin = sin_ref[...]
    ones = ones_ref[...]
    lane = lax.broadcasted_iota(jnp.int32, cos.shape, 1)
    first_half = (lane & (HEAD_DIM // 2)) == 0

    def head_norm(z, g):
        ss = z * z
        hi = ss.astype(BF16)
        lo = (ss - hi.astype(F32)).astype(BF16)
        tot = _dot(hi, ones) + _dot(lo, ones)
        return z * lax.rsqrt(tot * (1.0 / HEAD_DIM) + EPS) * g

    def rope(y):
        half = HEAD_DIM // 2
        partner = jnp.where(first_half, pltpu.roll(y, LANES - half, 1), pltpu.roll(y, half, 1))
        return y * cos + partner * sin

    zqkv_ref[...] = _dot(h, w_ref[:, W_ABC:W_PROJ])
    nw = ones.shape[0]
    for c in range(W_D // LANES):
        sl = slice(c * LANES, (c + 1) * LANES)
        if (c * LANES) % nw == 0:
            yq = head_norm(zqkv_ref[:, c * LANES:c * LANES + nw], qg_ref[...])
            yk = head_norm(zqkv_ref[:, W_D + c * LANES:W_D + c * LANES + nw], kg_ref[...])
        off = (c * LANES) % nw
        q = rope(yq[:, off:off + LANES]) * (HEAD_DIM ** -0.5)
        qb_ref[:, sl] = q.astype(BF16)
        qt_ref[c] = (q * LOG2E).T.astype(BF16)
        k = rope(yk[:, off:off + LANES])
        kb_ref[:, sl] = k.astype(BF16)
        v = zqkv_ref[:, 2 * W_D + c * LANES:2 * W_D + (c + 1) * LANES]
        vb_ref[:, sl] = v.astype(BF16)
        vt_ref[c, 0:LANES, :] = v.T.astype(BF16)
        vt_ref[c, LANES:VT_ROWS, :] = jnp.ones((VT_ROWS - LANES, v.shape[0]), BF16)

        @pl.when(is_prompt)
        def _():
            kt = k.T
            kt_ref[2 * c] = kt[0:HEAD_DIM]
            kt_ref[2 * c + 1] = kt[HEAD_DIM:LANES]
            vp_ref[:, sl] = v

        @pl.when(jnp.logical_not(is_prompt))
        def _():
            ks_ref[:, sl] = k
            vs_ref[:, sl] = v


def _inproj(cfg, l, depth, x, g, w_in, qg, kg, cos_t, sin_t, ones_bd, caches):
    n, tm, sp = cfg.n_tok, cfg.tm, cfg.s_prompt
    n_dec = n - sp
    assert sp % tm == 0 and n_dec % tm == 0
    npt = sp // tm
    row = lambda w: pl.BlockSpec((tm, w), lambda i: (i, 0))
    prompt_rows = pl.BlockSpec((None, tm, W_D), lambda i: (l, jnp.minimum(i, npt - 1), 0))
    sample_rows = pl.BlockSpec((None, tm, W_D), lambda i: (l, jnp.maximum(i - npt, 0), 0))
    n_in = 8
    aliased = [] if caches is None else list(caches)
    return pl.pallas_call(
        functools.partial(_inproj_kernel, n_prompt_tiles=npt),
        grid=(n // tm,),
        in_specs=[
            row(D_MODEL),
            _const_spec((1, D_MODEL), (0, 0)),
            _const_spec((None, D_MODEL, W_PROJ), (l, 0, 0)),
            _const_spec((1, NORM_W), (0, 0)),
            _const_spec((1, NORM_W), (0, 0)),
            row(LANES),
            row(LANES),
            _const_spec((NORM_W, NORM_W), (0, 0)),
        ] + [pl.BlockSpec(memory_space=pl.ANY) for _ in aliased],
        out_specs=[row(W_ABC), row(W_D), row(W_D), row(W_D),
                   pl.BlockSpec((N_HEADS, None, LANES, tm), lambda i: (0, i, 0, 0)),
                   pl.BlockSpec((N_HEADS, None, VT_ROWS, tm), lambda i: (0, i, 0, 0)),
                   pl.BlockSpec((None, 2 * N_HEADS, HEAD_DIM, tm), lambda i: (l, 0, 0, jnp.minimum(i, npt - 1))),
                   sample_rows, prompt_rows, sample_rows],
        out_shape=[
            jax.ShapeDtypeStruct((n, W_ABC), F32),
            jax.ShapeDtypeStruct((n, W_D), BF16),
            jax.ShapeDtypeStruct((n, W_D), BF16),
            jax.ShapeDtypeStruct((n, W_D), BF16),
            jax.ShapeDtypeStruct((N_HEADS, n // tm, LANES, tm), BF16),
            jax.ShapeDtypeStruct((N_HEADS, n // tm, VT_ROWS, tm), BF16),
            jax.ShapeDtypeStruct((depth, 2 * N_HEADS, HEAD_DIM, sp), F32),
            jax.ShapeDtypeStruct((depth, n_dec, W_D), F32),
            jax.ShapeDtypeStruct((depth, sp, W_D), F32),
            jax.ShapeDtypeStruct((depth, n_dec, W_D), F32),
        ],
        scratch_shapes=[pltpu.VMEM((tm, 3 * W_D), F32)],
        input_output_aliases={n_in + j: 6 + j for j in range(len(aliased))},
        compiler_params=pltpu.CompilerParams(
            dimension_semantics=("arbitrary",), vmem_limit_bytes=44 * MIB),
        name="inproj",
    )(x, g, w_in, qg, kg, cos_t, sin_t, ones_bd, *aliased)


MIX_ROWS = 128


def _mixer_compute(t, pos0, z_ref, pw_ref, ps_ref, sw_ref, cw_ref, cb_ref, lg_ref, lb_ref,
                   y_ref, pool_o, sconv_o, cconv_o, ea, eb, ec):
    u = z_ref[:, 0:W_A]
    ea[HALO:HALO + t, :] = u
    eb[HALO:HALO + t, :] = z_ref[:, W_A + W_B:W_A + 2 * W_B] * z_ref[:, W_A + 2 * W_B:W_A + 3 * W_B]
    zc = z_ref[:, W_A + 3 * W_B:W_A + 3 * W_B + W_C]
    ec[HALO:HALO + t, :] = zc * jax.nn.sigmoid(z_ref[:, W_A + 3 * W_B + W_C:W_ABC])

    rows = min(t, MIX_ROWS)
    for r0 in range(0, t, rows):
        base = HALO + r0
        lane = lax.broadcasted_iota(jnp.int32, (rows, W_A), 1)
        cur = ea[base:base + rows, :]
        acc = cur
        sums = {}
        for j in range(1, max(POOL_WINDOWS)):
            acc = acc + ea[base - j:base - j + rows, :]
            if j + 1 in POOL_WINDOWS:
                sums[j + 1] = acc
        tot = sums[POOL_WINDOWS[-1]]
        win = jnp.full((rows, W_A), float(POOL_WINDOWS[-1]), F32)
        for gi in range(len(POOL_WINDOWS) - 2, -1, -1):
            in_group = lane < (gi + 1) * GA
            tot = jnp.where(in_group, sums[POOL_WINDOWS[gi]], tot)
            win = jnp.where(in_group, float(POOL_WINDOWS[gi]), win)
        if pos0 is None:
            cnt = win
        else:
            pos1 = (pos0 + r0 + 1 + lax.broadcasted_iota(jnp.int32, (rows, W_A), 0)).astype(F32)
            cnt = jnp.minimum(pos1, win)
        d = (tot / cnt - cur).astype(BF16)
        y_ref[r0:r0 + rows, 0:W_A] = _dot(d, pw_ref[...]) * ps_ref[...]

        conv = sw_ref[SCONV_K - 1:SCONV_K, :] * eb[base:base + rows, :]
        for j in range(SCONV_K - 1):
            off = base - (SCONV_K - 1) + j
            conv = conv + sw_ref[j:j + 1, :] * eb[off:off + rows, :]
        y_ref[r0:r0 + rows, W_A:W_A + W_B] = z_ref[r0:r0 + rows, W_A:W_A + W_B] * conv

        conv = cw_ref[CCONV_K - 1:CCONV_K, :] * ec[base:base + rows, :]
        for j in range(CCONV_K - 1):
            off = base - (CCONV_K - 1) + j
            conv = conv + cw_ref[j:j + 1, :] * ec[off:off + rows, :]
        conv = conv + cb_ref[...]
        mu = jnp.mean(conv, axis=-1, keepdims=True)
        cen = conv - mu
        var = jnp.mean(cen * cen, axis=-1, keepdims=True)
        ln = cen * lax.rsqrt(var + EPS) * lg_ref[...] + lb_ref[...]
        y_ref[r0:r0 + rows, W_A + W_B:W_Y] = ln * jax.nn.sigmoid(ln)

    pool_o[...] = ea[HALO + t - 16:HALO + t, :]
    sconv_o[...] = eb[HALO + t - 8:HALO + t, :]
    cconv_o[...] = ec[HALO + t - 32:HALO + t, :]


def _mixer_prompt_kernel(z_ref, pw_ref, ps_ref, sw_ref, cw_ref, cb_ref, lg_ref, lb_ref,
                         y_ref, pool_o, sconv_o, cconv_o, ea, eb, ec, *, t):
    i = pl.program_id(0)

    @pl.when(i == 0)
    def _():
        zeros = jnp.zeros((HALO, W_A), F32)
        ea[0:HALO, :] = zeros
        eb[0:HALO, :] = zeros
        ec[0:HALO, :] = zeros

    @pl.when(i > 0)
    def _():
        ea[0:HALO, :] = ea[t:t + HALO, :]
        eb[0:HALO, :] = eb[t:t + HALO, :]
        ec[0:HALO, :] = ec[t:t + HALO, :]

    _mixer_compute(t, i * t, z_ref, pw_ref, ps_ref, sw_ref, cw_ref, cb_ref, lg_ref, lb_ref,
                   y_ref, pool_o, sconv_o, cconv_o, ea, eb, ec)


def _mixer_sample_kernel(z_ref, sp_ref, ss_ref, sc_ref, pw_ref, ps_ref, sw_ref, cw_ref, cb_ref,
                         lg_ref, lb_ref, y_in_ref, y_ref, pool_o, sconv_o, cconv_o, ea, eb, ec, *, t):
    del y_in_ref
    ea[0:HALO, :] = sp_ref[...]
    eb[0:HALO, :] = ss_ref[...]
    ec[0:HALO, :] = sc_ref[...]
    _mixer_compute(t, None, z_ref, pw_ref, ps_ref, sw_ref, cw_ref, cb_ref, lg_ref, lb_ref,
                   y_ref, pool_o, sconv_o, cconv_o, ea, eb, ec)


def _mixer_weight_specs():
    return [
        _const_spec((W_A, W_A), (0, 0)),
        _const_spec((1, W_A), (0, 0)),
        _const_spec((SCONV_K, W_B), (0, 0)),
        _const_spec((CCONV_K, W_C), (0, 0)),
        _const_spec((1, W_C), (0, 0)),
        _const_spec((1, W_C), (0, 0)),
        _const_spec((1, W_C), (0, 0)),
    ]


def _mixer_scratch(t):
    return [pltpu.VMEM((HALO + t, W_A), F32), pltpu.VMEM((HALO + t, W_B), F32),
            pltpu.VMEM((HALO + t, W_C), F32)]


def _mixer_prompt(cfg, zabc, weights):
    t = cfg.tmix
    const_out = lambda r: pl.BlockSpec((r, W_A), lambda i: (0, 0))
    return pl.pallas_call(
        functools.partial(_mixer_prompt_kernel, t=t),
        grid=(cfg.s_prompt // t,),
        in_specs=[pl.BlockSpec((t, W_ABC), lambda i: (i, 0))] + _mixer_weight_specs(),
        out_specs=[pl.BlockSpec((t, W_Y), lambda i: (i, 0)), const_out(16), const_out(8), const_out(32)],
        out_shape=[
            jax.ShapeDtypeStruct((cfg.n_tok, W_Y), F32),
            jax.ShapeDtypeStruct((16, W_A), F32),
            jax.ShapeDtypeStruct((8, W_B), F32),
            jax.ShapeDtypeStruct((32, W_C), F32),
        ],
        scratch_shapes=_mixer_scratch(t),
        compiler_params=pltpu.CompilerParams(dimension_semantics=("arbitrary",)),
        name="mixer_prompt",
    )(zabc, *weights)


def _mixer_sample(cfg, zabc, st_pool, st_sconv, st_cconv, weights, y_abc):
    t, nb = cfg.t_dec, cfg.n_dec
    row0 = cfg.s_prompt // t
    state_spec = lambda: pl.BlockSpec((None, HALO, W_A), lambda b: (b, 0, 0))
    out_state = lambda r: pl.BlockSpec((None, r, W_A), lambda b: (b, 0, 0))
    return pl.pallas_call(
        functools.partial(_mixer_sample_kernel, t=t),
        grid=(nb,),
        in_specs=[pl.BlockSpec((t, W_ABC), lambda b: (row0 + b, 0)),
                  state_spec(), state_spec(), state_spec()]
                 + _mixer_weight_specs()
                 + [pl.BlockSpec(memory_space=pl.ANY)],
        out_specs=[pl.BlockSpec((t, W_Y), lambda b: (row0 + b, 0)),
                   out_state(16), out_state(8), out_state(32)],
        out_shape=[
            jax.ShapeDtypeStruct((cfg.n_tok, W_Y), F32),
            jax.ShapeDtypeStruct((nb, 16, W_A), F32),
            jax.ShapeDtypeStruct((nb, 8, W_B), F32),
            jax.ShapeDtypeStruct((nb, 32, W_C), F32),
        ],
        scratch_shapes=_mixer_scratch(t),
        input_output_aliases={11: 0},
        compiler_params=pltpu.CompilerParams(dimension_semantics=("arbitrary",)),
        name="mixer_sample",
    )(zabc, st_pool, st_sconv, st_cconv, *weights, y_abc)


def _lambda(lq1, lk1, lq2, lk2, lam_init):
    s1 = jnp.sum(lq1[...] * lk1[...], axis=-1, keepdims=True)
    s2 = jnp.sum(lq2[...] * lk2[...], axis=-1, keepdims=True)
    return jnp.exp(s1) - jnp.exp(s2) + lam_init


def _stack_maps(q):
    lane = lax.broadcasted_iota(jnp.int32, q.shape, 1)
    zero = jnp.zeros_like(q)
    return jnp.concatenate([jnp.where(lane < HEAD_DIM, q, zero), jnp.where(lane >= HEAD_DIM, q, zero)],
                           axis=0)


def _diff_out(acc, l, lam, sg, lam_init, t):
    o = acc[0:t] / l[0:t] - lam * (acc[t:2 * t] / l[t:2 * t])
    return _rms(o, sg) * (1.0 - lam_init)


def _attn_prompt_kernel(qt_ref, k_ref, vt_ref, lq1, lk1, lq2, lk2, sg_ref, o_ref,
                        q2t_ref, s0_ref, s1_ref, m_ref, acc_ref, *, tk, lam_init):
    qi = pl.program_id(1)
    nq = 2 * tk
    chan = lax.broadcasted_iota(jnp.int32, (LANES, tk), 0)
    zero = jnp.zeros((LANES, tk), BF16)
    for half in range(2):
        qt = qt_ref[half]
        q2t_ref[:, half * tk:(half + 1) * tk] = jnp.where(chan < HEAD_DIM, qt, zero)
        q2t_ref[:, nq + half * tk:nq + (half + 1) * tk] = jnp.where(chan >= HEAD_DIM, qt, zero)
    m_ref[...] = jnp.full(m_ref.shape, -jnp.inf, F32)
    acc_ref[...] = jnp.zeros(acc_ref.shape, F32)

    def scores(j, s_ref):
        kb = k_ref[pl.ds(pl.multiple_of(j * tk, tk), tk), :]
        s_ref[...] = _dot(kb, q2t_ref[...])

    def consume(j, s_ref, diagonal):
        s = s_ref[...]
        if diagonal is not None:
            key = lax.broadcasted_iota(jnp.int32, s.shape, 0)
            col = lax.broadcasted_iota(jnp.int32, s.shape, 1)
            key_chunk = diagonal * (tk // CHUNK) + key // CHUNK
            s = jnp.where(key_chunk <= (col & (nq - 1)) // CHUNK, s, -jnp.inf)
        m_prev = m_ref[...]
        m_new = jnp.maximum(m_prev, jnp.max(s, axis=0, keepdims=True))
        alpha = jnp.exp2(m_prev - m_new)
        p = jnp.exp2(s - m_new).astype(BF16)
        acc_ref[...] = alpha * acc_ref[...] + _dot(vt_ref[j], p)
        m_ref[...] = m_new

    scores(0, s0_ref)

    def pair(i, carry):
        j = 2 * i
        scores(j + 1, s1_ref)
        consume(j, s0_ref, None)
        scores(j + 2, s0_ref)
        consume(j + 1, s1_ref, None)
        return carry

    lax.fori_loop(0, qi, pair, 0)
    scores(2 * qi + 1, s1_ref)
    consume(2 * qi, s0_ref, 0)
    consume(2 * qi + 1, s1_ref, 1)

    lam = _lambda(lq1, lk1, lq2, lk2, lam_init)
    acc = acc_ref[0:LANES, :]
    l = acc_ref[LANES:LANES + 1, :]
    ot = acc[:, 0:nq] / l[:, 0:nq] - lam * (acc[:, nq:2 * nq] / l[:, nq:2 * nq])
    o_ref[...] = _rms(ot.T, sg_ref[...]) * (1.0 - lam_init)


def _lam_specs():
    return [_const_spec((1, HEAD_DIM), (0, 0)) for _ in range(4)] + [_const_spec((1, LANES), (0, 0))]


def _attn_prompt(cfg, qt, kb, vt, lam_w, lam_init):
    s, tk = cfg.s_prompt, cfg.tm
    tq = 2 * tk
    assert tq & (tq - 1) == 0 and tk % CHUNK == 0 and s % tq == 0
    return pl.pallas_call(
        functools.partial(_attn_prompt_kernel, tk=tk, lam_init=lam_init),
        grid=(N_HEADS, s // tq),
        in_specs=[
            pl.BlockSpec((None, 2, LANES, tk), lambda h, i: (h, i, 0, 0)),
            pl.BlockSpec((s, LANES), lambda h, i: (0, h)),
            pl.BlockSpec((None, s // tk, VT_ROWS, tk), lambda h, i: (h, 0, 0, 0)),
        ] + _lam_specs(),
        out_specs=pl.BlockSpec((tq, LANES), lambda h, i: (i, h)),
        out_shape=jax.ShapeDtypeStruct((cfg.n_tok, W_D), F32),
        scratch_shapes=[
            pltpu.VMEM((LANES, 2 * tq), BF16),
            pltpu.VMEM((tk, 2 * tq), F32),
            pltpu.VMEM((tk, 2 * tq), F32),
            pltpu.VMEM((1, 2 * tq), F32),
            pltpu.VMEM((VT_ROWS, 2 * tq), F32),
        ],
        compiler_params=pltpu.CompilerParams(
            dimension_semantics=("arbitrary", "arbitrary"), vmem_limit_bytes=52 * MIB),
        name="attn_prompt",
    )(qt, kb, vt, *lam_w)


def _attn_sample_kernel(q_ref, kn_ref, vn_ref, kc_ref, vc_ref, lq1, lk1, lq2, lk2, sg_ref, yd_in_ref,
                        o_ref, *, t, lam_init):
    del yd_in_ref
    lam = _lambda(lq1, lk1, lq2, lk2, lam_init)
    past = vc_ref.shape[0] // N_HEADS
    for h in range(N_HEADS):
        sl = slice(h * LANES, (h + 1) * LANES)
        s_past, s_new = [], []
        for c in range(2):
            ch = slice(h * LANES + c * HEAD_DIM, h * LANES + (c + 1) * HEAD_DIM)
            q = q_ref[:, ch]
            s_past.append(_dot(q, kc_ref[2 * h + c].astype(BF16)))
            s_new.append(_dot_t(q, kn_ref[:, ch]))
        s_past = jnp.concatenate(s_past, axis=0)
        s_new = jnp.concatenate(s_new, axis=0)
        v_past = vc_ref[pl.ds(h, past, stride=N_HEADS), :]
        m = jnp.maximum(jnp.max(s_past, axis=-1, keepdims=True), jnp.max(s_new, axis=-1, keepdims=True))
        p_past = jnp.exp(s_past - m)
        p_new = jnp.exp(s_new - m)
        l = jnp.sum(p_past, axis=-1, keepdims=True) + jnp.sum(p_new, axis=-1, keepdims=True)
        acc = _dot(p_past.astype(BF16), v_past.astype(BF16)) + _dot(p_new.astype(BF16), vn_ref[:, sl])
        o_ref[:, sl] = _diff_out(acc, l, lam, sg_ref[...], lam_init, t)


def _attn_sample(cfg, l, qb, kb, vb, cache_k, cache_v, lam_w, lam_init, yd):
    t, nb = cfg.t_dec, cfg.n_dec
    row0 = cfg.s_prompt // t
    new_rows = lambda: pl.BlockSpec((t, W_D), lambda b: (row0 + b, 0))
    cache = lambda a: pl.BlockSpec((None, None) + a.shape[2:], lambda b: (l, b) + (0,) * (a.ndim - 2))
    return pl.pallas_call(
        functools.partial(_attn_sample_kernel, t=t, lam_init=lam_init),
        grid=(nb,),
        in_specs=[new_rows(), new_rows(), new_rows(), cache(cache_k), cache(cache_v)] + _lam_specs()
                 + [pl.BlockSpec(memory_space=pl.ANY)],
        out_specs=new_rows(),
        out_shape=jax.ShapeDtypeStruct((cfg.n_tok, W_D), F32),
        input_output_aliases={10: 0},
        compiler_params=pltpu.CompilerParams(
            dimension_semantics=("arbitrary",), vmem_limit_bytes=40 * MIB),
        name="attn_sample",
    )(qb, kb, vb, cache_k, cache_v, *lam_w, yd)


def _merge_kernel(x_ref, yabc_ref, yd_ref, g_ref, wg_ref, wpa_ref, wpb_ref, wpc_ref, wpd_ref, wo_ref, o_ref):
    x = x_ref[...]
    h = _rms(x, g_ref[...]).astype(BF16)
    branches = (
        (yabc_ref[:, 0:W_A], wpa_ref),
        (yabc_ref[:, W_A:W_A + W_B], wpb_ref),
        (yabc_ref[:, W_A + W_B:W_Y], wpc_ref),
        (yd_ref[...], wpd_ref),
    )
    merged = None
    for i, (y, wp_ref) in enumerate(branches):
        gate = jax.nn.sigmoid(_dot(h, wg_ref[:, i * D_MODEL:(i + 1) * D_MODEL]))
        term = gate * _dot(y.astype(BF16), wp_ref[...])
        merged = term if merged is None else merged + term
    o_ref[...] = x + _dot(merged.astype(BF16), wo_ref[...])


def _merge(cfg, l, x, y_abc, yd, g, w_gate, wpa, wpb, wpc, wpd, wo):
    n, tm = cfg.n_tok, cfg.tm
    row = lambda w: pl.BlockSpec((tm, w), lambda i: (i, 0))
    return pl.pallas_call(
        _merge_kernel,
        grid=(n // tm,),
        in_specs=[
            row(D_MODEL), row(W_Y), row(W_D),
            _const_spec((1, D_MODEL), (0, 0)),
            _const_spec((None, D_MODEL, 4 * D_MODEL), (l, 0, 0)),
            _const_spec((None, W_A, D_MODEL), (l, 0, 0)),
            _const_spec((None, W_B, D_MODEL), (l, 0, 0)),
            _const_spec((None, W_C, D_MODEL), (l, 0, 0)),
            _const_spec((None, W_D, D_MODEL), (l, 0, 0)),
            _const_spec((None, D_MODEL, D_MODEL), (l, 0, 0)),
        ],
        out_specs=row(D_MODEL),
        out_shape=jax.ShapeDtypeStruct((n, D_MODEL), F32),
        compiler_params=pltpu.CompilerParams(
            dimension_semantics=("arbitrary",), vmem_limit_bytes=48 * MIB),
        name="merge",
    )(x, y_abc, yd, g, w_gate, wpa, wpb, wpc, wpd, wo)


def _rope_tables(cfg):
    half = HEAD_DIM // 2
    inv_freq = ROPE_THETA ** (-jnp.arange(half, dtype=F32) / half)
    pos = jnp.concatenate([jnp.arange(cfg.s_prompt), jnp.tile(cfg.past + jnp.arange(cfg.t_dec), cfg.n_dec)])
    ang = pos.astype(F32)[:, None] * inv_freq[None, :]
    cos, sin = jnp.cos(ang), jnp.sin(ang)
    reps = LANES // HEAD_DIM
    cos_t = jnp.tile(jnp.concatenate([cos, cos], axis=1), (1, reps))
    sin_t = jnp.tile(jnp.concatenate([-sin, sin], axis=1), (1, reps))
    return cos_t, sin_t


def _pad_rows_top(a, rows):
    return jnp.pad(a, ((0, 0), (0, 0), (rows - a.shape[2], 0), (0, 0)))


def _forward(cfg, x_prompt, x_sample, cache_k, cache_v, state_pool, state_sconv, state_cconv,
             g_ffn1, w1_gate, w1_up, w1_down, g_mix, w_in, pool_w, pool_scale, sconv_w,
             cconv_w, cconv_b, ln_g, ln_b, q_norm_g, k_norm_g, lam_q1, lam_k1, lam_q2, lam_k2,
             subln_g, wp_a, wp_b, wp_c, wp_d, w_out, g_ffn2, w2_gate, w2_up, w2_down):
    depth = w_in.shape[0]
    sp, nb, td = cfg.s_prompt, cfg.n_dec, cfg.t_dec
    x = jnp.concatenate([x_prompt.reshape(sp, D_MODEL), x_sample.reshape(nb * td, D_MODEL)], axis=0)

    bf = lambda w: w.astype(BF16)
    w1g, w1u, w1d = bf(w1_gate), bf(w1_up), bf(w1_down)
    w2g, w2u, w2d = bf(w2_gate), bf(w2_up), bf(w2_down)
    w_proj, w_gate = bf(w_in[:, :, :W_PROJ]), bf(w_in[:, :, W_PROJ:])
    wpa, wpb, wpc, wpd, wo = bf(wp_a), bf(wp_b), bf(wp_c), bf(wp_d), bf(w_out)
    cos_t, sin_t = _rope_tables(cfg)
    ones_bd = jnp.kron(jnp.eye(NORM_W // HEAD_DIM, dtype=F32), jnp.ones((HEAD_DIM, HEAD_DIM), F32)).astype(BF16)
    eye_g = jnp.eye(len(POOL_WINDOWS), dtype=F32)
    ck = jnp.transpose(cache_k, (0, 1, 3, 4, 2))
    cv = cache_v.reshape(depth, nb, cfg.past * N_HEADS, 2 * HEAD_DIM)
    st_pool = _pad_rows_top(state_pool, HALO)
    st_sconv = _pad_rows_top(state_sconv, HALO)
    st_cconv = _pad_rows_top(state_cconv, HALO)
    row = lambda a: a.reshape(1, -1)

    outs = [[] for _ in range(6)]
    caches = None
    for l in range(depth):
        lam_init = 0.8 - 0.6 * math.exp(-0.3 * l)
        x = _ffn(cfg, l, x, row(g_ffn1[l]), w1g, w1u, w1d)
        qg = row(jnp.tile(q_norm_g[l], NORM_W // HEAD_DIM))
        kg = row(jnp.tile(k_norm_g[l], NORM_W // HEAD_DIM))
        zabc, qb, kb, vb, qt, vt, *caches = _inproj(cfg, l, depth, x, row(g_mix[l]), w_proj, qg, kg, cos_t, sin_t,
                                                    ones_bd, caches)

        pw_bd = (eye_g[:, None, :, None] * pool_w[l][:, :, None, :]).reshape(W_A, W_A).astype(BF16)
        mix_w = (pw_bd, row(pool_scale[l]), sconv_w[l], cconv_w[l], row(cconv_b[l]), row(ln_g[l]), row(ln_b[l]))
        y_abc, pool_p, sconv_p, cconv_p = _mixer_prompt(cfg, zabc, mix_w)
        y_abc, pool_s, sconv_s, cconv_s = _mixer_sample(cfg, zabc, st_pool[l], st_sconv[l], st_cconv[l],
                                                        mix_w, y_abc)

        lam_w = (row(lam_q1[l]), row(lam_k1[l]), row(lam_q2[l]), row(lam_k2[l]), row(subln_g[l]))
        yd = _attn_prompt(cfg, qt, kb, vt, lam_w, lam_init)
        yd = _attn_sample(cfg, l, qb, kb, vb, ck, cv, lam_w, lam_init, yd)

        x = _merge(cfg, l, x, y_abc, yd, row(g_mix[l]), w_gate, wpa, wpb, wpc, wpd, wo)
        x = _ffn(cfg, l, x, row(g_ffn2[l]), w2g, w2u, w2d)

        outs[0].append(pool_p[None, 16 - POOL_STATE:])
        outs[1].append(sconv_p[None, 8 - (SCONV_K - 1):])
        outs[2].append(cconv_p[None, 32 - (CCONV_K - 1):])
        outs[3].append(pool_s[:, 16 - POOL_STATE:])
        outs[4].append(sconv_s[:, 8 - (SCONV_K - 1):])
        outs[5].append(cconv_s[:, 32 - (CCONV_K - 1):])

    kt_all, ks_all, vp_all, vs_all = caches
    y_prompt = x[:sp].reshape(1, sp, D_MODEL)
    y_sample = x[sp:].reshape(nb, td, D_MODEL)
    k_prompt = jnp.transpose(kt_all.reshape(depth, 1, 2 * N_HEADS, HEAD_DIM, sp), (0, 1, 4, 2, 3))
    v_prompt = vp_all.reshape(depth, 1, sp, N_HEADS, 2 * HEAD_DIM)
    k_sample = ks_all.reshape(depth, nb, td, 2 * N_HEADS, HEAD_DIM)
    v_sample = vs_all.reshape(depth, nb, td, N_HEADS, 2 * HEAD_DIM)
    st = [jnp.stack(o) for o in outs]
    return (y_prompt, y_sample, k_prompt, v_prompt, st[0], st[1], st[2], k_sample, v_sample, st[3], st[4], st[5])


def kernel(x_prompt, x_sample, cache_k, cache_v, state_pool, state_sconv, state_cconv, g_ffn1, w1_gate, w1_up, w1_down, g_mix, w_in, pool_w, pool_scale, sconv_w, cconv_w, cconv_b, ln_g, ln_b, q_norm_g, k_norm_g, lam_q1, lam_k1, lam_q2, lam_k2, subln_g, wp_a, wp_b, wp_c, wp_d, w_out, g_ffn2, w2_gate, w2_up, w2_down):
    assert x_prompt.shape[0] == 1
    cfg = Cfg(s_prompt=x_prompt.shape[1], n_dec=x_sample.shape[0], t_dec=x_sample.shape[1],
              past=cache_k.shape[2], tm=512, tmix=512)
    return _forward(cfg, x_prompt, x_sample, cache_k, cache_v, state_pool, state_sconv, state_cconv,
                    g_ffn1, w1_gate, w1_up, w1_down, g_mix, w_in, pool_w, pool_scale, sconv_w,
                    cconv_w, cconv_b, ln_g, ln_b, q_norm_g, k_norm_g, lam_q1, lam_k1, lam_q2, lam_k2,
                    subln_g, wp_a, wp_b, wp_c, wp_d, w_out, g_ffn2, w2_gate, w2_up, w2_down)
```

```python
import functools
import math
from typing import NamedTuple

import jax
import jax.numpy as jnp
from jax import lax
from jax.experimental import pallas as pl
from jax.experimental.pallas import tpu as pltpu

F32 = jnp.float32
BF16 = jnp.bfloat16

D_MODEL = 1024
DEPTH = 4
CHUNK = 64
POOL_WINDOWS = (2, 4, 8, 16)
W_A = 256
GA = 64
POOL_STATE = 15
W_B = 256
SCONV_K = 3
W_C = 256
CCONV_K = 31
HEAD_DIM = 64
N_HEADS = 4
W_D = 512
ROPE_THETA = 10000.0
D_FF = 2816
EPS = 1e-6
W_ABC = W_A + 3 * W_B + 2 * W_C
W_PROJ = W_ABC + 3 * W_D
W_Y = W_A + W_B + W_C

LANES = 128
NORM_W = 256
VT_ROWS = LANES + 16
LOG2E = 1.4426950408889634
HALO = 32
MIB = 1024 * 1024


class Cfg(NamedTuple):
    s_prompt: int
    n_dec: int
    t_dec: int
    past: int
    tm: int
    tmix: int

    @property
    def n_tok(self):
        return self.s_prompt + self.n_dec * self.t_dec


def _const_spec(shape, index):
    return pl.BlockSpec(shape, lambda *_: index, pipeline_mode=pl.Buffered(1))


def _rms(x, g):
    ms = jnp.mean(x * x, axis=-1, keepdims=True)
    return x * lax.rsqrt(ms + EPS) * g


def _dot(a, b):
    return jnp.dot(a, b, preferred_element_type=F32)


def _dot_t(a, b):
    return lax.dot_general(a, b, (((1,), (1,)), ((), ())), preferred_element_type=F32)


def _ffn_kernel(x_ref, g_ref, wg_ref, wu_ref, wd_ref, o_ref):
    x = x_ref[...]
    h = _rms(x, g_ref[...]).astype(BF16)
    a = _dot(h, wg_ref[...])
    u = _dot(h, wu_ref[...])
    act = (a * jax.nn.sigmoid(a) * u).astype(BF16)
    o_ref[...] = x + 0.5 * _dot(act, wd_ref[...])


def _ffn(cfg, l, x, g, wg, wu, wd):
    n, tm = cfg.n_tok, cfg.tm
    return pl.pallas_call(
        _ffn_kernel,
        grid=(n // tm,),
        in_specs=[
            pl.BlockSpec((tm, D_MODEL), lambda i: (i, 0)),
            _const_spec((1, D_MODEL), (0, 0)),
            _const_spec((None, D_MODEL, D_FF), (l, 0, 0)),
            _const_spec((None, D_MODEL, D_FF), (l, 0, 0)),
            _const_spec((None, D_FF, D_MODEL), (l, 0, 0)),
        ],
        out_specs=pl.BlockSpec((tm, D_MODEL), lambda i: (i, 0)),
        out_shape=jax.ShapeDtypeStruct((n, D_MODEL), F32),
        compiler_params=pltpu.CompilerParams(
            dimension_semantics=("arbitrary",), vmem_limit_bytes=52 * MIB),
        name="ffn",
    )(x, g, wg, wu, wd)


def _inproj_kernel(x_ref, g_ref, w_ref, qg_ref, kg_ref, cos_ref, sin_ref, ones_ref, *rest, n_prompt_tiles):
    zabc_ref, qb_ref, kb_ref, vb_ref, qt_ref, vt_ref, kt_ref, ks_ref, vp_ref, vs_ref, zqkv_ref = rest[-11:]
    is_prompt = pl.program_id(0) < n_prompt_tiles
    h = _rms(x_ref[...], g_ref[...]).astype(BF16)
    zabc_ref[...] = _dot(h, w_ref[:, 0:W_ABC])
    cos = cos_ref[...]
    sin = sin_ref[...]
    ones = ones_ref[...]
    lane = lax.broadcasted_iota(jnp.int32, cos.shape, 1)
    first_half = (lane & (HEAD_DIM // 2)) == 0

    def head_norm(z, g):
        ss = z * z
        hi = ss.astype(BF16)
        lo = (ss - hi.astype(F32)).astype(BF16)
        tot = _dot(hi, ones) + _dot(lo, ones)
        return z * lax.rsqrt(tot * (1.0 / HEAD_DIM) + EPS) * g

    def rope(y):
        half = HEAD_DIM // 2
        partner = jnp.where(first_half, pltpu.roll(y, LANES - half, 1), pltpu.roll(y, half, 1))
        return y * cos + partner * sin

    zqkv_ref[...] = _dot(h, w_ref[:, W_ABC:W_PROJ])
    nw = ones.shape[0]
    for c in range(W_D // LANES):
        sl = slice(c * LANES, (c + 1) * LANES)
        if (c * LANES) % nw == 0:
            yq = head_norm(zqkv_ref[:, c * LANES:c * LANES + nw], qg_ref[...])
            yk = head_norm(zqkv_ref[:, W_D + c * LANES:W_D + c * LANES + nw], kg_ref[...])
        off = (c * LANES) % nw
        q = rope(yq[:, off:off + LANES]) * (HEAD_DIM ** -0.5)
        qb_ref[:, sl] = q.astype(BF16)
        qt_ref[c] = (q * LOG2E).T.astype(BF16)
        k = rope(yk[:, off:off + LANES])
        kb_ref[:, sl] = k.astype(BF16)
        zqkv_ref[:, W_D + c * LANES:W_D + (c + 1) * LANES] = k
        v = zqkv_ref[:, 2 * W_D + c * LANES:2 * W_D + (c + 1) * LANES]
        vb_ref[:, sl] = v.astype(BF16)
        vt_ref[c, 0:LANES, :] = v.T.astype(BF16)
        vt_ref[c, LANES:VT_ROWS, :] = jnp.ones((VT_ROWS - LANES, v.shape[0]), BF16)

    @pl.when(is_prompt)
    def _():
        for c in range(W_D // LANES):
            kt = zqkv_ref[:, W_D + c * LANES:W_D + (c + 1) * LANES].T
            kt_ref[2 * c] = kt[0:HEAD_DIM]
            kt_ref[2 * c + 1] = kt[HEAD_DIM:LANES]
        for hd in range(N_HEADS):
            vp_ref[pl.ds(hd, kt_ref.shape[-1], stride=N_HEADS), :] = (
                zqkv_ref[:, 2 * W_D + hd * LANES:2 * W_D + (hd + 1) * LANES])

    @pl.when(jnp.logical_not(is_prompt))
    def _():
        ks_ref[...] = zqkv_ref[:, W_D:2 * W_D]
        vs_ref[...] = zqkv_ref[:, 2 * W_D:3 * W_D]


def _inproj(cfg, l, depth, x, g, w_in, qg, kg, cos_t, sin_t, ones_bd, caches):
    n, tm, sp = cfg.n_tok, cfg.tm, cfg.s_prompt
    n_dec = n - sp
    assert sp % tm == 0 and n_dec % tm == 0
    npt = sp // tm
    row = lambda w: pl.BlockSpec((tm, w), lambda i: (i, 0))
    prompt_rows = pl.BlockSpec((None, tm * N_HEADS, LANES), lambda i: (l, jnp.minimum(i, npt - 1), 0))
    sample_rows = pl.BlockSpec((None, tm, W_D), lambda i: (l, jnp.maximum(i - npt, 0), 0))
    n_in = 8
    aliased = [] if caches is None else list(caches)
    return pl.pallas_call(
        functools.partial(_inproj_kernel, n_prompt_tiles=npt),
        grid=(n // tm,),
        in_specs=[
            row(D_MODEL),
            _const_spec((1, D_MODEL), (0, 0)),
            _const_spec((None, D_MODEL, W_PROJ), (l, 0, 0)),
            _const_spec((1, NORM_W), (0, 0)),
            _const_spec((1, NORM_W), (0, 0)),
            row(LANES),
            row(LANES),
            _const_spec((NORM_W, NORM_W), (0, 0)),
        ] + [pl.BlockSpec(memory_space=pl.ANY) for _ in aliased],
        out_specs=[row(W_ABC), row(W_D), row(W_D), row(W_D),
                   pl.BlockSpec((N_HEADS, None, LANES, tm), lambda i: (0, i, 0, 0)),
                   pl.BlockSpec((N_HEADS, None, VT_ROWS, tm), lambda i: (0, i, 0, 0)),
                   pl.BlockSpec((None, 2 * N_HEADS, HEAD_DIM, tm), lambda i: (l, 0, 0, jnp.minimum(i, npt - 1))),
                   sample_rows, prompt_rows, sample_rows],
        out_shape=[
            jax.ShapeDtypeStruct((n, W_ABC), F32),
            jax.ShapeDtypeStruct((n, W_D), BF16),
            jax.ShapeDtypeStruct((n, W_D), BF16),
            jax.ShapeDtypeStruct((n, W_D), BF16),
            jax.ShapeDtypeStruct((N_HEADS, n // tm, LANES, tm), BF16),
            jax.ShapeDtypeStruct((N_HEADS, n // tm, VT_ROWS, tm), BF16),
            jax.ShapeDtypeStruct((depth, 2 * N_HEADS, HEAD_DIM, sp), F32),
            jax.ShapeDtypeStruct((depth, n_dec, W_D), F32),
            jax.ShapeDtypeStruct((depth, sp * N_HEADS, LANES), F32),
            jax.ShapeDtypeStruct((depth, n_dec, W_D), F32),
        ],
        scratch_shapes=[pltpu.VMEM((tm, 3 * W_D), F32)],
        input_output_aliases={n_in + j: 6 + j for j in range(len(aliased))},
        compiler_params=pltpu.CompilerParams(
            dimension_semantics=("arbitrary",), vmem_limit_bytes=44 * MIB),
        name="inproj",
    )(x, g, w_in, qg, kg, cos_t, sin_t, ones_bd, *aliased)


MIX_ROWS = 128


SUBLANES = 8


def _shifted_rows(ref, base, rows, depth, tmp):
    out = {}
    for r in range(SUBLANES):
        js = [j for j in range(1, depth + 1) if (-j) % SUBLANES == r]
        if not js:
            continue
        start, length = base - max(js), max(js) - min(js) + rows
        tmp[r, 0:length, :] = ref[start:start + length, :]
        for j in js:
            out[j] = tmp[r, max(js) - j:max(js) - j + rows, :]
    return out


def _mixer_compute(t, pos0, z_ref, pw_ref, ps_ref, sw_ref, cw_ref, cb_ref, lg_ref, lb_ref,
                   y_ref, pool_o, sconv_o, cconv_o, ea, eb, ec, tmp):
    u = z_ref[:, 0:W_A]
    ea[HALO:HALO + t, :] = u
    eb[HALO:HALO + t, :] = z_ref[:, W_A + W_B:W_A + 2 * W_B] * z_ref[:, W_A + 2 * W_B:W_A + 3 * W_B]
    zc = z_ref[:, W_A + 3 * W_B:W_A + 3 * W_B + W_C]
    ec[HALO:HALO + t, :] = zc * jax.nn.sigmoid(z_ref[:, W_A + 3 * W_B + W_C:W_ABC])

    rows = min(t, MIX_ROWS)
    for r0 in range(0, t, rows):
        base = HALO + r0
        lane = lax.broadcasted_iota(jnp.int32, (rows, W_A), 1)
        cur = ea[base:base + rows, :]
        back = _shifted_rows(ea, base, rows, max(POOL_WINDOWS) - 1, tmp)
        acc = cur
        sums = {}
        for j in range(1, max(POOL_WINDOWS)):
            acc = acc + back[j]
            if j + 1 in POOL_WINDOWS:
                sums[j + 1] = acc
        tot = sums[POOL_WINDOWS[-1]]
        win = jnp.full((rows, W_A), float(POOL_WINDOWS[-1]), F32)
        for gi in range(len(POOL_WINDOWS) - 2, -1, -1):
            in_group = lane < (gi + 1) * GA
            tot = jnp.where(in_group, sums[POOL_WINDOWS[gi]], tot)
            win = jnp.where(in_group, float(POOL_WINDOWS[gi]), win)
        if pos0 is None:
            cnt = win
        else:
            pos1 = (pos0 + r0 + 1 + lax.broadcasted_iota(jnp.int32, (rows, W_A), 0)).astype(F32)
            cnt = jnp.minimum(pos1, win)
        d = (tot / cnt - cur).astype(BF16)
        y_ref[r0:r0 + rows, 0:W_A] = _dot(d, pw_ref[...]) * ps_ref[...]

        conv = sw_ref[SCONV_K - 1:SCONV_K, :] * eb[base:base + rows, :]
        for j in range(SCONV_K - 1):
            off = base - (SCONV_K - 1) + j
            conv = conv + sw_ref[j:j + 1, :] * eb[off:off + rows, :]
        y_ref[r0:r0 + rows, W_A:W_A + W_B] = z_ref[r0:r0 + rows, W_A:W_A + W_B] * conv

        back = _shifted_rows(ec, base, rows, CCONV_K - 1, tmp)
        conv = cw_ref[CCONV_K - 1:CCONV_K, :] * ec[base:base + rows, :]
        for j in range(CCONV_K - 1):
            conv = conv + cw_ref[j:j + 1, :] * back[CCONV_K - 1 - j]
        conv = conv + cb_ref[...]
        mu = jnp.mean(conv, axis=-1, keepdims=True)
        cen = conv - mu
        var = jnp.mean(cen * cen, axis=-1, keepdims=True)
        ln = cen * lax.rsqrt(var + EPS) * lg_ref[...] + lb_ref[...]
        y_ref[r0:r0 + rows, W_A + W_B:W_Y] = ln * jax.nn.sigmoid(ln)

    pool_o[...] = ea[HALO + t - 16:HALO + t, :]
    sconv_o[...] = eb[HALO + t - 8:HALO + t, :]
    cconv_o[...] = ec[HALO + t - 32:HALO + t, :]


def _mixer_prompt_kernel(z_ref, pw_ref, ps_ref, sw_ref, cw_ref, cb_ref, lg_ref, lb_ref,
                         y_ref, pool_o, sconv_o, cconv_o, ea, eb, ec, tmp, *, t):
    i = pl.program_id(0)

    @pl.when(i == 0)
    def _():
        zeros = jnp.zeros((HALO, W_A), F32)
        ea[0:HALO, :] = zeros
        eb[0:HALO, :] = zeros
        ec[0:HALO, :] = zeros

    @pl.when(i > 0)
    def _():
        ea[0:HALO, :] = ea[t:t + HALO, :]
        eb[0:HALO, :] = eb[t:t + HALO, :]
        ec[0:HALO, :] = ec[t:t + HALO, :]

    _mixer_compute(t, i * t, z_ref, pw_ref, ps_ref, sw_ref, cw_ref, cb_ref, lg_ref, lb_ref,
                   y_ref, pool_o, sconv_o, cconv_o, ea, eb, ec, tmp)


def _mixer_sample_kernel(z_ref, sp_ref, ss_ref, sc_ref, pw_ref, ps_ref, sw_ref, cw_ref, cb_ref,
                         lg_ref, lb_ref, y_in_ref, y_ref, pool_o, sconv_o, cconv_o, ea, eb, ec, tmp, *, t):
    del y_in_ref
    ea[0:HALO, :] = sp_ref[...]
    eb[0:HALO, :] = ss_ref[...]
    ec[0:HALO, :] = sc_ref[...]
    _mixer_compute(t, None, z_ref, pw_ref, ps_ref, sw_ref, cw_ref, cb_ref, lg_ref, lb_ref,
                   y_ref, pool_o, sconv_o, cconv_o, ea, eb, ec, tmp)


def _mixer_weight_specs():
    return [
        _const_spec((W_A, W_A), (0, 0)),
        _const_spec((1, W_A), (0, 0)),
        _const_spec((SCONV_K, W_B), (0, 0)),
        _const_spec((CCONV_K, W_C), (0, 0)),
        _const_spec((1, W_C), (0, 0)),
        _const_spec((1, W_C), (0, 0)),
        _const_spec((1, W_C), (0, 0)),
    ]


def _mixer_scratch(t):
    return [pltpu.VMEM((HALO + t, W_A), F32), pltpu.VMEM((HALO + t, W_B), F32),
            pltpu.VMEM((HALO + t, W_C), F32),
            pltpu.VMEM((SUBLANES, min(t, MIX_ROWS) + HALO, W_C), F32)]


def _mixer_prompt(cfg, zabc, weights):
    t = cfg.tmix
    const_out = lambda r: pl.BlockSpec((r, W_A), lambda i: (0, 0))
    return pl.pallas_call(
        functools.partial(_mixer_prompt_kernel, t=t),
        grid=(cfg.s_prompt // t,),
        in_specs=[pl.BlockSpec((t, W_ABC), lambda i: (i, 0))] + _mixer_weight_specs(),
        out_specs=[pl.BlockSpec((t, W_Y), lambda i: (i, 0)), const_out(16), const_out(8), const_out(32)],
        out_shape=[
            jax.ShapeDtypeStruct((cfg.n_tok, W_Y), F32),
            jax.ShapeDtypeStruct((16, W_A), F32),
            jax.ShapeDtypeStruct((8, W_B), F32),
            jax.ShapeDtypeStruct((32, W_C), F32),
        ],
        scratch_shapes=_mixer_scratch(t),
        compiler_params=pltpu.CompilerParams(dimension_semantics=("arbitrary",)),
        name="mixer_prompt",
    )(zabc, *weights)


def _mixer_sample(cfg, zabc, st_pool, st_sconv, st_cconv, weights, y_abc):
    t, nb = cfg.t_dec, cfg.n_dec
    row0 = cfg.s_prompt // t
    state_spec = lambda: pl.BlockSpec((None, HALO, W_A), lambda b: (b, 0, 0))
    out_state = lambda r: pl.BlockSpec((None, r, W_A), lambda b: (b, 0, 0))
    return pl.pallas_call(
        functools.partial(_mixer_sample_kernel, t=t),
        grid=(nb,),
        in_specs=[pl.BlockSpec((t, W_ABC), lambda b: (row0 + b, 0)),
                  state_spec(), state_spec(), state_spec()]
                 + _mixer_weight_specs()
                 + [pl.BlockSpec(memory_space=pl.ANY)],
        out_specs=[pl.BlockSpec((t, W_Y), lambda b: (row0 + b, 0)),
                   out_state(16), out_state(8), out_state(32)],
        out_shape=[
            jax.ShapeDtypeStruct((cfg.n_tok, W_Y), F32),
            jax.ShapeDtypeStruct((nb, 16, W_A), F32),
            jax.ShapeDtypeStruct((nb, 8, W_B), F32),
            jax.ShapeDtypeStruct((nb, 32, W_C), F32),
        ],
        scratch_shapes=_mixer_scratch(t),
        input_output_aliases={11: 0},
        compiler_params=pltpu.CompilerParams(dimension_semantics=("arbitrary",)),
        name="mixer_sample",
    )(zabc, st_pool, st_sconv, st_cconv, *weights, y_abc)


def _lambda(lq1, lk1, lq2, lk2, lam_init):
    s1 = jnp.sum(lq1[...] * lk1[...], axis=-1, keepdims=True)
    s2 = jnp.sum(lq2[...] * lk2[...], axis=-1, keepdims=True)
    return jnp.exp(s1) - jnp.exp(s2) + lam_init


def _stack_maps(q):
    lane = lax.broadcasted_iota(jnp.int32, q.shape, 1)
    zero = jnp.zeros_like(q)
    return jnp.concatenate([jnp.where(lane < HEAD_DIM, q, zero), jnp.where(lane >= HEAD_DIM, q, zero)],
                           axis=0)


def _diff_out(acc, l, lam, sg, lam_init, t):
    o = acc[0:t] / l[0:t] - lam * (acc[t:2 * t] / l[t:2 * t])
    return _rms(o, sg) * (1.0 - lam_init)


def _attn_prompt_kernel(qt_ref, k_ref, vt_ref, lq1, lk1, lq2, lk2, sg_ref, o_ref,
                        q2t_ref, s0_ref, s1_ref, mx0_ref, mx1_ref, m_ref, acc_ref, *, tk, lam_init):
    qi = pl.program_id(1)
    nq = 2 * tk
    chan = lax.broadcasted_iota(jnp.int32, (LANES, tk), 0)
    zero = jnp.zeros((LANES, tk), BF16)
    for half in range(2):
        qt = qt_ref[half]
        q2t_ref[:, half * tk:(half + 1) * tk] = jnp.where(chan < HEAD_DIM, qt, zero)
        q2t_ref[:, nq + half * tk:nq + (half + 1) * tk] = jnp.where(chan >= HEAD_DIM, qt, zero)
    m_ref[...] = jnp.full(m_ref.shape, -jnp.inf, F32)
    acc_ref[...] = jnp.zeros(acc_ref.shape, F32)

    def scores(j, s_ref, mx_ref, diagonal=None):
        kb = k_ref[pl.ds(pl.multiple_of(j * tk, tk), tk), :]
        s = _dot(kb, q2t_ref[...])
        if diagonal is not None:
            key = lax.broadcasted_iota(jnp.int32, s.shape, 0)
            col = lax.broadcasted_iota(jnp.int32, s.shape, 1)
            key_chunk = diagonal * (tk // CHUNK) + key // CHUNK
            s = jnp.where(key_chunk <= (col & (nq - 1)) // CHUNK, s, -jnp.inf)
        s_ref[...] = s
        mx_ref[...] = jnp.max(s, axis=0, keepdims=True)

    def consume(j, s_ref, mx_ref):
        m_prev = m_ref[...]
        m_new = jnp.maximum(m_prev, mx_ref[...])
        alpha = jnp.exp2(m_prev - m_new)
        p = jnp.exp2(s_ref[...] - m_new).astype(BF16)
        acc_ref[...] = alpha * acc_ref[...] + _dot(vt_ref[j], p)
        m_ref[...] = m_new

    @pl.when(qi == 0)
    def _():
        scores(0, s0_ref, mx0_ref, 0)

    @pl.when(qi > 0)
    def _():
        scores(0, s0_ref, mx0_ref)

    def pair(i, first_diagonal):
        j = 2 * i
        scores(j + 1, s1_ref, mx1_ref)
        consume(j, s0_ref, mx0_ref)
        scores(j + 2, s0_ref, mx0_ref, first_diagonal)
        consume(j + 1, s1_ref, mx1_ref)

    def full_pair(i, carry):
        pair(i, None)
        return carry

    lax.fori_loop(0, qi - 1, full_pair, 0)

    @pl.when(qi > 0)
    def _():
        pair(qi - 1, 0)

    scores(2 * qi + 1, s1_ref, mx1_ref, 1)
    consume(2 * qi, s0_ref, mx0_ref)
    consume(2 * qi + 1, s1_ref, mx1_ref)

    lam = _lambda(lq1, lk1, lq2, lk2, lam_init)
    acc = acc_ref[0:LANES, :]
    l = acc_ref[LANES:LANES + 1, :]
    ot = acc[:, 0:nq] / l[:, 0:nq] - lam * (acc[:, nq:2 * nq] / l[:, nq:2 * nq])
    o_ref[...] = _rms(ot.T, sg_ref[...]) * (1.0 - lam_init)


def _lam_specs():
    return [_const_spec((1, HEAD_DIM), (0, 0)) for _ in range(4)] + [_const_spec((1, LANES), (0, 0))]


def _attn_prompt(cfg, qt, kb, vt, lam_w, lam_init):
    s, tk = cfg.s_prompt, cfg.tm
    tq = 2 * tk
    assert tq & (tq - 1) == 0 and tk % CHUNK == 0 and s % tq == 0
    return pl.pallas_call(
        functools.partial(_attn_prompt_kernel, tk=tk, lam_init=lam_init),
        grid=(N_HEADS, s // tq),
        in_specs=[
            pl.BlockSpec((None, 2, LANES, tk), lambda h, i: (h, i, 0, 0)),
            pl.BlockSpec((s, LANES), lambda h, i: (0, h)),
            pl.BlockSpec((None, s // tk, VT_ROWS, tk), lambda h, i: (h, 0, 0, 0)),
        ] + _lam_specs(),
        out_specs=pl.BlockSpec((tq, LANES), lambda h, i: (i, h)),
        out_shape=jax.ShapeDtypeStruct((cfg.n_tok, W_D), F32),
        scratch_shapes=[
            pltpu.VMEM((LANES, 2 * tq), BF16),
            pltpu.VMEM((tk, 2 * tq), F32),
            pltpu.VMEM((tk, 2 * tq), F32),
            pltpu.VMEM((1, 2 * tq), F32),
            pltpu.VMEM((1, 2 * tq), F32),
            pltpu.VMEM((1, 2 * tq), F32),
            pltpu.VMEM((VT_ROWS, 2 * tq), F32),
        ],
        compiler_params=pltpu.CompilerParams(
            dimension_semantics=("arbitrary", "arbitrary"), vmem_limit_bytes=52 * MIB),
        name="attn_prompt",
    )(qt, kb, vt, *lam_w)


def _attn_sample_kernel(q_ref, kn_ref, vn_ref, kc_ref, vc_ref, lq1, lk1, lq2, lk2, sg_ref, yd_in_ref,
                        o_ref, *, t, lam_init):
    del yd_in_ref
    lam = _lambda(lq1, lk1, lq2, lk2, lam_init)
    past = vc_ref.shape[0] // N_HEADS
    for h in range(N_HEADS):
        sl = slice(h * LANES, (h + 1) * LANES)
        s_past, s_new = [], []
        for c in range(2):
            ch = slice(h * LANES + c * HEAD_DIM, h * LANES + (c + 1) * HEAD_DIM)
            q = q_ref[:, ch]
            s_past.append(_dot(q, kc_ref[2 * h + c].astype(BF16)))
            s_new.append(_dot_t(q, kn_ref[:, ch]))
        s_past = jnp.concatenate(s_past, axis=0)
        s_new = jnp.concatenate(s_new, axis=0)
        v_past = vc_ref[pl.ds(h, past, stride=N_HEADS), :]
        m = jnp.maximum(jnp.max(s_past, axis=-1, keepdims=True), jnp.max(s_new, axis=-1, keepdims=True))
        p_past = jnp.exp(s_past - m)
        p_new = jnp.exp(s_new - m)
        l = jnp.sum(p_past, axis=-1, keepdims=True) + jnp.sum(p_new, axis=-1, keepdims=True)
        acc = _dot(p_past.astype(BF16), v_past.astype(BF16)) + _dot(p_new.astype(BF16), vn_ref[:, sl])
        o_ref[:, sl] = _diff_out(acc, l, lam, sg_ref[...], lam_init, t)


def _attn_sample(cfg, l, qb, kb, vb, cache_k, cache_v, lam_w, lam_init, yd):
    t, nb = cfg.t_dec, cfg.n_dec
    row0 = cfg.s_prompt // t
    new_rows = lambda: pl.BlockSpec((t, W_D), lambda b: (row0 + b, 0))
    cache = lambda a: pl.BlockSpec((None, None) + a.shape[2:], lambda b: (l, b) + (0,) * (a.ndim - 2))
    return pl.pallas_call(
        functools.partial(_attn_sample_kernel, t=t, lam_init=lam_init),
        grid=(nb,),
        in_specs=[new_rows(), new_rows(), new_rows(), cache(cache_k), cache(cache_v)] + _lam_specs()
                 + [pl.BlockSpec(memory_space=pl.ANY)],
        out_specs=new_rows(),
        out_shape=jax.ShapeDtypeStruct((cfg.n_tok, W_D), F32),
        input_output_aliases={10: 0},
        compiler_params=pltpu.CompilerParams(
            dimension_semantics=("arbitrary",), vmem_limit_bytes=40 * MIB),
        name="attn_sample",
    )(qb, kb, vb, cache_k, cache_v, *lam_w, yd)


def _merge_kernel(x_ref, yabc_ref, yd_ref, g_ref, wg_ref, wpa_ref, wpb_ref, wpc_ref, wpd_ref, wo_ref, o_ref):
    x = x_ref[...]
    h = _rms(x, g_ref[...]).astype(BF16)
    branches = (
        (yabc_ref[:, 0:W_A], wpa_ref),
        (yabc_ref[:, W_A:W_A + W_B], wpb_ref),
        (yabc_ref[:, W_A + W_B:W_Y], wpc_ref),
        (yd_ref[...], wpd_ref),
    )
    merged = None
    for i, (y, wp_ref) in enumerate(branches):
        gate = jax.nn.sigmoid(_dot(h, wg_ref[:, i * D_MODEL:(i + 1) * D_MODEL]))
        term = gate * _dot(y.astype(BF16), wp_ref[...])
        merged = term if merged is None else merged + term
    o_ref[...] = x + _dot(merged.astype(BF16), wo_ref[...])


def _merge(cfg, l, x, y_abc, yd, g, w_gate, wpa, wpb, wpc, wpd, wo):
    n, tm = cfg.n_tok, cfg.tm
    row = lambda w: pl.BlockSpec((tm, w), lambda i: (i, 0))
    return pl.pallas_call(
        _merge_kernel,
        grid=(n // tm,),
        in_specs=[
            row(D_MODEL), row(W_Y), row(W_D),
            _const_spec((1, D_MODEL), (0, 0)),
            _const_spec((None, D_MODEL, 4 * D_MODEL), (l, 0, 0)),
            _const_spec((None, W_A, D_MODEL), (l, 0, 0)),
            _const_spec((None, W_B, D_MODEL), (l, 0, 0)),
            _const_spec((None, W_C, D_MODEL), (l, 0, 0)),
            _const_spec((None, W_D, D_MODEL), (l, 0, 0)),
            _const_spec((None, D_MODEL, D_MODEL), (l, 0, 0)),
        ],
        out_specs=row(D_MODEL),
        out_shape=jax.ShapeDtypeStruct((n, D_MODEL), F32),
        compiler_params=pltpu.CompilerParams(
            dimension_semantics=("arbitrary",), vmem_limit_bytes=48 * MIB),
        name="merge",
    )(x, y_abc, yd, g, w_gate, wpa, wpb, wpc, wpd, wo)


def _rope_tables(cfg):
    half = HEAD_DIM // 2
    inv_freq = ROPE_THETA ** (-jnp.arange(half, dtype=F32) / half)
    pos = jnp.concatenate([jnp.arange(cfg.s_prompt), jnp.tile(cfg.past + jnp.arange(cfg.t_dec), cfg.n_dec)])
    ang = pos.astype(F32)[:, None] * inv_freq[None, :]
    cos, sin = jnp.cos(ang), jnp.sin(ang)
    reps = LANES // HEAD_DIM
    cos_t = jnp.tile(jnp.concatenate([cos, cos], axis=1), (1, reps))
    sin_t = jnp.tile(jnp.concatenate([-sin, sin], axis=1), (1, reps))
    return cos_t, sin_t


def _pad_rows_top(a, rows):
    return jnp.pad(a, ((0, 0), (0, 0), (rows - a.shape[2], 0), (0, 0)))


def _forward(cfg, x_prompt, x_sample, cache_k, cache_v, state_pool, state_sconv, state_cconv,
             g_ffn1, w1_gate, w1_up, w1_down, g_mix, w_in, pool_w, pool_scale, sconv_w,
             cconv_w, cconv_b, ln_g, ln_b, q_norm_g, k_norm_g, lam_q1, lam_k1, lam_q2, lam_k2,
             subln_g, wp_a, wp_b, wp_c, wp_d, w_out, g_ffn2, w2_gate, w2_up, w2_down):
    depth = w_in.shape[0]
    sp, nb, td = cfg.s_prompt, cfg.n_dec, cfg.t_dec
    x = jnp.concatenate([x_prompt.reshape(sp, D_MODEL), x_sample.reshape(nb * td, D_MODEL)], axis=0)

    bf = lambda w: w.astype(BF16)
    w1g, w1u, w1d = bf(w1_gate), bf(w1_up), bf(w1_down)
    w2g, w2u, w2d = bf(w2_gate), bf(w2_up), bf(w2_down)
    w_proj, w_gate = bf(w_in[:, :, :W_PROJ]), bf(w_in[:, :, W_PROJ:])
    wpa, wpb, wpc, wpd, wo = bf(wp_a), bf(wp_b), bf(wp_c), bf(wp_d), bf(w_out)
    cos_t, sin_t = _rope_tables(cfg)
    ones_bd = jnp.kron(jnp.eye(NORM_W // HEAD_DIM, dtype=F32), jnp.ones((HEAD_DIM, HEAD_DIM), F32)).astype(BF16)
    eye_g = jnp.eye(len(POOL_WINDOWS), dtype=F32)
    ck = jnp.transpose(cache_k, (0, 1, 3, 4, 2))
    cv = cache_v.reshape(depth, nb, cfg.past * N_HEADS, 2 * HEAD_DIM)
    st_pool = _pad_rows_top(state_pool, HALO)
    st_sconv = _pad_rows_top(state_sconv, HALO)
    st_cconv = _pad_rows_top(state_cconv, HALO)
    row = lambda a: a.reshape(1, -1)

    outs = [[] for _ in range(6)]
    caches = None
    for l in range(depth):
        lam_init = 0.8 - 0.6 * math.exp(-0.3 * l)
        x = _ffn(cfg, l, x, row(g_ffn1[l]), w1g, w1u, w1d)
        qg = row(jnp.tile(q_norm_g[l], NORM_W // HEAD_DIM))
        kg = row(jnp.tile(k_norm_g[l], NORM_W // HEAD_DIM))
        zabc, qb, kb, vb, qt, vt, *caches = _inproj(cfg, l, depth, x, row(g_mix[l]), w_proj, qg, kg, cos_t, sin_t,
                                                    ones_bd, caches)

        pw_bd = (eye_g[:, None, :, None] * pool_w[l][:, :, None, :]).reshape(W_A, W_A).astype(BF16)
        mix_w = (pw_bd, row(pool_scale[l]), sconv_w[l], cconv_w[l], row(cconv_b[l]), row(ln_g[l]), row(ln_b[l]))
        y_abc, pool_p, sconv_p, cconv_p = _mixer_prompt(cfg, zabc, mix_w)
        y_abc, pool_s, sconv_s, cconv_s = _mixer_sample(cfg, zabc, st_pool[l], st_sconv[l], st_cconv[l],
                                                        mix_w, y_abc)

        lam_w = (row(lam_q1[l]), row(lam_k1[l]), row(lam_q2[l]), row(lam_k2[l]), row(subln_g[l]))
        yd = _attn_prompt(cfg, qt, kb, vt, lam_w, lam_init)
        yd = _attn_sample(cfg, l, qb, kb, vb, ck, cv, lam_w, lam_init, yd)

        x = _merge(cfg, l, x, y_abc, yd, row(g_mix[l]), w_gate, wpa, wpb, wpc, wpd, wo)
        x = _ffn(cfg, l, x, row(g_ffn2[l]), w2g, w2u, w2d)

        outs[0].append(pool_p[None, 16 - POOL_STATE:])
        outs[1].append(sconv_p[None, 8 - (SCONV_K - 1):])
        outs[2].append(cconv_p[None, 32 - (CCONV_K - 1):])
        outs[3].append(pool_s[:, 16 - POOL_STATE:])
        outs[4].append(sconv_s[:, 8 - (SCONV_K - 1):])
        outs[5].append(cconv_s[:, 32 - (CCONV_K - 1):])

    kt_all, ks_all, vp_all, vs_all = caches
    y_prompt = x[:sp].reshape(1, sp, D_MODEL)
    y_sample = x[sp:].reshape(nb, td, D_MODEL)
    k_prompt = jnp.transpose(kt_all.reshape(depth, 1, 2 * N_HEADS, HEAD_DIM, sp), (0, 1, 4, 2, 3))
    v_prompt = vp_all.reshape(depth, 1, sp, N_HEADS, 2 * HEAD_DIM)
    k_sample = ks_all.reshape(depth, nb, td, 2 * N_HEADS, HEAD_DIM)
    v_sample = vs_all.reshape(depth, nb, td, N_HEADS, 2 * HEAD_DIM)
    st = [jnp.stack(o) for o in outs]
    return (y_prompt, y_sample, k_prompt, v_prompt, st[0], st[1], st[2], k_sample, v_sample, st[3], st[4], st[5])


def kernel(x_prompt, x_sample, cache_k, cache_v, state_pool, state_sconv, state_cconv, g_ffn1, w1_gate, w1_up, w1_down, g_mix, w_in, pool_w, pool_scale, sconv_w, cconv_w, cconv_b, ln_g, ln_b, q_norm_g, k_norm_g, lam_q1, lam_k1, lam_q2, lam_k2, subln_g, wp_a, wp_b, wp_c, wp_d, w_out, g_ffn2, w2_gate, w2_up, w2_down):
    assert x_prompt.shape[0] == 1
    cfg = Cfg(s_prompt=x_prompt.shape[1], n_dec=x_sample.shape[0], t_dec=x_sample.shape[1],
              past=cache_k.shape[2], tm=512, tmix=512)
    return _forward(cfg, x_prompt, x_sample, cache_k, cache_v, state_pool, state_sconv, state_cconv,
                    g_ffn1, w1_gate, w1_up, w1_down, g_mix, w_in, pool_w, pool_scale, sconv_w,
                    cconv_w, cconv_b, ln_g, ln_b, q_norm_g, k_norm_g, lam_q1, lam_k1, lam_q2, lam_k2,
                    subln_g, wp_a, wp_b, wp_c, wp_d, w_out, g_ffn2, w2_gate, w2_up, w2_down)
```

```python
import functools
import math
from typing import NamedTuple

import jax
import jax.numpy as jnp
from jax import lax
from jax.experimental import pallas as pl
from jax.experimental.pallas import tpu as pltpu

F32 = jnp.float32
BF16 = jnp.bfloat16

D_MODEL = 1024
DEPTH = 4
CHUNK = 64
POOL_WINDOWS = (2, 4, 8, 16)
W_A = 256
GA = 64
POOL_STATE = 15
W_B = 256
SCONV_K = 3
W_C = 256
CCONV_K = 31
HEAD_DIM = 64
N_HEADS = 4
W_D = 512
ROPE_THETA = 10000.0
D_FF = 2816
EPS = 1e-6
W_ABC = W_A + 3 * W_B + 2 * W_C
W_PROJ = W_ABC + 3 * W_D
W_Y = W_A + W_B + W_C

LANES = 128
NORM_W = 256
VT_ROWS = LANES + 16
LOG2E = 1.4426950408889634
HALO = 32
MIB = 1024 * 1024


class Cfg(NamedTuple):
    s_prompt: int
    n_dec: int
    t_dec: int
    past: int
    tm: int
    tmix: int

    @property
    def n_tok(self):
        return self.s_prompt + self.n_dec * self.t_dec


def _const_spec(shape, index):
    return pl.BlockSpec(shape, lambda *_: index, pipeline_mode=pl.Buffered(1))


def _rms(x, g):
    ms = jnp.mean(x * x, axis=-1, keepdims=True)
    return x * lax.rsqrt(ms + EPS) * g


def _dot(a, b):
    return jnp.dot(a, b, preferred_element_type=F32)


def _dot_t(a, b):
    return lax.dot_general(a, b, (((1,), (1,)), ((), ())), preferred_element_type=F32)


def _ffn_kernel(*refs, n_prompt_tiles, split_in, split_out):
    refs = list(refs)
    is_prompt = pl.program_id(0) < n_prompt_tiles
    if split_in:
        xp_ref, xs_ref = refs[0:2]
        del refs[0:2]
        x = jnp.where(is_prompt, xp_ref[...], xs_ref[...])
    else:
        x = refs.pop(0)[...]
    g_ref, wg_ref, wu_ref, wd_ref = refs[0:4]
    h = _rms(x, g_ref[...]).astype(BF16)
    a = _dot(h, wg_ref[...])
    u = _dot(h, wu_ref[...])
    act = (a * jax.nn.sigmoid(a) * u).astype(BF16)
    half_step = 0.5 * _dot(act, wd_ref[...])
    if split_out:
        op_ref, os_ref = refs[4:6]

        @pl.when(is_prompt)
        def _():
            op_ref[...] = x + half_step

        @pl.when(jnp.logical_not(is_prompt))
        def _():
            os_ref[...] = x + half_step
    else:
        refs[4][...] = x + half_step


def _ffn(cfg, l, x, g, wg, wu, wd, split_out=False):
    n, tm, sp = cfg.n_tok, cfg.tm, cfg.s_prompt
    assert sp % tm == 0 and (n - sp) % tm == 0
    npt = sp // tm
    split_in = isinstance(x, tuple)
    rows = pl.BlockSpec((tm, D_MODEL), lambda i: (i, 0))
    prompt_rows = pl.BlockSpec((tm, D_MODEL), lambda i: (jnp.minimum(i, npt - 1), 0))
    sample_rows = pl.BlockSpec((tm, D_MODEL), lambda i: (jnp.maximum(i - npt, 0), 0))
    whole = jax.ShapeDtypeStruct((n, D_MODEL), F32)
    halves = [jax.ShapeDtypeStruct((sp, D_MODEL), F32), jax.ShapeDtypeStruct((n - sp, D_MODEL), F32)]
    return pl.pallas_call(
        functools.partial(_ffn_kernel, n_prompt_tiles=npt, split_in=split_in, split_out=split_out),
        grid=(n // tm,),
        in_specs=([prompt_rows, sample_rows] if split_in else [rows]) + [
            _const_spec((1, D_MODEL), (0, 0)),
            _const_spec((None, D_MODEL, D_FF), (l, 0, 0)),
            _const_spec((None, D_MODEL, D_FF), (l, 0, 0)),
            _const_spec((None, D_FF, D_MODEL), (l, 0, 0)),
        ],
        out_specs=[prompt_rows, sample_rows] if split_out else rows,
        out_shape=halves if split_out else whole,
        compiler_params=pltpu.CompilerParams(
            dimension_semantics=("arbitrary",), vmem_limit_bytes=52 * MIB),
        name="ffn",
    )(*(x if split_in else (x,)), g, wg, wu, wd)


def _inproj_kernel(x_ref, g_ref, w_ref, qg_ref, kg_ref, cos_ref, sin_ref, ones_ref, *rest, n_prompt_tiles):
    zabc_ref, qb_ref, kb_ref, vb_ref, qt_ref, vt_ref, kt_ref, ks_ref, vp_ref, vs_ref, zqkv_ref = rest[-11:]
    is_prompt = pl.program_id(0) < n_prompt_tiles
    h = _rms(x_ref[...], g_ref[...]).astype(BF16)
    zabc_ref[...] = _dot(h, w_ref[:, 0:W_ABC])
    cos = cos_ref[...]
    sin = sin_ref[...]
    ones = ones_ref[...]
    lane = lax.broadcasted_iota(jnp.int32, cos.shape, 1)
    first_half = (lane & (HEAD_DIM // 2)) == 0

    def head_norm(z, g):
        ss = z * z
        hi = ss.astype(BF16)
        lo = (ss - hi.astype(F32)).astype(BF16)
        tot = _dot(hi, ones) + _dot(lo, ones)
        return z * lax.rsqrt(tot * (1.0 / HEAD_DIM) + EPS) * g

    def rope(y):
        half = HEAD_DIM // 2
        partner = jnp.where(first_half, pltpu.roll(y, LANES - half, 1), pltpu.roll(y, half, 1))
        return y * cos + partner * sin

    zqkv_ref[...] = _dot(h, w_ref[:, W_ABC:W_PROJ])
    nw = ones.shape[0]
    for c in range(W_D // LANES):
        sl = slice(c * LANES, (c + 1) * LANES)
        if (c * LANES) % nw == 0:
            yq = head_norm(zqkv_ref[:, c * LANES:c * LANES + nw], qg_ref[...])
            yk = head_norm(zqkv_ref[:, W_D + c * LANES:W_D + c * LANES + nw], kg_ref[...])
        off = (c * LANES) % nw
        q = rope(yq[:, off:off + LANES]) * (HEAD_DIM ** -0.5)
        qb_ref[:, sl] = q.astype(BF16)
        qt_ref[c] = (q * LOG2E).T.astype(BF16)
        k = rope(yk[:, off:off + LANES])
        kb_ref[:, sl] = k.astype(BF16)
        zqkv_ref[:, W_D + c * LANES:W_D + (c + 1) * LANES] = k
        v = zqkv_ref[:, 2 * W_D + c * LANES:2 * W_D + (c + 1) * LANES]
        vb_ref[:, sl] = v.astype(BF16)
        vt_ref[c, 0:LANES, :] = v.T.astype(BF16)
        vt_ref[c, LANES:VT_ROWS, :] = jnp.ones((VT_ROWS - LANES, v.shape[0]), BF16)

    @pl.when(is_prompt)
    def _():
        for c in range(W_D // LANES):
            kt = zqkv_ref[:, W_D + c * LANES:W_D + (c + 1) * LANES].T
            kt_ref[2 * c] = kt[0:HEAD_DIM]
            kt_ref[2 * c + 1] = kt[HEAD_DIM:LANES]
        for hd in range(N_HEADS):
            vp_ref[pl.ds(hd, kt_ref.shape[-1], stride=N_HEADS), :] = (
                zqkv_ref[:, 2 * W_D + hd * LANES:2 * W_D + (hd + 1) * LANES])

    @pl.when(jnp.logical_not(is_prompt))
    def _():
        ks_ref[...] = zqkv_ref[:, W_D:2 * W_D]
        vs_ref[...] = zqkv_ref[:, 2 * W_D:3 * W_D]


def _inproj(cfg, l, depth, x, g, w_in, qg, kg, cos_t, sin_t, ones_bd, caches):
    n, tm, sp = cfg.n_tok, cfg.tm, cfg.s_prompt
    n_dec = n - sp
    assert sp % tm == 0 and n_dec % tm == 0
    npt = sp // tm
    row = lambda w: pl.BlockSpec((tm, w), lambda i: (i, 0))
    prompt_rows = pl.BlockSpec((None, tm * N_HEADS, LANES), lambda i: (l, jnp.minimum(i, npt - 1), 0))
    sample_rows = pl.BlockSpec((None, tm, W_D), lambda i: (l, jnp.maximum(i - npt, 0), 0))
    n_in = 8
    aliased = [] if caches is None else list(caches)
    return pl.pallas_call(
        functools.partial(_inproj_kernel, n_prompt_tiles=npt),
        grid=(n // tm,),
        in_specs=[
            row(D_MODEL),
            _const_spec((1, D_MODEL), (0, 0)),
            _const_spec((None, D_MODEL, W_PROJ), (l, 0, 0)),
            _const_spec((1, NORM_W), (0, 0)),
            _const_spec((1, NORM_W), (0, 0)),
            row(LANES),
            row(LANES),
            _const_spec((NORM_W, NORM_W), (0, 0)),
        ] + [pl.BlockSpec(memory_space=pl.ANY) for _ in aliased],
        out_specs=[row(W_ABC), row(W_D), row(W_D), row(W_D),
                   pl.BlockSpec((N_HEADS, None, LANES, tm), lambda i: (0, i, 0, 0)),
                   pl.BlockSpec((N_HEADS, None, VT_ROWS, tm), lambda i: (0, i, 0, 0)),
                   pl.BlockSpec((None, 2 * N_HEADS, HEAD_DIM, tm), lambda i: (l, 0, 0, jnp.minimum(i, npt - 1))),
                   sample_rows, prompt_rows, sample_rows],
        out_shape=[
            jax.ShapeDtypeStruct((n, W_ABC), F32),
            jax.ShapeDtypeStruct((n, W_D), BF16),
            jax.ShapeDtypeStruct((n, W_D), BF16),
            jax.ShapeDtypeStruct((n, W_D), BF16),
            jax.ShapeDtypeStruct((N_HEADS, n // tm, LANES, tm), BF16),
            jax.ShapeDtypeStruct((N_HEADS, n // tm, VT_ROWS, tm), BF16),
            jax.ShapeDtypeStruct((depth, 2 * N_HEADS, HEAD_DIM, sp), F32),
            jax.ShapeDtypeStruct((depth, n_dec, W_D), F32),
            jax.ShapeDtypeStruct((depth, sp * N_HEADS, LANES), F32),
            jax.ShapeDtypeStruct((depth, n_dec, W_D), F32),
        ],
        scratch_shapes=[pltpu.VMEM((tm, 3 * W_D), F32)],
        input_output_aliases={n_in + j: 6 + j for j in range(len(aliased))},
        compiler_params=pltpu.CompilerParams(
            dimension_semantics=("arbitrary",), vmem_limit_bytes=44 * MIB),
        name="inproj",
    )(x, g, w_in, qg, kg, cos_t, sin_t, ones_bd, *aliased)


MIX_ROWS = 128


SUBLANES = 8


def _shifted_rows(ref, base, rows, depth, tmp):
    out = {}
    for r in range(SUBLANES):
        js = [j for j in range(1, depth + 1) if (-j) % SUBLANES == r]
        if not js:
            continue
        start, length = base - max(js), max(js) - min(js) + rows
        tmp[r, 0:length, :] = ref[start:start + length, :]
        for j in js:
            out[j] = tmp[r, max(js) - j:max(js) - j + rows, :]
    return out


def _mixer_compute(t, pos0, z_ref, pw_ref, ps_ref, sw_ref, cw_ref, cb_ref, lg_ref, lb_ref,
                   y_ref, pool_o, sconv_o, cconv_o, ea, eb, ec, tmp):
    u = z_ref[:, 0:W_A]
    ea[HALO:HALO + t, :] = u
    eb[HALO:HALO + t, :] = z_ref[:, W_A + W_B:W_A + 2 * W_B] * z_ref[:, W_A + 2 * W_B:W_A + 3 * W_B]
    zc = z_ref[:, W_A + 3 * W_B:W_A + 3 * W_B + W_C]
    ec[HALO:HALO + t, :] = zc * jax.nn.sigmoid(z_ref[:, W_A + 3 * W_B + W_C:W_ABC])

    rows = min(t, MIX_ROWS)
    for r0 in range(0, t, rows):
        base = HALO + r0
        lane = lax.broadcasted_iota(jnp.int32, (rows, W_A), 1)
        cur = ea[base:base + rows, :]
        back = _shifted_rows(ea, base, rows, max(POOL_WINDOWS) - 1, tmp)
        acc = cur
        sums = {}
        for j in range(1, max(POOL_WINDOWS)):
            acc = acc + back[j]
            if j + 1 in POOL_WINDOWS:
                sums[j + 1] = acc
        tot = sums[POOL_WINDOWS[-1]]
        win = jnp.full((rows, W_A), float(POOL_WINDOWS[-1]), F32)
        for gi in range(len(POOL_WINDOWS) - 2, -1, -1):
            in_group = lane < (gi + 1) * GA
            tot = jnp.where(in_group, sums[POOL_WINDOWS[gi]], tot)
            win = jnp.where(in_group, float(POOL_WINDOWS[gi]), win)
        if pos0 is None:
            cnt = win
        else:
            pos1 = (pos0 + r0 + 1 + lax.broadcasted_iota(jnp.int32, (rows, W_A), 0)).astype(F32)
            cnt = jnp.minimum(pos1, win)
        d = (tot / cnt - cur).astype(BF16)
        y_ref[r0:r0 + rows, 0:W_A] = _dot(d, pw_ref[...]) * ps_ref[...]

        conv = sw_ref[SCONV_K - 1:SCONV_K, :] * eb[base:base + rows, :]
        for j in range(SCONV_K - 1):
            off = base - (SCONV_K - 1) + j
            conv = conv + sw_ref[j:j + 1, :] * eb[off:off + rows, :]
        y_ref[r0:r0 + rows, W_A:W_A + W_B] = z_ref[r0:r0 + rows, W_A:W_A + W_B] * conv

        back = _shifted_rows(ec, base, rows, CCONV_K - 1, tmp)
        conv = cw_ref[CCONV_K - 1:CCONV_K, :] * ec[base:base + rows, :]
        for j in range(CCONV_K - 1):
            conv = conv + cw_ref[j:j + 1, :] * back[CCONV_K - 1 - j]
        conv = conv + cb_ref[...]
        mu = jnp.mean(conv, axis=-1, keepdims=True)
        cen = conv - mu
        var = jnp.mean(cen * cen, axis=-1, keepdims=True)
        ln = cen * lax.rsqrt(var + EPS) * lg_ref[...] + lb_ref[...]
        y_ref[r0:r0 + rows, W_A + W_B:W_Y] = ln * jax.nn.sigmoid(ln)

    pool_o[...] = ea[HALO + t - 16:HALO + t, :]
    sconv_o[...] = eb[HALO + t - 8:HALO + t, :]
    cconv_o[...] = ec[HALO + t - 32:HALO + t, :]


def _mixer_prompt_kernel(z_ref, pw_ref, ps_ref, sw_ref, cw_ref, cb_ref, lg_ref, lb_ref,
                         y_ref, pool_o, sconv_o, cconv_o, ea, eb, ec, tmp, *, t):
    i = pl.program_id(0)

    @pl.when(i == 0)
    def _():
        zeros = jnp.zeros((HALO, W_A), F32)
        ea[0:HALO, :] = zeros
        eb[0:HALO, :] = zeros
        ec[0:HALO, :] = zeros

    @pl.when(i > 0)
    def _():
        ea[0:HALO, :] = ea[t:t + HALO, :]
        eb[0:HALO, :] = eb[t:t + HALO, :]
        ec[0:HALO, :] = ec[t:t + HALO, :]

    _mixer_compute(t, i * t, z_ref, pw_ref, ps_ref, sw_ref, cw_ref, cb_ref, lg_ref, lb_ref,
                   y_ref, pool_o, sconv_o, cconv_o, ea, eb, ec, tmp)


def _mixer_sample_kernel(z_ref, sp_ref, ss_ref, sc_ref, pw_ref, ps_ref, sw_ref, cw_ref, cb_ref,
                         lg_ref, lb_ref, y_in_ref, y_ref, pool_o, sconv_o, cconv_o, ea, eb, ec, tmp, *, t):
    del y_in_ref
    ea[0:HALO, :] = sp_ref[...]
    eb[0:HALO, :] = ss_ref[...]
    ec[0:HALO, :] = sc_ref[...]
    _mixer_compute(t, None, z_ref, pw_ref, ps_ref, sw_ref, cw_ref, cb_ref, lg_ref, lb_ref,
                   y_ref, pool_o, sconv_o, cconv_o, ea, eb, ec, tmp)


def _mixer_weight_specs():
    return [
        _const_spec((W_A, W_A), (0, 0)),
        _const_spec((1, W_A), (0, 0)),
        _const_spec((SCONV_K, W_B), (0, 0)),
        _const_spec((CCONV_K, W_C), (0, 0)),
        _const_spec((1, W_C), (0, 0)),
        _const_spec((1, W_C), (0, 0)),
        _const_spec((1, W_C), (0, 0)),
    ]


def _mixer_scratch(t):
    return [pltpu.VMEM((HALO + t, W_A), F32), pltpu.VMEM((HALO + t, W_B), F32),
            pltpu.VMEM((HALO + t, W_C), F32),
            pltpu.VMEM((SUBLANES, min(t, MIX_ROWS) + HALO, W_C), F32)]


def _mixer_prompt(cfg, zabc, weights):
    t = cfg.tmix
    const_out = lambda r: pl.BlockSpec((r, W_A), lambda i: (0, 0))
    return pl.pallas_call(
        functools.partial(_mixer_prompt_kernel, t=t),
        grid=(cfg.s_prompt // t,),
        in_specs=[pl.BlockSpec((t, W_ABC), lambda i: (i, 0))] + _mixer_weight_specs(),
        out_specs=[pl.BlockSpec((t, W_Y), lambda i: (i, 0)), const_out(16), const_out(8), const_out(32)],
        out_shape=[
            jax.ShapeDtypeStruct((cfg.n_tok, W_Y), F32),
            jax.ShapeDtypeStruct((16, W_A), F32),
            jax.ShapeDtypeStruct((8, W_B), F32),
            jax.ShapeDtypeStruct((32, W_C), F32),
        ],
        scratch_shapes=_mixer_scratch(t),
        compiler_params=pltpu.CompilerParams(dimension_semantics=("arbitrary",)),
        name="mixer_prompt",
    )(zabc, *weights)


def _mixer_sample(cfg, zabc, st_pool, st_sconv, st_cconv, weights, y_abc):
    t, nb = cfg.t_dec, cfg.n_dec
    row0 = cfg.s_prompt // t
    state_spec = lambda: pl.BlockSpec((None, HALO, W_A), lambda b: (b, 0, 0))
    out_state = lambda r: pl.BlockSpec((None, r, W_A), lambda b: (b, 0, 0))
    return pl.pallas_call(
        functools.partial(_mixer_sample_kernel, t=t),
        grid=(nb,),
        in_specs=[pl.BlockSpec((t, W_ABC), lambda b: (row0 + b, 0)),
                  state_spec(), state_spec(), state_spec()]
                 + _mixer_weight_specs()
                 + [pl.BlockSpec(memory_space=pl.ANY)],
        out_specs=[pl.BlockSpec((t, W_Y), lambda b: (row0 + b, 0)),
                   out_state(16), out_state(8), out_state(32)],
        out_shape=[
            jax.ShapeDtypeStruct((cfg.n_tok, W_Y), F32),
            jax.ShapeDtypeStruct((nb, 16, W_A), F32),
            jax.ShapeDtypeStruct((nb, 8, W_B), F32),
            jax.ShapeDtypeStruct((nb, 32, W_C), F32),
        ],
        scratch_shapes=_mixer_scratch(t),
        input_output_aliases={11: 0},
        compiler_params=pltpu.CompilerParams(dimension_semantics=("arbitrary",)),
        name="mixer_sample",
    )(zabc, st_pool, st_sconv, st_cconv, *weights, y_abc)


def _lambda(lq1, lk1, lq2, lk2, lam_init):
    s1 = jnp.sum(lq1[...] * lk1[...], axis=-1, keepdims=True)
    s2 = jnp.sum(lq2[...] * lk2[...], axis=-1, keepdims=True)
    return jnp.exp(s1) - jnp.exp(s2) + lam_init


def _stack_maps(q):
    lane = lax.broadcasted_iota(jnp.int32, q.shape, 1)
    zero = jnp.zeros_like(q)
    return jnp.concatenate([jnp.where(lane < HEAD_DIM, q, zero), jnp.where(lane >= HEAD_DIM, q, zero)],
                           axis=0)


def _diff_out(acc, l, lam, sg, lam_init, t):
    o = acc[0:t] / l[0:t] - lam * (acc[t:2 * t] / l[t:2 * t])
    return _rms(o, sg) * (1.0 - lam_init)


def _attn_prompt_kernel(qt_ref, k_ref, vt_ref, lq1, lk1, lq2, lk2, sg_ref, o_ref,
                        q2t_ref, s0_ref, s1_ref, mx0_ref, mx1_ref, m_ref, acc_ref, *, tk, lam_init):
    qi = pl.program_id(1)
    nq = 2 * tk
    chan = lax.broadcasted_iota(jnp.int32, (LANES, tk), 0)
    zero = jnp.zeros((LANES, tk), BF16)
    for half in range(2):
        qt = qt_ref[half]
        q2t_ref[:, half * tk:(half + 1) * tk] = jnp.where(chan < HEAD_DIM, qt, zero)
        q2t_ref[:, nq + half * tk:nq + (half + 1) * tk] = jnp.where(chan >= HEAD_DIM, qt, zero)
    m_ref[...] = jnp.full(m_ref.shape, -jnp.inf, F32)
    acc_ref[...] = jnp.zeros(acc_ref.shape, F32)

    def scores(j, s_ref, mx_ref, diagonal=None):
        kb = k_ref[pl.ds(pl.multiple_of(j * tk, tk), tk), :]
        s = _dot(kb, q2t_ref[...])
        if diagonal is not None:
            key = lax.broadcasted_iota(jnp.int32, s.shape, 0)
            col = lax.broadcasted_iota(jnp.int32, s.shape, 1)
            key_chunk = diagonal * (tk // CHUNK) + key // CHUNK
            s = jnp.where(key_chunk <= (col & (nq - 1)) // CHUNK, s, -jnp.inf)
        s_ref[...] = s
        mx_ref[...] = jnp.max(s, axis=0, keepdims=True)

    def consume(j, s_ref, mx_ref):
        m_prev = m_ref[...]
        m_new = jnp.maximum(m_prev, mx_ref[...])
        alpha = jnp.exp2(m_prev - m_new)
        p = jnp.exp2(s_ref[...] - m_new).astype(BF16)
        acc_ref[...] = alpha * acc_ref[...] + _dot(vt_ref[j], p)
        m_ref[...] = m_new

    @pl.when(qi == 0)
    def _():
        scores(0, s0_ref, mx0_ref, 0)

    @pl.when(qi > 0)
    def _():
        scores(0, s0_ref, mx0_ref)

    def pair(i, first_diagonal):
        j = 2 * i
        scores(j + 1, s1_ref, mx1_ref)
        consume(j, s0_ref, mx0_ref)
        scores(j + 2, s0_ref, mx0_ref, first_diagonal)
        consume(j + 1, s1_ref, mx1_ref)

    def full_pair(i, carry):
        pair(i, None)
        return carry

    lax.fori_loop(0, qi - 1, full_pair, 0)

    @pl.when(qi > 0)
    def _():
        pair(qi - 1, 0)

    late = (slice(tk, nq), slice(nq + tk, 2 * nq))
    kb = k_ref[pl.ds(pl.multiple_of((2 * qi + 1) * tk, tk), tk), :]
    s_late = _dot(kb, jnp.concatenate([q2t_ref[:, c] for c in late], axis=1))
    key = lax.broadcasted_iota(jnp.int32, s_late.shape, 0)
    col = lax.broadcasted_iota(jnp.int32, s_late.shape, 1)
    s_late = jnp.where(key // CHUNK <= (col & (tk - 1)) // CHUNK, s_late, -jnp.inf)
    consume(2 * qi, s0_ref, mx0_ref)
    m_prev = jnp.concatenate([m_ref[:, c] for c in late], axis=1)
    m_new = jnp.maximum(m_prev, jnp.max(s_late, axis=0, keepdims=True))
    alpha = jnp.exp2(m_prev - m_new)
    pv = _dot(vt_ref[2 * qi + 1], jnp.exp2(s_late - m_new).astype(BF16))
    for n, c in enumerate(late):
        acc_ref[:, c] = alpha[:, n * tk:(n + 1) * tk] * acc_ref[:, c] + pv[:, n * tk:(n + 1) * tk]

    lam = _lambda(lq1, lk1, lq2, lk2, lam_init)
    acc = acc_ref[0:LANES, :]
    l = acc_ref[LANES:LANES + 1, :]
    ot = acc[:, 0:nq] / l[:, 0:nq] - lam * (acc[:, nq:2 * nq] / l[:, nq:2 * nq])
    o_ref[...] = _rms(ot.T, sg_ref[...]) * (1.0 - lam_init)


def _lam_specs():
    return [_const_spec((1, HEAD_DIM), (0, 0)) for _ in range(4)] + [_const_spec((1, LANES), (0, 0))]


def _attn_prompt(cfg, qt, kb, vt, lam_w, lam_init):
    s, tk = cfg.s_prompt, cfg.tm
    tq = 2 * tk
    assert tq & (tq - 1) == 0 and tk % CHUNK == 0 and s % tq == 0
    return pl.pallas_call(
        functools.partial(_attn_prompt_kernel, tk=tk, lam_init=lam_init),
        grid=(N_HEADS, s // tq),
        in_specs=[
            pl.BlockSpec((None, 2, LANES, tk), lambda h, i: (h, i, 0, 0)),
            pl.BlockSpec((s, LANES), lambda h, i: (0, h)),
            pl.BlockSpec((None, s // tk, VT_ROWS, tk), lambda h, i: (h, 0, 0, 0)),
        ] + _lam_specs(),
        out_specs=pl.BlockSpec((tq, LANES), lambda h, i: (i, h)),
        out_shape=jax.ShapeDtypeStruct((cfg.n_tok, W_D), F32),
        scratch_shapes=[
            pltpu.VMEM((LANES, 2 * tq), BF16),
            pltpu.VMEM((tk, 2 * tq), F32),
            pltpu.VMEM((tk, 2 * tq), F32),
            pltpu.VMEM((1, 2 * tq), F32),
            pltpu.VMEM((1, 2 * tq), F32),
            pltpu.VMEM((1, 2 * tq), F32),
            pltpu.VMEM((VT_ROWS, 2 * tq), F32),
        ],
        compiler_params=pltpu.CompilerParams(
            dimension_semantics=("arbitrary", "arbitrary"), vmem_limit_bytes=52 * MIB),
        name="attn_prompt",
    )(qt, kb, vt, *lam_w)


def _attn_sample_kernel(q_ref, kn_ref, vn_ref, kc_ref, vc_ref, lq1, lk1, lq2, lk2, sg_ref, yd_in_ref,
                        o_ref, *, t, lam_init):
    del yd_in_ref
    lam = _lambda(lq1, lk1, lq2, lk2, lam_init)
    past = vc_ref.shape[0] // N_HEADS
    for h in range(N_HEADS):
        sl = slice(h * LANES, (h + 1) * LANES)
        s_past, s_new = [], []
        for c in range(2):
            ch = slice(h * LANES + c * HEAD_DIM, h * LANES + (c + 1) * HEAD_DIM)
            q = q_ref[:, ch]
            s_past.append(_dot(q, kc_ref[2 * h + c].astype(BF16)))
            s_new.append(_dot_t(q, kn_ref[:, ch]))
        s_past = jnp.concatenate(s_past, axis=0)
        s_new = jnp.concatenate(s_new, axis=0)
        v_past = vc_ref[pl.ds(h, past, stride=N_HEADS), :]
        m = jnp.maximum(jnp.max(s_past, axis=-1, keepdims=True), jnp.max(s_new, axis=-1, keepdims=True))
        p_past = jnp.exp(s_past - m)
        p_new = jnp.exp(s_new - m)
        l = jnp.sum(p_past, axis=-1, keepdims=True) + jnp.sum(p_new, axis=-1, keepdims=True)
        acc = _dot(p_past.astype(BF16), v_past.astype(BF16)) + _dot(p_new.astype(BF16), vn_ref[:, sl])
        o_ref[:, sl] = _diff_out(acc, l, lam, sg_ref[...], lam_init, t)


def _attn_sample(cfg, l, qb, kb, vb, cache_k, cache_v, lam_w, lam_init, yd):
    t, nb = cfg.t_dec, cfg.n_dec
    row0 = cfg.s_prompt // t
    new_rows = lambda: pl.BlockSpec((t, W_D), lambda b: (row0 + b, 0))
    cache = lambda a: pl.BlockSpec((None, None) + a.shape[2:], lambda b: (l, b) + (0,) * (a.ndim - 2))
    return pl.pallas_call(
        functools.partial(_attn_sample_kernel, t=t, lam_init=lam_init),
        grid=(nb,),
        in_specs=[new_rows(), new_rows(), new_rows(), cache(cache_k), cache(cache_v)] + _lam_specs()
                 + [pl.BlockSpec(memory_space=pl.ANY)],
        out_specs=new_rows(),
        out_shape=jax.ShapeDtypeStruct((cfg.n_tok, W_D), F32),
        input_output_aliases={10: 0},
        compiler_params=pltpu.CompilerParams(
            dimension_semantics=("arbitrary",), vmem_limit_bytes=40 * MIB),
        name="attn_sample",
    )(qb, kb, vb, cache_k, cache_v, *lam_w, yd)


def _merge_kernel(x_ref, yabc_ref, yd_ref, g_ref, wg_ref, wpa_ref, wpb_ref, wpc_ref, wpd_ref, wo_ref, o_ref):
    x = x_ref[...]
    h = _rms(x, g_ref[...]).astype(BF16)
    branches = (
        (yabc_ref[:, 0:W_A], wpa_ref),
        (yabc_ref[:, W_A:W_A + W_B], wpb_ref),
        (yabc_ref[:, W_A + W_B:W_Y], wpc_ref),
        (yd_ref[...], wpd_ref),
    )
    merged = None
    for i, (y, wp_ref) in enumerate(branches):
        gate = jax.nn.sigmoid(_dot(h, wg_ref[:, i * D_MODEL:(i + 1) * D_MODEL]))
        term = gate * _dot(y.astype(BF16), wp_ref[...])
        merged = term if merged is None else merged + term
    o_ref[...] = x + _dot(merged.astype(BF16), wo_ref[...])


def _merge(cfg, l, x, y_abc, yd, g, w_gate, wpa, wpb, wpc, wpd, wo):
    n, tm = cfg.n_tok, cfg.tm
    row = lambda w: pl.BlockSpec((tm, w), lambda i: (i, 0))
    return pl.pallas_call(
        _merge_kernel,
        grid=(n // tm,),
        in_specs=[
            row(D_MODEL), row(W_Y), row(W_D),
            _const_spec((1, D_MODEL), (0, 0)),
            _const_spec((None, D_MODEL, 4 * D_MODEL), (l, 0, 0)),
            _const_spec((None, W_A, D_MODEL), (l, 0, 0)),
            _const_spec((None, W_B, D_MODEL), (l, 0, 0)),
            _const_spec((None, W_C, D_MODEL), (l, 0, 0)),
            _const_spec((None, W_D, D_MODEL), (l, 0, 0)),
            _const_spec((None, D_MODEL, D_MODEL), (l, 0, 0)),
        ],
        out_specs=row(D_MODEL),
        out_shape=jax.ShapeDtypeStruct((n, D_MODEL), F32),
        compiler_params=pltpu.CompilerParams(
            dimension_semantics=("arbitrary",), vmem_limit_bytes=48 * MIB),
        name="merge",
    )(x, y_abc, yd, g, w_gate, wpa, wpb, wpc, wpd, wo)


def _rope_tables(cfg):
    half = HEAD_DIM // 2
    inv_freq = ROPE_THETA ** (-jnp.arange(half, dtype=F32) / half)
    pos = jnp.concatenate([jnp.arange(cfg.s_prompt), jnp.tile(cfg.past + jnp.arange(cfg.t_dec), cfg.n_dec)])
    ang = pos.astype(F32)[:, None] * inv_freq[None, :]
    cos, sin = jnp.cos(ang), jnp.sin(ang)
    reps = LANES // HEAD_DIM
    cos_t = jnp.tile(jnp.concatenate([cos, cos], axis=1), (1, reps))
    sin_t = jnp.tile(jnp.concatenate([-sin, sin], axis=1), (1, reps))
    return cos_t, sin_t


def _pad_rows_top(a, rows):
    return jnp.pad(a, ((0, 0), (0, 0), (rows - a.shape[2], 0), (0, 0)))


def _forward(cfg, x_prompt, x_sample, cache_k, cache_v, state_pool, state_sconv, state_cconv,
             g_ffn1, w1_gate, w1_up, w1_down, g_mix, w_in, pool_w, pool_scale, sconv_w,
             cconv_w, cconv_b, ln_g, ln_b, q_norm_g, k_norm_g, lam_q1, lam_k1, lam_q2, lam_k2,
             subln_g, wp_a, wp_b, wp_c, wp_d, w_out, g_ffn2, w2_gate, w2_up, w2_down):
    depth = w_in.shape[0]
    sp, nb, td = cfg.s_prompt, cfg.n_dec, cfg.t_dec
    x = (x_prompt.reshape(sp, D_MODEL), x_sample.reshape(nb * td, D_MODEL))

    bf = lambda w: w.astype(BF16)
    w1g, w1u, w1d = bf(w1_gate), bf(w1_up), bf(w1_down)
    w2g, w2u, w2d = bf(w2_gate), bf(w2_up), bf(w2_down)
    w_proj, w_gate = bf(w_in[:, :, :W_PROJ]), bf(w_in[:, :, W_PROJ:])
    wpa, wpb, wpc, wpd, wo = bf(wp_a), bf(wp_b), bf(wp_c), bf(wp_d), bf(w_out)
    cos_t, sin_t = _rope_tables(cfg)
    ones_bd = jnp.kron(jnp.eye(NORM_W // HEAD_DIM, dtype=F32), jnp.ones((HEAD_DIM, HEAD_DIM), F32)).astype(BF16)
    eye_g = jnp.eye(len(POOL_WINDOWS), dtype=F32)
    ck = jnp.transpose(cache_k, (0, 1, 3, 4, 2))
    cv = cache_v.reshape(depth, nb, cfg.past * N_HEADS, 2 * HEAD_DIM)
    st_pool = _pad_rows_top(state_pool, HALO)
    st_sconv = _pad_rows_top(state_sconv, HALO)
    st_cconv = _pad_rows_top(state_cconv, HALO)
    row = lambda a: a.reshape(1, -1)

    outs = [[] for _ in range(6)]
    caches = None
    for l in range(depth):
        lam_init = 0.8 - 0.6 * math.exp(-0.3 * l)
        x = _ffn(cfg, l, x, row(g_ffn1[l]), w1g, w1u, w1d)
        qg = row(jnp.tile(q_norm_g[l], NORM_W // HEAD_DIM))
        kg = row(jnp.tile(k_norm_g[l], NORM_W // HEAD_DIM))
        zabc, qb, kb, vb, qt, vt, *caches = _inproj(cfg, l, depth, x, row(g_mix[l]), w_proj, qg, kg, cos_t, sin_t,
                                                    ones_bd, caches)

        pw_bd = (eye_g[:, None, :, None] * pool_w[l][:, :, None, :]).reshape(W_A, W_A).astype(BF16)
        mix_w = (pw_bd, row(pool_scale[l]), sconv_w[l], cconv_w[l], row(cconv_b[l]), row(ln_g[l]), row(ln_b[l]))
        y_abc, pool_p, sconv_p, cconv_p = _mixer_prompt(cfg, zabc, mix_w)
        y_abc, pool_s, sconv_s, cconv_s = _mixer_sample(cfg, zabc, st_pool[l], st_sconv[l], st_cconv[l],
                                                        mix_w, y_abc)

        lam_w = (row(lam_q1[l]), row(lam_k1[l]), row(lam_q2[l]), row(lam_k2[l]), row(subln_g[l]))
        yd = _attn_prompt(cfg, qt, kb, vt, lam_w, lam_init)
        yd = _attn_sample(cfg, l, qb, kb, vb, ck, cv, lam_w, lam_init, yd)

        x = _merge(cfg, l, x, y_abc, yd, row(g_mix[l]), w_gate, wpa, wpb, wpc, wpd, wo)
        x = _ffn(cfg, l, x, row(g_ffn2[l]), w2g, w2u, w2d, split_out=(l == depth - 1))

        outs[0].append(pool_p[None, 16 - POOL_STATE:])
        outs[1].append(sconv_p[None, 8 - (SCONV_K - 1):])
        outs[2].append(cconv_p[None, 32 - (CCONV_K - 1):])
        outs[3].append(pool_s[:, 16 - POOL_STATE:])
        outs[4].append(sconv_s[:, 8 - (SCONV_K - 1):])
        outs[5].append(cconv_s[:, 32 - (CCONV_K - 1):])

    kt_all, ks_all, vp_all, vs_all = caches
    y_prompt = x[0].reshape(1, sp, D_MODEL)
    y_sample = x[1].reshape(nb, td, D_MODEL)
    k_prompt = jnp.transpose(kt_all.reshape(depth, 1, 2 * N_HEADS, HEAD_DIM, sp), (0, 1, 4, 2, 3))
    v_prompt = vp_all.reshape(depth, 1, sp, N_HEADS, 2 * HEAD_DIM)
    k_sample = ks_all.reshape(depth, nb, td, 2 * N_HEADS, HEAD_DIM)
    v_sample = vs_all.reshape(depth, nb, td, N_HEADS, 2 * HEAD_DIM)
    st = [jnp.stack(o) for o in outs]
    return (y_prompt, y_sample, k_prompt, v_prompt, st[0], st[1], st[2], k_sample, v_sample, st[3], st[4], st[5])


def kernel(x_prompt, x_sample, cache_k, cache_v, state_pool, state_sconv, state_cconv, g_ffn1, w1_gate, w1_up, w1_down, g_mix, w_in, pool_w, pool_scale, sconv_w, cconv_w, cconv_b, ln_g, ln_b, q_norm_g, k_norm_g, lam_q1, lam_k1, lam_q2, lam_k2, subln_g, wp_a, wp_b, wp_c, wp_d, w_out, g_ffn2, w2_gate, w2_up, w2_down):
    assert x_prompt.shape[0] == 1
    cfg = Cfg(s_prompt=x_prompt.shape[1], n_dec=x_sample.shape[0], t_dec=x_sample.shape[1],
              past=cache_k.shape[2], tm=512, tmix=512)
    return _forward(cfg, x_prompt, x_sample, cache_k, cache_v, state_pool, state_sconv, state_cconv,
                    g_ffn1, w1_gate, w1_up, w1_down, g_mix, w_in, pool_w, pool_scale, sconv_w,
                    cconv_w, cconv_b, ln_g, ln_b, q_norm_g, k_norm_g, lam_q1, lam_k1, lam_q2, lam_k2,
                    subln_g, wp_a, wp_b, wp_c, wp_d, w_out, g_ffn2, w2_gate, w2_up, w2_down)
```

```python
import functools
import math
from typing import NamedTuple

import jax
import jax.numpy as jnp
from jax import lax
from jax.experimental import pallas as pl
from jax.experimental.pallas import tpu as pltpu

F32 = jnp.float32
BF16 = jnp.bfloat16

D_MODEL = 1024
DEPTH = 4
CHUNK = 64
POOL_WINDOWS = (2, 4, 8, 16)
W_A = 256
GA = 64
POOL_STATE = 15
W_B = 256
SCONV_K = 3
W_C = 256
CCONV_K = 31
HEAD_DIM = 64
N_HEADS = 4
W_D = 512
ROPE_THETA = 10000.0
D_FF = 2816
EPS = 1e-6
W_ABC = W_A + 3 * W_B + 2 * W_C
W_PROJ = W_ABC + 3 * W_D
W_Y = W_A + W_B + W_C

LANES = 128
NORM_W = 256
COL_GROUPS = 4
VT_ROWS = LANES + 16
LOG2E = 1.4426950408889634
HALO = 32
MIB = 1024 * 1024


class Cfg(NamedTuple):
    s_prompt: int
    n_dec: int
    t_dec: int
    past: int
    tm: int
    tmix: int

    @property
    def n_tok(self):
        return self.s_prompt + self.n_dec * self.t_dec


def _const_spec(shape, index):
    return pl.BlockSpec(shape, lambda *_: index, pipeline_mode=pl.Buffered(1))


def _rms(x, g):
    ms = jnp.mean(x * x, axis=-1, keepdims=True)
    return x * lax.rsqrt(ms + EPS) * g


def _dot(a, b):
    return jnp.dot(a, b, preferred_element_type=F32)


def _dot_t(a, b):
    return lax.dot_general(a, b, (((1,), (1,)), ((), ())), preferred_element_type=F32)


def _ffn_kernel(*refs, n_prompt_tiles, split_in, split_out):
    refs = list(refs)
    is_prompt = pl.program_id(0) < n_prompt_tiles
    if split_in:
        xp_ref, xs_ref = refs[0:2]
        del refs[0:2]
        x = jnp.where(is_prompt, xp_ref[...], xs_ref[...])
    else:
        x = refs.pop(0)[...]
    g_ref, wg_ref, wu_ref, wd_ref = refs[0:4]
    h = _rms(x, g_ref[...]).astype(BF16)
    a = _dot(h, wg_ref[...])
    u = _dot(h, wu_ref[...])
    act = (a * jax.nn.sigmoid(a) * u).astype(BF16)
    half_step = 0.5 * _dot(act, wd_ref[...])
    if split_out:
        op_ref, os_ref = refs[4:6]

        @pl.when(is_prompt)
        def _():
            op_ref[...] = x + half_step

        @pl.when(jnp.logical_not(is_prompt))
        def _():
            os_ref[...] = x + half_step
    else:
        refs[4][...] = x + half_step


def _ffn(cfg, l, x, g, wg, wu, wd, split_out=False):
    n, tm, sp = cfg.n_tok, cfg.tm, cfg.s_prompt
    assert sp % tm == 0 and (n - sp) % tm == 0
    npt = sp // tm
    split_in = isinstance(x, tuple)
    rows = pl.BlockSpec((tm, D_MODEL), lambda i: (i, 0))
    prompt_rows = pl.BlockSpec((tm, D_MODEL), lambda i: (jnp.minimum(i, npt - 1), 0))
    sample_rows = pl.BlockSpec((tm, D_MODEL), lambda i: (jnp.maximum(i - npt, 0), 0))
    whole = jax.ShapeDtypeStruct((n, D_MODEL), F32)
    halves = [jax.ShapeDtypeStruct((sp, D_MODEL), F32), jax.ShapeDtypeStruct((n - sp, D_MODEL), F32)]
    return pl.pallas_call(
        functools.partial(_ffn_kernel, n_prompt_tiles=npt, split_in=split_in, split_out=split_out),
        grid=(n // tm,),
        in_specs=([prompt_rows, sample_rows] if split_in else [rows]) + [
            _const_spec((1, D_MODEL), (0, 0)),
            _const_spec((None, D_MODEL, D_FF), (l, 0, 0)),
            _const_spec((None, D_MODEL, D_FF), (l, 0, 0)),
            _const_spec((None, D_FF, D_MODEL), (l, 0, 0)),
        ],
        out_specs=[prompt_rows, sample_rows] if split_out else rows,
        out_shape=halves if split_out else whole,
        compiler_params=pltpu.CompilerParams(
            dimension_semantics=("arbitrary",), vmem_limit_bytes=52 * MIB),
        name="ffn",
    )(*(x if split_in else (x,)), g, wg, wu, wd)


def _inproj_kernel(x_ref, g_ref, w_ref, qg_ref, kg_ref, cos_ref, sin_ref, ones_ref, *rest, n_prompt_tiles):
    zabc_ref, qb_ref, kb_ref, vb_ref, qt_ref, vt_ref, kt_ref, ks_ref, vp_ref, vs_ref, zqkv_ref = rest[-11:]
    is_prompt = pl.program_id(0) < n_prompt_tiles
    h = _rms(x_ref[...], g_ref[...]).astype(BF16)
    assert W_ABC % (2 * NORM_W) == 0 and W_ABC // (2 * NORM_W) <= W_D // LANES
    cos = cos_ref[...]
    sin = sin_ref[...]
    ones = ones_ref[...]
    lane = lax.broadcasted_iota(jnp.int32, cos.shape, 1)
    first_half = (lane & (HEAD_DIM // 2)) == 0

    def head_norm(z, g):
        ss = z * z
        hi = ss.astype(BF16)
        lo = (ss - hi.astype(F32)).astype(BF16)
        tot = _dot(hi, ones) + _dot(lo, ones)
        return z * lax.rsqrt(tot * (1.0 / HEAD_DIM) + EPS) * g

    def rope(y):
        half = HEAD_DIM // 2
        partner = jnp.where(first_half, pltpu.roll(y, LANES - half, 1), pltpu.roll(y, half, 1))
        return y * cos + partner * sin

    nw = ones.shape[0]
    abc_w = 2 * NORM_W
    for c in range(W_D // LANES):
        if c * abc_w < W_ABC:
            zabc_ref[:, c * abc_w:(c + 1) * abc_w] = _dot(h, w_ref[:, c * abc_w:(c + 1) * abc_w])
        sl = slice(c * LANES, (c + 1) * LANES)
        if (c * LANES) % nw == 0:
            for part in range(3):
                lo = part * W_D + c * LANES
                zqkv_ref[:, lo:lo + nw] = _dot(h, w_ref[:, W_ABC + lo:W_ABC + lo + nw])
            yq = head_norm(zqkv_ref[:, c * LANES:c * LANES + nw], qg_ref[...])
            yk = head_norm(zqkv_ref[:, W_D + c * LANES:W_D + c * LANES + nw], kg_ref[...])
        off = (c * LANES) % nw
        q = rope(yq[:, off:off + LANES]) * (HEAD_DIM ** -0.5)
        qb_ref[:, sl] = q.astype(BF16)
        qt_ref[c] = (q * LOG2E).T.astype(BF16)
        k = rope(yk[:, off:off + LANES])
        kb_ref[:, sl] = k.astype(BF16)
        zqkv_ref[:, W_D + c * LANES:W_D + (c + 1) * LANES] = k
        v = zqkv_ref[:, 2 * W_D + c * LANES:2 * W_D + (c + 1) * LANES]
        vb_ref[:, sl] = v.astype(BF16)
        vt_ref[c, 0:LANES, :] = v.T.astype(BF16)
        vt_ref[c, LANES:VT_ROWS, :] = jnp.ones((VT_ROWS - LANES, v.shape[0]), BF16)

    @pl.when(is_prompt)
    def _():
        for c in range(W_D // LANES):
            kt = zqkv_ref[:, W_D + c * LANES:W_D + (c + 1) * LANES].T
            kt_ref[2 * c] = kt[0:HEAD_DIM]
            kt_ref[2 * c + 1] = kt[HEAD_DIM:LANES]
        for hd in range(N_HEADS):
            vp_ref[pl.ds(hd, kt_ref.shape[-1], stride=N_HEADS), :] = (
                zqkv_ref[:, 2 * W_D + hd * LANES:2 * W_D + (hd + 1) * LANES])

    @pl.when(jnp.logical_not(is_prompt))
    def _():
        ks_ref[...] = zqkv_ref[:, W_D:2 * W_D]
        vs_ref[...] = zqkv_ref[:, 2 * W_D:3 * W_D]


def _inproj(cfg, l, depth, x, g, w_in, qg, kg, cos_t, sin_t, ones_bd, caches):
    n, tm, sp = cfg.n_tok, cfg.tm, cfg.s_prompt
    n_dec = n - sp
    assert sp % tm == 0 and n_dec % tm == 0
    npt = sp // tm
    row = lambda w: pl.BlockSpec((tm, w), lambda i: (i, 0))
    prompt_rows = pl.BlockSpec((None, tm * N_HEADS, LANES), lambda i: (l, jnp.minimum(i, npt - 1), 0))
    sample_rows = pl.BlockSpec((None, tm, W_D), lambda i: (l, jnp.maximum(i - npt, 0), 0))
    n_in = 8
    aliased = [] if caches is None else list(caches)
    return pl.pallas_call(
        functools.partial(_inproj_kernel, n_prompt_tiles=npt),
        grid=(n // tm,),
        in_specs=[
            row(D_MODEL),
            _const_spec((1, D_MODEL), (0, 0)),
            _const_spec((None, D_MODEL, W_PROJ), (l, 0, 0)),
            _const_spec((1, NORM_W), (0, 0)),
            _const_spec((1, NORM_W), (0, 0)),
            row(LANES),
            row(LANES),
            _const_spec((NORM_W, NORM_W), (0, 0)),
        ] + [pl.BlockSpec(memory_space=pl.ANY) for _ in aliased],
        out_specs=[row(W_ABC), row(W_D), row(W_D), row(W_D),
                   pl.BlockSpec((N_HEADS, None, LANES, tm), lambda i: (0, i, 0, 0)),
                   pl.BlockSpec((N_HEADS, None, VT_ROWS, tm), lambda i: (0, i, 0, 0)),
                   pl.BlockSpec((None, 2 * N_HEADS, HEAD_DIM, tm), lambda i: (l, 0, 0, jnp.minimum(i, npt - 1))),
                   sample_rows, prompt_rows, sample_rows],
        out_shape=[
            jax.ShapeDtypeStruct((n, W_ABC), F32),
            jax.ShapeDtypeStruct((n, W_D), BF16),
            jax.ShapeDtypeStruct((n, W_D), BF16),
            jax.ShapeDtypeStruct((n, W_D), BF16),
            jax.ShapeDtypeStruct((N_HEADS, n // tm, LANES, tm), BF16),
            jax.ShapeDtypeStruct((N_HEADS, n // tm, VT_ROWS, tm), BF16),
            jax.ShapeDtypeStruct((depth, 2 * N_HEADS, HEAD_DIM, sp), F32),
            jax.ShapeDtypeStruct((depth, n_dec, W_D), F32),
            jax.ShapeDtypeStruct((depth, sp * N_HEADS, LANES), F32),
            jax.ShapeDtypeStruct((depth, n_dec, W_D), F32),
        ],
        scratch_shapes=[pltpu.VMEM((tm, 3 * W_D), F32)],
        input_output_aliases={n_in + j: 6 + j for j in range(len(aliased))},
        compiler_params=pltpu.CompilerParams(
            dimension_semantics=("arbitrary",), vmem_limit_bytes=44 * MIB),
        name="inproj",
    )(x, g, w_in, qg, kg, cos_t, sin_t, ones_bd, *aliased)


MIX_ROWS = 128


SUBLANES = 8


def _shifted_rows(ref, base, rows, depth, tmp):
    out = {}
    for r in range(SUBLANES):
        js = [j for j in range(1, depth + 1) if (-j) % SUBLANES == r]
        if not js:
            continue
        start, length = base - max(js), max(js) - min(js) + rows
        tmp[r, 0:length, :] = ref[start:start + length, :]
        for j in js:
            out[j] = tmp[r, max(js) - j:max(js) - j + rows, :]
    return out


def _mixer_compute(t, pos0, z_ref, pw_ref, ps_ref, sw_ref, cw_ref, cb_ref, lg_ref, lb_ref,
                   y_ref, pool_o, sconv_o, cconv_o, ea, eb, ec, tmp):
    u = z_ref[:, 0:W_A]
    ea[HALO:HALO + t, :] = u
    eb[HALO:HALO + t, :] = z_ref[:, W_A + W_B:W_A + 2 * W_B] * z_ref[:, W_A + 2 * W_B:W_A + 3 * W_B]
    zc = z_ref[:, W_A + 3 * W_B:W_A + 3 * W_B + W_C]
    ec[HALO:HALO + t, :] = zc * jax.nn.sigmoid(z_ref[:, W_A + 3 * W_B + W_C:W_ABC])

    rows = min(t, MIX_ROWS)
    for r0 in range(0, t, rows):
        base = HALO + r0
        lane = lax.broadcasted_iota(jnp.int32, (rows, W_A), 1)
        cur = ea[base:base + rows, :]
        back = _shifted_rows(ea, base, rows, max(POOL_WINDOWS) - 1, tmp)
        acc = cur
        sums = {}
        for j in range(1, max(POOL_WINDOWS)):
            acc = acc + back[j]
            if j + 1 in POOL_WINDOWS:
                sums[j + 1] = acc
        tot = sums[POOL_WINDOWS[-1]]
        win = jnp.full((rows, W_A), float(POOL_WINDOWS[-1]), F32)
        for gi in range(len(POOL_WINDOWS) - 2, -1, -1):
            in_group = lane < (gi + 1) * GA
            tot = jnp.where(in_group, sums[POOL_WINDOWS[gi]], tot)
            win = jnp.where(in_group, float(POOL_WINDOWS[gi]), win)
        if pos0 is None:
            cnt = win
        else:
            pos1 = (pos0 + r0 + 1 + lax.broadcasted_iota(jnp.int32, (rows, W_A), 0)).astype(F32)
            cnt = jnp.minimum(pos1, win)
        d = (tot / cnt - cur).astype(BF16)
        y_ref[r0:r0 + rows, 0:W_A] = _dot(d, pw_ref[...]) * ps_ref[...]

        conv = sw_ref[SCONV_K - 1:SCONV_K, :] * eb[base:base + rows, :]
        for j in range(SCONV_K - 1):
            off = base - (SCONV_K - 1) + j
            conv = conv + sw_ref[j:j + 1, :] * eb[off:off + rows, :]
        y_ref[r0:r0 + rows, W_A:W_A + W_B] = z_ref[r0:r0 + rows, W_A:W_A + W_B] * conv

        back = _shifted_rows(ec, base, rows, CCONV_K - 1, tmp)
        conv = cw_ref[CCONV_K - 1:CCONV_K, :] * ec[base:base + rows, :]
        for j in range(CCONV_K - 1):
            conv = conv + cw_ref[j:j + 1, :] * back[CCONV_K - 1 - j]
        conv = conv + cb_ref[...]
        mu = jnp.mean(conv, axis=-1, keepdims=True)
        cen = conv - mu
        var = jnp.mean(cen * cen, axis=-1, keepdims=True)
        ln = cen * lax.rsqrt(var + EPS) * lg_ref[...] + lb_ref[...]
        y_ref[r0:r0 + rows, W_A + W_B:W_Y] = ln * jax.nn.sigmoid(ln)

    pool_o[...] = ea[HALO + t - 16:HALO + t, :]
    sconv_o[...] = eb[HALO + t - 8:HALO + t, :]
    cconv_o[...] = ec[HALO + t - 32:HALO + t, :]


def _mixer_prompt_kernel(z_ref, pw_ref, ps_ref, sw_ref, cw_ref, cb_ref, lg_ref, lb_ref,
                         y_ref, pool_o, sconv_o, cconv_o, ea, eb, ec, tmp, *, t):
    i = pl.program_id(0)

    @pl.when(i == 0)
    def _():
        zeros = jnp.zeros((HALO, W_A), F32)
        ea[0:HALO, :] = zeros
        eb[0:HALO, :] = zeros
        ec[0:HALO, :] = zeros

    @pl.when(i > 0)
    def _():
        ea[0:HALO, :] = ea[t:t + HALO, :]
        eb[0:HALO, :] = eb[t:t + HALO, :]
        ec[0:HALO, :] = ec[t:t + HALO, :]

    _mixer_compute(t, i * t, z_ref, pw_ref, ps_ref, sw_ref, cw_ref, cb_ref, lg_ref, lb_ref,
                   y_ref, pool_o, sconv_o, cconv_o, ea, eb, ec, tmp)


def _mixer_sample_kernel(z_ref, sp_ref, ss_ref, sc_ref, pw_ref, ps_ref, sw_ref, cw_ref, cb_ref,
                         lg_ref, lb_ref, y_in_ref, y_ref, pool_o, sconv_o, cconv_o, ea, eb, ec, tmp, *, t):
    del y_in_ref
    ea[0:HALO, :] = sp_ref[...]
    eb[0:HALO, :] = ss_ref[...]
    ec[0:HALO, :] = sc_ref[...]
    _mixer_compute(t, None, z_ref, pw_ref, ps_ref, sw_ref, cw_ref, cb_ref, lg_ref, lb_ref,
                   y_ref, pool_o, sconv_o, cconv_o, ea, eb, ec, tmp)


def _mixer_weight_specs():
    return [
        _const_spec((W_A, W_A), (0, 0)),
        _const_spec((1, W_A), (0, 0)),
        _const_spec((SCONV_K, W_B), (0, 0)),
        _const_spec((CCONV_K, W_C), (0, 0)),
        _const_spec((1, W_C), (0, 0)),
        _const_spec((1, W_C), (0, 0)),
        _const_spec((1, W_C), (0, 0)),
    ]


def _mixer_scratch(t):
    return [pltpu.VMEM((HALO + t, W_A), F32), pltpu.VMEM((HALO + t, W_B), F32),
            pltpu.VMEM((HALO + t, W_C), F32),
            pltpu.VMEM((SUBLANES, min(t, MIX_ROWS) + HALO, W_C), F32)]


def _mixer_prompt(cfg, zabc, weights):
    t = cfg.tmix
    const_out = lambda r: pl.BlockSpec((r, W_A), lambda i: (0, 0))
    return pl.pallas_call(
        functools.partial(_mixer_prompt_kernel, t=t),
        grid=(cfg.s_prompt // t,),
        in_specs=[pl.BlockSpec((t, W_ABC), lambda i: (i, 0))] + _mixer_weight_specs(),
        out_specs=[pl.BlockSpec((t, W_Y), lambda i: (i, 0)), const_out(16), const_out(8), const_out(32)],
        out_shape=[
            jax.ShapeDtypeStruct((cfg.n_tok, W_Y), F32),
            jax.ShapeDtypeStruct((16, W_A), F32),
            jax.ShapeDtypeStruct((8, W_B), F32),
            jax.ShapeDtypeStruct((32, W_C), F32),
        ],
        scratch_shapes=_mixer_scratch(t),
        compiler_params=pltpu.CompilerParams(dimension_semantics=("arbitrary",)),
        name="mixer_prompt",
    )(zabc, *weights)


def _mixer_sample(cfg, zabc, st_pool, st_sconv, st_cconv, weights, y_abc):
    t, nb = cfg.t_dec, cfg.n_dec
    row0 = cfg.s_prompt // t
    state_spec = lambda: pl.BlockSpec((None, HALO, W_A), lambda b: (b, 0, 0))
    out_state = lambda r: pl.BlockSpec((None, r, W_A), lambda b: (b, 0, 0))
    return pl.pallas_call(
        functools.partial(_mixer_sample_kernel, t=t),
        grid=(nb,),
        in_specs=[pl.BlockSpec((t, W_ABC), lambda b: (row0 + b, 0)),
                  state_spec(), state_spec(), state_spec()]
                 + _mixer_weight_specs()
                 + [pl.BlockSpec(memory_space=pl.ANY)],
        out_specs=[pl.BlockSpec((t, W_Y), lambda b: (row0 + b, 0)),
                   out_state(16), out_state(8), out_state(32)],
        out_shape=[
            jax.ShapeDtypeStruct((cfg.n_tok, W_Y), F32),
            jax.ShapeDtypeStruct((nb, 16, W_A), F32),
            jax.ShapeDtypeStruct((nb, 8, W_B), F32),
            jax.ShapeDtypeStruct((nb, 32, W_C), F32),
        ],
        scratch_shapes=_mixer_scratch(t),
        input_output_aliases={11: 0},
        compiler_params=pltpu.CompilerParams(dimension_semantics=("arbitrary",)),
        name="mixer_sample",
    )(zabc, st_pool, st_sconv, st_cconv, *weights, y_abc)


def _lambda(lq1, lk1, lq2, lk2, lam_init):
    s1 = jnp.sum(lq1[...] * lk1[...], axis=-1, keepdims=True)
    s2 = jnp.sum(lq2[...] * lk2[...], axis=-1, keepdims=True)
    return jnp.exp(s1) - jnp.exp(s2) + lam_init


def _stack_maps(q):
    lane = lax.broadcasted_iota(jnp.int32, q.shape, 1)
    zero = jnp.zeros_like(q)
    return jnp.concatenate([jnp.where(lane < HEAD_DIM, q, zero), jnp.where(lane >= HEAD_DIM, q, zero)],
                           axis=0)


def _diff_out(acc, l, lam, sg, lam_init, t):
    o = acc[0:t] / l[0:t] - lam * (acc[t:2 * t] / l[t:2 * t])
    return _rms(o, sg) * (1.0 - lam_init)


def _attn_prompt_kernel(qt_ref, k_ref, vt_ref, lq1, lk1, lq2, lk2, sg_ref, o_ref,
                        q2t_ref, s0_ref, s1_ref, mx0_ref, mx1_ref, m_ref, acc_ref, *, tk, lam_init):
    qi = pl.program_id(1)
    nq = 2 * tk
    chan = lax.broadcasted_iota(jnp.int32, (LANES, tk), 0)
    zero = jnp.zeros((LANES, tk), BF16)
    for half in range(2):
        qt = qt_ref[half]
        q2t_ref[:, half * tk:(half + 1) * tk] = jnp.where(chan < HEAD_DIM, qt, zero)
        q2t_ref[:, nq + half * tk:nq + (half + 1) * tk] = jnp.where(chan >= HEAD_DIM, qt, zero)
    m_ref[...] = jnp.full(m_ref.shape, -jnp.inf, F32)
    acc_ref[...] = jnp.zeros(acc_ref.shape, F32)

    ncols = 2 * nq // COL_GROUPS

    def scores(j, s_ref, mx_ref, diagonal=None, group=None):
        cols = slice(None) if group is None else slice(group * ncols, (group + 1) * ncols)
        kb = k_ref[pl.ds(pl.multiple_of(j * tk, tk), tk), :]
        s = _dot(kb, q2t_ref[:, cols])
        if diagonal is not None:
            key = lax.broadcasted_iota(jnp.int32, s.shape, 0)
            col = lax.broadcasted_iota(jnp.int32, s.shape, 1) + (0 if group is None else group * ncols)
            key_chunk = diagonal * (tk // CHUNK) + key // CHUNK
            s = jnp.where(key_chunk <= (col & (nq - 1)) // CHUNK, s, -jnp.inf)
        s_ref[:, cols] = s
        mx_ref[:, cols] = jnp.max(s, axis=0, keepdims=True)

    def consume(j, s_ref, mx_ref, group=None):
        cols = slice(None) if group is None else slice(group * ncols, (group + 1) * ncols)
        m_prev = m_ref[:, cols]
        m_new = jnp.maximum(m_prev, mx_ref[:, cols])
        alpha = jnp.exp2(m_prev - m_new)
        p = jnp.exp2(s_ref[:, cols] - m_new).astype(BF16)
        acc_ref[:, cols] = alpha * acc_ref[:, cols] + _dot(vt_ref[j], p)
        m_ref[:, cols] = m_new

    @pl.when(qi == 0)
    def _():
        scores(0, s0_ref, mx0_ref, 0)

    @pl.when(qi > 0)
    def _():
        scores(0, s0_ref, mx0_ref)

    def pair(i, first_diagonal):
        j = 2 * i
        for g in range(COL_GROUPS):
            scores(j + 1, s1_ref, mx1_ref, None, g)
            consume(j, s0_ref, mx0_ref, g)
        for g in range(COL_GROUPS):
            scores(j + 2, s0_ref, mx0_ref, first_diagonal, g)
            consume(j + 1, s1_ref, mx1_ref, g)

    def full_pair(i, carry):
        pair(i, None)
        return carry

    lax.fori_loop(0, qi - 1, full_pair, 0)

    @pl.when(qi > 0)
    def _():
        pair(qi - 1, 0)

    assert tk % ncols == 0
    late = [g for g in range(COL_GROUPS) if (g * ncols) % nq >= tk]
    for g in range(COL_GROUPS):
        if g in late:
            scores(2 * qi + 1, s1_ref, mx1_ref, 1, g)
        consume(2 * qi, s0_ref, mx0_ref, g)
    for g in late:
        consume(2 * qi + 1, s1_ref, mx1_ref, g)

    lam = _lambda(lq1, lk1, lq2, lk2, lam_init)
    acc = acc_ref[0:LANES, :]
    l = acc_ref[LANES:LANES + 1, :]
    ot = acc[:, 0:nq] / l[:, 0:nq] - lam * (acc[:, nq:2 * nq] / l[:, nq:2 * nq])
    o_ref[...] = _rms(ot.T, sg_ref[...]) * (1.0 - lam_init)


def _lam_specs():
    return [_const_spec((1, HEAD_DIM), (0, 0)) for _ in range(4)] + [_const_spec((1, LANES), (0, 0))]


def _attn_prompt(cfg, qt, kb, vt, lam_w, lam_init):
    s, tk = cfg.s_prompt, cfg.tm
    tq = 2 * tk
    assert tq & (tq - 1) == 0 and tk % CHUNK == 0 and s % tq == 0
    return pl.pallas_call(
        functools.partial(_attn_prompt_kernel, tk=tk, lam_init=lam_init),
        grid=(N_HEADS, s // tq),
        in_specs=[
            pl.BlockSpec((None, 2, LANES, tk), lambda h, i: (h, i, 0, 0)),
            pl.BlockSpec((s, LANES), lambda h, i: (0, h)),
            pl.BlockSpec((None, s // tk, VT_ROWS, tk), lambda h, i: (h, 0, 0, 0)),
        ] + _lam_specs(),
        out_specs=pl.BlockSpec((tq, LANES), lambda h, i: (i, h)),
        out_shape=jax.ShapeDtypeStruct((cfg.n_tok, W_D), F32),
        scratch_shapes=[
            pltpu.VMEM((LANES, 2 * tq), BF16),
            pltpu.VMEM((tk, 2 * tq), F32),
            pltpu.VMEM((tk, 2 * tq), F32),
            pltpu.VMEM((1, 2 * tq), F32),
            pltpu.VMEM((1, 2 * tq), F32),
            pltpu.VMEM((1, 2 * tq), F32),
            pltpu.VMEM((VT_ROWS, 2 * tq), F32),
        ],
        compiler_params=pltpu.CompilerParams(
            dimension_semantics=("arbitrary", "arbitrary"), vmem_limit_bytes=52 * MIB),
        name="attn_prompt",
    )(qt, kb, vt, *lam_w)


def _attn_sample_kernel(q_ref, kn_ref, vn_ref, kc_ref, vc_ref, lq1, lk1, lq2, lk2, sg_ref, yd_in_ref,
                        o_ref, *, t, lam_init):
    del yd_in_ref
    lam = _lambda(lq1, lk1, lq2, lk2, lam_init)
    past = vc_ref.shape[0] // N_HEADS
    for h in range(N_HEADS):
        sl = slice(h * LANES, (h + 1) * LANES)
        s_past, s_new = [], []
        for c in range(2):
            ch = slice(h * LANES + c * HEAD_DIM, h * LANES + (c + 1) * HEAD_DIM)
            q = q_ref[:, ch]
            s_past.append(_dot(q, kc_ref[2 * h + c].astype(BF16)))
            s_new.append(_dot_t(q, kn_ref[:, ch]))
        s_past = jnp.concatenate(s_past, axis=0)
        s_new = jnp.concatenate(s_new, axis=0)
        v_past = vc_ref[pl.ds(h, past, stride=N_HEADS), :]
        m = jnp.maximum(jnp.max(s_past, axis=-1, keepdims=True), jnp.max(s_new, axis=-1, keepdims=True))
        p_past = jnp.exp(s_past - m)
        p_new = jnp.exp(s_new - m)
        l = jnp.sum(p_past, axis=-1, keepdims=True) + jnp.sum(p_new, axis=-1, keepdims=True)
        acc = _dot(p_past.astype(BF16), v_past.astype(BF16)) + _dot(p_new.astype(BF16), vn_ref[:, sl])
        o_ref[:, sl] = _diff_out(acc, l, lam, sg_ref[...], lam_init, t)


def _attn_sample(cfg, l, qb, kb, vb, cache_k, cache_v, lam_w, lam_init, yd):
    t, nb = cfg.t_dec, cfg.n_dec
    row0 = cfg.s_prompt // t
    new_rows = lambda: pl.BlockSpec((t, W_D), lambda b: (row0 + b, 0))
    cache = lambda a: pl.BlockSpec((None, None) + a.shape[2:], lambda b: (l, b) + (0,) * (a.ndim - 2))
    return pl.pallas_call(
        functools.partial(_attn_sample_kernel, t=t, lam_init=lam_init),
        grid=(nb,),
        in_specs=[new_rows(), new_rows(), new_rows(), cache(cache_k), cache(cache_v)] + _lam_specs()
                 + [pl.BlockSpec(memory_space=pl.ANY)],
        out_specs=new_rows(),
        out_shape=jax.ShapeDtypeStruct((cfg.n_tok, W_D), F32),
        input_output_aliases={10: 0},
        compiler_params=pltpu.CompilerParams(
            dimension_semantics=("arbitrary",), vmem_limit_bytes=40 * MIB),
        name="attn_sample",
    )(qb, kb, vb, cache_k, cache_v, *lam_w, yd)


def _merge_kernel(x_ref, yabc_ref, yd_ref, g_ref, wg_ref, wpa_ref, wpb_ref, wpc_ref, wpd_ref, wo_ref, o_ref):
    x = x_ref[...]
    h = _rms(x, g_ref[...]).astype(BF16)
    branches = (
        (yabc_ref[:, 0:W_A], wpa_ref),
        (yabc_ref[:, W_A:W_A + W_B], wpb_ref),
        (yabc_ref[:, W_A + W_B:W_Y], wpc_ref),
        (yd_ref[...], wpd_ref),
    )
    merged = None
    for i, (y, wp_ref) in enumerate(branches):
        gate = jax.nn.sigmoid(_dot(h, wg_ref[:, i * D_MODEL:(i + 1) * D_MODEL]))
        term = gate * _dot(y.astype(BF16), wp_ref[...])
        merged = term if merged is None else merged + term
    o_ref[...] = x + _dot(merged.astype(BF16), wo_ref[...])


def _merge(cfg, l, x, y_abc, yd, g, w_gate, wpa, wpb, wpc, wpd, wo):
    n, tm = cfg.n_tok, cfg.tm
    row = lambda w: pl.BlockSpec((tm, w), lambda i: (i, 0))
    return pl.pallas_call(
        _merge_kernel,
        grid=(n // tm,),
        in_specs=[
            row(D_MODEL), row(W_Y), row(W_D),
            _const_spec((1, D_MODEL), (0, 0)),
            _const_spec((None, D_MODEL, 4 * D_MODEL), (l, 0, 0)),
            _const_spec((None, W_A, D_MODEL), (l, 0, 0)),
            _const_spec((None, W_B, D_MODEL), (l, 0, 0)),
            _const_spec((None, W_C, D_MODEL), (l, 0, 0)),
            _const_spec((None, W_D, D_MODEL), (l, 0, 0)),
            _const_spec((None, D_MODEL, D_MODEL), (l, 0, 0)),
        ],
        out_specs=row(D_MODEL),
        out_shape=jax.ShapeDtypeStruct((n, D_MODEL), F32),
        compiler_params=pltpu.CompilerParams(
            dimension_semantics=("arbitrary",), vmem_limit_bytes=48 * MIB),
        name="merge",
    )(x, y_abc, yd, g, w_gate, wpa, wpb, wpc, wpd, wo)


def _rope_tables(cfg):
    half = HEAD_DIM // 2
    inv_freq = ROPE_THETA ** (-jnp.arange(half, dtype=F32) / half)
    pos = jnp.concatenate([jnp.arange(cfg.s_prompt), jnp.tile(cfg.past + jnp.arange(cfg.t_dec), cfg.n_dec)])
    ang = pos.astype(F32)[:, None] * inv_freq[None, :]
    cos, sin = jnp.cos(ang), jnp.sin(ang)
    reps = LANES // HEAD_DIM
    cos_t = jnp.tile(jnp.concatenate([cos, cos], axis=1), (1, reps))
    sin_t = jnp.tile(jnp.concatenate([-sin, sin], axis=1), (1, reps))
    return cos_t, sin_t


def _pad_rows_top(a, rows):
    return jnp.pad(a, ((0, 0), (0, 0), (rows - a.shape[2], 0), (0, 0)))


def _forward(cfg, x_prompt, x_sample, cache_k, cache_v, state_pool, state_sconv, state_cconv,
             g_ffn1, w1_gate, w1_up, w1_down, g_mix, w_in, pool_w, pool_scale, sconv_w,
             cconv_w, cconv_b, ln_g, ln_b, q_norm_g, k_norm_g, lam_q1, lam_k1, lam_q2, lam_k2,
             subln_g, wp_a, wp_b, wp_c, wp_d, w_out, g_ffn2, w2_gate, w2_up, w2_down):
    depth = w_in.shape[0]
    sp, nb, td = cfg.s_prompt, cfg.n_dec, cfg.t_dec
    x = (x_prompt.reshape(sp, D_MODEL), x_sample.reshape(nb * td, D_MODEL))

    bf = lambda w: w.astype(BF16)
    w1g, w1u, w1d = bf(w1_gate), bf(w1_up), bf(w1_down)
    w2g, w2u, w2d = bf(w2_gate), bf(w2_up), bf(w2_down)
    w_proj, w_gate = bf(w_in[:, :, :W_PROJ]), bf(w_in[:, :, W_PROJ:])
    wpa, wpb, wpc, wpd, wo = bf(wp_a), bf(wp_b), bf(wp_c), bf(wp_d), bf(w_out)
    cos_t, sin_t = _rope_tables(cfg)
    ones_bd = jnp.kron(jnp.eye(NORM_W // HEAD_DIM, dtype=F32), jnp.ones((HEAD_DIM, HEAD_DIM), F32)).astype(BF16)
    eye_g = jnp.eye(len(POOL_WINDOWS), dtype=F32)
    ck = jnp.transpose(cache_k, (0, 1, 3, 4, 2))
    cv = cache_v.reshape(depth, nb, cfg.past * N_HEADS, 2 * HEAD_DIM)
    st_pool = _pad_rows_top(state_pool, HALO)
    st_sconv = _pad_rows_top(state_sconv, HALO)
    st_cconv = _pad_rows_top(state_cconv, HALO)
    row = lambda a: a.reshape(1, -1)

    outs = [[] for _ in range(6)]
    caches = None
    for l in range(depth):
        lam_init = 0.8 - 0.6 * math.exp(-0.3 * l)
        x = _ffn(cfg, l, x, row(g_ffn1[l]), w1g, w1u, w1d)
        qg = row(jnp.tile(q_norm_g[l], NORM_W // HEAD_DIM))
        kg = row(jnp.tile(k_norm_g[l], NORM_W // HEAD_DIM))
        zabc, qb, kb, vb, qt, vt, *caches = _inproj(cfg, l, depth, x, row(g_mix[l]), w_proj, qg, kg, cos_t, sin_t,
                                                    ones_bd, caches)

        pw_bd = (eye_g[:, None, :, None] * pool_w[l][:, :, None, :]).reshape(W_A, W_A).astype(BF16)
        mix_w = (pw_bd, row(pool_scale[l]), sconv_w[l], cconv_w[l], row(cconv_b[l]), row(ln_g[l]), row(ln_b[l]))
        y_abc, pool_p, sconv_p, cconv_p = _mixer_prompt(cfg, zabc, mix_w)
        y_abc, pool_s, sconv_s, cconv_s = _mixer_sample(cfg, zabc, st_pool[l], st_sconv[l], st_cconv[l],
                                                        mix_w, y_abc)

        lam_w = (row(lam_q1[l]), row(lam_k1[l]), row(lam_q2[l]), row(lam_k2[l]), row(subln_g[l]))
        yd = _attn_prompt(cfg, qt, kb, vt, lam_w, lam_init)
        yd = _attn_sample(cfg, l, qb, kb, vb, ck, cv, lam_w, lam_init, yd)

        x = _merge(cfg, l, x, y_abc, yd, row(g_mix[l]), w_gate, wpa, wpb, wpc, wpd, wo)
        x = _ffn(cfg, l, x, row(g_ffn2[l]), w2g, w2u, w2d, split_out=(l == depth - 1))

        outs[0].append(pool_p[None, 16 - POOL_STATE:])
        outs[1].append(sconv_p[None, 8 - (SCONV_K - 1):])
        outs[2].append(cconv_p[None, 32 - (CCONV_K - 1):])
        outs[3].append(pool_s[:, 16 - POOL_STATE:])
        outs[4].append(sconv_s[:, 8 - (SCONV_K - 1):])
        outs[5].append(cconv_s[:, 32 - (CCONV_K - 1):])

    kt_all, ks_all, vp_all, vs_all = caches
    y_prompt = x[0].reshape(1, sp, D_MODEL)
    y_sample = x[1].reshape(nb, td, D_MODEL)
    k_prompt = jnp.transpose(kt_all.reshape(depth, 1, 2 * N_HEADS, HEAD_DIM, sp), (0, 1, 4, 2, 3))
    v_prompt = vp_all.reshape(depth, 1, sp, N_HEADS, 2 * HEAD_DIM)
    k_sample = ks_all.reshape(depth, nb, td, 2 * N_HEADS, HEAD_DIM)
    v_sample = vs_all.reshape(depth, nb, td, N_HEADS, 2 * HEAD_DIM)
    st = [jnp.stack(o) for o in outs]
    return (y_prompt, y_sample, k_prompt, v_prompt, st[0], st[1], st[2], k_sample, v_sample, st[3], st[4], st[5])


def kernel(x_prompt, x_sample, cache_k, cache_v, state_pool, state_sconv, state_cconv, g_ffn1, w1_gate, w1_up, w1_down, g_mix, w_in, pool_w, pool_scale, sconv_w, cconv_w, cconv_b, ln_g, ln_b, q_norm_g, k_norm_g, lam_q1, lam_k1, lam_q2, lam_k2, subln_g, wp_a, wp_b, wp_c, wp_d, w_out, g_ffn2, w2_gate, w2_up, w2_down):
    assert x_prompt.shape[0] == 1
    cfg = Cfg(s_prompt=x_prompt.shape[1], n_dec=x_sample.shape[0], t_dec=x_sample.shape[1],
              past=cache_k.shape[2], tm=512, tmix=512)
    return _forward(cfg, x_prompt, x_sample, cache_k, cache_v, state_pool, state_sconv, state_cconv,
                    g_ffn1, w1_gate, w1_up, w1_down, g_mix, w_in, pool_w, pool_scale, sconv_w,
                    cconv_w, cconv_b, ln_g, ln_b, q_norm_g, k_norm_g, lam_q1, lam_k1, lam_q2, lam_k2,
                    subln_g, wp_a, wp_b, wp_c, wp_d, w_out, g_ffn2, w2_gate, w2_up, w2_down)
```

```python
import functools
import math
from typing import NamedTuple

import jax
import jax.numpy as jnp
from jax import lax
from jax.experimental import pallas as pl
from jax.experimental.pallas import tpu as pltpu

F32 = jnp.float32
BF16 = jnp.bfloat16

D_MODEL = 1024
DEPTH = 4
CHUNK = 64
POOL_WINDOWS = (2, 4, 8, 16)
W_A = 256
GA = 64
POOL_STATE = 15
W_B = 256
SCONV_K = 3
W_C = 256
CCONV_K = 31
HEAD_DIM = 64
N_HEADS = 4
W_D = 512
ROPE_THETA = 10000.0
D_FF = 2816
EPS = 1e-6
W_ABC = W_A + 3 * W_B + 2 * W_C
W_PROJ = W_ABC + 3 * W_D
W_Y = W_A + W_B + W_C

LANES = 128
NORM_W = 256
COL_GROUPS = 4
VT_ROWS = LANES + 16
LOG2E = 1.4426950408889634
HALO = 32
MIB = 1024 * 1024


class Cfg(NamedTuple):
    s_prompt: int
    n_dec: int
    t_dec: int
    past: int
    tm: int
    tmix: int

    @property
    def n_tok(self):
        return self.s_prompt + self.n_dec * self.t_dec


def _const_spec(shape, index):
    return pl.BlockSpec(shape, lambda *_: index, pipeline_mode=pl.Buffered(1))


def _rms(x, g):
    ms = jnp.mean(x * x, axis=-1, keepdims=True)
    return x * lax.rsqrt(ms + EPS) * g


def _dot(a, b):
    return jnp.dot(a, b, preferred_element_type=F32)


def _dot_t(a, b):
    return lax.dot_general(a, b, (((1,), (1,)), ((), ())), preferred_element_type=F32)


def _ffn_kernel(*refs, n_prompt_tiles, split_in, split_out):
    refs = list(refs)
    is_prompt = pl.program_id(0) < n_prompt_tiles
    if split_in:
        xp_ref, xs_ref = refs[0:2]
        del refs[0:2]
        x = jnp.where(is_prompt, xp_ref[...], xs_ref[...])
    else:
        x = refs.pop(0)[...]
    g_ref, wg_ref, wu_ref, wd_ref = refs[0:4]
    h = _rms(x, g_ref[...]).astype(BF16)
    a = _dot(h, wg_ref[...])
    u = _dot(h, wu_ref[...])
    act = (a * jax.nn.sigmoid(a) * u).astype(BF16)
    half_step = 0.5 * _dot(act, wd_ref[...])
    if split_out:
        op_ref, os_ref = refs[4:6]

        @pl.when(is_prompt)
        def _():
            op_ref[...] = x + half_step

        @pl.when(jnp.logical_not(is_prompt))
        def _():
            os_ref[...] = x + half_step
    else:
        refs[4][...] = x + half_step


def _ffn(cfg, l, x, g, wg, wu, wd, split_out=False):
    n, tm, sp = cfg.n_tok, cfg.tm, cfg.s_prompt
    assert sp % tm == 0 and (n - sp) % tm == 0
    npt = sp // tm
    split_in = isinstance(x, tuple)
    rows = pl.BlockSpec((tm, D_MODEL), lambda i: (i, 0))
    prompt_rows = pl.BlockSpec((tm, D_MODEL), lambda i: (jnp.minimum(i, npt - 1), 0))
    sample_rows = pl.BlockSpec((tm, D_MODEL), lambda i: (jnp.maximum(i - npt, 0), 0))
    whole = jax.ShapeDtypeStruct((n, D_MODEL), F32)
    halves = [jax.ShapeDtypeStruct((sp, D_MODEL), F32), jax.ShapeDtypeStruct((n - sp, D_MODEL), F32)]
    return pl.pallas_call(
        functools.partial(_ffn_kernel, n_prompt_tiles=npt, split_in=split_in, split_out=split_out),
        grid=(n // tm,),
        in_specs=([prompt_rows, sample_rows] if split_in else [rows]) + [
            _const_spec((1, D_MODEL), (0, 0)),
            _const_spec((None, D_MODEL, D_FF), (l, 0, 0)),
            _const_spec((None, D_MODEL, D_FF), (l, 0, 0)),
            _const_spec((None, D_FF, D_MODEL), (l, 0, 0)),
        ],
        out_specs=[prompt_rows, sample_rows] if split_out else rows,
        out_shape=halves if split_out else whole,
        compiler_params=pltpu.CompilerParams(
            dimension_semantics=("arbitrary",), vmem_limit_bytes=52 * MIB),
        name="ffn",
    )(*(x if split_in else (x,)), g, wg, wu, wd)


def _inproj_kernel(x_ref, g_ref, w_ref, qg_ref, kg_ref, cos_ref, sin_ref, ones_ref, *rest, n_prompt_tiles):
    zabc_ref, qb_ref, kb_ref, vb_ref, qt_ref, vt_ref, kt_ref, ks_ref, vp_ref, vs_ref, zqkv_ref = rest[-11:]
    is_prompt = pl.program_id(0) < n_prompt_tiles
    h = _rms(x_ref[...], g_ref[...]).astype(BF16)
    assert W_ABC % (2 * NORM_W) == 0 and W_ABC // (2 * NORM_W) <= W_D // LANES
    cos = cos_ref[...]
    sin = sin_ref[...]
    ones = ones_ref[...]
    lane = lax.broadcasted_iota(jnp.int32, cos.shape, 1)
    first_half = (lane & (HEAD_DIM // 2)) == 0

    def head_norm(z, g):
        ss = z * z
        hi = ss.astype(BF16)
        lo = (ss - hi.astype(F32)).astype(BF16)
        tot = _dot(hi, ones) + _dot(lo, ones)
        return z * lax.rsqrt(tot * (1.0 / HEAD_DIM) + EPS) * g

    def rope(y):
        half = HEAD_DIM // 2
        partner = jnp.where(first_half, pltpu.roll(y, LANES - half, 1), pltpu.roll(y, half, 1))
        return y * cos + partner * sin

    nw = ones.shape[0]
    abc_w = 2 * NORM_W
    for c in range(W_D // LANES):
        if c * abc_w < W_ABC:
            zabc_ref[:, c * abc_w:(c + 1) * abc_w] = _dot(h, w_ref[:, c * abc_w:(c + 1) * abc_w])
        sl = slice(c * LANES, (c + 1) * LANES)
        if (c * LANES) % nw == 0:
            for part in range(3):
                lo = part * W_D + c * LANES
                zqkv_ref[:, lo:lo + nw] = _dot(h, w_ref[:, W_ABC + lo:W_ABC + lo + nw])
            yq = head_norm(zqkv_ref[:, c * LANES:c * LANES + nw], qg_ref[...])
            yk = head_norm(zqkv_ref[:, W_D + c * LANES:W_D + c * LANES + nw], kg_ref[...])
        off = (c * LANES) % nw
        q = rope(yq[:, off:off + LANES]) * (HEAD_DIM ** -0.5)
        qb_ref[:, sl] = q.astype(BF16)
        qt_ref[c] = (q * LOG2E).T.astype(BF16)
        k = rope(yk[:, off:off + LANES])
        kb_ref[:, sl] = k.astype(BF16)
        zqkv_ref[:, W_D + c * LANES:W_D + (c + 1) * LANES] = k
        v = zqkv_ref[:, 2 * W_D + c * LANES:2 * W_D + (c + 1) * LANES]
        vb_ref[:, sl] = v.astype(BF16)
        vt_ref[c, 0:LANES, :] = v.T.astype(BF16)
        vt_ref[c, LANES:VT_ROWS, :] = jnp.ones((VT_ROWS - LANES, v.shape[0]), BF16)

    @pl.when(is_prompt)
    def _():
        for c in range(W_D // LANES):
            kt = zqkv_ref[:, W_D + c * LANES:W_D + (c + 1) * LANES].T
            kt_ref[2 * c] = kt[0:HEAD_DIM]
            kt_ref[2 * c + 1] = kt[HEAD_DIM:LANES]
        for hd in range(N_HEADS):
            vp_ref[pl.ds(hd, kt_ref.shape[-1], stride=N_HEADS), :] = (
                zqkv_ref[:, 2 * W_D + hd * LANES:2 * W_D + (hd + 1) * LANES])

    @pl.when(jnp.logical_not(is_prompt))
    def _():
        ks_ref[...] = zqkv_ref[:, W_D:2 * W_D]
        vs_ref[...] = zqkv_ref[:, 2 * W_D:3 * W_D]


def _inproj(cfg, l, depth, x, g, w_in, qg, kg, cos_t, sin_t, ones_bd, caches):
    n, tm, sp = cfg.n_tok, cfg.tm, cfg.s_prompt
    n_dec = n - sp
    assert sp % tm == 0 and n_dec % tm == 0
    npt = sp // tm
    row = lambda w: pl.BlockSpec((tm, w), lambda i: (i, 0))
    prompt_rows = pl.BlockSpec((None, tm * N_HEADS, LANES), lambda i: (l, jnp.minimum(i, npt - 1), 0))
    sample_rows = pl.BlockSpec((None, tm, W_D), lambda i: (l, jnp.maximum(i - npt, 0), 0))
    n_in = 8
    aliased = [] if caches is None else list(caches)
    return pl.pallas_call(
        functools.partial(_inproj_kernel, n_prompt_tiles=npt),
        grid=(n // tm,),
        in_specs=[
            row(D_MODEL),
            _const_spec((1, D_MODEL), (0, 0)),
            _const_spec((None, D_MODEL, W_PROJ), (l, 0, 0)),
            _const_spec((1, NORM_W), (0, 0)),
            _const_spec((1, NORM_W), (0, 0)),
            row(LANES),
            row(LANES),
            _const_spec((NORM_W, NORM_W), (0, 0)),
        ] + [pl.BlockSpec(memory_space=pl.ANY) for _ in aliased],
        out_specs=[row(W_ABC), row(W_D), row(W_D), row(W_D),
                   pl.BlockSpec((N_HEADS, None, LANES, tm), lambda i: (0, i, 0, 0)),
                   pl.BlockSpec((N_HEADS, None, VT_ROWS, tm), lambda i: (0, i, 0, 0)),
                   pl.BlockSpec((None, 2 * N_HEADS, HEAD_DIM, tm), lambda i: (l, 0, 0, jnp.minimum(i, npt - 1))),
                   sample_rows, prompt_rows, sample_rows],
        out_shape=[
            jax.ShapeDtypeStruct((n, W_ABC), F32),
            jax.ShapeDtypeStruct((n, W_D), BF16),
            jax.ShapeDtypeStruct((n, W_D), BF16),
            jax.ShapeDtypeStruct((n, W_D), BF16),
            jax.ShapeDtypeStruct((N_HEADS, n // tm, LANES, tm), BF16),
            jax.ShapeDtypeStruct((N_HEADS, n // tm, VT_ROWS, tm), BF16),
            jax.ShapeDtypeStruct((depth, 2 * N_HEADS, HEAD_DIM, sp), F32),
            jax.ShapeDtypeStruct((depth, n_dec, W_D), F32),
            jax.ShapeDtypeStruct((depth, sp * N_HEADS, LANES), F32),
            jax.ShapeDtypeStruct((depth, n_dec, W_D), F32),
        ],
        scratch_shapes=[pltpu.VMEM((tm, 3 * W_D), F32)],
        input_output_aliases={n_in + j: 6 + j for j in range(len(aliased))},
        compiler_params=pltpu.CompilerParams(
            dimension_semantics=("arbitrary",), vmem_limit_bytes=44 * MIB),
        name="inproj",
    )(x, g, w_in, qg, kg, cos_t, sin_t, ones_bd, *aliased)


MIX_ROWS = 128
SUBLANES = 8


def _shifted_rows(ref, base, rows, depth, tmp):
    out = {}
    for r in range(SUBLANES):
        js = [j for j in range(1, depth + 1) if (-j) % SUBLANES == r]
        if not js:
            continue
        start, length = base - max(js), max(js) - min(js) + rows
        tmp[r, 0:length, :] = ref[start:start + length, :]
        for j in js:
            out[j] = tmp[r, max(js) - j:max(js) - j + rows, :]
    return out


def _mixer_compute(t, pos0, z_ref, pw_ref, ps_ref, sw_ref, cw_ref, cb_ref, lg_ref, lb_ref,
                   y_ref, pool_o, sconv_o, cconv_o, ea, eb, ec, tmp):
    u = z_ref[:, 0:W_A]
    ea[HALO:HALO + t, :] = u
    eb[HALO:HALO + t, :] = z_ref[:, W_A + W_B:W_A + 2 * W_B] * z_ref[:, W_A + 2 * W_B:W_A + 3 * W_B]
    zc = z_ref[:, W_A + 3 * W_B:W_A + 3 * W_B + W_C]
    ec[HALO:HALO + t, :] = zc * jax.nn.sigmoid(z_ref[:, W_A + 3 * W_B + W_C:W_ABC])

    rows = min(t, MIX_ROWS)
    for r0 in range(0, t, rows):
        base = HALO + r0
        lane = lax.broadcasted_iota(jnp.int32, (rows, W_A), 1)
        cur = ea[base:base + rows, :]
        back = _shifted_rows(ea, base, rows, max(POOL_WINDOWS) - 1, tmp)
        acc = cur
        sums = {}
        for j in range(1, max(POOL_WINDOWS)):
            acc = acc + back[j]
            if j + 1 in POOL_WINDOWS:
                sums[j + 1] = acc
        tot = sums[POOL_WINDOWS[-1]]
        win = jnp.full((rows, W_A), float(POOL_WINDOWS[-1]), F32)
        for gi in range(len(POOL_WINDOWS) - 2, -1, -1):
            in_group = lane < (gi + 1) * GA
            tot = jnp.where(in_group, sums[POOL_WINDOWS[gi]], tot)
            win = jnp.where(in_group, float(POOL_WINDOWS[gi]), win)
        if pos0 is None:
            cnt = win
        else:
            pos1 = (pos0 + r0 + 1 + lax.broadcasted_iota(jnp.int32, (rows, W_A), 0)).astype(F32)
            cnt = jnp.minimum(pos1, win)
        d = (tot / cnt - cur).astype(BF16)
        y_ref[r0:r0 + rows, 0:W_A] = _dot(d, pw_ref[...]) * ps_ref[...]

        conv = sw_ref[SCONV_K - 1:SCONV_K, :] * eb[base:base + rows, :]
        for j in range(SCONV_K - 1):
            off = base - (SCONV_K - 1) + j
            conv = conv + sw_ref[j:j + 1, :] * eb[off:off + rows, :]
        y_ref[r0:r0 + rows, W_A:W_A + W_B] = z_ref[r0:r0 + rows, W_A:W_A + W_B] * conv

        back = _shifted_rows(ec, base, rows, CCONV_K - 1, tmp)
        conv = cw_ref[CCONV_K - 1:CCONV_K, :] * ec[base:base + rows, :]
        for j in range(CCONV_K - 1):
            conv = conv + cw_ref[j:j + 1, :] * back[CCONV_K - 1 - j]
        conv = conv + cb_ref[...]
        mu = jnp.mean(conv, axis=-1, keepdims=True)
        cen = conv - mu
        var = jnp.mean(cen * cen, axis=-1, keepdims=True)
        ln = cen * lax.rsqrt(var + EPS) * lg_ref[...] + lb_ref[...]
        y_ref[r0:r0 + rows, W_A + W_B:W_Y] = ln * jax.nn.sigmoid(ln)

    pool_o[...] = ea[HALO + t - 16:HALO + t, :]
    sconv_o[...] = eb[HALO + t - 8:HALO + t, :]
    cconv_o[...] = ec[HALO + t - 32:HALO + t, :]


def _mixer_prompt_kernel(z_ref, pw_ref, ps_ref, sw_ref, cw_ref, cb_ref, lg_ref, lb_ref,
                         y_ref, pool_o, sconv_o, cconv_o, ea, eb, ec, tmp, *, t):
    i = pl.program_id(0)

    @pl.when(i == 0)
    def _():
        zeros = jnp.zeros((HALO, W_A), F32)
        ea[0:HALO, :] = zeros
        eb[0:HALO, :] = zeros
        ec[0:HALO, :] = zeros

    @pl.when(i > 0)
    def _():
        ea[0:HALO, :] = ea[t:t + HALO, :]
        eb[0:HALO, :] = eb[t:t + HALO, :]
        ec[0:HALO, :] = ec[t:t + HALO, :]

    _mixer_compute(t, i * t, z_ref, pw_ref, ps_ref, sw_ref, cw_ref, cb_ref, lg_ref, lb_ref,
                   y_ref, pool_o, sconv_o, cconv_o, ea, eb, ec, tmp)


def _mixer_sample_kernel(z_ref, sp_ref, ss_ref, sc_ref, pw_ref, ps_ref, sw_ref, cw_ref, cb_ref,
                         lg_ref, lb_ref, y_in_ref, y_ref, pool_o, sconv_o, cconv_o, ea, eb, ec, tmp, *, t):
    del y_in_ref
    ea[0:HALO, :] = sp_ref[...]
    eb[0:HALO, :] = ss_ref[...]
    ec[0:HALO, :] = sc_ref[...]
    _mixer_compute(t, None, z_ref, pw_ref, ps_ref, sw_ref, cw_ref, cb_ref, lg_ref, lb_ref,
                   y_ref, pool_o, sconv_o, cconv_o, ea, eb, ec, tmp)


def _mixer_weight_specs():
    return [
        _const_spec((W_A, W_A), (0, 0)),
        _const_spec((1, W_A), (0, 0)),
        _const_spec((SCONV_K, W_B), (0, 0)),
        _const_spec((CCONV_K, W_C), (0, 0)),
        _const_spec((1, W_C), (0, 0)),
        _const_spec((1, W_C), (0, 0)),
        _const_spec((1, W_C), (0, 0)),
    ]


def _mixer_scratch(t):
    return [pltpu.VMEM((HALO + t, W_A), F32), pltpu.VMEM((HALO + t, W_B), F32),
            pltpu.VMEM((HALO + t, W_C), F32),
            pltpu.VMEM((SUBLANES, min(t, MIX_ROWS) + HALO, W_C), F32)]


def _mixer_prompt(cfg, zabc, weights):
    t = cfg.tmix
    const_out = lambda r: pl.BlockSpec((r, W_A), lambda i: (0, 0))
    return pl.pallas_call(
        functools.partial(_mixer_prompt_kernel, t=t),
        grid=(cfg.s_prompt // t,),
        in_specs=[pl.BlockSpec((t, W_ABC), lambda i: (i, 0))] + _mixer_weight_specs(),
        out_specs=[pl.BlockSpec((t, W_Y), lambda i: (i, 0)), const_out(16), const_out(8), const_out(32)],
        out_shape=[
            jax.ShapeDtypeStruct((cfg.n_tok, W_Y), F32),
            jax.ShapeDtypeStruct((16, W_A), F32),
            jax.ShapeDtypeStruct((8, W_B), F32),
            jax.ShapeDtypeStruct((32, W_C), F32),
        ],
        scratch_shapes=_mixer_scratch(t),
        compiler_params=pltpu.CompilerParams(dimension_semantics=("arbitrary",)),
        name="mixer_prompt",
    )(zabc, *weights)


def _mixer_sample(cfg, zabc, st_pool, st_sconv, st_cconv, weights, y_abc):
    t, nb = cfg.t_dec, cfg.n_dec
    row0 = cfg.s_prompt // t
    state_spec = lambda: pl.BlockSpec((None, HALO, W_A), lambda b: (b, 0, 0))
    out_state = lambda r: pl.BlockSpec((None, r, W_A), lambda b: (b, 0, 0))
    return pl.pallas_call(
        functools.partial(_mixer_sample_kernel, t=t),
        grid=(nb,),
        in_specs=[pl.BlockSpec((t, W_ABC), lambda b: (row0 + b, 0)),
                  state_spec(), state_spec(), state_spec()]
                 + _mixer_weight_specs()
                 + [pl.BlockSpec(memory_space=pl.ANY)],
        out_specs=[pl.BlockSpec((t, W_Y), lambda b: (row0 + b, 0)),
                   out_state(16), out_state(8), out_state(32)],
        out_shape=[
            jax.ShapeDtypeStruct((cfg.n_tok, W_Y), F32),
            jax.ShapeDtypeStruct((nb, 16, W_A), F32),
            jax.ShapeDtypeStruct((nb, 8, W_B), F32),
            jax.ShapeDtypeStruct((nb, 32, W_C), F32),
        ],
        scratch_shapes=_mixer_scratch(t),
        input_output_aliases={11: 0},
        compiler_params=pltpu.CompilerParams(dimension_semantics=("arbitrary",)),
        name="mixer_sample",
    )(zabc, st_pool, st_sconv, st_cconv, *weights, y_abc)


def _lambda(lq1, lk1, lq2, lk2, lam_init):
    s1 = jnp.sum(lq1[...] * lk1[...], axis=-1, keepdims=True)
    s2 = jnp.sum(lq2[...] * lk2[...], axis=-1, keepdims=True)
    return jnp.exp(s1) - jnp.exp(s2) + lam_init


def _diff_out(acc, l, lam, sg, lam_init, t):
    o = acc[0:t] / l[0:t] - lam * (acc[t:2 * t] / l[t:2 * t])
    return _rms(o, sg) * (1.0 - lam_init)


def _attn_prompt_kernel(qt_ref, qtn_ref, k_ref, vt_ref, lq1, lk1, lq2, lk2, sg_ref, o_ref,
                        q2t_ref, q2tn_ref, s0_ref, s1_ref, mx0_ref, mx1_ref, m_ref, acc_ref, *, tk, lam_init):
    qi = pl.program_id(1)
    nq = 2 * tk
    chan = lax.broadcasted_iota(jnp.int32, (LANES, tk), 0)
    zero = jnp.zeros((LANES, tk), BF16)
    for src_ref, dst_ref in ((qt_ref, q2t_ref), (qtn_ref, q2tn_ref)):
        for half in range(2):
            qt = src_ref[half]
            dst_ref[:, half * tk:(half + 1) * tk] = jnp.where(chan < HEAD_DIM, qt, zero)
            dst_ref[:, nq + half * tk:nq + (half + 1) * tk] = jnp.where(chan >= HEAD_DIM, qt, zero)
    m_ref[...] = jnp.full(m_ref.shape, -jnp.inf, F32)
    acc_ref[...] = jnp.zeros(acc_ref.shape, F32)

    ncols = 2 * nq // COL_GROUPS

    def scores(j, s_ref, mx_ref, diagonal=None, group=None, queries=q2t_ref):
        cols = slice(None) if group is None else slice(group * ncols, (group + 1) * ncols)
        kb = k_ref[pl.ds(pl.multiple_of(j * tk, tk), tk), :]
        s = _dot(kb, queries[:, cols])
        if diagonal is not None:
            key = lax.broadcasted_iota(jnp.int32, s.shape, 0)
            col = lax.broadcasted_iota(jnp.int32, s.shape, 1) + (0 if group is None else group * ncols)
            key_chunk = diagonal * (tk // CHUNK) + key // CHUNK
            s = jnp.where(key_chunk <= (col & (nq - 1)) // CHUNK, s, -jnp.inf)
        s_ref[:, cols] = s
        mx_ref[:, cols] = jnp.max(s, axis=0, keepdims=True)

    def consume(j, s_ref, mx_ref, group=None):
        cols = slice(None) if group is None else slice(group * ncols, (group + 1) * ncols)
        m_prev = m_ref[:, cols]
        m_new = jnp.maximum(m_prev, mx_ref[:, cols])
        alpha = jnp.exp2(m_prev - m_new)
        p = jnp.exp2(s_ref[:, cols] - m_new).astype(BF16)
        acc_ref[:, cols] = alpha * acc_ref[:, cols] + _dot(vt_ref[j], p)
        m_ref[:, cols] = m_new

    @pl.when(qi == 0)
    def _():
        scores(0, s0_ref, mx0_ref, 0)

    def pair(i, first_diagonal):
        j = 2 * i
        for g in range(COL_GROUPS):
            scores(j + 1, s1_ref, mx1_ref, None, g)
            consume(j, s0_ref, mx0_ref, g)
        for g in range(COL_GROUPS):
            scores(j + 2, s0_ref, mx0_ref, first_diagonal, g)
            consume(j + 1, s1_ref, mx1_ref, g)

    def full_pair(i, carry):
        pair(i, None)
        return carry

    lax.fori_loop(0, qi - 1, full_pair, 0)

    @pl.when(qi > 0)
    def _():
        pair(qi - 1, 0)

    assert tk % ncols == 0
    late = [g for g in range(COL_GROUPS) if (g * ncols) % nq >= tk]
    for g in range(COL_GROUPS):
        if g in late:
            scores(2 * qi + 1, s1_ref, mx1_ref, 1, g)
        consume(2 * qi, s0_ref, mx0_ref, g)
        scores(0, s0_ref, mx0_ref, None, g, q2tn_ref)
    for g in late:
        consume(2 * qi + 1, s1_ref, mx1_ref, g)

    lam = _lambda(lq1, lk1, lq2, lk2, lam_init)
    acc = acc_ref[0:LANES, :]
    l = acc_ref[LANES:LANES + 1, :]
    ot = acc[:, 0:nq] / l[:, 0:nq] - lam * (acc[:, nq:2 * nq] / l[:, nq:2 * nq])
    o_ref[...] = _rms(ot.T, sg_ref[...]) * (1.0 - lam_init)


def _lam_specs():
    return [_const_spec((1, HEAD_DIM), (0, 0)) for _ in range(4)] + [_const_spec((1, LANES), (0, 0))]


def _attn_prompt(cfg, qt, kb, vt, lam_w, lam_init):
    s, tk = cfg.s_prompt, cfg.tm
    tq = 2 * tk
    assert tq & (tq - 1) == 0 and tk % CHUNK == 0 and s % tq == 0
    return pl.pallas_call(
        functools.partial(_attn_prompt_kernel, tk=tk, lam_init=lam_init),
        grid=(N_HEADS, s // tq),
        in_specs=[
            pl.BlockSpec((None, 2, LANES, tk), lambda h, i: (h, i, 0, 0)),
            pl.BlockSpec((None, 2, LANES, tk), lambda h, i: (h, jnp.minimum(i + 1, s // tq - 1), 0, 0)),
            pl.BlockSpec((s, LANES), lambda h, i: (0, h)),
            pl.BlockSpec((None, s // tk, VT_ROWS, tk), lambda h, i: (h, 0, 0, 0)),
        ] + _lam_specs(),
        out_specs=pl.BlockSpec((tq, LANES), lambda h, i: (i, h)),
        out_shape=jax.ShapeDtypeStruct((cfg.n_tok, W_D), F32),
        scratch_shapes=[
            pltpu.VMEM((LANES, 2 * tq), BF16),
            pltpu.VMEM((LANES, 2 * tq), BF16),
            pltpu.VMEM((tk, 2 * tq), F32),
            pltpu.VMEM((tk, 2 * tq), F32),
            pltpu.VMEM((1, 2 * tq), F32),
            pltpu.VMEM((1, 2 * tq), F32),
            pltpu.VMEM((1, 2 * tq), F32),
            pltpu.VMEM((VT_ROWS, 2 * tq), F32),
        ],
        compiler_params=pltpu.CompilerParams(
            dimension_semantics=("arbitrary", "arbitrary"), vmem_limit_bytes=52 * MIB),
        name="attn_prompt",
    )(qt, qt, kb, vt, *lam_w)


def _attn_sample_kernel(q_ref, kn_ref, vn_ref, kc_ref, vc_ref, lq1, lk1, lq2, lk2, sg_ref, yd_in_ref,
                        o_ref, *, t, lam_init):
    del yd_in_ref
    lam = _lambda(lq1, lk1, lq2, lk2, lam_init)
    past = vc_ref.shape[0] // N_HEADS
    for h in range(N_HEADS):
        sl = slice(h * LANES, (h + 1) * LANES)
        s_past, s_new = [], []
        for c in range(2):
            ch = slice(h * LANES + c * HEAD_DIM, h * LANES + (c + 1) * HEAD_DIM)
            q = q_ref[:, ch]
            s_past.append(_dot(q, kc_ref[2 * h + c].astype(BF16)))
            s_new.append(_dot_t(q, kn_ref[:, ch]))
        s_past = jnp.concatenate(s_past, axis=0)
        s_new = jnp.concatenate(s_new, axis=0)
        v_past = vc_ref[pl.ds(h, past, stride=N_HEADS), :]
        m = jnp.maximum(jnp.max(s_past, axis=-1, keepdims=True), jnp.max(s_new, axis=-1, keepdims=True))
        p_past = jnp.exp(s_past - m)
        p_new = jnp.exp(s_new - m)
        l = jnp.sum(p_past, axis=-1, keepdims=True) + jnp.sum(p_new, axis=-1, keepdims=True)
        acc = _dot(p_past.astype(BF16), v_past.astype(BF16)) + _dot(p_new.astype(BF16), vn_ref[:, sl])
        o_ref[:, sl] = _diff_out(acc, l, lam, sg_ref[...], lam_init, t)


def _attn_sample(cfg, l, qb, kb, vb, cache_k, cache_v, lam_w, lam_init, yd):
    t, nb = cfg.t_dec, cfg.n_dec
    row0 = cfg.s_prompt // t
    new_rows = lambda: pl.BlockSpec((t, W_D), lambda b: (row0 + b, 0))
    cache = lambda a: pl.BlockSpec((None, None) + a.shape[2:], lambda b: (l, b) + (0,) * (a.ndim - 2))
    return pl.pallas_call(
        functools.partial(_attn_sample_kernel, t=t, lam_init=lam_init),
        grid=(nb,),
        in_specs=[new_rows(), new_rows(), new_rows(), cache(cache_k), cache(cache_v)] + _lam_specs()
                 + [pl.BlockSpec(memory_space=pl.ANY)],
        out_specs=new_rows(),
        out_shape=jax.ShapeDtypeStruct((cfg.n_tok, W_D), F32),
        input_output_aliases={10: 0},
        compiler_params=pltpu.CompilerParams(
            dimension_semantics=("arbitrary",), vmem_limit_bytes=40 * MIB),
        name="attn_sample",
    )(qb, kb, vb, cache_k, cache_v, *lam_w, yd)


def _merge_kernel(x_ref, yabc_ref, yd_ref, g_ref, wg_ref, wpa_ref, wpb_ref, wpc_ref, wpd_ref, wo_ref, o_ref):
    x = x_ref[...]
    h = _rms(x, g_ref[...]).astype(BF16)
    branches = (
        (yabc_ref[:, 0:W_A], wpa_ref),
        (yabc_ref[:, W_A:W_A + W_B], wpb_ref),
        (yabc_ref[:, W_A + W_B:W_Y], wpc_ref),
        (yd_ref[...], wpd_ref),
    )
    merged = None
    for i, (y, wp_ref) in enumerate(branches):
        gate = jax.nn.sigmoid(_dot(h, wg_ref[:, i * D_MODEL:(i + 1) * D_MODEL]))
        term = gate * _dot(y.astype(BF16), wp_ref[...])
        merged = term if merged is None else merged + term
    o_ref[...] = x + _dot(merged.astype(BF16), wo_ref[...])


def _merge(cfg, l, x, y_abc, yd, g, w_gate, wpa, wpb, wpc, wpd, wo):
    n, tm = cfg.n_tok, cfg.tm
    row = lambda w: pl.BlockSpec((tm, w), lambda i: (i, 0))
    return pl.pallas_call(
        _merge_kernel,
        grid=(n // tm,),
        in_specs=[
            row(D_MODEL), row(W_Y), row(W_D),
            _const_spec((1, D_MODEL), (0, 0)),
            _const_spec((None, D_MODEL, 4 * D_MODEL), (l, 0, 0)),
            _const_spec((None, W_A, D_MODEL), (l, 0, 0)),
            _const_spec((None, W_B, D_MODEL), (l, 0, 0)),
            _const_spec((None, W_C, D_MODEL), (l, 0, 0)),
            _const_spec((None, W_D, D_MODEL), (l, 0, 0)),
            _const_spec((None, D_MODEL, D_MODEL), (l, 0, 0)),
        ],
        out_specs=row(D_MODEL),
        out_shape=jax.ShapeDtypeStruct((n, D_MODEL), F32),
        compiler_params=pltpu.CompilerParams(
            dimension_semantics=("arbitrary",), vmem_limit_bytes=48 * MIB),
        name="merge",
    )(x, y_abc, yd, g, w_gate, wpa, wpb, wpc, wpd, wo)


def _rope_tables(cfg):
    half = HEAD_DIM // 2
    inv_freq = ROPE_THETA ** (-jnp.arange(half, dtype=F32) / half)
    pos = jnp.concatenate([jnp.arange(cfg.s_prompt), jnp.tile(cfg.past + jnp.arange(cfg.t_dec), cfg.n_dec)])
    ang = pos.astype(F32)[:, None] * inv_freq[None, :]
    cos, sin = jnp.cos(ang), jnp.sin(ang)
    reps = LANES // HEAD_DIM
    cos_t = jnp.tile(jnp.concatenate([cos, cos], axis=1), (1, reps))
    sin_t = jnp.tile(jnp.concatenate([-sin, sin], axis=1), (1, reps))
    return cos_t, sin_t


def _pad_rows_top(a, rows):
    return jnp.pad(a, ((0, 0), (0, 0), (rows - a.shape[2], 0), (0, 0)))


def _forward(cfg, x_prompt, x_sample, cache_k, cache_v, state_pool, state_sconv, state_cconv,
             g_ffn1, w1_gate, w1_up, w1_down, g_mix, w_in, pool_w, pool_scale, sconv_w,
             cconv_w, cconv_b, ln_g, ln_b, q_norm_g, k_norm_g, lam_q1, lam_k1, lam_q2, lam_k2,
             subln_g, wp_a, wp_b, wp_c, wp_d, w_out, g_ffn2, w2_gate, w2_up, w2_down):
    depth = w_in.shape[0]
    sp, nb, td = cfg.s_prompt, cfg.n_dec, cfg.t_dec
    x = (x_prompt.reshape(sp, D_MODEL), x_sample.reshape(nb * td, D_MODEL))

    bf = lambda w: w.astype(BF16)
    w1g, w1u, w1d = bf(w1_gate), bf(w1_up), bf(w1_down)
    w2g, w2u, w2d = bf(w2_gate), bf(w2_up), bf(w2_down)
    w_proj, w_gate = bf(w_in[:, :, :W_PROJ]), bf(w_in[:, :, W_PROJ:])
    wpa, wpb, wpc, wpd, wo = bf(wp_a), bf(wp_b), bf(wp_c), bf(wp_d), bf(w_out)
    cos_t, sin_t = _rope_tables(cfg)
    ones_bd = jnp.kron(jnp.eye(NORM_W // HEAD_DIM, dtype=F32), jnp.ones((HEAD_DIM, HEAD_DIM), F32)).astype(BF16)
    eye_g = jnp.eye(len(POOL_WINDOWS), dtype=F32)
    ck = jnp.transpose(cache_k, (0, 1, 3, 4, 2))
    cv = cache_v.reshape(depth, nb, cfg.past * N_HEADS, 2 * HEAD_DIM)
    st_pool = _pad_rows_top(state_pool, HALO)
    st_sconv = _pad_rows_top(state_sconv, HALO)
    st_cconv = _pad_rows_top(state_cconv, HALO)
    row = lambda a: a.reshape(1, -1)

    outs = [[] for _ in range(6)]
    caches = None
    for l in range(depth):
        lam_init = 0.8 - 0.6 * math.exp(-0.3 * l)
        x = _ffn(cfg, l, x, row(g_ffn1[l]), w1g, w1u, w1d)
        qg = row(jnp.tile(q_norm_g[l], NORM_W // HEAD_DIM))
        kg = row(jnp.tile(k_norm_g[l], NORM_W // HEAD_DIM))
        zabc, qb, kb, vb, qt, vt, *caches = _inproj(cfg, l, depth, x, row(g_mix[l]), w_proj, qg, kg, cos_t, sin_t,
                                                    ones_bd, caches)

        pw_bd = (eye_g[:, None, :, None] * pool_w[l][:, :, None, :]).reshape(W_A, W_A).astype(BF16)
        mix_w = (pw_bd, row(pool_scale[l]), sconv_w[l], cconv_w[l], row(cconv_b[l]), row(ln_g[l]), row(ln_b[l]))
        y_abc, pool_p, sconv_p, cconv_p = _mixer_prompt(cfg, zabc, mix_w)
        y_abc, pool_s, sconv_s, cconv_s = _mixer_sample(cfg, zabc, st_pool[l], st_sconv[l], st_cconv[l],
                                                        mix_w, y_abc)

        lam_w = (row(lam_q1[l]), row(lam_k1[l]), row(lam_q2[l]), row(lam_k2[l]), row(subln_g[l]))
        yd = _attn_prompt(cfg, qt, kb, vt, lam_w, lam_init)
        yd = _attn_sample(cfg, l, qb, kb, vb, ck, cv, lam_w, lam_init, yd)

        x = _merge(cfg, l, x, y_abc, yd, row(g_mix[l]), w_gate, wpa, wpb, wpc, wpd, wo)
        x = _ffn(cfg, l, x, row(g_ffn2[l]), w2g, w2u, w2d, split_out=(l == depth - 1))

        outs[0].append(pool_p[None, 16 - POOL_STATE:])
        outs[1].append(sconv_p[None, 8 - (SCONV_K - 1):])
        outs[2].append(cconv_p[None, 32 - (CCONV_K - 1):])
        outs[3].append(pool_s[:, 16 - POOL_STATE:])
        outs[4].append(sconv_s[:, 8 - (SCONV_K - 1):])
        outs[5].append(cconv_s[:, 32 - (CCONV_K - 1):])

    kt_all, ks_all, vp_all, vs_all = caches
    y_prompt = x[0].reshape(1, sp, D_MODEL)
    y_sample = x[1].reshape(nb, td, D_MODEL)
    k_prompt = jnp.transpose(kt_all.reshape(depth, 1, 2 * N_HEADS, HEAD_DIM, sp), (0, 1, 4, 2, 3))
    v_prompt = vp_all.reshape(depth, 1, sp, N_HEADS, 2 * HEAD_DIM)
    k_sample = ks_all.reshape(depth, nb, td, 2 * N_HEADS, HEAD_DIM)
    v_sample = vs_all.reshape(depth, nb, td, N_HEADS, 2 * HEAD_DIM)
    st = [jnp.stack(o) for o in outs]
    return (y_prompt, y_sample, k_prompt, v_prompt, st[0], st[1], st[2], k_sample, v_sample, st[3], st[4], st[5])


def kernel(x_prompt, x_sample, cache_k, cache_v, state_pool, state_sconv, state_cconv, g_ffn1, w1_gate, w1_up, w1_down, g_mix, w_in, pool_w, pool_scale, sconv_w, cconv_w, cconv_b, ln_g, ln_b, q_norm_g, k_norm_g, lam_q1, lam_k1, lam_q2, lam_k2, subln_g, wp_a, wp_b, wp_c, wp_d, w_out, g_ffn2, w2_gate, w2_up, w2_down):
    assert x_prompt.shape[0] == 1
    cfg = Cfg(s_prompt=x_prompt.shape[1], n_dec=x_sample.shape[0], t_dec=x_sample.shape[1],
              past=cache_k.shape[2], tm=512, tmix=512)
    return _forward(cfg, x_prompt, x_sample, cache_k, cache_v, state_pool, state_sconv, state_cconv,
                    g_ffn1, w1_gate, w1_up, w1_down, g_mix, w_in, pool_w, pool_scale, sconv_w,
                    cconv_w, cconv_b, ln_g, ln_b, q_norm_g, k_norm_g, lam_q1, lam_k1, lam_q2, lam_k2,
                    subln_g, wp_a, wp_b, wp_c, wp_d, w_out, g_ffn2, w2_gate, w2_up, w2_down)
```

```python
import functools
import math
from typing import NamedTuple

import jax
import jax.numpy as jnp
from jax import lax
from jax.experimental import pallas as pl
from jax.experimental.pallas import tpu as pltpu

F32 = jnp.float32
BF16 = jnp.bfloat16

D_MODEL = 1024
DEPTH = 4
CHUNK = 64
POOL_WINDOWS = (2, 4, 8, 16)
W_A = 256
GA = 64
POOL_STATE = 15
W_B = 256
SCONV_K = 3
W_C = 256
CCONV_K = 31
HEAD_DIM = 64
N_HEADS = 4
W_D = 512
ROPE_THETA = 10000.0
D_FF = 2816
EPS = 1e-6
W_ABC = W_A + 3 * W_B + 2 * W_C
W_PROJ = W_ABC + 3 * W_D
W_Y = W_A + W_B + W_C

LANES = 128
NORM_W = 256
COL_GROUPS = 4
VT_ROWS = LANES + 16
LOG2E = 1.4426950408889634
HALO = 32
MIB = 1024 * 1024


class Cfg(NamedTuple):
    s_prompt: int
    n_dec: int
    t_dec: int
    past: int
    tm: int
    tmix: int

    @property
    def n_tok(self):
        return self.s_prompt + self.n_dec * self.t_dec


def _const_spec(shape, index):
    return pl.BlockSpec(shape, lambda *_: index, pipeline_mode=pl.Buffered(1))


def _rms(x, g):
    ms = jnp.mean(x * x, axis=-1, keepdims=True)
    return x * lax.rsqrt(ms + EPS) * g


def _dot(a, b):
    return jnp.dot(a, b, preferred_element_type=F32)


def _dot_t(a, b):
    return lax.dot_general(a, b, (((1,), (1,)), ((), ())), preferred_element_type=F32)


def _ffn_kernel(*refs, n_prompt_tiles, split_in, split_out):
    refs = list(refs)
    is_prompt = pl.program_id(0) < n_prompt_tiles
    if split_in:
        xp_ref, xs_ref = refs[0:2]
        del refs[0:2]
        x = jnp.where(is_prompt, xp_ref[...], xs_ref[...])
    else:
        x = refs.pop(0)[...]
    g_ref, wg_ref, wu_ref, wd_ref = refs[0:4]
    halves = []
    for xh in (x[0:x.shape[0] // 2], x[x.shape[0] // 2:]):
        h = _rms(xh, g_ref[...]).astype(BF16)
        a = _dot(h, wg_ref[...])
        u = _dot(h, wu_ref[...])
        act = (a * jax.nn.sigmoid(a) * u).astype(BF16)
        halves.append(0.5 * _dot(act, wd_ref[...]))
    half_step = jnp.concatenate(halves, axis=0)
    if split_out:
        op_ref, os_ref = refs[4:6]

        @pl.when(is_prompt)
        def _():
            op_ref[...] = x + half_step

        @pl.when(jnp.logical_not(is_prompt))
        def _():
            os_ref[...] = x + half_step
    else:
        refs[4][...] = x + half_step


def _ffn(cfg, l, x, g, wg, wu, wd, split_out=False):
    n, tm, sp = cfg.n_tok, cfg.tm, cfg.s_prompt
    assert sp % tm == 0 and (n - sp) % tm == 0
    npt = sp // tm
    split_in = isinstance(x, tuple)
    rows = pl.BlockSpec((tm, D_MODEL), lambda i: (i, 0))
    prompt_rows = pl.BlockSpec((tm, D_MODEL), lambda i: (jnp.minimum(i, npt - 1), 0))
    sample_rows = pl.BlockSpec((tm, D_MODEL), lambda i: (jnp.maximum(i - npt, 0), 0))
    whole = jax.ShapeDtypeStruct((n, D_MODEL), F32)
    halves = [jax.ShapeDtypeStruct((sp, D_MODEL), F32), jax.ShapeDtypeStruct((n - sp, D_MODEL), F32)]
    return pl.pallas_call(
        functools.partial(_ffn_kernel, n_prompt_tiles=npt, split_in=split_in, split_out=split_out),
        grid=(n // tm,),
        in_specs=([prompt_rows, sample_rows] if split_in else [rows]) + [
            _const_spec((1, D_MODEL), (0, 0)),
            _const_spec((None, D_MODEL, D_FF), (l, 0, 0)),
            _const_spec((None, D_MODEL, D_FF), (l, 0, 0)),
            _const_spec((None, D_FF, D_MODEL), (l, 0, 0)),
        ],
        out_specs=[prompt_rows, sample_rows] if split_out else rows,
        out_shape=halves if split_out else whole,
        compiler_params=pltpu.CompilerParams(
            dimension_semantics=("arbitrary",), vmem_limit_bytes=52 * MIB),
        name="ffn",
    )(*(x if split_in else (x,)), g, wg, wu, wd)


def _inproj_kernel(x_ref, g_ref, w_ref, qg_ref, kg_ref, cos_ref, sin_ref, ones_ref, *rest, n_prompt_tiles):
    zabc_ref, qb_ref, kb_ref, vb_ref, qt_ref, vt_ref, kt_ref, ks_ref, vp_ref, vs_ref, zqkv_ref = rest[-11:]
    is_prompt = pl.program_id(0) < n_prompt_tiles
    h = _rms(x_ref[...], g_ref[...]).astype(BF16)
    assert W_ABC % (2 * NORM_W) == 0 and W_ABC // (2 * NORM_W) <= W_D // LANES
    cos = cos_ref[...]
    sin = sin_ref[...]
    ones = ones_ref[...]
    lane = lax.broadcasted_iota(jnp.int32, cos.shape, 1)
    first_half = (lane & (HEAD_DIM // 2)) == 0

    def head_norm(z, g):
        ss = z * z
        hi = ss.astype(BF16)
        lo = (ss - hi.astype(F32)).astype(BF16)
        tot = _dot(hi, ones) + _dot(lo, ones)
        return z * lax.rsqrt(tot * (1.0 / HEAD_DIM) + EPS) * g

    def rope(y):
        half = HEAD_DIM // 2
        partner = jnp.where(first_half, pltpu.roll(y, LANES - half, 1), pltpu.roll(y, half, 1))
        return y * cos + partner * sin

    nw = ones.shape[0]
    abc_w = 2 * NORM_W
    n_chunks = W_D // LANES
    for g0 in range(0, W_D, nw):
        for part in range(3):
            lo = part * W_D + g0
            zqkv_ref[:, lo:lo + nw] = _dot(h, w_ref[:, W_ABC + lo:W_ABC + lo + nw])
    yq = [head_norm(zqkv_ref[:, g0:g0 + nw], qg_ref[...]) for g0 in range(0, W_D, nw)]
    yk = [head_norm(zqkv_ref[:, W_D + g0:W_D + g0 + nw], kg_ref[...]) for g0 in range(0, W_D, nw)]
    for c in range(n_chunks):
        sl = slice(c * LANES, (c + 1) * LANES)
        grp, off = divmod(c * LANES, nw)
        q = rope(yq[grp][:, off:off + LANES]) * (HEAD_DIM ** -0.5)
        qb_ref[:, sl] = q.astype(BF16)
        qt_ref[c] = (q * LOG2E).T.astype(BF16)
        k = rope(yk[grp][:, off:off + LANES])
        kb_ref[:, sl] = k.astype(BF16)
        zqkv_ref[:, W_D + c * LANES:W_D + (c + 1) * LANES] = k
        v = zqkv_ref[:, 2 * W_D + c * LANES:2 * W_D + (c + 1) * LANES]
        vb_ref[:, sl] = v.astype(BF16)
        vt_ref[c, 0:LANES, :] = v.T.astype(BF16)
        vt_ref[c, LANES:VT_ROWS, :] = jnp.ones((VT_ROWS - LANES, v.shape[0]), BF16)
        if c * abc_w < W_ABC:
            zabc_ref[:, c * abc_w:(c + 1) * abc_w] = _dot(h, w_ref[:, c * abc_w:(c + 1) * abc_w])

    @pl.when(is_prompt)
    def _():
        for c in range(W_D // LANES):
            kt = zqkv_ref[:, W_D + c * LANES:W_D + (c + 1) * LANES].T
            kt_ref[2 * c] = kt[0:HEAD_DIM]
            kt_ref[2 * c + 1] = kt[HEAD_DIM:LANES]
        for hd in range(N_HEADS):
            vp_ref[pl.ds(hd, kt_ref.shape[-1], stride=N_HEADS), :] = (
                zqkv_ref[:, 2 * W_D + hd * LANES:2 * W_D + (hd + 1) * LANES])

    @pl.when(jnp.logical_not(is_prompt))
    def _():
        ks_ref[...] = zqkv_ref[:, W_D:2 * W_D]
        vs_ref[...] = zqkv_ref[:, 2 * W_D:3 * W_D]


def _inproj(cfg, l, depth, x, g, w_in, qg, kg, cos_t, sin_t, ones_bd, caches):
    n, tm, sp = cfg.n_tok, cfg.tm, cfg.s_prompt
    n_dec = n - sp
    assert sp % tm == 0 and n_dec % tm == 0
    npt = sp // tm
    row = lambda w: pl.BlockSpec((tm, w), lambda i: (i, 0))
    prompt_rows = pl.BlockSpec((None, tm * N_HEADS, LANES), lambda i: (l, jnp.minimum(i, npt - 1), 0))
    sample_rows = pl.BlockSpec((None, tm, W_D), lambda i: (l, jnp.maximum(i - npt, 0), 0))
    n_in = 8
    aliased = [] if caches is None else list(caches)
    return pl.pallas_call(
        functools.partial(_inproj_kernel, n_prompt_tiles=npt),
        grid=(n // tm,),
        in_specs=[
            row(D_MODEL),
            _const_spec((1, D_MODEL), (0, 0)),
            _const_spec((None, D_MODEL, W_PROJ), (l, 0, 0)),
            _const_spec((1, NORM_W), (0, 0)),
            _const_spec((1, NORM_W), (0, 0)),
            row(LANES),
            row(LANES),
            _const_spec((NORM_W, NORM_W), (0, 0)),
        ] + [pl.BlockSpec(memory_space=pl.ANY) for _ in aliased],
        out_specs=[row(W_ABC), row(W_D), row(W_D), row(W_D),
                   pl.BlockSpec((N_HEADS, None, LANES, tm), lambda i: (0, i, 0, 0)),
                   pl.BlockSpec((N_HEADS, None, VT_ROWS, tm), lambda i: (0, i, 0, 0)),
                   pl.BlockSpec((None, 2 * N_HEADS, HEAD_DIM, tm), lambda i: (l, 0, 0, jnp.minimum(i, npt - 1))),
                   sample_rows, prompt_rows, sample_rows],
        out_shape=[
            jax.ShapeDtypeStruct((n, W_ABC), F32),
            jax.ShapeDtypeStruct((n, W_D), BF16),
            jax.ShapeDtypeStruct((n, W_D), BF16),
            jax.ShapeDtypeStruct((n, W_D), BF16),
            jax.ShapeDtypeStruct((N_HEADS, n // tm, LANES, tm), BF16),
            jax.ShapeDtypeStruct((N_HEADS, n // tm, VT_ROWS, tm), BF16),
            jax.ShapeDtypeStruct((depth, 2 * N_HEADS, HEAD_DIM, sp), F32),
            jax.ShapeDtypeStruct((depth, n_dec, W_D), F32),
            jax.ShapeDtypeStruct((depth, sp * N_HEADS, LANES), F32),
            jax.ShapeDtypeStruct((depth, n_dec, W_D), F32),
        ],
        scratch_shapes=[pltpu.VMEM((tm, 3 * W_D), F32)],
        input_output_aliases={n_in + j: 6 + j for j in range(len(aliased))},
        compiler_params=pltpu.CompilerParams(
            dimension_semantics=("arbitrary",), vmem_limit_bytes=44 * MIB),
        name="inproj",
    )(x, g, w_in, qg, kg, cos_t, sin_t, ones_bd, *aliased)


MIX_ROWS = 128
SUBLANES = 8


def _shifted_rows(ref, base, rows, depth, tmp):
    out = {}
    for r in range(SUBLANES):
        js = [j for j in range(1, depth + 1) if (-j) % SUBLANES == r]
        if not js:
            continue
        start, length = base - max(js), max(js) - min(js) + rows
        tmp[r, 0:length, :] = ref[start:start + length, :]
        for j in js:
            out[j] = tmp[r, max(js) - j:max(js) - j + rows, :]
    return out


def _mixer_compute(t, pos0, z_ref, pw_ref, ps_ref, sw_ref, cw_ref, cb_ref, lg_ref, lb_ref,
                   y_ref, pool_o, sconv_o, cconv_o, ea, eb, ec, tmp):
    u = z_ref[:, 0:W_A]
    ea[HALO:HALO + t, :] = u
    eb[HALO:HALO + t, :] = z_ref[:, W_A + W_B:W_A + 2 * W_B] * z_ref[:, W_A + 2 * W_B:W_A + 3 * W_B]
    zc = z_ref[:, W_A + 3 * W_B:W_A + 3 * W_B + W_C]
    ec[HALO:HALO + t, :] = zc * jax.nn.sigmoid(z_ref[:, W_A + 3 * W_B + W_C:W_ABC])

    rows = min(t, MIX_ROWS)
    for r0 in range(0, t, rows):
        base = HALO + r0
        lane = lax.broadcasted_iota(jnp.int32, (rows, W_A), 1)
        cur = ea[base:base + rows, :]
        back = _shifted_rows(ea, base, rows, max(POOL_WINDOWS) - 1, tmp)
        acc = cur
        sums = {}
        for j in range(1, max(POOL_WINDOWS)):
            acc = acc + back[j]
            if j + 1 in POOL_WINDOWS:
                sums[j + 1] = acc
        tot = sums[POOL_WINDOWS[-1]]
        win = jnp.full((rows, W_A), float(POOL_WINDOWS[-1]), F32)
        for gi in range(len(POOL_WINDOWS) - 2, -1, -1):
            in_group = lane < (gi + 1) * GA
            tot = jnp.where(in_group, sums[POOL_WINDOWS[gi]], tot)
            win = jnp.where(in_group, float(POOL_WINDOWS[gi]), win)
        if pos0 is None:
            cnt = win
        else:
            pos1 = (pos0 + r0 + 1 + lax.broadcasted_iota(jnp.int32, (rows, W_A), 0)).astype(F32)
            cnt = jnp.minimum(pos1, win)
        d = (tot / cnt - cur).astype(BF16)
        y_ref[r0:r0 + rows, 0:W_A] = _dot(d, pw_ref[...]) * ps_ref[...]

        conv = sw_ref[SCONV_K - 1:SCONV_K, :] * eb[base:base + rows, :]
        for j in range(SCONV_K - 1):
            off = base - (SCONV_K - 1) + j
            conv = conv + sw_ref[j:j + 1, :] * eb[off:off + rows, :]
        y_ref[r0:r0 + rows, W_A:W_A + W_B] = z_ref[r0:r0 + rows, W_A:W_A + W_B] * conv

        back = _shifted_rows(ec, base, rows, CCONV_K - 1, tmp)
        conv = cw_ref[CCONV_K - 1:CCONV_K, :] * ec[base:base + rows, :]
        for j in range(CCONV_K - 1):
            conv = conv + cw_ref[j:j + 1, :] * back[CCONV_K - 1 - j]
        conv = conv + cb_ref[...]
        mu = jnp.mean(conv, axis=-1, keepdims=True)
        cen = conv - mu
        var = jnp.mean(cen * cen, axis=-1, keepdims=True)
        ln = cen * lax.rsqrt(var + EPS) * lg_ref[...] + lb_ref[...]
        y_ref[r0:r0 + rows, W_A + W_B:W_Y] = ln * jax.nn.sigmoid(ln)

    pool_o[...] = ea[HALO + t - 16:HALO + t, :]
    sconv_o[...] = eb[HALO + t - 8:HALO + t, :]
    cconv_o[...] = ec[HALO + t - 32:HALO + t, :]


def _mixer_prompt_kernel(z_ref, pw_ref, ps_ref, sw_ref, cw_ref, cb_ref, lg_ref, lb_ref,
                         y_ref, pool_o, sconv_o, cconv_o, ea, eb, ec, tmp, *, t):
    i = pl.program_id(0)

    @pl.when(i == 0)
    def _():
        zeros = jnp.zeros((HALO, W_A), F32)
        ea[0:HALO, :] = zeros
        eb[0:HALO, :] = zeros
        ec[0:HALO, :] = zeros

    @pl.when(i > 0)
    def _():
        ea[0:HALO, :] = ea[t:t + HALO, :]
        eb[0:HALO, :] = eb[t:t + HALO, :]
        ec[0:HALO, :] = ec[t:t + HALO, :]

    _mixer_compute(t, i * t, z_ref, pw_ref, ps_ref, sw_ref, cw_ref, cb_ref, lg_ref, lb_ref,
                   y_ref, pool_o, sconv_o, cconv_o, ea, eb, ec, tmp)


def _mixer_sample_kernel(z_ref, sp_ref, ss_ref, sc_ref, pw_ref, ps_ref, sw_ref, cw_ref, cb_ref,
                         lg_ref, lb_ref, y_in_ref, y_ref, pool_o, sconv_o, cconv_o, ea, eb, ec, tmp, *, t):
    del y_in_ref
    ea[0:HALO, :] = sp_ref[...]
    eb[0:HALO, :] = ss_ref[...]
    ec[0:HALO, :] = sc_ref[...]
    _mixer_compute(t, None, z_ref, pw_ref, ps_ref, sw_ref, cw_ref, cb_ref, lg_ref, lb_ref,
                   y_ref, pool_o, sconv_o, cconv_o, ea, eb, ec, tmp)


def _mixer_weight_specs():
    return [
        _const_spec((W_A, W_A), (0, 0)),
        _const_spec((1, W_A), (0, 0)),
        _const_spec((SCONV_K, W_B), (0, 0)),
        _const_spec((CCONV_K, W_C), (0, 0)),
        _const_spec((1, W_C), (0, 0)),
        _const_spec((1, W_C), (0, 0)),
        _const_spec((1, W_C), (0, 0)),
    ]


def _mixer_scratch(t):
    return [pltpu.VMEM((HALO + t, W_A), F32), pltpu.VMEM((HALO + t, W_B), F32),
            pltpu.VMEM((HALO + t, W_C), F32),
            pltpu.VMEM((SUBLANES, min(t, MIX_ROWS) + HALO, W_C), F32)]


def _mixer_prompt(cfg, zabc, weights):
    t = cfg.tmix
    const_out = lambda r: pl.BlockSpec((r, W_A), lambda i: (0, 0))
    return pl.pallas_call(
        functools.partial(_mixer_prompt_kernel, t=t),
        grid=(cfg.s_prompt // t,),
        in_specs=[pl.BlockSpec((t, W_ABC), lambda i: (i, 0))] + _mixer_weight_specs(),
        out_specs=[pl.BlockSpec((t, W_Y), lambda i: (i, 0)), const_out(16), const_out(8), const_out(32)],
        out_shape=[
            jax.ShapeDtypeStruct((cfg.n_tok, W_Y), F32),
            jax.ShapeDtypeStruct((16, W_A), F32),
            jax.ShapeDtypeStruct((8, W_B), F32),
            jax.ShapeDtypeStruct((32, W_C), F32),
        ],
        scratch_shapes=_mixer_scratch(t),
        compiler_params=pltpu.CompilerParams(dimension_semantics=("arbitrary",)),
        name="mixer_prompt",
    )(zabc, *weights)


def _mixer_sample(cfg, zabc, st_pool, st_sconv, st_cconv, weights, y_abc):
    t, nb = cfg.t_dec, cfg.n_dec
    row0 = cfg.s_prompt // t
    state_spec = lambda: pl.BlockSpec((None, HALO, W_A), lambda b: (b, 0, 0))
    out_state = lambda r: pl.BlockSpec((None, r, W_A), lambda b: (b, 0, 0))
    return pl.pallas_call(
        functools.partial(_mixer_sample_kernel, t=t),
        grid=(nb,),
        in_specs=[pl.BlockSpec((t, W_ABC), lambda b: (row0 + b, 0)),
                  state_spec(), state_spec(), state_spec()]
                 + _mixer_weight_specs()
                 + [pl.BlockSpec(memory_space=pl.ANY)],
        out_specs=[pl.BlockSpec((t, W_Y), lambda b: (row0 + b, 0)),
                   out_state(16), out_state(8), out_state(32)],
        out_shape=[
            jax.ShapeDtypeStruct((cfg.n_tok, W_Y), F32),
            jax.ShapeDtypeStruct((nb, 16, W_A), F32),
            jax.ShapeDtypeStruct((nb, 8, W_B), F32),
            jax.ShapeDtypeStruct((nb, 32, W_C), F32),
        ],
        scratch_shapes=_mixer_scratch(t),
        input_output_aliases={11: 0},
        compiler_params=pltpu.CompilerParams(dimension_semantics=("arbitrary",)),
        name="mixer_sample",
    )(zabc, st_pool, st_sconv, st_cconv, *weights, y_abc)


def _lambda(lq1, lk1, lq2, lk2, lam_init):
    s1 = jnp.sum(lq1[...] * lk1[...], axis=-1, keepdims=True)
    s2 = jnp.sum(lq2[...] * lk2[...], axis=-1, keepdims=True)
    return jnp.exp(s1) - jnp.exp(s2) + lam_init


def _diff_out(acc, l, lam, sg, lam_init, t):
    o = acc[0:t] / l[0:t] - lam * (acc[t:2 * t] / l[t:2 * t])
    return _rms(o, sg) * (1.0 - lam_init)


def _attn_prompt_kernel(qt_ref, qtn_ref, k_ref, vt_ref, lq1, lk1, lq2, lk2, sg_ref, o_ref,
                        q2t_ref, q2tn_ref, s0_ref, s1_ref, mx0_ref, mx1_ref, m_ref, acc_ref, *, tk, lam_init):
    qi = pl.program_id(1)
    nq = 2 * tk
    chan = lax.broadcasted_iota(jnp.int32, (LANES, tk), 0)
    zero = jnp.zeros((LANES, tk), BF16)
    for src_ref, dst_ref in ((qt_ref, q2t_ref), (qtn_ref, q2tn_ref)):
        for half in range(2):
            qt = src_ref[half]
            dst_ref[:, half * tk:(half + 1) * tk] = jnp.where(chan < HEAD_DIM, qt, zero)
            dst_ref[:, nq + half * tk:nq + (half + 1) * tk] = jnp.where(chan >= HEAD_DIM, qt, zero)
    m_ref[...] = jnp.full(m_ref.shape, -jnp.inf, F32)
    acc_ref[...] = jnp.zeros(acc_ref.shape, F32)

    ncols = 2 * nq // COL_GROUPS

    def scores(j, s_ref, mx_ref, diagonal=None, group=None, queries=q2t_ref):
        cols = slice(None) if group is None else slice(group * ncols, (group + 1) * ncols)
        kb = k_ref[pl.ds(pl.multiple_of(j * tk, tk), tk), :]
        s = _dot(kb, queries[:, cols])
        if diagonal is not None:
            key = lax.broadcasted_iota(jnp.int32, s.shape, 0)
            col = lax.broadcasted_iota(jnp.int32, s.shape, 1) + (0 if group is None else group * ncols)
            key_chunk = diagonal * (tk // CHUNK) + key // CHUNK
            s = jnp.where(key_chunk <= (col & (nq - 1)) // CHUNK, s, -jnp.inf)
        s_ref[:, cols] = s
        mx_ref[:, cols] = jnp.max(s, axis=0, keepdims=True)

    def consume(j, s_ref, mx_ref, group=None):
        cols = slice(None) if group is None else slice(group * ncols, (group + 1) * ncols)
        m_prev = m_ref[:, cols]
        m_new = jnp.maximum(m_prev, mx_ref[:, cols])
        alpha = jnp.exp2(m_prev - m_new)
        p = jnp.exp2(s_ref[:, cols] - m_new).astype(BF16)
        acc_ref[:, cols] = alpha * acc_ref[:, cols] + _dot(vt_ref[j], p)
        m_ref[:, cols] = m_new

    @pl.when(qi == 0)
    def _():
        scores(0, s0_ref, mx0_ref, 0)

    def pair(i, first_diagonal):
        j = 2 * i
        for g in range(COL_GROUPS):
            scores(j + 1, s1_ref, mx1_ref, None, g)
            consume(j, s0_ref, mx0_ref, g)
        for g in range(COL_GROUPS):
            scores(j + 2, s0_ref, mx0_ref, first_diagonal, g)
            consume(j + 1, s1_ref, mx1_ref, g)

    def full_pair(i, carry):
        pair(i, None)
        return carry

    lax.fori_loop(0, qi - 1, full_pair, 0)

    @pl.when(qi > 0)
    def _():
        pair(qi - 1, 0)

    assert tk % ncols == 0
    late = [g for g in range(COL_GROUPS) if (g * ncols) % nq >= tk]
    for g in range(COL_GROUPS):
        if g in late:
            scores(2 * qi + 1, s1_ref, mx1_ref, 1, g)
        consume(2 * qi, s0_ref, mx0_ref, g)
        scores(0, s0_ref, mx0_ref, None, g, q2tn_ref)
    for g in late:
        consume(2 * qi + 1, s1_ref, mx1_ref, g)

    lam = _lambda(lq1, lk1, lq2, lk2, lam_init)
    acc = acc_ref[0:LANES, :]
    l = acc_ref[LANES:LANES + 1, :]
    ot = acc[:, 0:nq] / l[:, 0:nq] - lam * (acc[:, nq:2 * nq] / l[:, nq:2 * nq])
    o_ref[...] = _rms(ot.T, sg_ref[...]) * (1.0 - lam_init)


def _lam_specs():
    return [_const_spec((1, HEAD_DIM), (0, 0)) for _ in range(4)] + [_const_spec((1, LANES), (0, 0))]


def _attn_prompt(cfg, qt, kb, vt, lam_w, lam_init):
    s, tk = cfg.s_prompt, cfg.tm
    tq = 2 * tk
    assert tq & (tq - 1) == 0 and tk % CHUNK == 0 and s % tq == 0
    return pl.pallas_call(
        functools.partial(_attn_prompt_kernel, tk=tk, lam_init=lam_init),
        grid=(N_HEADS, s // tq),
        in_specs=[
            pl.BlockSpec((None, 2, LANES, tk), lambda h, i: (h, i, 0, 0)),
            pl.BlockSpec((None, 2, LANES, tk), lambda h, i: (h, jnp.minimum(i + 1, s // tq - 1), 0, 0)),
            pl.BlockSpec((s, LANES), lambda h, i: (0, h)),
            pl.BlockSpec((None, s // tk, VT_ROWS, tk), lambda h, i: (h, 0, 0, 0)),
        ] + _lam_specs(),
        out_specs=pl.BlockSpec((tq, LANES), lambda h, i: (i, h)),
        out_shape=jax.ShapeDtypeStruct((cfg.n_tok, W_D), F32),
        scratch_shapes=[
            pltpu.VMEM((LANES, 2 * tq), BF16),
            pltpu.VMEM((LANES, 2 * tq), BF16),
            pltpu.VMEM((tk, 2 * tq), F32),
            pltpu.VMEM((tk, 2 * tq), F32),
            pltpu.VMEM((1, 2 * tq), F32),
            pltpu.VMEM((1, 2 * tq), F32),
            pltpu.VMEM((1, 2 * tq), F32),
            pltpu.VMEM((VT_ROWS, 2 * tq), F32),
        ],
        compiler_params=pltpu.CompilerParams(
            dimension_semantics=("arbitrary", "arbitrary"), vmem_limit_bytes=52 * MIB),
        name="attn_prompt",
    )(qt, qt, kb, vt, *lam_w)


def _attn_sample_kernel(q_ref, kn_ref, vn_ref, kc_ref, vc_ref, lq1, lk1, lq2, lk2, sg_ref, yd_in_ref,
                        o_ref, *, t, lam_init):
    del yd_in_ref
    lam = _lambda(lq1, lk1, lq2, lk2, lam_init)
    past = vc_ref.shape[0] // N_HEADS
    for h in range(N_HEADS):
        sl = slice(h * LANES, (h + 1) * LANES)
        s_past, s_new = [], []
        for c in range(2):
            ch = slice(h * LANES + c * HEAD_DIM, h * LANES + (c + 1) * HEAD_DIM)
            q = q_ref[:, ch]
            s_past.append(_dot(q, kc_ref[2 * h + c].astype(BF16)))
            s_new.append(_dot_t(q, kn_ref[:, ch]))
        s_past = jnp.concatenate(s_past, axis=0)
        s_new = jnp.concatenate(s_new, axis=0)
        v_past = vc_ref[pl.ds(h, past, stride=N_HEADS), :]
        m = jnp.maximum(jnp.max(s_past, axis=-1, keepdims=True), jnp.max(s_new, axis=-1, keepdims=True))
        p_past = jnp.exp(s_past - m)
        p_new = jnp.exp(s_new - m)
        l = jnp.sum(p_past, axis=-1, keepdims=True) + jnp.sum(p_new, axis=-1, keepdims=True)
        acc = _dot(p_past.astype(BF16), v_past.astype(BF16)) + _dot(p_new.astype(BF16), vn_ref[:, sl])
        o_ref[:, sl] = _diff_out(acc, l, lam, sg_ref[...], lam_init, t)


def _attn_sample(cfg, l, qb, kb, vb, cache_k, cache_v, lam_w, lam_init, yd):
    t, nb = cfg.t_dec, cfg.n_dec
    row0 = cfg.s_prompt // t
    new_rows = lambda: pl.BlockSpec((t, W_D), lambda b: (row0 + b, 0))
    cache = lambda a: pl.BlockSpec((None, None) + a.shape[2:], lambda b: (l, b) + (0,) * (a.ndim - 2))
    return pl.pallas_call(
        functools.partial(_attn_sample_kernel, t=t, lam_init=lam_init),
        grid=(nb,),
        in_specs=[new_rows(), new_rows(), new_rows(), cache(cache_k), cache(cache_v)] + _lam_specs()
                 + [pl.BlockSpec(memory_space=pl.ANY)],
        out_specs=new_rows(),
        out_shape=jax.ShapeDtypeStruct((cfg.n_tok, W_D), F32),
        input_output_aliases={10: 0},
        compiler_params=pltpu.CompilerParams(
            dimension_semantics=("arbitrary",), vmem_limit_bytes=40 * MIB),
        name="attn_sample",
    )(qb, kb, vb, cache_k, cache_v, *lam_w, yd)


def _merge_kernel(x_ref, yabc_ref, yd_ref, g_ref, wg_ref, wpa_ref, wpb_ref, wpc_ref, wpd_ref, wo_ref, o_ref):
    half = x_ref.shape[0] // 2
    for rows in (slice(0, half), slice(half, 2 * half)):
        x = x_ref[rows, :]
        h = _rms(x, g_ref[...]).astype(BF16)
        branches = (
            (yabc_ref[rows, 0:W_A], wpa_ref),
            (yabc_ref[rows, W_A:W_A + W_B], wpb_ref),
            (yabc_ref[rows, W_A + W_B:W_Y], wpc_ref),
            (yd_ref[rows, :], wpd_ref),
        )
        merged = None
        for i, (y, wp_ref) in enumerate(branches):
            gate = jax.nn.sigmoid(_dot(h, wg_ref[:, i * D_MODEL:(i + 1) * D_MODEL]))
            term = gate * _dot(y.astype(BF16), wp_ref[...])
            merged = term if merged is None else merged + term
        o_ref[rows, :] = x + _dot(merged.astype(BF16), wo_ref[...])


def _merge(cfg, l, x, y_abc, yd, g, w_gate, wpa, wpb, wpc, wpd, wo):
    n, tm = cfg.n_tok, cfg.tm
    row = lambda w: pl.BlockSpec((tm, w), lambda i: (i, 0))
    return pl.pallas_call(
        _merge_kernel,
        grid=(n // tm,),
        in_specs=[
            row(D_MODEL), row(W_Y), row(W_D),
            _const_spec((1, D_MODEL), (0, 0)),
            _const_spec((None, D_MODEL, 4 * D_MODEL), (l, 0, 0)),
            _const_spec((None, W_A, D_MODEL), (l, 0, 0)),
            _const_spec((None, W_B, D_MODEL), (l, 0, 0)),
            _const_spec((None, W_C, D_MODEL), (l, 0, 0)),
            _const_spec((None, W_D, D_MODEL), (l, 0, 0)),
            _const_spec((None, D_MODEL, D_MODEL), (l, 0, 0)),
        ],
        out_specs=row(D_MODEL),
        out_shape=jax.ShapeDtypeStruct((n, D_MODEL), F32),
        compiler_params=pltpu.CompilerParams(
            dimension_semantics=("arbitrary",), vmem_limit_bytes=48 * MIB),
        name="merge",
    )(x, y_abc, yd, g, w_gate, wpa, wpb, wpc, wpd, wo)


def _rope_tables(cfg):
    half = HEAD_DIM // 2
    inv_freq = ROPE_THETA ** (-jnp.arange(half, dtype=F32) / half)
    pos = jnp.concatenate([jnp.arange(cfg.s_prompt), jnp.tile(cfg.past + jnp.arange(cfg.t_dec), cfg.n_dec)])
    ang = pos.astype(F32)[:, None] * inv_freq[None, :]
    cos, sin = jnp.cos(ang), jnp.sin(ang)
    reps = LANES // HEAD_DIM
    cos_t = jnp.tile(jnp.concatenate([cos, cos], axis=1), (1, reps))
    sin_t = jnp.tile(jnp.concatenate([-sin, sin], axis=1), (1, reps))
    return cos_t, sin_t


def _pad_rows_top(a, rows):
    return jnp.pad(a, ((0, 0), (0, 0), (rows - a.shape[2], 0), (0, 0)))


def _forward(cfg, x_prompt, x_sample, cache_k, cache_v, state_pool, state_sconv, state_cconv,
             g_ffn1, w1_gate, w1_up, w1_down, g_mix, w_in, pool_w, pool_scale, sconv_w,
             cconv_w, cconv_b, ln_g, ln_b, q_norm_g, k_norm_g, lam_q1, lam_k1, lam_q2, lam_k2,
             subln_g, wp_a, wp_b, wp_c, wp_d, w_out, g_ffn2, w2_gate, w2_up, w2_down):
    depth = w_in.shape[0]
    sp, nb, td = cfg.s_prompt, cfg.n_dec, cfg.t_dec
    x = (x_prompt.reshape(sp, D_MODEL), x_sample.reshape(nb * td, D_MODEL))

    bf = lambda w: w.astype(BF16)
    w1g, w1u, w1d = bf(w1_gate), bf(w1_up), bf(w1_down)
    w2g, w2u, w2d = bf(w2_gate), bf(w2_up), bf(w2_down)
    w_proj, w_gate = bf(w_in[:, :, :W_PROJ]), bf(w_in[:, :, W_PROJ:])
    wpa, wpb, wpc, wpd, wo = bf(wp_a), bf(wp_b), bf(wp_c), bf(wp_d), bf(w_out)
    cos_t, sin_t = _rope_tables(cfg)
    ones_bd = jnp.kron(jnp.eye(NORM_W // HEAD_DIM, dtype=F32), jnp.ones((HEAD_DIM, HEAD_DIM), F32)).astype(BF16)
    eye_g = jnp.eye(len(POOL_WINDOWS), dtype=F32)
    ck = jnp.transpose(cache_k, (0, 1, 3, 4, 2))
    cv = cache_v.reshape(depth, nb, cfg.past * N_HEADS, 2 * HEAD_DIM)
    st_pool = _pad_rows_top(state_pool, HALO)
    st_sconv = _pad_rows_top(state_sconv, HALO)
    st_cconv = _pad_rows_top(state_cconv, HALO)
    row = lambda a: a.reshape(1, -1)

    outs = [[] for _ in range(6)]
    caches = None
    for l in range(depth):
        lam_init = 0.8 - 0.6 * math.exp(-0.3 * l)
        x = _ffn(cfg, l, x, row(g_ffn1[l]), w1g, w1u, w1d)
        qg = row(jnp.tile(q_norm_g[l], NORM_W // HEAD_DIM))
        kg = row(jnp.tile(k_norm_g[l], NORM_W // HEAD_DIM))
        zabc, qb, kb, vb, qt, vt, *caches = _inproj(cfg, l, depth, x, row(g_mix[l]), w_proj, qg, kg, cos_t, sin_t,
                                                    ones_bd, caches)

        pw_bd = (eye_g[:, None, :, None] * pool_w[l][:, :, None, :]).reshape(W_A, W_A).astype(BF16)
        mix_w = (pw_bd, row(pool_scale[l]), sconv_w[l], cconv_w[l], row(cconv_b[l]), row(ln_g[l]), row(ln_b[l]))
        y_abc, pool_p, sconv_p, cconv_p = _mixer_prompt(cfg, zabc, mix_w)
        y_abc, pool_s, sconv_s, cconv_s = _mixer_sample(cfg, zabc, st_pool[l], st_sconv[l], st_cconv[l],
                                                        mix_w, y_abc)

        lam_w = (row(lam_q1[l]), row(lam_k1[l]), row(lam_q2[l]), row(lam_k2[l]), row(subln_g[l]))
        yd = _attn_prompt(cfg, qt, kb, vt, lam_w, lam_init)
        yd = _attn_sample(cfg, l, qb, kb, vb, ck, cv, lam_w, lam_init, yd)

        x = _merge(cfg, l, x, y_abc, yd, row(g_mix[l]), w_gate, wpa, wpb, wpc, wpd, wo)
        x = _ffn(cfg, l, x, row(g_ffn2[l]), w2g, w2u, w2d, split_out=(l == depth - 1))

        outs[0].append(pool_p[None, 16 - POOL_STATE:])
        outs[1].append(sconv_p[None, 8 - (SCONV_K - 1):])
        outs[2].append(cconv_p[None, 32 - (CCONV_K - 1):])
        outs[3].append(pool_s[:, 16 - POOL_STATE:])
        outs[4].append(sconv_s[:, 8 - (SCONV_K - 1):])
        outs[5].append(cconv_s[:, 32 - (CCONV_K - 1):])

    kt_all, ks_all, vp_all, vs_all = caches
    y_prompt = x[0].reshape(1, sp, D_MODEL)
    y_sample = x[1].reshape(nb, td, D_MODEL)
    k_prompt = jnp.transpose(kt_all.reshape(depth, 1, 2 * N_HEADS, HEAD_DIM, sp), (0, 1, 4, 2, 3))
    v_prompt = vp_all.reshape(depth, 1, sp, N_HEADS, 2 * HEAD_DIM)
    k_sample = ks_all.reshape(depth, nb, td, 2 * N_HEADS, HEAD_DIM)
    v_sample = vs_all.reshape(depth, nb, td, N_HEADS, 2 * HEAD_DIM)
    st = [jnp.stack(o) for o in outs]
    return (y_prompt, y_sample, k_prompt, v_prompt, st[0], st[1], st[2], k_sample, v_sample, st[3], st[4], st[5])


def kernel(x_prompt, x_sample, cache_k, cache_v, state_pool, state_sconv, state_cconv, g_ffn1, w1_gate, w1_up, w1_down, g_mix, w_in, pool_w, pool_scale, sconv_w, cconv_w, cconv_b, ln_g, ln_b, q_norm_g, k_norm_g, lam_q1, lam_k1, lam_q2, lam_k2, subln_g, wp_a, wp_b, wp_c, wp_d, w_out, g_ffn2, w2_gate, w2_up, w2_down):
    assert x_prompt.shape[0] == 1
    cfg = Cfg(s_prompt=x_prompt.shape[1], n_dec=x_sample.shape[0], t_dec=x_sample.shape[1],
              past=cache_k.shape[2], tm=512, tmix=512)
    return _forward(cfg, x_prompt, x_sample, cache_k, cache_v, state_pool, state_sconv, state_cconv,
                    g_ffn1, w1_gate, w1_up, w1_down, g_mix, w_in, pool_w, pool_scale, sconv_w,
                    cconv_w, cconv_b, ln_g, ln_b, q_norm_g, k_norm_g, lam_q1, lam_k1, lam_q2, lam_k2,
                    subln_g, wp_a, wp_b, wp_c, wp_d, w_out, g_ffn2, w2_gate, w2_up, w2_down)
```

```python
import functools
import math
from typing import NamedTuple

import jax
import jax.numpy as jnp
from jax import lax
from jax.experimental import pallas as pl
from jax.experimental.pallas import tpu as pltpu

F32 = jnp.float32
BF16 = jnp.bfloat16

D_MODEL = 1024
DEPTH = 4
CHUNK = 64
POOL_WINDOWS = (2, 4, 8, 16)
W_A = 256
GA = 64
POOL_STATE = 15
W_B = 256
SCONV_K = 3
W_C = 256
CCONV_K = 31
HEAD_DIM = 64
N_HEADS = 4
W_D = 512
ROPE_THETA = 10000.0
D_FF = 2816
EPS = 1e-6
W_ABC = W_A + 3 * W_B + 2 * W_C
W_PROJ = W_ABC + 3 * W_D
W_Y = W_A + W_B + W_C

LANES = 128
NORM_W = 256
COL_GROUPS = 4
VT_ROWS = LANES + 16
LOG2E = 1.4426950408889634
HALO = 32
MIB = 1024 * 1024


class Cfg(NamedTuple):
    s_prompt: int
    n_dec: int
    t_dec: int
    past: int
    tm: int
    tmix: int

    @property
    def n_tok(self):
        return self.s_prompt + self.n_dec * self.t_dec


def _const_spec(shape, index):
    return pl.BlockSpec(shape, lambda *_: index, pipeline_mode=pl.Buffered(1))


def _rms(x, g):
    ms = jnp.mean(x * x, axis=-1, keepdims=True)
    return x * lax.rsqrt(ms + EPS) * g


def _dot(a, b):
    return jnp.dot(a, b, preferred_element_type=F32)


def _dot_t(a, b):
    return lax.dot_general(a, b, (((1,), (1,)), ((), ())), preferred_element_type=F32)


def _ffn_kernel(*refs, n_prompt_tiles, split_in, split_out):
    refs = list(refs)
    is_prompt = pl.program_id(0) < n_prompt_tiles
    if split_in:
        xp_ref, xs_ref = refs[0:2]
        del refs[0:2]
        x = jnp.where(is_prompt, xp_ref[...], xs_ref[...])
    else:
        x = refs.pop(0)[...]
    g_ref, wg_ref, wu_ref, wd_ref = refs[0:4]
    halves = []
    for xh in (x[0:x.shape[0] // 2], x[x.shape[0] // 2:]):
        h = _rms(xh, g_ref[...]).astype(BF16)
        a = _dot(h, wg_ref[...])
        u = _dot(h, wu_ref[...])
        act = (a * jax.nn.sigmoid(a) * u).astype(BF16)
        halves.append(0.5 * _dot(act, wd_ref[...]))
    half_step = jnp.concatenate(halves, axis=0)
    if split_out:
        op_ref, os_ref = refs[4:6]

        @pl.when(is_prompt)
        def _():
            op_ref[...] = x + half_step

        @pl.when(jnp.logical_not(is_prompt))
        def _():
            os_ref[...] = x + half_step
    else:
        refs[4][...] = x + half_step


def _ffn(cfg, l, x, g, wg, wu, wd, split_out=False):
    n, tm, sp = cfg.n_tok, cfg.tm, cfg.s_prompt
    assert sp % tm == 0 and (n - sp) % tm == 0
    npt = sp // tm
    split_in = isinstance(x, tuple)
    rows = pl.BlockSpec((tm, D_MODEL), lambda i: (i, 0))
    prompt_rows = pl.BlockSpec((tm, D_MODEL), lambda i: (jnp.minimum(i, npt - 1), 0))
    sample_rows = pl.BlockSpec((tm, D_MODEL), lambda i: (jnp.maximum(i - npt, 0), 0))
    whole = jax.ShapeDtypeStruct((n, D_MODEL), F32)
    halves = [jax.ShapeDtypeStruct((sp, D_MODEL), F32), jax.ShapeDtypeStruct((n - sp, D_MODEL), F32)]
    return pl.pallas_call(
        functools.partial(_ffn_kernel, n_prompt_tiles=npt, split_in=split_in, split_out=split_out),
        grid=(n // tm,),
        in_specs=([prompt_rows, sample_rows] if split_in else [rows]) + [
            _const_spec((1, D_MODEL), (0, 0)),
            _const_spec((None, D_MODEL, D_FF), (l, 0, 0)),
            _const_spec((None, D_MODEL, D_FF), (l, 0, 0)),
            _const_spec((None, D_FF, D_MODEL), (l, 0, 0)),
        ],
        out_specs=[prompt_rows, sample_rows] if split_out else rows,
        out_shape=halves if split_out else whole,
        compiler_params=pltpu.CompilerParams(
            dimension_semantics=("arbitrary",), vmem_limit_bytes=52 * MIB),
        name="ffn",
    )(*(x if split_in else (x,)), g, wg, wu, wd)


def _inproj_kernel(x_ref, g_ref, w_ref, qg_ref, kg_ref, cos_ref, sin_ref, ones_ref, *rest, n_prompt_tiles):
    zabc_ref, qb_ref, kb_ref, vb_ref, qt_ref, vt_ref, kt_ref, ks_ref, vp_ref, vs_ref, zqkv_ref = rest[-11:]
    is_prompt = pl.program_id(0) < n_prompt_tiles
    h = _rms(x_ref[...], g_ref[...]).astype(BF16)
    assert W_ABC % (2 * NORM_W) == 0 and W_ABC // (2 * NORM_W) <= W_D // LANES
    cos = cos_ref[...]
    sin = sin_ref[...]
    ones = ones_ref[...]
    lane = lax.broadcasted_iota(jnp.int32, cos.shape, 1)
    first_half = (lane & (HEAD_DIM // 2)) == 0

    def head_norm(z, g):
        ss = z * z
        hi = ss.astype(BF16)
        lo = (ss - hi.astype(F32)).astype(BF16)
        tot = _dot(hi, ones) + _dot(lo, ones)
        return z * lax.rsqrt(tot * (1.0 / HEAD_DIM) + EPS) * g

    def rope(y):
        half = HEAD_DIM // 2
        partner = jnp.where(first_half, pltpu.roll(y, LANES - half, 1), pltpu.roll(y, half, 1))
        return y * cos + partner * sin

    nw = ones.shape[0]
    abc_w = 2 * NORM_W
    n_chunks = W_D // LANES
    for g0 in range(0, W_D, nw):
        for part in range(3):
            lo = part * W_D + g0
            zqkv_ref[:, lo:lo + nw] = _dot(h, w_ref[:, W_ABC + lo:W_ABC + lo + nw])
    yq = [head_norm(zqkv_ref[:, g0:g0 + nw], qg_ref[...]) for g0 in range(0, W_D, nw)]
    yk = [head_norm(zqkv_ref[:, W_D + g0:W_D + g0 + nw], kg_ref[...]) for g0 in range(0, W_D, nw)]
    for c in range(n_chunks):
        sl = slice(c * LANES, (c + 1) * LANES)
        grp, off = divmod(c * LANES, nw)
        q = rope(yq[grp][:, off:off + LANES]) * (HEAD_DIM ** -0.5)
        qb_ref[:, sl] = q.astype(BF16)
        qt_ref[c] = (q * LOG2E).T.astype(BF16)
        k = rope(yk[grp][:, off:off + LANES])
        kb_ref[:, sl] = k.astype(BF16)
        zqkv_ref[:, W_D + c * LANES:W_D + (c + 1) * LANES] = k
        v = zqkv_ref[:, 2 * W_D + c * LANES:2 * W_D + (c + 1) * LANES]
        vb_ref[:, sl] = v.astype(BF16)
        vt_ref[c, 0:LANES, :] = v.T.astype(BF16)
        vt_ref[c, LANES:VT_ROWS, :] = jnp.ones((VT_ROWS - LANES, v.shape[0]), BF16)
        if c * abc_w < W_ABC:
            zabc_ref[:, c * abc_w:(c + 1) * abc_w] = _dot(h, w_ref[:, c * abc_w:(c + 1) * abc_w])

    @pl.when(is_prompt)
    def _():
        for c in range(W_D // LANES):
            kt = zqkv_ref[:, W_D + c * LANES:W_D + (c + 1) * LANES].T
            kt_ref[2 * c] = kt[0:HEAD_DIM]
            kt_ref[2 * c + 1] = kt[HEAD_DIM:LANES]
        for hd in range(N_HEADS):
            vp_ref[pl.ds(hd, kt_ref.shape[-1], stride=N_HEADS), :] = (
                zqkv_ref[:, 2 * W_D + hd * LANES:2 * W_D + (hd + 1) * LANES])

    @pl.when(jnp.logical_not(is_prompt))
    def _():
        ks_ref[...] = zqkv_ref[:, W_D:2 * W_D]
        vs_ref[...] = zqkv_ref[:, 2 * W_D:3 * W_D]


def _inproj(cfg, l, depth, x, g, w_in, qg, kg, cos_t, sin_t, ones_bd, caches):
    n, tm, sp = cfg.n_tok, cfg.tm, cfg.s_prompt
    n_dec = n - sp
    assert sp % tm == 0 and n_dec % tm == 0
    npt = sp // tm
    row = lambda w: pl.BlockSpec((tm, w), lambda i: (i, 0))
    prompt_rows = pl.BlockSpec((None, tm * N_HEADS, LANES), lambda i: (l, jnp.minimum(i, npt - 1), 0))
    sample_rows = pl.BlockSpec((None, tm, W_D), lambda i: (l, jnp.maximum(i - npt, 0), 0))
    n_in = 8
    aliased = [] if caches is None else list(caches)
    return pl.pallas_call(
        functools.partial(_inproj_kernel, n_prompt_tiles=npt),
        grid=(n // tm,),
        in_specs=[
            row(D_MODEL),
            _const_spec((1, D_MODEL), (0, 0)),
            _const_spec((None, D_MODEL, W_PROJ), (l, 0, 0)),
            _const_spec((1, NORM_W), (0, 0)),
            _const_spec((1, NORM_W), (0, 0)),
            row(LANES),
            row(LANES),
            _const_spec((NORM_W, NORM_W), (0, 0)),
        ] + [pl.BlockSpec(memory_space=pl.ANY) for _ in aliased],
        out_specs=[row(W_ABC), row(W_D), row(W_D), row(W_D),
                   pl.BlockSpec((N_HEADS, None, LANES, tm), lambda i: (0, i, 0, 0)),
                   pl.BlockSpec((N_HEADS, None, VT_ROWS, tm), lambda i: (0, i, 0, 0)),
                   pl.BlockSpec((None, 2 * N_HEADS, HEAD_DIM, tm), lambda i: (l, 0, 0, jnp.minimum(i, npt - 1))),
                   sample_rows, prompt_rows, sample_rows],
        out_shape=[
            jax.ShapeDtypeStruct((n, W_ABC), F32),
            jax.ShapeDtypeStruct((n, W_D), BF16),
            jax.ShapeDtypeStruct((n, W_D), BF16),
            jax.ShapeDtypeStruct((n, W_D), BF16),
            jax.ShapeDtypeStruct((N_HEADS, n // tm, LANES, tm), BF16),
            jax.ShapeDtypeStruct((N_HEADS, n // tm, VT_ROWS, tm), BF16),
            jax.ShapeDtypeStruct((depth, 2 * N_HEADS, HEAD_DIM, sp), F32),
            jax.ShapeDtypeStruct((depth, n_dec, W_D), F32),
            jax.ShapeDtypeStruct((depth, sp * N_HEADS, LANES), F32),
            jax.ShapeDtypeStruct((depth, n_dec, W_D), F32),
        ],
        scratch_shapes=[pltpu.VMEM((tm, 3 * W_D), F32)],
        input_output_aliases={n_in + j: 6 + j for j in range(len(aliased))},
        compiler_params=pltpu.CompilerParams(
            dimension_semantics=("arbitrary",), vmem_limit_bytes=44 * MIB),
        name="inproj",
    )(x, g, w_in, qg, kg, cos_t, sin_t, ones_bd, *aliased)


MIX_ROWS = 128
SUBLANES = 8


def _shifted_rows(ref, base, rows, depth, tmp):
    out = {}
    for r in range(SUBLANES):
        js = [j for j in range(1, depth + 1) if (-j) % SUBLANES == r]
        if not js:
            continue
        start, length = base - max(js), max(js) - min(js) + rows
        tmp[r, 0:length, :] = ref[start:start + length, :]
        for j in js:
            out[j] = tmp[r, max(js) - j:max(js) - j + rows, :]
    return out


def _mixer_compute(t, pos0, z_ref, pw_ref, ps_ref, sw_ref, cw_ref, cb_ref, lg_ref, lb_ref,
                   y_ref, pool_o, sconv_o, cconv_o, ea, eb, ec, tmp):
    u = z_ref[:, 0:W_A]
    ea[HALO:HALO + t, :] = u
    eb[HALO:HALO + t, :] = z_ref[:, W_A + W_B:W_A + 2 * W_B] * z_ref[:, W_A + 2 * W_B:W_A + 3 * W_B]
    zc = z_ref[:, W_A + 3 * W_B:W_A + 3 * W_B + W_C]
    ec[HALO:HALO + t, :] = zc * jax.nn.sigmoid(z_ref[:, W_A + 3 * W_B + W_C:W_ABC])

    rows = min(t, MIX_ROWS)
    for r0 in range(0, t, rows):
        base = HALO + r0
        lane = lax.broadcasted_iota(jnp.int32, (rows, W_A), 1)
        cur = ea[base:base + rows, :]
        back = _shifted_rows(ea, base, rows, max(POOL_WINDOWS) - 1, tmp)
        acc = cur
        sums = {}
        for j in range(1, max(POOL_WINDOWS)):
            acc = acc + back[j]
            if j + 1 in POOL_WINDOWS:
                sums[j + 1] = acc
        tot = sums[POOL_WINDOWS[-1]]
        win = jnp.full((rows, W_A), float(POOL_WINDOWS[-1]), F32)
        for gi in range(len(POOL_WINDOWS) - 2, -1, -1):
            in_group = lane < (gi + 1) * GA
            tot = jnp.where(in_group, sums[POOL_WINDOWS[gi]], tot)
            win = jnp.where(in_group, float(POOL_WINDOWS[gi]), win)
        if pos0 is None:
            cnt = win
        else:
            pos1 = (pos0 + r0 + 1 + lax.broadcasted_iota(jnp.int32, (rows, W_A), 0)).astype(F32)
            cnt = jnp.minimum(pos1, win)
        d = (tot / cnt - cur).astype(BF16)
        y_ref[r0:r0 + rows, 0:W_A] = _dot(d, pw_ref[...]) * ps_ref[...]

        conv = sw_ref[SCONV_K - 1:SCONV_K, :] * eb[base:base + rows, :]
        for j in range(SCONV_K - 1):
            off = base - (SCONV_K - 1) + j
            conv = conv + sw_ref[j:j + 1, :] * eb[off:off + rows, :]
        y_ref[r0:r0 + rows, W_A:W_A + W_B] = z_ref[r0:r0 + rows, W_A:W_A + W_B] * conv

        back = _shifted_rows(ec, base, rows, CCONV_K - 1, tmp)
        conv = cw_ref[CCONV_K - 1:CCONV_K, :] * ec[base:base + rows, :]
        for j in range(CCONV_K - 1):
            conv = conv + cw_ref[j:j + 1, :] * back[CCONV_K - 1 - j]
        conv = conv + cb_ref[...]
        mu = jnp.mean(conv, axis=-1, keepdims=True)
        cen = conv - mu
        var = jnp.mean(cen * cen, axis=-1, keepdims=True)
        ln = cen * lax.rsqrt(var + EPS) * lg_ref[...] + lb_ref[...]
        y_ref[r0:r0 + rows, W_A + W_B:W_Y] = ln * jax.nn.sigmoid(ln)

    pool_o[...] = ea[HALO + t - 16:HALO + t, :]
    sconv_o[...] = eb[HALO + t - 8:HALO + t, :]
    cconv_o[...] = ec[HALO + t - 32:HALO + t, :]


def _mixer_prompt_kernel(z_ref, pw_ref, ps_ref, sw_ref, cw_ref, cb_ref, lg_ref, lb_ref,
                         y_ref, pool_o, sconv_o, cconv_o, ea, eb, ec, tmp, *, t):
    i = pl.program_id(0)

    @pl.when(i == 0)
    def _():
        zeros = jnp.zeros((HALO, W_A), F32)
        ea[0:HALO, :] = zeros
        eb[0:HALO, :] = zeros
        ec[0:HALO, :] = zeros

    @pl.when(i > 0)
    def _():
        ea[0:HALO, :] = ea[t:t + HALO, :]
        eb[0:HALO, :] = eb[t:t + HALO, :]
        ec[0:HALO, :] = ec[t:t + HALO, :]

    _mixer_compute(t, i * t, z_ref, pw_ref, ps_ref, sw_ref, cw_ref, cb_ref, lg_ref, lb_ref,
                   y_ref, pool_o, sconv_o, cconv_o, ea, eb, ec, tmp)


def _mixer_sample_kernel(z_ref, sp_ref, ss_ref, sc_ref, pw_ref, ps_ref, sw_ref, cw_ref, cb_ref,
                         lg_ref, lb_ref, y_in_ref, y_ref, pool_o, sconv_o, cconv_o, ea, eb, ec, tmp, *, t):
    del y_in_ref
    ea[0:HALO, :] = sp_ref[...]
    eb[0:HALO, :] = ss_ref[...]
    ec[0:HALO, :] = sc_ref[...]
    _mixer_compute(t, None, z_ref, pw_ref, ps_ref, sw_ref, cw_ref, cb_ref, lg_ref, lb_ref,
                   y_ref, pool_o, sconv_o, cconv_o, ea, eb, ec, tmp)


def _mixer_weight_specs():
    return [
        _const_spec((W_A, W_A), (0, 0)),
        _const_spec((1, W_A), (0, 0)),
        _const_spec((SCONV_K, W_B), (0, 0)),
        _const_spec((CCONV_K, W_C), (0, 0)),
        _const_spec((1, W_C), (0, 0)),
        _const_spec((1, W_C), (0, 0)),
        _const_spec((1, W_C), (0, 0)),
    ]


def _mixer_scratch(t):
    return [pltpu.VMEM((HALO + t, W_A), F32), pltpu.VMEM((HALO + t, W_B), F32),
            pltpu.VMEM((HALO + t, W_C), F32),
            pltpu.VMEM((SUBLANES, min(t, MIX_ROWS) + HALO, W_C), F32)]


def _mixer_prompt(cfg, zabc, weights):
    t = cfg.tmix
    const_out = lambda r: pl.BlockSpec((r, W_A), lambda i: (0, 0))
    return pl.pallas_call(
        functools.partial(_mixer_prompt_kernel, t=t),
        grid=(cfg.s_prompt // t,),
        in_specs=[pl.BlockSpec((t, W_ABC), lambda i: (i, 0))] + _mixer_weight_specs(),
        out_specs=[pl.BlockSpec((t, W_Y), lambda i: (i, 0)), const_out(16), const_out(8), const_out(32)],
        out_shape=[
            jax.ShapeDtypeStruct((cfg.n_tok, W_Y), F32),
            jax.ShapeDtypeStruct((16, W_A), F32),
            jax.ShapeDtypeStruct((8, W_B), F32),
            jax.ShapeDtypeStruct((32, W_C), F32),
        ],
        scratch_shapes=_mixer_scratch(t),
        compiler_params=pltpu.CompilerParams(dimension_semantics=("arbitrary",)),
        name="mixer_prompt",
    )(zabc, *weights)


def _mixer_sample(cfg, zabc, st_pool, st_sconv, st_cconv, weights, y_abc):
    t, nb = cfg.t_dec, cfg.n_dec
    row0 = cfg.s_prompt // t
    state_spec = lambda: pl.BlockSpec((None, HALO, W_A), lambda b: (b, 0, 0))
    out_state = lambda r: pl.BlockSpec((None, r, W_A), lambda b: (b, 0, 0))
    return pl.pallas_call(
        functools.partial(_mixer_sample_kernel, t=t),
        grid=(nb,),
        in_specs=[pl.BlockSpec((t, W_ABC), lambda b: (row0 + b, 0)),
                  state_spec(), state_spec(), state_spec()]
                 + _mixer_weight_specs()
                 + [pl.BlockSpec(memory_space=pl.ANY)],
        out_specs=[pl.BlockSpec((t, W_Y), lambda b: (row0 + b, 0)),
                   out_state(16), out_state(8), out_state(32)],
        out_shape=[
            jax.ShapeDtypeStruct((cfg.n_tok, W_Y), F32),
            jax.ShapeDtypeStruct((nb, 16, W_A), F32),
            jax.ShapeDtypeStruct((nb, 8, W_B), F32),
            jax.ShapeDtypeStruct((nb, 32, W_C), F32),
        ],
        scratch_shapes=_mixer_scratch(t),
        input_output_aliases={11: 0},
        compiler_params=pltpu.CompilerParams(dimension_semantics=("arbitrary",)),
        name="mixer_sample",
    )(zabc, st_pool, st_sconv, st_cconv, *weights, y_abc)


def _lambda(lq1, lk1, lq2, lk2, lam_init):
    s1 = jnp.sum(lq1[...] * lk1[...], axis=-1, keepdims=True)
    s2 = jnp.sum(lq2[...] * lk2[...], axis=-1, keepdims=True)
    return jnp.exp(s1) - jnp.exp(s2) + lam_init


def _diff_out(acc, l, lam, sg, lam_init, t):
    o = acc[0:t] / l[0:t] - lam * (acc[t:2 * t] / l[t:2 * t])
    return _rms(o, sg) * (1.0 - lam_init)


def _attn_prompt_kernel(qt_ref, qtn_ref, k_ref, vt_ref, lq1, lk1, lq2, lk2, sgc_ref, o_ref,
                        q2t_ref, q2tn_ref, s0_ref, s1_ref, mx0_ref, mx1_ref, m_ref, acc_ref, *, tk, lam_init):
    qi = pl.program_id(1)
    nq = 2 * tk
    chan = lax.broadcasted_iota(jnp.int32, (LANES, tk), 0)
    zero = jnp.zeros((LANES, tk), BF16)
    for src_ref, dst_ref in ((qt_ref, q2t_ref), (qtn_ref, q2tn_ref)):
        for half in range(2):
            qt = src_ref[half]
            dst_ref[:, half * tk:(half + 1) * tk] = jnp.where(chan < HEAD_DIM, qt, zero)
            dst_ref[:, nq + half * tk:nq + (half + 1) * tk] = jnp.where(chan >= HEAD_DIM, qt, zero)
    m_ref[...] = jnp.full(m_ref.shape, -jnp.inf, F32)
    acc_ref[...] = jnp.zeros(acc_ref.shape, F32)

    ncols = 2 * nq // COL_GROUPS

    def scores(j, s_ref, mx_ref, diagonal=None, group=None, queries=q2t_ref):
        cols = slice(None) if group is None else slice(group * ncols, (group + 1) * ncols)
        kb = k_ref[pl.ds(pl.multiple_of(j * tk, tk), tk), :]
        s = _dot(kb, queries[:, cols])
        if diagonal is not None:
            key = lax.broadcasted_iota(jnp.int32, s.shape, 0)
            col = lax.broadcasted_iota(jnp.int32, s.shape, 1) + (0 if group is None else group * ncols)
            key_chunk = diagonal * (tk // CHUNK) + key // CHUNK
            s = jnp.where(key_chunk <= (col & (nq - 1)) // CHUNK, s, -jnp.inf)
        s_ref[:, cols] = s
        mx_ref[:, cols] = jnp.max(s, axis=0, keepdims=True)

    def consume(j, s_ref, mx_ref, group=None):
        cols = slice(None) if group is None else slice(group * ncols, (group + 1) * ncols)
        m_prev = m_ref[:, cols]
        m_new = jnp.maximum(m_prev, mx_ref[:, cols])
        alpha = jnp.exp2(m_prev - m_new)
        p = jnp.exp2(s_ref[:, cols] - m_new).astype(BF16)
        acc_ref[:, cols] = alpha * acc_ref[:, cols] + _dot(vt_ref[j], p)
        m_ref[:, cols] = m_new

    @pl.when(qi == 0)
    def _():
        scores(0, s0_ref, mx0_ref, 0)

    def pair(i, first_diagonal):
        j = 2 * i
        for g in range(COL_GROUPS):
            scores(j + 1, s1_ref, mx1_ref, None, g)
            consume(j, s0_ref, mx0_ref, g)
        for g in range(COL_GROUPS):
            scores(j + 2, s0_ref, mx0_ref, first_diagonal, g)
            consume(j + 1, s1_ref, mx1_ref, g)

    def full_pair(i, carry):
        pair(i, None)
        return carry

    lax.fori_loop(0, qi - 1, full_pair, 0)

    @pl.when(qi > 0)
    def _():
        pair(qi - 1, 0)

    assert tk % ncols == 0
    late = [g for g in range(COL_GROUPS) if (g * ncols) % nq >= tk]
    for g in range(COL_GROUPS):
        if g in late:
            scores(2 * qi + 1, s1_ref, mx1_ref, 1, g)
        consume(2 * qi, s0_ref, mx0_ref, g)
        scores(0, s0_ref, mx0_ref, None, g, q2tn_ref)
    for g in late:
        consume(2 * qi + 1, s1_ref, mx1_ref, g)

    lam = _lambda(lq1, lk1, lq2, lk2, lam_init)
    acc = acc_ref[0:LANES, :]
    l = acc_ref[LANES:LANES + 1, :]
    ot = acc[:, 0:nq] * (1.0 / l[:, 0:nq]) - acc[:, nq:2 * nq] * (lam / l[:, nq:2 * nq])
    ms = jnp.mean(ot * ot, axis=0, keepdims=True)
    o_ref[...] = ot * lax.rsqrt(ms + EPS) * sgc_ref[...] * (1.0 - lam_init)


def _lam_specs():
    return [_const_spec((1, HEAD_DIM), (0, 0)) for _ in range(4)] + [_const_spec((1, LANES), (0, 0))]


def _attn_prompt(cfg, qt, kb, vt, lam_w, lam_init):
    s, tk = cfg.s_prompt, cfg.tm
    tq = 2 * tk
    assert tq & (tq - 1) == 0 and tk % CHUNK == 0 and s % tq == 0
    return pl.pallas_call(
        functools.partial(_attn_prompt_kernel, tk=tk, lam_init=lam_init),
        grid=(N_HEADS, s // tq),
        in_specs=[
            pl.BlockSpec((None, 2, LANES, tk), lambda h, i: (h, i, 0, 0)),
            pl.BlockSpec((None, 2, LANES, tk), lambda h, i: (h, jnp.minimum(i + 1, s // tq - 1), 0, 0)),
            pl.BlockSpec((s, LANES), lambda h, i: (0, h)),
            pl.BlockSpec((None, s // tk, VT_ROWS, tk), lambda h, i: (h, 0, 0, 0)),
        ] + _lam_specs()[:4] + [_const_spec((LANES, 1), (0, 0))],
        out_specs=pl.BlockSpec((LANES, tq), lambda h, i: (h, i)),
        out_shape=jax.ShapeDtypeStruct((W_D, s), F32),
        scratch_shapes=[
            pltpu.VMEM((LANES, 2 * tq), BF16),
            pltpu.VMEM((LANES, 2 * tq), BF16),
            pltpu.VMEM((tk, 2 * tq), F32),
            pltpu.VMEM((tk, 2 * tq), F32),
            pltpu.VMEM((1, 2 * tq), F32),
            pltpu.VMEM((1, 2 * tq), F32),
            pltpu.VMEM((1, 2 * tq), F32),
            pltpu.VMEM((VT_ROWS, 2 * tq), F32),
        ],
        compiler_params=pltpu.CompilerParams(
            dimension_semantics=("arbitrary", "arbitrary"), vmem_limit_bytes=52 * MIB),
        name="attn_prompt",
    )(qt, qt, kb, vt, *lam_w[:4], lam_w[4].reshape(LANES, 1))


def _attn_sample_kernel(q_ref, kn_ref, vn_ref, kc_ref, vc_ref, lq1, lk1, lq2, lk2, sg_ref,
                        o_ref, *, t, lam_init):
    lam = _lambda(lq1, lk1, lq2, lk2, lam_init)
    past = vc_ref.shape[0] // N_HEADS
    for h in range(N_HEADS):
        sl = slice(h * LANES, (h + 1) * LANES)
        s_past, s_new = [], []
        for c in range(2):
            ch = slice(h * LANES + c * HEAD_DIM, h * LANES + (c + 1) * HEAD_DIM)
            q = q_ref[:, ch]
            s_past.append(_dot(q, kc_ref[2 * h + c].astype(BF16)))
            s_new.append(_dot_t(q, kn_ref[:, ch]))
        s_past = jnp.concatenate(s_past, axis=0)
        s_new = jnp.concatenate(s_new, axis=0)
        v_past = vc_ref[pl.ds(h, past, stride=N_HEADS), :]
        m = jnp.maximum(jnp.max(s_past, axis=-1, keepdims=True), jnp.max(s_new, axis=-1, keepdims=True))
        p_past = jnp.exp(s_past - m)
        p_new = jnp.exp(s_new - m)
        l = jnp.sum(p_past, axis=-1, keepdims=True) + jnp.sum(p_new, axis=-1, keepdims=True)
        acc = _dot(p_past.astype(BF16), v_past.astype(BF16)) + _dot(p_new.astype(BF16), vn_ref[:, sl])
        o_ref[:, sl] = _diff_out(acc, l, lam, sg_ref[...], lam_init, t)


def _attn_sample(cfg, l, qb, kb, vb, cache_k, cache_v, lam_w, lam_init):
    t, nb = cfg.t_dec, cfg.n_dec
    row0 = cfg.s_prompt // t
    new_rows = lambda: pl.BlockSpec((t, W_D), lambda b: (row0 + b, 0))
    cache = lambda a: pl.BlockSpec((None, None) + a.shape[2:], lambda b: (l, b) + (0,) * (a.ndim - 2))
    return pl.pallas_call(
        functools.partial(_attn_sample_kernel, t=t, lam_init=lam_init),
        grid=(nb,),
        in_specs=[new_rows(), new_rows(), new_rows(), cache(cache_k), cache(cache_v)] + _lam_specs(),
        out_specs=pl.BlockSpec((t, W_D), lambda b: (b, 0)),
        out_shape=jax.ShapeDtypeStruct((nb * t, W_D), F32),
        compiler_params=pltpu.CompilerParams(
            dimension_semantics=("arbitrary",), vmem_limit_bytes=40 * MIB),
        name="attn_sample",
    )(qb, kb, vb, cache_k, cache_v, *lam_w)


def _merge_kernel(x_ref, yabc_ref, ydt_ref, yds_ref, g_ref, wg_ref, wpa_ref, wpb_ref, wpc_ref, wpd_ref, wo_ref,
                  o_ref, *, n_prompt_tiles):
    is_prompt = pl.program_id(0) < n_prompt_tiles
    half = x_ref.shape[0] // 2
    for rows in (slice(0, half), slice(half, 2 * half)):
        yd = jnp.where(is_prompt, ydt_ref[:, rows].T, yds_ref[rows, :])
        x = x_ref[rows, :]
        h = _rms(x, g_ref[...]).astype(BF16)
        branches = (
            (yabc_ref[rows, 0:W_A], wpa_ref),
            (yabc_ref[rows, W_A:W_A + W_B], wpb_ref),
            (yabc_ref[rows, W_A + W_B:W_Y], wpc_ref),
            (yd, wpd_ref),
        )
        merged = None
        for i, (y, wp_ref) in enumerate(branches):
            gate = jax.nn.sigmoid(_dot(h, wg_ref[:, i * D_MODEL:(i + 1) * D_MODEL]))
            term = gate * _dot(y.astype(BF16), wp_ref[...])
            merged = term if merged is None else merged + term
        o_ref[rows, :] = x + _dot(merged.astype(BF16), wo_ref[...])


def _merge(cfg, l, x, y_abc, yd_t, yd_s, g, w_gate, wpa, wpb, wpc, wpd, wo):
    n, tm, sp = cfg.n_tok, cfg.tm, cfg.s_prompt
    assert sp % tm == 0 and (n - sp) % tm == 0
    npt = sp // tm
    row = lambda w: pl.BlockSpec((tm, w), lambda i: (i, 0))
    return pl.pallas_call(
        functools.partial(_merge_kernel, n_prompt_tiles=npt),
        grid=(n // tm,),
        in_specs=[
            row(D_MODEL), row(W_Y),
            pl.BlockSpec((W_D, tm), lambda i: (0, jnp.minimum(i, npt - 1))),
            pl.BlockSpec((tm, W_D), lambda i: (jnp.maximum(i - npt, 0), 0)),
            _const_spec((1, D_MODEL), (0, 0)),
            _const_spec((None, D_MODEL, 4 * D_MODEL), (l, 0, 0)),
            _const_spec((None, W_A, D_MODEL), (l, 0, 0)),
            _const_spec((None, W_B, D_MODEL), (l, 0, 0)),
            _const_spec((None, W_C, D_MODEL), (l, 0, 0)),
            _const_spec((None, W_D, D_MODEL), (l, 0, 0)),
            _const_spec((None, D_MODEL, D_MODEL), (l, 0, 0)),
        ],
        out_specs=row(D_MODEL),
        out_shape=jax.ShapeDtypeStruct((n, D_MODEL), F32),
        compiler_params=pltpu.CompilerParams(
            dimension_semantics=("arbitrary",), vmem_limit_bytes=48 * MIB),
        name="merge",
    )(x, y_abc, yd_t, yd_s, g, w_gate, wpa, wpb, wpc, wpd, wo)


def _rope_tables(cfg):
    half = HEAD_DIM // 2
    inv_freq = ROPE_THETA ** (-jnp.arange(half, dtype=F32) / half)
    pos = jnp.concatenate([jnp.arange(cfg.s_prompt), jnp.tile(cfg.past + jnp.arange(cfg.t_dec), cfg.n_dec)])
    ang = pos.astype(F32)[:, None] * inv_freq[None, :]
    cos, sin = jnp.cos(ang), jnp.sin(ang)
    reps = LANES // HEAD_DIM
    cos_t = jnp.tile(jnp.concatenate([cos, cos], axis=1), (1, reps))
    sin_t = jnp.tile(jnp.concatenate([-sin, sin], axis=1), (1, reps))
    return cos_t, sin_t


def _pad_rows_top(a, rows):
    return jnp.pad(a, ((0, 0), (0, 0), (rows - a.shape[2], 0), (0, 0)))


def _forward(cfg, x_prompt, x_sample, cache_k, cache_v, state_pool, state_sconv, state_cconv,
             g_ffn1, w1_gate, w1_up, w1_down, g_mix, w_in, pool_w, pool_scale, sconv_w,
             cconv_w, cconv_b, ln_g, ln_b, q_norm_g, k_norm_g, lam_q1, lam_k1, lam_q2, lam_k2,
             subln_g, wp_a, wp_b, wp_c, wp_d, w_out, g_ffn2, w2_gate, w2_up, w2_down):
    depth = w_in.shape[0]
    sp, nb, td = cfg.s_prompt, cfg.n_dec, cfg.t_dec
    x = (x_prompt.reshape(sp, D_MODEL), x_sample.reshape(nb * td, D_MODEL))

    bf = lambda w: w.astype(BF16)
    w1g, w1u, w1d = bf(w1_gate), bf(w1_up), bf(w1_down)
    w2g, w2u, w2d = bf(w2_gate), bf(w2_up), bf(w2_down)
    w_proj, w_gate = bf(w_in[:, :, :W_PROJ]), bf(w_in[:, :, W_PROJ:])
    wpa, wpb, wpc, wpd, wo = bf(wp_a), bf(wp_b), bf(wp_c), bf(wp_d), bf(w_out)
    cos_t, sin_t = _rope_tables(cfg)
    ones_bd = jnp.kron(jnp.eye(NORM_W // HEAD_DIM, dtype=F32), jnp.ones((HEAD_DIM, HEAD_DIM), F32)).astype(BF16)
    eye_g = jnp.eye(len(POOL_WINDOWS), dtype=F32)
    ck = jnp.transpose(cache_k, (0, 1, 3, 4, 2))
    cv = cache_v.reshape(depth, nb, cfg.past * N_HEADS, 2 * HEAD_DIM)
    st_pool = _pad_rows_top(state_pool, HALO)
    st_sconv = _pad_rows_top(state_sconv, HALO)
    st_cconv = _pad_rows_top(state_cconv, HALO)
    row = lambda a: a.reshape(1, -1)

    outs = [[] for _ in range(6)]
    caches = None
    for l in range(depth):
        lam_init = 0.8 - 0.6 * math.exp(-0.3 * l)
        x = _ffn(cfg, l, x, row(g_ffn1[l]), w1g, w1u, w1d)
        qg = row(jnp.tile(q_norm_g[l], NORM_W // HEAD_DIM))
        kg = row(jnp.tile(k_norm_g[l], NORM_W // HEAD_DIM))
        zabc, qb, kb, vb, qt, vt, *caches = _inproj(cfg, l, depth, x, row(g_mix[l]), w_proj, qg, kg, cos_t, sin_t,
                                                    ones_bd, caches)

        pw_bd = (eye_g[:, None, :, None] * pool_w[l][:, :, None, :]).reshape(W_A, W_A).astype(BF16)
        mix_w = (pw_bd, row(pool_scale[l]), sconv_w[l], cconv_w[l], row(cconv_b[l]), row(ln_g[l]), row(ln_b[l]))
        y_abc, pool_p, sconv_p, cconv_p = _mixer_prompt(cfg, zabc, mix_w)
        y_abc, pool_s, sconv_s, cconv_s = _mixer_sample(cfg, zabc, st_pool[l], st_sconv[l], st_cconv[l],
                                                        mix_w, y_abc)

        lam_w = (row(lam_q1[l]), row(lam_k1[l]), row(lam_q2[l]), row(lam_k2[l]), row(subln_g[l]))
        yd_t = _attn_prompt(cfg, qt, kb, vt, lam_w, lam_init)
        yd_s = _attn_sample(cfg, l, qb, kb, vb, ck, cv, lam_w, lam_init)

        x = _merge(cfg, l, x, y_abc, yd_t, yd_s, row(g_mix[l]), w_gate, wpa, wpb, wpc, wpd, wo)
        x = _ffn(cfg, l, x, row(g_ffn2[l]), w2g, w2u, w2d, split_out=(l == depth - 1))

        outs[0].append(pool_p[None, 16 - POOL_STATE:])
        outs[1].append(sconv_p[None, 8 - (SCONV_K - 1):])
        outs[2].append(cconv_p[None, 32 - (CCONV_K - 1):])
        outs[3].append(pool_s[:, 16 - POOL_STATE:])
        outs[4].append(sconv_s[:, 8 - (SCONV_K - 1):])
        outs[5].append(cconv_s[:, 32 - (CCONV_K - 1):])

    kt_all, ks_all, vp_all, vs_all = caches
    y_prompt = x[0].reshape(1, sp, D_MODEL)
    y_sample = x[1].reshape(nb, td, D_MODEL)
    k_prompt = jnp.transpose(kt_all.reshape(depth, 1, 2 * N_HEADS, HEAD_DIM, sp), (0, 1, 4, 2, 3))
    v_prompt = vp_all.reshape(depth, 1, sp, N_HEADS, 2 * HEAD_DIM)
    k_sample = ks_all.reshape(depth, nb, td, 2 * N_HEADS, HEAD_DIM)
    v_sample = vs_all.reshape(depth, nb, td, N_HEADS, 2 * HEAD_DIM)
    st = [jnp.stack(o) for o in outs]
    return (y_prompt, y_sample, k_prompt, v_prompt, st[0], st[1], st[2], k_sample, v_sample, st[3], st[4], st[5])


def kernel(x_prompt, x_sample, cache_k, cache_v, state_pool, state_sconv, state_cconv, g_ffn1, w1_gate, w1_up, w1_down, g_mix, w_in, pool_w, pool_scale, sconv_w, cconv_w, cconv_b, ln_g, ln_b, q_norm_g, k_norm_g, lam_q1, lam_k1, lam_q2, lam_k2, subln_g, wp_a, wp_b, wp_c, wp_d, w_out, g_ffn2, w2_gate, w2_up, w2_down):
    assert x_prompt.shape[0] == 1
    cfg = Cfg(s_prompt=x_prompt.shape[1], n_dec=x_sample.shape[0], t_dec=x_sample.shape[1],
              past=cache_k.shape[2], tm=512, tmix=512)
    return _forward(cfg, x_prompt, x_sample, cache_k, cache_v, state_pool, state_sconv, state_cconv,
                    g_ffn1, w1_gate, w1_up, w1_down, g_mix, w_in, pool_w, pool_scale, sconv_w,
                    cconv_w, cconv_b, ln_g, ln_b, q_norm_g, k_norm_g, lam_q1, lam_k1, lam_q2, lam_k2,
                    subln_g, wp_a, wp_b, wp_c, wp_d, w_out, g_ffn2, w2_gate, w2_up, w2_down)
```

```python
import functools
import math
from typing import NamedTuple

import jax
import jax.numpy as jnp
from jax import lax
from jax.experimental import pallas as pl
from jax.experimental.pallas import tpu as pltpu

F32 = jnp.float32
BF16 = jnp.bfloat16

D_MODEL = 1024
DEPTH = 4
CHUNK = 64
POOL_WINDOWS = (2, 4, 8, 16)
W_A = 256
GA = 64
POOL_STATE = 15
W_B = 256
SCONV_K = 3
W_C = 256
CCONV_K = 31
HEAD_DIM = 64
N_HEADS = 4
W_D = 512
ROPE_THETA = 10000.0
D_FF = 2816
EPS = 1e-6
W_ABC = W_A + 3 * W_B + 2 * W_C
W_PROJ = W_ABC + 3 * W_D
W_Y = W_A + W_B + W_C

LANES = 128
NORM_W = 256
COL_GROUPS = 4
VT_ROWS = LANES + 16
LOG2E = 1.4426950408889634
HALO = 32
MIB = 1024 * 1024


class Cfg(NamedTuple):
    s_prompt: int
    n_dec: int
    t_dec: int
    past: int
    tm: int
    tmix: int

    @property
    def n_tok(self):
        return self.s_prompt + self.n_dec * self.t_dec


def _const_spec(shape, index):
    return pl.BlockSpec(shape, lambda *_: index, pipeline_mode=pl.Buffered(1))


def _rms(x, g):
    ms = jnp.mean(x * x, axis=-1, keepdims=True)
    return x * lax.rsqrt(ms + EPS) * g


def _dot(a, b):
    return jnp.dot(a, b, preferred_element_type=F32)


def _dot_t(a, b):
    return lax.dot_general(a, b, (((1,), (1,)), ((), ())), preferred_element_type=F32)


def _ffn_kernel(*refs, n_prompt_tiles, split_in, split_out):
    refs = list(refs)
    is_prompt = pl.program_id(0) < n_prompt_tiles
    if split_in:
        xp_ref, xs_ref = refs[0:2]
        del refs[0:2]
        x = jnp.where(is_prompt, xp_ref[...], xs_ref[...])
    else:
        x = refs.pop(0)[...]
    g_ref, wg_ref, wu_ref, wd_ref = refs[0:4]
    halves = []
    for xh in (x[0:x.shape[0] // 2], x[x.shape[0] // 2:]):
        h = _rms(xh, g_ref[...]).astype(BF16)
        a = _dot(h, wg_ref[...])
        u = _dot(h, wu_ref[...])
        act = (a * jax.nn.sigmoid(a) * u).astype(BF16)
        halves.append(0.5 * _dot(act, wd_ref[...]))
    half_step = jnp.concatenate(halves, axis=0)
    if split_out:
        op_ref, os_ref = refs[4:6]

        @pl.when(is_prompt)
        def _():
            op_ref[...] = x + half_step

        @pl.when(jnp.logical_not(is_prompt))
        def _():
            os_ref[...] = x + half_step
    else:
        refs[4][...] = x + half_step


def _ffn(cfg, l, x, g, wg, wu, wd, split_out=False):
    n, tm, sp = cfg.n_tok, cfg.tm, cfg.s_prompt
    assert sp % tm == 0 and (n - sp) % tm == 0
    npt = sp // tm
    split_in = isinstance(x, tuple)
    rows = pl.BlockSpec((tm, D_MODEL), lambda i: (i, 0))
    prompt_rows = pl.BlockSpec((tm, D_MODEL), lambda i: (jnp.minimum(i, npt - 1), 0))
    sample_rows = pl.BlockSpec((tm, D_MODEL), lambda i: (jnp.maximum(i - npt, 0), 0))
    whole = jax.ShapeDtypeStruct((n, D_MODEL), F32)
    halves = [jax.ShapeDtypeStruct((sp, D_MODEL), F32), jax.ShapeDtypeStruct((n - sp, D_MODEL), F32)]
    return pl.pallas_call(
        functools.partial(_ffn_kernel, n_prompt_tiles=npt, split_in=split_in, split_out=split_out),
        grid=(n // tm,),
        in_specs=([prompt_rows, sample_rows] if split_in else [rows]) + [
            _const_spec((1, D_MODEL), (0, 0)),
            _const_spec((None, D_MODEL, D_FF), (l, 0, 0)),
            _const_spec((None, D_MODEL, D_FF), (l, 0, 0)),
            _const_spec((None, D_FF, D_MODEL), (l, 0, 0)),
        ],
        out_specs=[prompt_rows, sample_rows] if split_out else rows,
        out_shape=halves if split_out else whole,
        compiler_params=pltpu.CompilerParams(
            dimension_semantics=("arbitrary",), vmem_limit_bytes=52 * MIB),
        name="ffn",
    )(*(x if split_in else (x,)), g, wg, wu, wd)


def _inproj_kernel(x_ref, g_ref, w_ref, qg_ref, kg_ref, cos_ref, sin_ref, ones_ref, *rest, n_prompt_tiles):
    zabc_ref, qb_ref, kb_ref, vb_ref, qt_ref, vt_ref, kt_ref, ks_ref, vp_ref, vs_ref, zqkv_ref = rest[-11:]
    is_prompt = pl.program_id(0) < n_prompt_tiles
    h = _rms(x_ref[...], g_ref[...]).astype(BF16)
    assert W_ABC % (2 * NORM_W) == 0 and W_ABC // (2 * NORM_W) <= W_D // LANES
    cos = cos_ref[...]
    sin = sin_ref[...]
    ones = ones_ref[...]
    lane = lax.broadcasted_iota(jnp.int32, cos.shape, 1)
    first_half = (lane & (HEAD_DIM // 2)) == 0

    def head_norm(z, g):
        ss = z * z
        hi = ss.astype(BF16)
        lo = (ss - hi.astype(F32)).astype(BF16)
        tot = _dot(hi, ones) + _dot(lo, ones)
        return z * lax.rsqrt(tot * (1.0 / HEAD_DIM) + EPS) * g

    def rope(y):
        half = HEAD_DIM // 2
        partner = jnp.where(first_half, pltpu.roll(y, LANES - half, 1), pltpu.roll(y, half, 1))
        return y * cos + partner * sin

    nw = ones.shape[0]
    abc_w = 2 * NORM_W
    n_chunks = W_D // LANES
    for g0 in range(0, W_D, nw):
        for part in range(3):
            lo = part * W_D + g0
            zqkv_ref[:, lo:lo + nw] = _dot(h, w_ref[:, W_ABC + lo:W_ABC + lo + nw])
    yq = [head_norm(zqkv_ref[:, g0:g0 + nw], qg_ref[...]) for g0 in range(0, W_D, nw)]
    yk = [head_norm(zqkv_ref[:, W_D + g0:W_D + g0 + nw], kg_ref[...]) for g0 in range(0, W_D, nw)]
    for c in range(n_chunks):
        sl = slice(c * LANES, (c + 1) * LANES)
        grp, off = divmod(c * LANES, nw)
        q = rope(yq[grp][:, off:off + LANES]) * (HEAD_DIM ** -0.5)
        qb_ref[:, sl] = q.astype(BF16)
        qt_ref[c] = (q * LOG2E).T.astype(BF16)
        k = rope(yk[grp][:, off:off + LANES])
        kb_ref[:, sl] = k.astype(BF16)
        zqkv_ref[:, W_D + c * LANES:W_D + (c + 1) * LANES] = k
        v = zqkv_ref[:, 2 * W_D + c * LANES:2 * W_D + (c + 1) * LANES]
        vb_ref[:, sl] = v.astype(BF16)
        vt_ref[c, 0:LANES, :] = v.T.astype(BF16)
        vt_ref[c, LANES:VT_ROWS, :] = jnp.ones((VT_ROWS - LANES, v.shape[0]), BF16)
        if c * abc_w < W_ABC:
            zabc_ref[:, c * abc_w:(c + 1) * abc_w] = _dot(h, w_ref[:, c * abc_w:(c + 1) * abc_w])

    @pl.when(is_prompt)
    def _():
        for c in range(W_D // LANES):
            kt = zqkv_ref[:, W_D + c * LANES:W_D + (c + 1) * LANES].T
            kt_ref[2 * c] = kt[0:HEAD_DIM]
            kt_ref[2 * c + 1] = kt[HEAD_DIM:LANES]
        for hd in range(N_HEADS):
            vp_ref[pl.ds(hd, kt_ref.shape[-1], stride=N_HEADS), :] = (
                zqkv_ref[:, 2 * W_D + hd * LANES:2 * W_D + (hd + 1) * LANES])

    @pl.when(jnp.logical_not(is_prompt))
    def _():
        ks_ref[...] = zqkv_ref[:, W_D:2 * W_D]
        vs_ref[...] = zqkv_ref[:, 2 * W_D:3 * W_D]


def _inproj(cfg, l, depth, x, g, w_in, qg, kg, cos_t, sin_t, ones_bd, caches):
    n, tm, sp = cfg.n_tok, cfg.tm, cfg.s_prompt
    n_dec = n - sp
    assert sp % tm == 0 and n_dec % tm == 0
    npt = sp // tm
    row = lambda w: pl.BlockSpec((tm, w), lambda i: (i, 0))
    prompt_rows = pl.BlockSpec((None, tm * N_HEADS, LANES), lambda i: (l, jnp.minimum(i, npt - 1), 0))
    sample_rows = pl.BlockSpec((None, tm, W_D), lambda i: (l, jnp.maximum(i - npt, 0), 0))
    n_in = 8
    aliased = [] if caches is None else list(caches)
    return pl.pallas_call(
        functools.partial(_inproj_kernel, n_prompt_tiles=npt),
        grid=(n // tm,),
        in_specs=[
            row(D_MODEL),
            _const_spec((1, D_MODEL), (0, 0)),
            _const_spec((None, D_MODEL, W_PROJ), (l, 0, 0)),
            _const_spec((1, NORM_W), (0, 0)),
            _const_spec((1, NORM_W), (0, 0)),
            row(LANES),
            row(LANES),
            _const_spec((NORM_W, NORM_W), (0, 0)),
        ] + [pl.BlockSpec(memory_space=pl.ANY) for _ in aliased],
        out_specs=[row(W_ABC), row(W_D), row(W_D), row(W_D),
                   pl.BlockSpec((N_HEADS, None, LANES, tm), lambda i: (0, i, 0, 0)),
                   pl.BlockSpec((N_HEADS, None, VT_ROWS, tm), lambda i: (0, i, 0, 0)),
                   pl.BlockSpec((None, 2 * N_HEADS, HEAD_DIM, tm), lambda i: (l, 0, 0, jnp.minimum(i, npt - 1))),
                   sample_rows, prompt_rows, sample_rows],
        out_shape=[
            jax.ShapeDtypeStruct((n, W_ABC), F32),
            jax.ShapeDtypeStruct((n, W_D), BF16),
            jax.ShapeDtypeStruct((n, W_D), BF16),
            jax.ShapeDtypeStruct((n, W_D), BF16),
            jax.ShapeDtypeStruct((N_HEADS, n // tm, LANES, tm), BF16),
            jax.ShapeDtypeStruct((N_HEADS, n // tm, VT_ROWS, tm), BF16),
            jax.ShapeDtypeStruct((depth, 2 * N_HEADS, HEAD_DIM, sp), F32),
            jax.ShapeDtypeStruct((depth, n_dec, W_D), F32),
            jax.ShapeDtypeStruct((depth, sp * N_HEADS, LANES), F32),
            jax.ShapeDtypeStruct((depth, n_dec, W_D), F32),
        ],
        scratch_shapes=[pltpu.VMEM((tm, 3 * W_D), F32)],
        input_output_aliases={n_in + j: 6 + j for j in range(len(aliased))},
        compiler_params=pltpu.CompilerParams(
            dimension_semantics=("arbitrary",), vmem_limit_bytes=44 * MIB),
        name="inproj",
    )(x, g, w_in, qg, kg, cos_t, sin_t, ones_bd, *aliased)


SAMPLE_MIX_GROUP = 8
MIX_ROWS = 128
SUBLANES = 8


def _shifted_rows(ref, base, rows, depth, tmp):
    out = {}
    for r in range(SUBLANES):
        js = [j for j in range(1, depth + 1) if (-j) % SUBLANES == r]
        if not js:
            continue
        start, length = base - max(js), max(js) - min(js) + rows
        tmp[r, 0:length, :] = ref[start:start + length, :]
        for j in js:
            out[j] = tmp[r, max(js) - j:max(js) - j + rows, :]
    return out


def _mixer_compute(t, pos0, z_ref, pw_ref, ps_ref, sw_ref, cw_ref, cb_ref, lg_ref, lb_ref,
                   y_ref, pool_o, sconv_o, cconv_o, ea, eb, ec, tmp):
    u = z_ref[:, 0:W_A]
    ea[HALO:HALO + t, :] = u
    eb[HALO:HALO + t, :] = z_ref[:, W_A + W_B:W_A + 2 * W_B] * z_ref[:, W_A + 2 * W_B:W_A + 3 * W_B]
    zc = z_ref[:, W_A + 3 * W_B:W_A + 3 * W_B + W_C]
    ec[HALO:HALO + t, :] = zc * jax.nn.sigmoid(z_ref[:, W_A + 3 * W_B + W_C:W_ABC])

    rows = min(t, MIX_ROWS)
    for r0 in range(0, t, rows):
        base = HALO + r0
        lane = lax.broadcasted_iota(jnp.int32, (rows, W_A), 1)
        cur = ea[base:base + rows, :]
        back = _shifted_rows(ea, base, rows, max(POOL_WINDOWS) - 1, tmp)
        acc = cur
        sums = {}
        for j in range(1, max(POOL_WINDOWS)):
            acc = acc + back[j]
            if j + 1 in POOL_WINDOWS:
                sums[j + 1] = acc
        tot = sums[POOL_WINDOWS[-1]]
        win = jnp.full((rows, W_A), float(POOL_WINDOWS[-1]), F32)
        for gi in range(len(POOL_WINDOWS) - 2, -1, -1):
            in_group = lane < (gi + 1) * GA
            tot = jnp.where(in_group, sums[POOL_WINDOWS[gi]], tot)
            win = jnp.where(in_group, float(POOL_WINDOWS[gi]), win)
        if pos0 is None:
            cnt = win
        else:
            pos1 = (pos0 + r0 + 1 + lax.broadcasted_iota(jnp.int32, (rows, W_A), 0)).astype(F32)
            cnt = jnp.minimum(pos1, win)
        d = (tot / cnt - cur).astype(BF16)
        y_ref[r0:r0 + rows, 0:W_A] = _dot(d, pw_ref[...]) * ps_ref[...]

        conv = sw_ref[SCONV_K - 1:SCONV_K, :] * eb[base:base + rows, :]
        for j in range(SCONV_K - 1):
            off = base - (SCONV_K - 1) + j
            conv = conv + sw_ref[j:j + 1, :] * eb[off:off + rows, :]
        y_ref[r0:r0 + rows, W_A:W_A + W_B] = z_ref[r0:r0 + rows, W_A:W_A + W_B] * conv

        back = _shifted_rows(ec, base, rows, CCONV_K - 1, tmp)
        conv = cw_ref[CCONV_K - 1:CCONV_K, :] * ec[base:base + rows, :]
        for j in range(CCONV_K - 1):
            conv = conv + cw_ref[j:j + 1, :] * back[CCONV_K - 1 - j]
        conv = conv + cb_ref[...]
        mu = jnp.mean(conv, axis=-1, keepdims=True)
        cen = conv - mu
        var = jnp.mean(cen * cen, axis=-1, keepdims=True)
        ln = cen * lax.rsqrt(var + EPS) * lg_ref[...] + lb_ref[...]
        y_ref[r0:r0 + rows, W_A + W_B:W_Y] = ln * jax.nn.sigmoid(ln)

    pool_o[...] = ea[HALO + t - 16:HALO + t, :]
    sconv_o[...] = eb[HALO + t - 8:HALO + t, :]
    cconv_o[...] = ec[HALO + t - 32:HALO + t, :]


def _mixer_prompt_kernel(z_ref, pw_ref, ps_ref, sw_ref, cw_ref, cb_ref, lg_ref, lb_ref,
                         y_ref, pool_o, sconv_o, cconv_o, ea, eb, ec, tmp, *, t):
    i = pl.program_id(0)

    @pl.when(i == 0)
    def _():
        zeros = jnp.zeros((HALO, W_A), F32)
        ea[0:HALO, :] = zeros
        eb[0:HALO, :] = zeros
        ec[0:HALO, :] = zeros

    @pl.when(i > 0)
    def _():
        ea[0:HALO, :] = ea[t:t + HALO, :]
        eb[0:HALO, :] = eb[t:t + HALO, :]
        ec[0:HALO, :] = ec[t:t + HALO, :]

    _mixer_compute(t, i * t, z_ref, pw_ref, ps_ref, sw_ref, cw_ref, cb_ref, lg_ref, lb_ref,
                   y_ref, pool_o, sconv_o, cconv_o, ea, eb, ec, tmp)


def _mixer_sample_kernel(z_ref, sp_ref, ss_ref, sc_ref, pw_ref, ps_ref, sw_ref, cw_ref, cb_ref,
                         lg_ref, lb_ref, y_in_ref, y_ref, pool_o, sconv_o, cconv_o, ea, eb, ec, tmp, *, t, group):
    del y_in_ref
    for s in range(group):
        rows = pl.ds(s * t, t)
        ea[0:HALO, :] = sp_ref[s]
        eb[0:HALO, :] = ss_ref[s]
        ec[0:HALO, :] = sc_ref[s]
        _mixer_compute(t, None, z_ref.at[rows], pw_ref, ps_ref, sw_ref, cw_ref, cb_ref, lg_ref, lb_ref,
                       y_ref.at[rows], pool_o.at[s], sconv_o.at[s], cconv_o.at[s], ea, eb, ec, tmp)


def _mixer_weight_specs():
    return [
        _const_spec((W_A, W_A), (0, 0)),
        _const_spec((1, W_A), (0, 0)),
        _const_spec((SCONV_K, W_B), (0, 0)),
        _const_spec((CCONV_K, W_C), (0, 0)),
        _const_spec((1, W_C), (0, 0)),
        _const_spec((1, W_C), (0, 0)),
        _const_spec((1, W_C), (0, 0)),
    ]


def _mixer_scratch(t):
    return [pltpu.VMEM((HALO + t, W_A), F32), pltpu.VMEM((HALO + t, W_B), F32),
            pltpu.VMEM((HALO + t, W_C), F32),
            pltpu.VMEM((SUBLANES, min(t, MIX_ROWS) + HALO, W_C), F32)]


def _mixer_prompt(cfg, zabc, weights):
    t = cfg.tmix
    const_out = lambda r: pl.BlockSpec((r, W_A), lambda i: (0, 0))
    return pl.pallas_call(
        functools.partial(_mixer_prompt_kernel, t=t),
        grid=(cfg.s_prompt // t,),
        in_specs=[pl.BlockSpec((t, W_ABC), lambda i: (i, 0))] + _mixer_weight_specs(),
        out_specs=[pl.BlockSpec((t, W_Y), lambda i: (i, 0)), const_out(16), const_out(8), const_out(32)],
        out_shape=[
            jax.ShapeDtypeStruct((cfg.n_tok, W_Y), F32),
            jax.ShapeDtypeStruct((16, W_A), F32),
            jax.ShapeDtypeStruct((8, W_B), F32),
            jax.ShapeDtypeStruct((32, W_C), F32),
        ],
        scratch_shapes=_mixer_scratch(t),
        compiler_params=pltpu.CompilerParams(dimension_semantics=("arbitrary",)),
        name="mixer_prompt",
    )(zabc, *weights)


def _mixer_sample(cfg, zabc, st_pool, st_sconv, st_cconv, weights, y_abc):
    t, nb = cfg.t_dec, cfg.n_dec
    group = math.gcd(nb, SAMPLE_MIX_GROUP)
    assert cfg.s_prompt % (group * t) == 0
    row0 = cfg.s_prompt // (group * t)
    state_spec = lambda: pl.BlockSpec((group, HALO, W_A), lambda b: (b, 0, 0))
    out_state = lambda r: pl.BlockSpec((group, r, W_A), lambda b: (b, 0, 0))
    return pl.pallas_call(
        functools.partial(_mixer_sample_kernel, t=t, group=group),
        grid=(nb // group,),
        in_specs=[pl.BlockSpec((group * t, W_ABC), lambda b: (row0 + b, 0)),
                  state_spec(), state_spec(), state_spec()]
                 + _mixer_weight_specs()
                 + [pl.BlockSpec(memory_space=pl.ANY)],
        out_specs=[pl.BlockSpec((group * t, W_Y), lambda b: (row0 + b, 0)),
                   out_state(16), out_state(8), out_state(32)],
        out_shape=[
            jax.ShapeDtypeStruct((cfg.n_tok, W_Y), F32),
            jax.ShapeDtypeStruct((nb, 16, W_A), F32),
            jax.ShapeDtypeStruct((nb, 8, W_B), F32),
            jax.ShapeDtypeStruct((nb, 32, W_C), F32),
        ],
        scratch_shapes=_mixer_scratch(t),
        input_output_aliases={11: 0},
        compiler_params=pltpu.CompilerParams(dimension_semantics=("arbitrary",)),
        name="mixer_sample",
    )(zabc, st_pool, st_sconv, st_cconv, *weights, y_abc)


def _lambda(lq1, lk1, lq2, lk2, lam_init):
    s1 = jnp.sum(lq1[...] * lk1[...], axis=-1, keepdims=True)
    s2 = jnp.sum(lq2[...] * lk2[...], axis=-1, keepdims=True)
    return jnp.exp(s1) - jnp.exp(s2) + lam_init


def _diff_out(acc, l, lam, sg, lam_init, t):
    o = acc[0:t] / l[0:t] - lam * (acc[t:2 * t] / l[t:2 * t])
    return _rms(o, sg) * (1.0 - lam_init)


def _attn_prompt_kernel(qt_ref, qtn_ref, k_ref, vt_ref, lq1, lk1, lq2, lk2, sgc_ref, o_ref,
                        q2t_ref, q2tn_ref, s0_ref, s1_ref, mx0_ref, mx1_ref, m_ref, acc_ref, *, tk, lam_init):
    qi = pl.program_id(1)
    nq = 2 * tk
    chan = lax.broadcasted_iota(jnp.int32, (LANES, tk), 0)
    zero = jnp.zeros((LANES, tk), BF16)
    for src_ref, dst_ref in ((qt_ref, q2t_ref), (qtn_ref, q2tn_ref)):
        for half in range(2):
            qt = src_ref[half]
            dst_ref[:, half * tk:(half + 1) * tk] = jnp.where(chan < HEAD_DIM, qt, zero)
            dst_ref[:, nq + half * tk:nq + (half + 1) * tk] = jnp.where(chan >= HEAD_DIM, qt, zero)
    m_ref[...] = jnp.full(m_ref.shape, -jnp.inf, F32)
    acc_ref[...] = jnp.zeros(acc_ref.shape, F32)

    ncols = 2 * nq // COL_GROUPS

    def scores(j, s_ref, mx_ref, diagonal=None, group=None, queries=q2t_ref):
        cols = slice(None) if group is None else slice(group * ncols, (group + 1) * ncols)
        kb = k_ref[pl.ds(pl.multiple_of(j * tk, tk), tk), :]
        s = _dot(kb, queries[:, cols])
        if diagonal is not None:
            key = lax.broadcasted_iota(jnp.int32, s.shape, 0)
            col = lax.broadcasted_iota(jnp.int32, s.shape, 1) + (0 if group is None else group * ncols)
            key_chunk = diagonal * (tk // CHUNK) + key // CHUNK
            s = jnp.where(key_chunk <= (col & (nq - 1)) // CHUNK, s, -jnp.inf)
        s_ref[:, cols] = s
        mx_ref[:, cols] = jnp.max(s, axis=0, keepdims=True)

    def consume(j, s_ref, mx_ref, group=None):
        cols = slice(None) if group is None else slice(group * ncols, (group + 1) * ncols)
        m_prev = m_ref[:, cols]
        m_new = jnp.maximum(m_prev, mx_ref[:, cols])
        alpha = jnp.exp2(m_prev - m_new)
        p = jnp.exp2(s_ref[:, cols] - m_new).astype(BF16)
        acc_ref[:, cols] = alpha * acc_ref[:, cols] + _dot(vt_ref[j], p)
        m_ref[:, cols] = m_new

    @pl.when(qi == 0)
    def _():
        scores(0, s0_ref, mx0_ref, 0)

    def pair(i, first_diagonal):
        j = 2 * i
        for g in range(COL_GROUPS):
            scores(j + 1, s1_ref, mx1_ref, None, g)
            consume(j, s0_ref, mx0_ref, g)
        for g in range(COL_GROUPS):
            scores(j + 2, s0_ref, mx0_ref, first_diagonal, g)
            consume(j + 1, s1_ref, mx1_ref, g)

    def full_pair(i, carry):
        pair(i, None)
        return carry

    lax.fori_loop(0, qi - 1, full_pair, 0)

    @pl.when(qi > 0)
    def _():
        pair(qi - 1, 0)

    assert tk % ncols == 0
    late = [g for g in range(COL_GROUPS) if (g * ncols) % nq >= tk]
    for g in range(COL_GROUPS):
        if g in late:
            scores(2 * qi + 1, s1_ref, mx1_ref, 1, g)
        consume(2 * qi, s0_ref, mx0_ref, g)
        scores(0, s0_ref, mx0_ref, None, g, q2tn_ref)
    for g in late:
        consume(2 * qi + 1, s1_ref, mx1_ref, g)

    lam = _lambda(lq1, lk1, lq2, lk2, lam_init)
    acc = acc_ref[0:LANES, :]
    l = acc_ref[LANES:LANES + 1, :]
    ot = acc[:, 0:nq] * (1.0 / l[:, 0:nq]) - acc[:, nq:2 * nq] * (lam / l[:, nq:2 * nq])
    ms = jnp.mean(ot * ot, axis=0, keepdims=True)
    o_ref[...] = ot * lax.rsqrt(ms + EPS) * sgc_ref[...] * (1.0 - lam_init)


def _lam_specs():
    return [_const_spec((1, HEAD_DIM), (0, 0)) for _ in range(4)] + [_const_spec((1, LANES), (0, 0))]


def _attn_prompt(cfg, qt, kb, vt, lam_w, lam_init):
    s, tk = cfg.s_prompt, cfg.tm
    tq = 2 * tk
    assert tq & (tq - 1) == 0 and tk % CHUNK == 0 and s % tq == 0
    return pl.pallas_call(
        functools.partial(_attn_prompt_kernel, tk=tk, lam_init=lam_init),
        grid=(N_HEADS, s // tq),
        in_specs=[
            pl.BlockSpec((None, 2, LANES, tk), lambda h, i: (h, i, 0, 0)),
            pl.BlockSpec((None, 2, LANES, tk), lambda h, i: (h, jnp.minimum(i + 1, s // tq - 1), 0, 0)),
            pl.BlockSpec((s, LANES), lambda h, i: (0, h)),
            pl.BlockSpec((None, s // tk, VT_ROWS, tk), lambda h, i: (h, 0, 0, 0)),
        ] + _lam_specs()[:4] + [_const_spec((LANES, 1), (0, 0))],
        out_specs=pl.BlockSpec((LANES, tq), lambda h, i: (h, i)),
        out_shape=jax.ShapeDtypeStruct((W_D, s), F32),
        scratch_shapes=[
            pltpu.VMEM((LANES, 2 * tq), BF16),
            pltpu.VMEM((LANES, 2 * tq), BF16),
            pltpu.VMEM((tk, 2 * tq), F32),
            pltpu.VMEM((tk, 2 * tq), F32),
            pltpu.VMEM((1, 2 * tq), F32),
            pltpu.VMEM((1, 2 * tq), F32),
            pltpu.VMEM((1, 2 * tq), F32),
            pltpu.VMEM((VT_ROWS, 2 * tq), F32),
        ],
        compiler_params=pltpu.CompilerParams(
            dimension_semantics=("arbitrary", "arbitrary"), vmem_limit_bytes=52 * MIB),
        name="attn_prompt",
    )(qt, qt, kb, vt, *lam_w[:4], lam_w[4].reshape(LANES, 1))


def _attn_sample_kernel(q_ref, kn_ref, vn_ref, kc_ref, vc_ref, lq1, lk1, lq2, lk2, sg_ref,
                        o_ref, *, t, lam_init):
    lam = _lambda(lq1, lk1, lq2, lk2, lam_init)
    past = vc_ref.shape[0] // N_HEADS
    for h in range(N_HEADS):
        sl = slice(h * LANES, (h + 1) * LANES)
        s_past, s_new = [], []
        for c in range(2):
            ch = slice(h * LANES + c * HEAD_DIM, h * LANES + (c + 1) * HEAD_DIM)
            q = q_ref[:, ch]
            s_past.append(_dot(q, kc_ref[2 * h + c].astype(BF16)))
            s_new.append(_dot_t(q, kn_ref[:, ch]))
        s_past = jnp.concatenate(s_past, axis=0)
        s_new = jnp.concatenate(s_new, axis=0)
        v_past = vc_ref[pl.ds(h, past, stride=N_HEADS), :]
        m = jnp.maximum(jnp.max(s_past, axis=-1, keepdims=True), jnp.max(s_new, axis=-1, keepdims=True))
        p_past = jnp.exp(s_past - m)
        p_new = jnp.exp(s_new - m)
        l = jnp.sum(p_past, axis=-1, keepdims=True) + jnp.sum(p_new, axis=-1, keepdims=True)
        acc = _dot(p_past.astype(BF16), v_past.astype(BF16)) + _dot(p_new.astype(BF16), vn_ref[:, sl])
        o_ref[:, sl] = _diff_out(acc, l, lam, sg_ref[...], lam_init, t)


def _attn_sample(cfg, l, qb, kb, vb, cache_k, cache_v, lam_w, lam_init):
    t, nb = cfg.t_dec, cfg.n_dec
    row0 = cfg.s_prompt // t
    new_rows = lambda: pl.BlockSpec((t, W_D), lambda b: (row0 + b, 0))
    cache = lambda a: pl.BlockSpec((None, None) + a.shape[2:], lambda b: (l, b) + (0,) * (a.ndim - 2))
    return pl.pallas_call(
        functools.partial(_attn_sample_kernel, t=t, lam_init=lam_init),
        grid=(nb,),
        in_specs=[new_rows(), new_rows(), new_rows(), cache(cache_k), cache(cache_v)] + _lam_specs(),
        out_specs=pl.BlockSpec((t, W_D), lambda b: (b, 0)),
        out_shape=jax.ShapeDtypeStruct((nb * t, W_D), F32),
        compiler_params=pltpu.CompilerParams(
            dimension_semantics=("arbitrary",), vmem_limit_bytes=40 * MIB),
        name="attn_sample",
    )(qb, kb, vb, cache_k, cache_v, *lam_w)


def _merge_kernel(x_ref, yabc_ref, ydt_ref, yds_ref, g_ref, wg_ref, wpa_ref, wpb_ref, wpc_ref, wpd_ref, wo_ref,
                  o_ref, *, n_prompt_tiles):
    is_prompt = pl.program_id(0) < n_prompt_tiles
    half = x_ref.shape[0] // 2
    for rows in (slice(0, half), slice(half, 2 * half)):
        yd = jnp.where(is_prompt, ydt_ref[:, rows].T, yds_ref[rows, :])
        x = x_ref[rows, :]
        h = _rms(x, g_ref[...]).astype(BF16)
        branches = (
            (yabc_ref[rows, 0:W_A], wpa_ref),
            (yabc_ref[rows, W_A:W_A + W_B], wpb_ref),
            (yabc_ref[rows, W_A + W_B:W_Y], wpc_ref),
            (yd, wpd_ref),
        )
        merged = None
        for i, (y, wp_ref) in enumerate(branches):
            gate = jax.nn.sigmoid(_dot(h, wg_ref[:, i * D_MODEL:(i + 1) * D_MODEL]))
            term = gate * _dot(y.astype(BF16), wp_ref[...])
            merged = term if merged is None else merged + term
        o_ref[rows, :] = x + _dot(merged.astype(BF16), wo_ref[...])


def _merge(cfg, l, x, y_abc, yd_t, yd_s, g, w_gate, wpa, wpb, wpc, wpd, wo):
    n, tm, sp = cfg.n_tok, cfg.tm, cfg.s_prompt
    assert sp % tm == 0 and (n - sp) % tm == 0
    npt = sp // tm
    row = lambda w: pl.BlockSpec((tm, w), lambda i: (i, 0))
    return pl.pallas_call(
        functools.partial(_merge_kernel, n_prompt_tiles=npt),
        grid=(n // tm,),
        in_specs=[
            row(D_MODEL), row(W_Y),
            pl.BlockSpec((W_D, tm), lambda i: (0, jnp.minimum(i, npt - 1))),
            pl.BlockSpec((tm, W_D), lambda i: (jnp.maximum(i - npt, 0), 0)),
            _const_spec((1, D_MODEL), (0, 0)),
            _const_spec((None, D_MODEL, 4 * D_MODEL), (l, 0, 0)),
            _const_spec((None, W_A, D_MODEL), (l, 0, 0)),
            _const_spec((None, W_B, D_MODEL), (l, 0, 0)),
            _const_spec((None, W_C, D_MODEL), (l, 0, 0)),
            _const_spec((None, W_D, D_MODEL), (l, 0, 0)),
            _const_spec((None, D_MODEL, D_MODEL), (l, 0, 0)),
        ],
        out_specs=row(D_MODEL),
        out_shape=jax.ShapeDtypeStruct((n, D_MODEL), F32),
        compiler_params=pltpu.CompilerParams(
            dimension_semantics=("arbitrary",), vmem_limit_bytes=48 * MIB),
        name="merge",
    )(x, y_abc, yd_t, yd_s, g, w_gate, wpa, wpb, wpc, wpd, wo)


def _rope_tables(cfg):
    half = HEAD_DIM // 2
    inv_freq = ROPE_THETA ** (-jnp.arange(half, dtype=F32) / half)
    pos = jnp.concatenate([jnp.arange(cfg.s_prompt), jnp.tile(cfg.past + jnp.arange(cfg.t_dec), cfg.n_dec)])
    ang = pos.astype(F32)[:, None] * inv_freq[None, :]
    cos, sin = jnp.cos(ang), jnp.sin(ang)
    reps = LANES // HEAD_DIM
    cos_t = jnp.tile(jnp.concatenate([cos, cos], axis=1), (1, reps))
    sin_t = jnp.tile(jnp.concatenate([-sin, sin], axis=1), (1, reps))
    return cos_t, sin_t


def _pad_rows_top(a, rows):
    return jnp.pad(a, ((0, 0), (0, 0), (rows - a.shape[2], 0), (0, 0)))


def _forward(cfg, x_prompt, x_sample, cache_k, cache_v, state_pool, state_sconv, state_cconv,
             g_ffn1, w1_gate, w1_up, w1_down, g_mix, w_in, pool_w, pool_scale, sconv_w,
             cconv_w, cconv_b, ln_g, ln_b, q_norm_g, k_norm_g, lam_q1, lam_k1, lam_q2, lam_k2,
             subln_g, wp_a, wp_b, wp_c, wp_d, w_out, g_ffn2, w2_gate, w2_up, w2_down):
    depth = w_in.shape[0]
    sp, nb, td = cfg.s_prompt, cfg.n_dec, cfg.t_dec
    x = (x_prompt.reshape(sp, D_MODEL), x_sample.reshape(nb * td, D_MODEL))

    bf = lambda w: w.astype(BF16)
    w1g, w1u, w1d = bf(w1_gate), bf(w1_up), bf(w1_down)
    w2g, w2u, w2d = bf(w2_gate), bf(w2_up), bf(w2_down)
    w_proj, w_gate = bf(w_in[:, :, :W_PROJ]), bf(w_in[:, :, W_PROJ:])
    wpa, wpb, wpc, wpd, wo = bf(wp_a), bf(wp_b), bf(wp_c), bf(wp_d), bf(w_out)
    cos_t, sin_t = _rope_tables(cfg)
    ones_bd = jnp.kron(jnp.eye(NORM_W // HEAD_DIM, dtype=F32), jnp.ones((HEAD_DIM, HEAD_DIM), F32)).astype(BF16)
    eye_g = jnp.eye(len(POOL_WINDOWS), dtype=F32)
    ck = jnp.transpose(cache_k, (0, 1, 3, 4, 2))
    cv = cache_v.reshape(depth, nb, cfg.past * N_HEADS, 2 * HEAD_DIM)
    st_pool = _pad_rows_top(state_pool, HALO)
    st_sconv = _pad_rows_top(state_sconv, HALO)
    st_cconv = _pad_rows_top(state_cconv, HALO)
    row = lambda a: a.reshape(1, -1)

    outs = [[] for _ in range(6)]
    caches = None
    for l in range(depth):
        lam_init = 0.8 - 0.6 * math.exp(-0.3 * l)
        x = _ffn(cfg, l, x, row(g_ffn1[l]), w1g, w1u, w1d)
        qg = row(jnp.tile(q_norm_g[l], NORM_W // HEAD_DIM))
        kg = row(jnp.tile(k_norm_g[l], NORM_W // HEAD_DIM))
        zabc, qb, kb, vb, qt, vt, *caches = _inproj(cfg, l, depth, x, row(g_mix[l]), w_proj, qg, kg, cos_t, sin_t,
                                                    ones_bd, caches)

        pw_bd = (eye_g[:, None, :, None] * pool_w[l][:, :, None, :]).reshape(W_A, W_A).astype(BF16)
        mix_w = (pw_bd, row(pool_scale[l]), sconv_w[l], cconv_w[l], row(cconv_b[l]), row(ln_g[l]), row(ln_b[l]))
        y_abc, pool_p, sconv_p, cconv_p = _mixer_prompt(cfg, zabc, mix_w)
        y_abc, pool_s, sconv_s, cconv_s = _mixer_sample(cfg, zabc, st_pool[l], st_sconv[l], st_cconv[l],
                                                        mix_w, y_abc)

        lam_w = (row(lam_q1[l]), row(lam_k1[l]), row(lam_q2[l]), row(lam_k2[l]), row(subln_g[l]))
        yd_t = _attn_prompt(cfg, qt, kb, vt, lam_w, lam_init)
        yd_s = _attn_sample(cfg, l, qb, kb, vb, ck, cv, lam_w, lam_init)

        x = _merge(cfg, l, x, y_abc, yd_t, yd_s, row(g_mix[l]), w_gate, wpa, wpb, wpc, wpd, wo)
        x = _ffn(cfg, l, x, row(g_ffn2[l]), w2g, w2u, w2d, split_out=(l == depth - 1))

        outs[0].append(pool_p[None, 16 - POOL_STATE:])
        outs[1].append(sconv_p[None, 8 - (SCONV_K - 1):])
        outs[2].append(cconv_p[None, 32 - (CCONV_K - 1):])
        outs[3].append(pool_s[:, 16 - POOL_STATE:])
        outs[4].append(sconv_s[:, 8 - (SCONV_K - 1):])
        outs[5].append(cconv_s[:, 32 - (CCONV_K - 1):])

    kt_all, ks_all, vp_all, vs_all = caches
    y_prompt = x[0].reshape(1, sp, D_MODEL)
    y_sample = x[1].reshape(nb, td, D_MODEL)
    k_prompt = jnp.transpose(kt_all.reshape(depth, 1, 2 * N_HEADS, HEAD_DIM, sp), (0, 1, 4, 2, 3))
    v_prompt = vp_all.reshape(depth, 1, sp, N_HEADS, 2 * HEAD_DIM)
    k_sample = ks_all.reshape(depth, nb, td, 2 * N_HEADS, HEAD_DIM)
    v_sample = vs_all.reshape(depth, nb, td, N_HEADS, 2 * HEAD_DIM)
    st = [jnp.stack(o) for o in outs]
    return (y_prompt, y_sample, k_prompt, v_prompt, st[0], st[1], st[2], k_sample, v_sample, st[3], st[4], st[5])


def kernel(x_prompt, x_sample, cache_k, cache_v, state_pool, state_sconv, state_cconv, g_ffn1, w1_gate, w1_up, w1_down, g_mix, w_in, pool_w, pool_scale, sconv_w, cconv_w, cconv_b, ln_g, ln_b, q_norm_g, k_norm_g, lam_q1, lam_k1, lam_q2, lam_k2, subln_g, wp_a, wp_b, wp_c, wp_d, w_out, g_ffn2, w2_gate, w2_up, w2_down):
    assert x_prompt.shape[0] == 1
    cfg = Cfg(s_prompt=x_prompt.shape[1], n_dec=x_sample.shape[0], t_dec=x_sample.shape[1],
              past=cache_k.shape[2], tm=512, tmix=512)
    return _forward(cfg, x_prompt, x_sample, cache_k, cache_v, state_pool, state_sconv, state_cconv,
                    g_ffn1, w1_gate, w1_up, w1_down, g_mix, w_in, pool_w, pool_scale, sconv_w,
                    cconv_w, cconv_b, ln_g, ln_b, q_norm_g, k_norm_g, lam_q1, lam_k1, lam_q2, lam_k2,
                    subln_g, wp_a, wp_b, wp_c, wp_d, w_out, g_ffn2, w2_gate, w2_up, w2_down)
```

```python
import functools
import math
from typing import NamedTuple

import jax
import jax.numpy as jnp
from jax import lax
from jax.experimental import pallas as pl
from jax.experimental.pallas import tpu as pltpu

F32 = jnp.float32
BF16 = jnp.bfloat16

D_MODEL = 1024
DEPTH = 4
CHUNK = 64
POOL_WINDOWS = (2, 4, 8, 16)
W_A = 256
GA = 64
POOL_STATE = 15
W_B = 256
SCONV_K = 3
W_C = 256
CCONV_K = 31
HEAD_DIM = 64
N_HEADS = 4
W_D = 512
ROPE_THETA = 10000.0
D_FF = 2816
EPS = 1e-6
W_ABC = W_A + 3 * W_B + 2 * W_C
W_PROJ = W_ABC + 3 * W_D
W_Y = W_A + W_B + W_C

LANES = 128
NORM_W = 256
COL_GROUPS = 4
VT_ROWS = LANES + 16
LOG2E = 1.4426950408889634
HALO = 32
MIB = 1024 * 1024


class Cfg(NamedTuple):
    s_prompt: int
    n_dec: int
    t_dec: int
    past: int
    tm: int
    tmix: int

    @property
    def n_tok(self):
        return self.s_prompt + self.n_dec * self.t_dec


def _const_spec(shape, index):
    return pl.BlockSpec(shape, lambda *_: index, pipeline_mode=pl.Buffered(1))


def _rms(x, g):
    ms = jnp.mean(x * x, axis=-1, keepdims=True)
    return x * lax.rsqrt(ms + EPS) * g


def _dot(a, b):
    return jnp.dot(a, b, preferred_element_type=F32)


def _dot_t(a, b):
    return lax.dot_general(a, b, (((1,), (1,)), ((), ())), preferred_element_type=F32)


def _ffn_kernel(*refs, n_prompt_tiles, split_in, split_out):
    refs = list(refs)
    is_prompt = pl.program_id(0) < n_prompt_tiles
    if split_in:
        xp_ref, xs_ref = refs[0:2]
        del refs[0:2]
        x = jnp.where(is_prompt, xp_ref[...], xs_ref[...])
    else:
        x = refs.pop(0)[...]
    g_ref, wg_ref, wu_ref, wd_ref = refs[0:4]
    halves = []
    for xh in (x[0:x.shape[0] // 2], x[x.shape[0] // 2:]):
        h = _rms(xh, g_ref[...]).astype(BF16)
        a = _dot(h, wg_ref[...])
        u = _dot(h, wu_ref[...])
        act = (a * jax.nn.sigmoid(a) * u).astype(BF16)
        halves.append(0.5 * _dot(act, wd_ref[...]))
    half_step = jnp.concatenate(halves, axis=0)
    if split_out:
        op_ref, os_ref = refs[4:6]

        @pl.when(is_prompt)
        def _():
            op_ref[...] = x + half_step

        @pl.when(jnp.logical_not(is_prompt))
        def _():
            os_ref[...] = x + half_step
    else:
        refs[4][...] = x + half_step


def _ffn(cfg, l, x, g, wg, wu, wd, split_out=False):
    n, tm, sp = cfg.n_tok, cfg.tm, cfg.s_prompt
    assert sp % tm == 0 and (n - sp) % tm == 0
    npt = sp // tm
    split_in = isinstance(x, tuple)
    rows = pl.BlockSpec((tm, D_MODEL), lambda i: (i, 0))
    prompt_rows = pl.BlockSpec((tm, D_MODEL), lambda i: (jnp.minimum(i, npt - 1), 0))
    sample_rows = pl.BlockSpec((tm, D_MODEL), lambda i: (jnp.maximum(i - npt, 0), 0))
    whole = jax.ShapeDtypeStruct((n, D_MODEL), F32)
    halves = [jax.ShapeDtypeStruct((sp, D_MODEL), F32), jax.ShapeDtypeStruct((n - sp, D_MODEL), F32)]
    return pl.pallas_call(
        functools.partial(_ffn_kernel, n_prompt_tiles=npt, split_in=split_in, split_out=split_out),
        grid=(n // tm,),
        in_specs=([prompt_rows, sample_rows] if split_in else [rows]) + [
            _const_spec((1, D_MODEL), (0, 0)),
            _const_spec((None, D_MODEL, D_FF), (l, 0, 0)),
            _const_spec((None, D_MODEL, D_FF), (l, 0, 0)),
            _const_spec((None, D_FF, D_MODEL), (l, 0, 0)),
        ],
        out_specs=[prompt_rows, sample_rows] if split_out else rows,
        out_shape=halves if split_out else whole,
        compiler_params=pltpu.CompilerParams(
            dimension_semantics=("arbitrary",), vmem_limit_bytes=52 * MIB),
        name="ffn",
    )(*(x if split_in else (x,)), g, wg, wu, wd)


def _inproj_kernel(x_ref, g_ref, w_ref, qg_ref, kg_ref, cos_ref, sin_ref, ones_ref, *rest, n_prompt_tiles):
    zabc_ref, qb_ref, kb_ref, vb_ref, qt_ref, vt_ref, kt_ref, ks_ref, vp_ref, vs_ref, zqkv_ref = rest[-11:]
    is_prompt = pl.program_id(0) < n_prompt_tiles
    h = _rms(x_ref[...], g_ref[...]).astype(BF16)
    assert W_ABC % (2 * NORM_W) == 0 and W_ABC // (2 * NORM_W) <= W_D // LANES
    cos = cos_ref[...]
    sin = sin_ref[...]
    ones = ones_ref[...]
    lane = lax.broadcasted_iota(jnp.int32, cos.shape, 1)
    first_half = (lane & (HEAD_DIM // 2)) == 0

    def head_norm(z, g):
        ss = z * z
        hi = ss.astype(BF16)
        lo = (ss - hi.astype(F32)).astype(BF16)
        tot = _dot(hi, ones) + _dot(lo, ones)
        return z * lax.rsqrt(tot * (1.0 / HEAD_DIM) + EPS) * g

    def rope(y):
        half = HEAD_DIM // 2
        partner = jnp.where(first_half, pltpu.roll(y, LANES - half, 1), pltpu.roll(y, half, 1))
        return y * cos + partner * sin

    nw = ones.shape[0]
    abc_w = 2 * NORM_W
    n_chunks = W_D // LANES
    for g0 in range(0, W_D, nw):
        for part in range(3):
            lo = part * W_D + g0
            zqkv_ref[:, lo:lo + nw] = _dot(h, w_ref[:, W_ABC + lo:W_ABC + lo + nw])
    yq = [head_norm(zqkv_ref[:, g0:g0 + nw], qg_ref[...]) for g0 in range(0, W_D, nw)]
    yk = [head_norm(zqkv_ref[:, W_D + g0:W_D + g0 + nw], kg_ref[...]) for g0 in range(0, W_D, nw)]
    for c in range(n_chunks):
        sl = slice(c * LANES, (c + 1) * LANES)
        grp, off = divmod(c * LANES, nw)
        q = rope(yq[grp][:, off:off + LANES]) * (HEAD_DIM ** -0.5)
        qb_ref[:, sl] = q.astype(BF16)
        qt_ref[c] = (q * LOG2E).T.astype(BF16)
        k = rope(yk[grp][:, off:off + LANES])
        kb_ref[:, sl] = k.astype(BF16)
        zqkv_ref[:, W_D + c * LANES:W_D + (c + 1) * LANES] = k
        v = zqkv_ref[:, 2 * W_D + c * LANES:2 * W_D + (c + 1) * LANES]
        vb_ref[:, sl] = v.astype(BF16)
        vt_ref[c, 0:LANES, :] = v.T.astype(BF16)
        vt_ref[c, LANES:VT_ROWS, :] = jnp.ones((VT_ROWS - LANES, v.shape[0]), BF16)
        if c * abc_w < W_ABC:
            zabc_ref[:, c * abc_w:(c + 1) * abc_w] = _dot(h, w_ref[:, c * abc_w:(c + 1) * abc_w])

    @pl.when(is_prompt)
    def _():
        for c in range(W_D // LANES):
            kt = zqkv_ref[:, W_D + c * LANES:W_D + (c + 1) * LANES].T
            kt_ref[2 * c] = kt[0:HEAD_DIM]
            kt_ref[2 * c + 1] = kt[HEAD_DIM:LANES]
        for hd in range(N_HEADS):
            vp_ref[pl.ds(hd, kt_ref.shape[-1], stride=N_HEADS), :] = (
                zqkv_ref[:, 2 * W_D + hd * LANES:2 * W_D + (hd + 1) * LANES])

    @pl.when(jnp.logical_not(is_prompt))
    def _():
        ks_ref[...] = zqkv_ref[:, W_D:2 * W_D]
        vs_ref[...] = zqkv_ref[:, 2 * W_D:3 * W_D]


def _inproj(cfg, l, depth, x, g, w_in, qg, kg, cos_t, sin_t, ones_bd, caches):
    n, tm, sp = cfg.n_tok, cfg.tm, cfg.s_prompt
    n_dec = n - sp
    assert sp % tm == 0 and n_dec % tm == 0
    npt = sp // tm
    row = lambda w: pl.BlockSpec((tm, w), lambda i: (i, 0))
    prompt_rows = pl.BlockSpec((None, tm * N_HEADS, LANES), lambda i: (l, jnp.minimum(i, npt - 1), 0))
    sample_rows = pl.BlockSpec((None, tm, W_D), lambda i: (l, jnp.maximum(i - npt, 0), 0))
    n_in = 8
    aliased = [] if caches is None else list(caches)
    return pl.pallas_call(
        functools.partial(_inproj_kernel, n_prompt_tiles=npt),
        grid=(n // tm,),
        in_specs=[
            row(D_MODEL),
            _const_spec((1, D_MODEL), (0, 0)),
            _const_spec((None, D_MODEL, W_PROJ), (l, 0, 0)),
            _const_spec((1, NORM_W), (0, 0)),
            _const_spec((1, NORM_W), (0, 0)),
            row(LANES),
            row(LANES),
            _const_spec((NORM_W, NORM_W), (0, 0)),
        ] + [pl.BlockSpec(memory_space=pl.ANY) for _ in aliased],
        out_specs=[row(W_ABC), row(W_D), row(W_D), row(W_D),
                   pl.BlockSpec((N_HEADS, None, LANES, tm), lambda i: (0, i, 0, 0)),
                   pl.BlockSpec((N_HEADS, None, VT_ROWS, tm), lambda i: (0, i, 0, 0)),
                   pl.BlockSpec((None, 2 * N_HEADS, HEAD_DIM, tm), lambda i: (l, 0, 0, jnp.minimum(i, npt - 1))),
                   sample_rows, prompt_rows, sample_rows],
        out_shape=[
            jax.ShapeDtypeStruct((n, W_ABC), F32),
            jax.ShapeDtypeStruct((n, W_D), BF16),
            jax.ShapeDtypeStruct((n, W_D), BF16),
            jax.ShapeDtypeStruct((n, W_D), BF16),
            jax.ShapeDtypeStruct((N_HEADS, n // tm, LANES, tm), BF16),
            jax.ShapeDtypeStruct((N_HEADS, n // tm, VT_ROWS, tm), BF16),
            jax.ShapeDtypeStruct((depth, 2 * N_HEADS, HEAD_DIM, sp), F32),
            jax.ShapeDtypeStruct((depth, n_dec, W_D), F32),
            jax.ShapeDtypeStruct((depth, sp * N_HEADS, LANES), F32),
            jax.ShapeDtypeStruct((depth, n_dec, W_D), F32),
        ],
        scratch_shapes=[pltpu.VMEM((tm, 3 * W_D), F32)],
        input_output_aliases={n_in + j: 6 + j for j in range(len(aliased))},
        compiler_params=pltpu.CompilerParams(
            dimension_semantics=("arbitrary",), vmem_limit_bytes=44 * MIB),
        name="inproj",
    )(x, g, w_in, qg, kg, cos_t, sin_t, ones_bd, *aliased)


SAMPLE_MIX_GROUP = 8
MIX_ROWS = 128
SUBLANES = 8


def _shifted_rows(ref, base, rows, depth, tmp):
    out = {}
    for r in range(SUBLANES):
        js = [j for j in range(1, depth + 1) if (-j) % SUBLANES == r]
        if not js:
            continue
        start, length = base - max(js), max(js) - min(js) + rows
        tmp[r, 0:length, :] = ref[start:start + length, :]
        for j in js:
            out[j] = tmp[r, max(js) - j:max(js) - j + rows, :]
    return out


def _mixer_compute(t, pos0, z_ref, pw_ref, ps_ref, sw_ref, cw_ref, cb_ref, lg_ref, lb_ref,
                   y_ref, pool_o, sconv_o, cconv_o, ea, eb, ec, tmp):
    u = z_ref[:, 0:W_A]
    ea[HALO:HALO + t, :] = u
    eb[HALO:HALO + t, :] = z_ref[:, W_A + W_B:W_A + 2 * W_B] * z_ref[:, W_A + 2 * W_B:W_A + 3 * W_B]
    zc = z_ref[:, W_A + 3 * W_B:W_A + 3 * W_B + W_C]
    ec[HALO:HALO + t, :] = zc * jax.nn.sigmoid(z_ref[:, W_A + 3 * W_B + W_C:W_ABC])

    rows = min(t, MIX_ROWS)
    for r0 in range(0, t, rows):
        base = HALO + r0
        lane = lax.broadcasted_iota(jnp.int32, (rows, W_A), 1)
        cur = ea[base:base + rows, :]
        back = _shifted_rows(ea, base, rows, max(POOL_WINDOWS) - 1, tmp)
        acc = cur
        sums = {}
        for j in range(1, max(POOL_WINDOWS)):
            acc = acc + back[j]
            if j + 1 in POOL_WINDOWS:
                sums[j + 1] = acc
        tot = sums[POOL_WINDOWS[-1]]
        win = jnp.full((rows, W_A), float(POOL_WINDOWS[-1]), F32)
        for gi in range(len(POOL_WINDOWS) - 2, -1, -1):
            in_group = lane < (gi + 1) * GA
            tot = jnp.where(in_group, sums[POOL_WINDOWS[gi]], tot)
            win = jnp.where(in_group, float(POOL_WINDOWS[gi]), win)
        if pos0 is None:
            cnt = win
        else:
            pos1 = (pos0 + r0 + 1 + lax.broadcasted_iota(jnp.int32, (rows, W_A), 0)).astype(F32)
            cnt = jnp.minimum(pos1, win)
        d = (tot / cnt - cur).astype(BF16)
        y_ref[r0:r0 + rows, 0:W_A] = _dot(d, pw_ref[...]) * ps_ref[...]

        conv = sw_ref[SCONV_K - 1:SCONV_K, :] * eb[base:base + rows, :]
        for j in range(SCONV_K - 1):
            off = base - (SCONV_K - 1) + j
            conv = conv + sw_ref[j:j + 1, :] * eb[off:off + rows, :]
        y_ref[r0:r0 + rows, W_A:W_A + W_B] = z_ref[r0:r0 + rows, W_A:W_A + W_B] * conv

        back = _shifted_rows(ec, base, rows, CCONV_K - 1, tmp)
        conv = cw_ref[CCONV_K - 1:CCONV_K, :] * ec[base:base + rows, :]
        for j in range(CCONV_K - 1):
            conv = conv + cw_ref[j:j + 1, :] * back[CCONV_K - 1 - j]
        conv = conv + cb_ref[...]
        mu = jnp.mean(conv, axis=-1, keepdims=True)
        cen = conv - mu
        var = jnp.mean(cen * cen, axis=-1, keepdims=True)
        ln = cen * lax.rsqrt(var + EPS) * lg_ref[...] + lb_ref[...]
        y_ref[r0:r0 + rows, W_A + W_B:W_Y] = ln * jax.nn.sigmoid(ln)

    pool_o[...] = ea[HALO + t - 16:HALO + t, :]
    sconv_o[...] = eb[HALO + t - 8:HALO + t, :]
    cconv_o[...] = ec[HALO + t - 32:HALO + t, :]


def _mixer_prompt_kernel(z_ref, pw_ref, ps_ref, sw_ref, cw_ref, cb_ref, lg_ref, lb_ref,
                         y_ref, pool_o, sconv_o, cconv_o, ea, eb, ec, tmp, *, t):
    i = pl.program_id(0)

    @pl.when(i == 0)
    def _():
        zeros = jnp.zeros((HALO, W_A), F32)
        ea[0:HALO, :] = zeros
        eb[0:HALO, :] = zeros
        ec[0:HALO, :] = zeros

    @pl.when(i > 0)
    def _():
        ea[0:HALO, :] = ea[t:t + HALO, :]
        eb[0:HALO, :] = eb[t:t + HALO, :]
        ec[0:HALO, :] = ec[t:t + HALO, :]

    _mixer_compute(t, i * t, z_ref, pw_ref, ps_ref, sw_ref, cw_ref, cb_ref, lg_ref, lb_ref,
                   y_ref, pool_o, sconv_o, cconv_o, ea, eb, ec, tmp)


def _mixer_sample_kernel(z_ref, sp_ref, ss_ref, sc_ref, pw_ref, ps_ref, sw_ref, cw_ref, cb_ref,
                         lg_ref, lb_ref, y_in_ref, y_ref, pool_o, sconv_o, cconv_o, ea, eb, ec, tmp, *, t, group):
    del y_in_ref
    for s in range(group):
        rows = pl.ds(s * t, t)
        ea[0:HALO, :] = sp_ref[s]
        eb[0:HALO, :] = ss_ref[s]
        ec[0:HALO, :] = sc_ref[s]
        _mixer_compute(t, None, z_ref.at[rows], pw_ref, ps_ref, sw_ref, cw_ref, cb_ref, lg_ref, lb_ref,
                       y_ref.at[rows], pool_o.at[s], sconv_o.at[s], cconv_o.at[s], ea, eb, ec, tmp)


def _mixer_weight_specs():
    return [
        _const_spec((W_A, W_A), (0, 0)),
        _const_spec((1, W_A), (0, 0)),
        _const_spec((SCONV_K, W_B), (0, 0)),
        _const_spec((CCONV_K, W_C), (0, 0)),
        _const_spec((1, W_C), (0, 0)),
        _const_spec((1, W_C), (0, 0)),
        _const_spec((1, W_C), (0, 0)),
    ]


def _mixer_scratch(t):
    return [pltpu.VMEM((HALO + t, W_A), F32), pltpu.VMEM((HALO + t, W_B), F32),
            pltpu.VMEM((HALO + t, W_C), F32),
            pltpu.VMEM((SUBLANES, min(t, MIX_ROWS) + HALO, W_C), F32)]


def _mixer_prompt(cfg, zabc, weights):
    t = cfg.tmix
    const_out = lambda r: pl.BlockSpec((r, W_A), lambda i: (0, 0))
    return pl.pallas_call(
        functools.partial(_mixer_prompt_kernel, t=t),
        grid=(cfg.s_prompt // t,),
        in_specs=[pl.BlockSpec((t, W_ABC), lambda i: (i, 0))] + _mixer_weight_specs(),
        out_specs=[pl.BlockSpec((t, W_Y), lambda i: (i, 0)), const_out(16), const_out(8), const_out(32)],
        out_shape=[
            jax.ShapeDtypeStruct((cfg.n_tok, W_Y), F32),
            jax.ShapeDtypeStruct((16, W_A), F32),
            jax.ShapeDtypeStruct((8, W_B), F32),
            jax.ShapeDtypeStruct((32, W_C), F32),
        ],
        scratch_shapes=_mixer_scratch(t),
        compiler_params=pltpu.CompilerParams(dimension_semantics=("arbitrary",)),
        name="mixer_prompt",
    )(zabc, *weights)


def _mixer_sample(cfg, zabc, st_pool, st_sconv, st_cconv, weights, y_abc):
    t, nb = cfg.t_dec, cfg.n_dec
    group = math.gcd(nb, SAMPLE_MIX_GROUP)
    assert cfg.s_prompt % (group * t) == 0
    row0 = cfg.s_prompt // (group * t)
    state_spec = lambda: pl.BlockSpec((group, HALO, W_A), lambda b: (b, 0, 0))
    out_state = lambda r: pl.BlockSpec((group, r, W_A), lambda b: (b, 0, 0))
    return pl.pallas_call(
        functools.partial(_mixer_sample_kernel, t=t, group=group),
        grid=(nb // group,),
        in_specs=[pl.BlockSpec((group * t, W_ABC), lambda b: (row0 + b, 0)),
                  state_spec(), state_spec(), state_spec()]
                 + _mixer_weight_specs()
                 + [pl.BlockSpec(memory_space=pl.ANY)],
        out_specs=[pl.BlockSpec((group * t, W_Y), lambda b: (row0 + b, 0)),
                   out_state(16), out_state(8), out_state(32)],
        out_shape=[
            jax.ShapeDtypeStruct((cfg.n_tok, W_Y), F32),
            jax.ShapeDtypeStruct((nb, 16, W_A), F32),
            jax.ShapeDtypeStruct((nb, 8, W_B), F32),
            jax.ShapeDtypeStruct((nb, 32, W_C), F32),
        ],
        scratch_shapes=_mixer_scratch(t),
        input_output_aliases={11: 0},
        compiler_params=pltpu.CompilerParams(dimension_semantics=("arbitrary",)),
        name="mixer_sample",
    )(zabc, st_pool, st_sconv, st_cconv, *weights, y_abc)


def _lambda(lq1, lk1, lq2, lk2, lam_init):
    s1 = jnp.sum(lq1[...] * lk1[...], axis=-1, keepdims=True)
    s2 = jnp.sum(lq2[...] * lk2[...], axis=-1, keepdims=True)
    return jnp.exp(s1) - jnp.exp(s2) + lam_init


def _diff_out(acc, l, lam, sg, lam_init, t):
    o = acc[0:t] / l[0:t] - lam * (acc[t:2 * t] / l[t:2 * t])
    return _rms(o, sg) * (1.0 - lam_init)


def _attn_prompt_kernel(qt_ref, qtn_ref, k_ref, vt_ref, lq1, lk1, lq2, lk2, sgc_ref, o_ref,
                        q2t_ref, q2tn_ref, s0_ref, s1_ref, mx0_ref, mx1_ref, m_ref, acc_ref, *, tk, lam_init):
    qi = pl.program_id(1)
    nq = 2 * tk
    chan = lax.broadcasted_iota(jnp.int32, (LANES, tk), 0)
    zero = jnp.zeros((LANES, tk), BF16)
    for src_ref, dst_ref in ((qt_ref, q2t_ref), (qtn_ref, q2tn_ref)):
        for half in range(2):
            qt = src_ref[half]
            dst_ref[:, half * tk:(half + 1) * tk] = jnp.where(chan < HEAD_DIM, qt, zero)
            dst_ref[:, nq + half * tk:nq + (half + 1) * tk] = jnp.where(chan >= HEAD_DIM, qt, zero)
    m_ref[...] = jnp.full(m_ref.shape, -jnp.inf, F32)
    acc_ref[...] = jnp.zeros(acc_ref.shape, F32)

    ncols = 2 * nq // COL_GROUPS

    def scores(j, s_ref, mx_ref, diagonal=None, group=None, queries=q2t_ref):
        cols = slice(None) if group is None else slice(group * ncols, (group + 1) * ncols)
        kb = k_ref[pl.ds(pl.multiple_of(j * tk, tk), tk), :]
        s = _dot(kb, queries[:, cols])
        if diagonal is not None:
            key = lax.broadcasted_iota(jnp.int32, s.shape, 0)
            col = lax.broadcasted_iota(jnp.int32, s.shape, 1) + (0 if group is None else group * ncols)
            key_chunk = diagonal * (tk // CHUNK) + key // CHUNK
            s = jnp.where(key_chunk <= (col & (nq - 1)) // CHUNK, s, -jnp.inf)
        s_ref[:, cols] = s
        mx_ref[:, cols] = jnp.max(s, axis=0, keepdims=True)

    def consume(j, s_ref, mx_ref, group=None):
        cols = slice(None) if group is None else slice(group * ncols, (group + 1) * ncols)
        m_prev = m_ref[:, cols]
        m_new = jnp.maximum(m_prev, mx_ref[:, cols])
        alpha = jnp.exp2(m_prev - m_new)
        p = jnp.exp2(s_ref[:, cols] - m_new).astype(BF16)
        acc_ref[:, cols] = alpha * acc_ref[:, cols] + _dot(vt_ref[j], p)
        m_ref[:, cols] = m_new

    @pl.when(qi == 0)
    def _():
        scores(0, s0_ref, mx0_ref, 0)

    def pair(i, first_diagonal):
        j = 2 * i
        for g in range(COL_GROUPS):
            scores(j + 1, s1_ref, mx1_ref, None, g)
            consume(j, s0_ref, mx0_ref, g)
        for g in range(COL_GROUPS):
            scores(j + 2, s0_ref, mx0_ref, first_diagonal, g)
            consume(j + 1, s1_ref, mx1_ref, g)

    n_plain = jnp.maximum(qi - 1, 0)

    def two_pairs(i, carry):
        pair(2 * i, None)
        pair(2 * i + 1, None)
        return carry

    lax.fori_loop(0, n_plain // 2, two_pairs, 0)

    @pl.when(n_plain % 2 == 1)
    def _():
        pair(n_plain - 1, None)

    @pl.when(qi > 0)
    def _():
        pair(qi - 1, 0)

    assert tk % ncols == 0
    late = [g for g in range(COL_GROUPS) if (g * ncols) % nq >= tk]
    for g in range(COL_GROUPS):
        if g in late:
            scores(2 * qi + 1, s1_ref, mx1_ref, 1, g)
        consume(2 * qi, s0_ref, mx0_ref, g)
        scores(0, s0_ref, mx0_ref, None, g, q2tn_ref)
    for g in late:
        consume(2 * qi + 1, s1_ref, mx1_ref, g)

    lam = _lambda(lq1, lk1, lq2, lk2, lam_init)
    acc = acc_ref[0:LANES, :]
    l = acc_ref[LANES:LANES + 1, :]
    ot = acc[:, 0:nq] * (1.0 / l[:, 0:nq]) - acc[:, nq:2 * nq] * (lam / l[:, nq:2 * nq])
    ms = jnp.mean(ot * ot, axis=0, keepdims=True)
    o_ref[...] = ot * lax.rsqrt(ms + EPS) * sgc_ref[...] * (1.0 - lam_init)


def _lam_specs():
    return [_const_spec((1, HEAD_DIM), (0, 0)) for _ in range(4)] + [_const_spec((1, LANES), (0, 0))]


def _attn_prompt(cfg, qt, kb, vt, lam_w, lam_init):
    s, tk = cfg.s_prompt, cfg.tm
    tq = 2 * tk
    assert tq & (tq - 1) == 0 and tk % CHUNK == 0 and s % tq == 0
    return pl.pallas_call(
        functools.partial(_attn_prompt_kernel, tk=tk, lam_init=lam_init),
        grid=(N_HEADS, s // tq),
        in_specs=[
            pl.BlockSpec((None, 2, LANES, tk), lambda h, i: (h, i, 0, 0)),
            pl.BlockSpec((None, 2, LANES, tk), lambda h, i: (h, jnp.minimum(i + 1, s // tq - 1), 0, 0)),
            pl.BlockSpec((s, LANES), lambda h, i: (0, h)),
            pl.BlockSpec((None, s // tk, VT_ROWS, tk), lambda h, i: (h, 0, 0, 0)),
        ] + _lam_specs()[:4] + [_const_spec((LANES, 1), (0, 0))],
        out_specs=pl.BlockSpec((LANES, tq), lambda h, i: (h, i)),
        out_shape=jax.ShapeDtypeStruct((W_D, s), F32),
        scratch_shapes=[
            pltpu.VMEM((LANES, 2 * tq), BF16),
            pltpu.VMEM((LANES, 2 * tq), BF16),
            pltpu.VMEM((tk, 2 * tq), F32),
            pltpu.VMEM((tk, 2 * tq), F32),
            pltpu.VMEM((1, 2 * tq), F32),
            pltpu.VMEM((1, 2 * tq), F32),
            pltpu.VMEM((1, 2 * tq), F32),
            pltpu.VMEM((VT_ROWS, 2 * tq), F32),
        ],
        compiler_params=pltpu.CompilerParams(
            dimension_semantics=("arbitrary", "arbitrary"), vmem_limit_bytes=52 * MIB),
        name="attn_prompt",
    )(qt, qt, kb, vt, *lam_w[:4], lam_w[4].reshape(LANES, 1))


def _attn_sample_kernel(q_ref, kn_ref, vn_ref, kc_ref, vc_ref, lq1, lk1, lq2, lk2, sg_ref,
                        o_ref, *, t, lam_init):
    lam = _lambda(lq1, lk1, lq2, lk2, lam_init)
    past = vc_ref.shape[0] // N_HEADS
    for h in range(N_HEADS):
        sl = slice(h * LANES, (h + 1) * LANES)
        s_past, s_new = [], []
        for c in range(2):
            ch = slice(h * LANES + c * HEAD_DIM, h * LANES + (c + 1) * HEAD_DIM)
            q = q_ref[:, ch]
            s_past.append(_dot(q, kc_ref[2 * h + c].astype(BF16)))
            s_new.append(_dot_t(q, kn_ref[:, ch]))
        s_past = jnp.concatenate(s_past, axis=0)
        s_new = jnp.concatenate(s_new, axis=0)
        v_past = vc_ref[pl.ds(h, past, stride=N_HEADS), :]
        m = jnp.maximum(jnp.max(s_past, axis=-1, keepdims=True), jnp.max(s_new, axis=-1, keepdims=True))
        p_past = jnp.exp(s_past - m)
        p_new = jnp.exp(s_new - m)
        l = jnp.sum(p_past, axis=-1, keepdims=True) + jnp.sum(p_new, axis=-1, keepdims=True)
        acc = _dot(p_past.astype(BF16), v_past.astype(BF16)) + _dot(p_new.astype(BF16), vn_ref[:, sl])
        o_ref[:, sl] = _diff_out(acc, l, lam, sg_ref[...], lam_init, t)


def _attn_sample(cfg, l, qb, kb, vb, cache_k, cache_v, lam_w, lam_init):
    t, nb = cfg.t_dec, cfg.n_dec
    row0 = cfg.s_prompt // t
    new_rows = lambda: pl.BlockSpec((t, W_D), lambda b: (row0 + b, 0))
    cache = lambda a: pl.BlockSpec((None, None) + a.shape[2:], lambda b: (l, b) + (0,) * (a.ndim - 2))
    return pl.pallas_call(
        functools.partial(_attn_sample_kernel, t=t, lam_init=lam_init),
        grid=(nb,),
        in_specs=[new_rows(), new_rows(), new_rows(), cache(cache_k), cache(cache_v)] + _lam_specs(),
        out_specs=pl.BlockSpec((t, W_D), lambda b: (b, 0)),
        out_shape=jax.ShapeDtypeStruct((nb * t, W_D), F32),
        compiler_params=pltpu.CompilerParams(
            dimension_semantics=("arbitrary",), vmem_limit_bytes=40 * MIB),
        name="attn_sample",
    )(qb, kb, vb, cache_k, cache_v, *lam_w)


def _merge_kernel(x_ref, yabc_ref, ydt_ref, yds_ref, g_ref, wg_ref, wpa_ref, wpb_ref, wpc_ref, wpd_ref, wo_ref,
                  o_ref, *, n_prompt_tiles):
    is_prompt = pl.program_id(0) < n_prompt_tiles
    half = x_ref.shape[0] // 2
    for rows in (slice(0, half), slice(half, 2 * half)):
        yd = jnp.where(is_prompt, ydt_ref[:, rows].T, yds_ref[rows, :])
        x = x_ref[rows, :]
        h = _rms(x, g_ref[...]).astype(BF16)
        branches = (
            (yabc_ref[rows, 0:W_A], wpa_ref),
            (yabc_ref[rows, W_A:W_A + W_B], wpb_ref),
            (yabc_ref[rows, W_A + W_B:W_Y], wpc_ref),
            (yd, wpd_ref),
        )
        merged = None
        for i, (y, wp_ref) in enumerate(branches):
            gate = jax.nn.sigmoid(_dot(h, wg_ref[:, i * D_MODEL:(i + 1) * D_MODEL]))
            term = gate * _dot(y.astype(BF16), wp_ref[...])
            merged = term if merged is None else merged + term
        o_ref[rows, :] = x + _dot(merged.astype(BF16), wo_ref[...])


def _merge(cfg, l, x, y_abc, yd_t, yd_s, g, w_gate, wpa, wpb, wpc, wpd, wo):
    n, tm, sp = cfg.n_tok, cfg.tm, cfg.s_prompt
    assert sp % tm == 0 and (n - sp) % tm == 0
    npt = sp // tm
    row = lambda w: pl.BlockSpec((tm, w), lambda i: (i, 0))
    return pl.pallas_call(
        functools.partial(_merge_kernel, n_prompt_tiles=npt),
        grid=(n // tm,),
        in_specs=[
            row(D_MODEL), row(W_Y),
            pl.BlockSpec((W_D, tm), lambda i: (0, jnp.minimum(i, npt - 1))),
            pl.BlockSpec((tm, W_D), lambda i: (jnp.maximum(i - npt, 0), 0)),
            _const_spec((1, D_MODEL), (0, 0)),
            _const_spec((None, D_MODEL, 4 * D_MODEL), (l, 0, 0)),
            _const_spec((None, W_A, D_MODEL), (l, 0, 0)),
            _const_spec((None, W_B, D_MODEL), (l, 0, 0)),
            _const_spec((None, W_C, D_MODEL), (l, 0, 0)),
            _const_spec((None, W_D, D_MODEL), (l, 0, 0)),
            _const_spec((None, D_MODEL, D_MODEL), (l, 0, 0)),
        ],
        out_specs=row(D_MODEL),
        out_shape=jax.ShapeDtypeStruct((n, D_MODEL), F32),
        compiler_params=pltpu.CompilerParams(
            dimension_semantics=("arbitrary",), vmem_limit_bytes=48 * MIB),
        name="merge",
    )(x, y_abc, yd_t, yd_s, g, w_gate, wpa, wpb, wpc, wpd, wo)


def _rope_tables(cfg):
    half = HEAD_DIM // 2
    inv_freq = ROPE_THETA ** (-jnp.arange(half, dtype=F32) / half)
    pos = jnp.concatenate([jnp.arange(cfg.s_prompt), jnp.tile(cfg.past + jnp.arange(cfg.t_dec), cfg.n_dec)])
    ang = pos.astype(F32)[:, None] * inv_freq[None, :]
    cos, sin = jnp.cos(ang), jnp.sin(ang)
    reps = LANES // HEAD_DIM
    cos_t = jnp.tile(jnp.concatenate([cos, cos], axis=1), (1, reps))
    sin_t = jnp.tile(jnp.concatenate([-sin, sin], axis=1), (1, reps))
    return cos_t, sin_t


def _pad_rows_top(a, rows):
    return jnp.pad(a, ((0, 0), (0, 0), (rows - a.shape[2], 0), (0, 0)))


def _forward(cfg, x_prompt, x_sample, cache_k, cache_v, state_pool, state_sconv, state_cconv,
             g_ffn1, w1_gate, w1_up, w1_down, g_mix, w_in, pool_w, pool_scale, sconv_w,
             cconv_w, cconv_b, ln_g, ln_b, q_norm_g, k_norm_g, lam_q1, lam_k1, lam_q2, lam_k2,
             subln_g, wp_a, wp_b, wp_c, wp_d, w_out, g_ffn2, w2_gate, w2_up, w2_down):
    depth = w_in.shape[0]
    sp, nb, td = cfg.s_prompt, cfg.n_dec, cfg.t_dec
    x = (x_prompt.reshape(sp, D_MODEL), x_sample.reshape(nb * td, D_MODEL))

    bf = lambda w: w.astype(BF16)
    w1g, w1u, w1d = bf(w1_gate), bf(w1_up), bf(w1_down)
    w2g, w2u, w2d = bf(w2_gate), bf(w2_up), bf(w2_down)
    w_proj, w_gate = bf(w_in[:, :, :W_PROJ]), bf(w_in[:, :, W_PROJ:])
    wpa, wpb, wpc, wpd, wo = bf(wp_a), bf(wp_b), bf(wp_c), bf(wp_d), bf(w_out)
    cos_t, sin_t = _rope_tables(cfg)
    ones_bd = jnp.kron(jnp.eye(NORM_W // HEAD_DIM, dtype=F32), jnp.ones((HEAD_DIM, HEAD_DIM), F32)).astype(BF16)
    eye_g = jnp.eye(len(POOL_WINDOWS), dtype=F32)
    ck = jnp.transpose(cache_k, (0, 1, 3, 4, 2))
    cv = cache_v.reshape(depth, nb, cfg.past * N_HEADS, 2 * HEAD_DIM)
    st_pool = _pad_rows_top(state_pool, HALO)
    st_sconv = _pad_rows_top(state_sconv, HALO)
    st_cconv = _pad_rows_top(state_cconv, HALO)
    row = lambda a: a.reshape(1, -1)

    outs = [[] for _ in range(6)]
    caches = None
    for l in range(depth):
        lam_init = 0.8 - 0.6 * math.exp(-0.3 * l)
        x = _ffn(cfg, l, x, row(g_ffn1[l]), w1g, w1u, w1d)
        qg = row(jnp.tile(q_norm_g[l], NORM_W // HEAD_DIM))
        kg = row(jnp.tile(k_norm_g[l], NORM_W // HEAD_DIM))
        zabc, qb, kb, vb, qt, vt, *caches = _inproj(cfg, l, depth, x, row(g_mix[l]), w_proj, qg, kg, cos_t, sin_t,
                                                    ones_bd, caches)

        pw_bd = (eye_g[:, None, :, None] * pool_w[l][:, :, None, :]).reshape(W_A, W_A).astype(BF16)
        mix_w = (pw_bd, row(pool_scale[l]), sconv_w[l], cconv_w[l], row(cconv_b[l]), row(ln_g[l]), row(ln_b[l]))
        y_abc, pool_p, sconv_p, cconv_p = _mixer_prompt(cfg, zabc, mix_w)
        y_abc, pool_s, sconv_s, cconv_s = _mixer_sample(cfg, zabc, st_pool[l], st_sconv[l], st_cconv[l],
                                                        mix_w, y_abc)

        lam_w = (row(lam_q1[l]), row(lam_k1[l]), row(lam_q2[l]), row(lam_k2[l]), row(subln_g[l]))
        yd_t = _attn_prompt(cfg, qt, kb, vt, lam_w, lam_init)
        yd_s = _attn_sample(cfg, l, qb, kb, vb, ck, cv, lam_w, lam_init)

        x = _merge(cfg, l, x, y_abc, yd_t, yd_s, row(g_mix[l]), w_gate, wpa, wpb, wpc, wpd, wo)
        x = _ffn(cfg, l, x, row(g_ffn2[l]), w2g, w2u, w2d, split_out=(l == depth - 1))

        outs[0].append(pool_p[None, 16 - POOL_STATE:])
        outs[1].append(sconv_p[None, 8 - (SCONV_K - 1):])
        outs[2].append(cconv_p[None, 32 - (CCONV_K - 1):])
        outs[3].append(pool_s[:, 16 - POOL_STATE:])
        outs[4].append(sconv_s[:, 8 - (SCONV_K - 1):])
        outs[5].append(cconv_s[:, 32 - (CCONV_K - 1):])

    kt_all, ks_all, vp_all, vs_all = caches
    y_prompt = x[0].reshape(1, sp, D_MODEL)
    y_sample = x[1].reshape(nb, td, D_MODEL)
    k_prompt = jnp.transpose(kt_all.reshape(depth, 1, 2 * N_HEADS, HEAD_DIM, sp), (0, 1, 4, 2, 3))
    v_prompt = vp_all.reshape(depth, 1, sp, N_HEADS, 2 * HEAD_DIM)
    k_sample = ks_all.reshape(depth, nb, td, 2 * N_HEADS, HEAD_DIM)
    v_sample = vs_all.reshape(depth, nb, td, N_HEADS, 2 * HEAD_DIM)
    st = [jnp.stack(o) for o in outs]
    return (y_prompt, y_sample, k_prompt, v_prompt, st[0], st[1], st[2], k_sample, v_sample, st[3], st[4], st[5])


def kernel(x_prompt, x_sample, cache_k, cache_v, state_pool, state_sconv, state_cconv, g_ffn1, w1_gate, w1_up, w1_down, g_mix, w_in, pool_w, pool_scale, sconv_w, cconv_w, cconv_b, ln_g, ln_b, q_norm_g, k_norm_g, lam_q1, lam_k1, lam_q2, lam_k2, subln_g, wp_a, wp_b, wp_c, wp_d, w_out, g_ffn2, w2_gate, w2_up, w2_down):
    assert x_prompt.shape[0] == 1
    cfg = Cfg(s_prompt=x_prompt.shape[1], n_dec=x_sample.shape[0], t_dec=x_sample.shape[1],
              past=cache_k.shape[2], tm=512, tmix=512)
    return _forward(cfg, x_prompt, x_sample, cache_k, cache_v, state_pool, state_sconv, state_cconv,
                    g_ffn1, w1_gate, w1_up, w1_down, g_mix, w_in, pool_w, pool_scale, sconv_w,
                    cconv_w, cconv_b, ln_g, ln_b, q_norm_g, k_norm_g, lam_q1, lam_k1, lam_q2, lam_k2,
                    subln_g, wp_a, wp_b, wp_c, wp_d, w_out, g_ffn2, w2_gate, w2_up, w2_down)
```

```python
import functools
import math
from typing import NamedTuple

import jax
import jax.numpy as jnp
from jax import lax
from jax.experimental import pallas as pl
from jax.experimental.pallas import tpu as pltpu

F32 = jnp.float32
BF16 = jnp.bfloat16

D_MODEL = 1024
DEPTH = 4
CHUNK = 64
POOL_WINDOWS = (2, 4, 8, 16)
W_A = 256
GA = 64
POOL_STATE = 15
W_B = 256
SCONV_K = 3
W_C = 256
CCONV_K = 31
HEAD_DIM = 64
N_HEADS = 4
W_D = 512
ROPE_THETA = 10000.0
D_FF = 2816
EPS = 1e-6
W_ABC = W_A + 3 * W_B + 2 * W_C
W_PROJ = W_ABC + 3 * W_D
W_Y = W_A + W_B + W_C

LANES = 128
NORM_W = 256
PAIRS_PER_TRIP = 4
COL_GROUPS = 4
VT_ROWS = LANES + 16
LOG2E = 1.4426950408889634
HALO = 32
MIB = 1024 * 1024


class Cfg(NamedTuple):
    s_prompt: int
    n_dec: int
    t_dec: int
    past: int
    tm: int
    tmix: int

    @property
    def n_tok(self):
        return self.s_prompt + self.n_dec * self.t_dec


def _const_spec(shape, index):
    return pl.BlockSpec(shape, lambda *_: index, pipeline_mode=pl.Buffered(1))


def _rms(x, g):
    ms = jnp.mean(x * x, axis=-1, keepdims=True)
    return x * lax.rsqrt(ms + EPS) * g


def _dot(a, b):
    return jnp.dot(a, b, preferred_element_type=F32)


def _dot_t(a, b):
    return lax.dot_general(a, b, (((1,), (1,)), ((), ())), preferred_element_type=F32)


def _ffn_kernel(*refs, n_prompt_tiles, split_in, split_out):
    refs = list(refs)
    is_prompt = pl.program_id(0) < n_prompt_tiles
    if split_in:
        xp_ref, xs_ref = refs[0:2]
        del refs[0:2]
        x = jnp.where(is_prompt, xp_ref[...], xs_ref[...])
    else:
        x = refs.pop(0)[...]
    g_ref, wg_ref, wu_ref, wd_ref = refs[0:4]
    halves = []
    for xh in (x[0:x.shape[0] // 2], x[x.shape[0] // 2:]):
        h = _rms(xh, g_ref[...]).astype(BF16)
        a = _dot(h, wg_ref[...])
        u = _dot(h, wu_ref[...])
        act = (a * jax.nn.sigmoid(a) * u).astype(BF16)
        halves.append(0.5 * _dot(act, wd_ref[...]))
    half_step = jnp.concatenate(halves, axis=0)
    if split_out:
        op_ref, os_ref = refs[4:6]

        @pl.when(is_prompt)
        def _():
            op_ref[...] = x + half_step

        @pl.when(jnp.logical_not(is_prompt))
        def _():
            os_ref[...] = x + half_step
    else:
        refs[4][...] = x + half_step


def _ffn(cfg, l, x, g, wg, wu, wd, split_out=False):
    n, tm, sp = cfg.n_tok, cfg.tm, cfg.s_prompt
    assert sp % tm == 0 and (n - sp) % tm == 0
    npt = sp // tm
    split_in = isinstance(x, tuple)
    rows = pl.BlockSpec((tm, D_MODEL), lambda i: (i, 0))
    prompt_rows = pl.BlockSpec((tm, D_MODEL), lambda i: (jnp.minimum(i, npt - 1), 0))
    sample_rows = pl.BlockSpec((tm, D_MODEL), lambda i: (jnp.maximum(i - npt, 0), 0))
    whole = jax.ShapeDtypeStruct((n, D_MODEL), F32)
    halves = [jax.ShapeDtypeStruct((sp, D_MODEL), F32), jax.ShapeDtypeStruct((n - sp, D_MODEL), F32)]
    return pl.pallas_call(
        functools.partial(_ffn_kernel, n_prompt_tiles=npt, split_in=split_in, split_out=split_out),
        grid=(n // tm,),
        in_specs=([prompt_rows, sample_rows] if split_in else [rows]) + [
            _const_spec((1, D_MODEL), (0, 0)),
            _const_spec((None, D_MODEL, D_FF), (l, 0, 0)),
            _const_spec((None, D_MODEL, D_FF), (l, 0, 0)),
            _const_spec((None, D_FF, D_MODEL), (l, 0, 0)),
        ],
        out_specs=[prompt_rows, sample_rows] if split_out else rows,
        out_shape=halves if split_out else whole,
        compiler_params=pltpu.CompilerParams(
            dimension_semantics=("arbitrary",), vmem_limit_bytes=52 * MIB),
        name="ffn",
    )(*(x if split_in else (x,)), g, wg, wu, wd)


def _inproj_kernel(x_ref, g_ref, w_ref, qg_ref, kg_ref, cos_ref, sin_ref, ones_ref, *rest, n_prompt_tiles):
    zabc_ref, qb_ref, kb_ref, vb_ref, qt_ref, vt_ref, kt_ref, ks_ref, vp_ref, vs_ref, zqkv_ref = rest[-11:]
    is_prompt = pl.program_id(0) < n_prompt_tiles
    h = _rms(x_ref[...], g_ref[...]).astype(BF16)
    assert W_ABC % (2 * NORM_W) == 0 and W_ABC // (2 * NORM_W) <= W_D // LANES
    cos = cos_ref[...]
    sin = sin_ref[...]
    ones = ones_ref[...]
    lane = lax.broadcasted_iota(jnp.int32, cos.shape, 1)
    first_half = (lane & (HEAD_DIM // 2)) == 0

    def head_norm(z, g):
        ss = z * z
        hi = ss.astype(BF16)
        lo = (ss - hi.astype(F32)).astype(BF16)
        tot = _dot(hi, ones) + _dot(lo, ones)
        return z * lax.rsqrt(tot * (1.0 / HEAD_DIM) + EPS) * g

    def rope(y):
        half = HEAD_DIM // 2
        partner = jnp.where(first_half, pltpu.roll(y, LANES - half, 1), pltpu.roll(y, half, 1))
        return y * cos + partner * sin

    nw = ones.shape[0]
    abc_w = 2 * NORM_W
    n_chunks = W_D // LANES
    for g0 in range(0, W_D, nw):
        for part in range(3):
            lo = part * W_D + g0
            zqkv_ref[:, lo:lo + nw] = _dot(h, w_ref[:, W_ABC + lo:W_ABC + lo + nw])
    yq = [head_norm(zqkv_ref[:, g0:g0 + nw], qg_ref[...]) for g0 in range(0, W_D, nw)]
    yk = [head_norm(zqkv_ref[:, W_D + g0:W_D + g0 + nw], kg_ref[...]) for g0 in range(0, W_D, nw)]
    for c in range(n_chunks):
        sl = slice(c * LANES, (c + 1) * LANES)
        grp, off = divmod(c * LANES, nw)
        q = rope(yq[grp][:, off:off + LANES]) * (HEAD_DIM ** -0.5)
        qb_ref[:, sl] = q.astype(BF16)
        qt_ref[c] = (q * LOG2E).T.astype(BF16)
        k = rope(yk[grp][:, off:off + LANES])
        kb_ref[:, sl] = k.astype(BF16)
        zqkv_ref[:, W_D + c * LANES:W_D + (c + 1) * LANES] = k
        v = zqkv_ref[:, 2 * W_D + c * LANES:2 * W_D + (c + 1) * LANES]
        vb_ref[:, sl] = v.astype(BF16)
        vt_ref[c, 0:LANES, :] = v.T.astype(BF16)
        vt_ref[c, LANES:VT_ROWS, :] = jnp.ones((VT_ROWS - LANES, v.shape[0]), BF16)
        if c * abc_w < W_ABC:
            zabc_ref[:, c * abc_w:(c + 1) * abc_w] = _dot(h, w_ref[:, c * abc_w:(c + 1) * abc_w])

    @pl.when(is_prompt)
    def _():
        for c in range(W_D // LANES):
            kt = zqkv_ref[:, W_D + c * LANES:W_D + (c + 1) * LANES].T
            kt_ref[2 * c] = kt[0:HEAD_DIM]
            kt_ref[2 * c + 1] = kt[HEAD_DIM:LANES]
        for hd in range(N_HEADS):
            vp_ref[pl.ds(hd, kt_ref.shape[-1], stride=N_HEADS), :] = (
                zqkv_ref[:, 2 * W_D + hd * LANES:2 * W_D + (hd + 1) * LANES])

    @pl.when(jnp.logical_not(is_prompt))
    def _():
        ks_ref[...] = zqkv_ref[:, W_D:2 * W_D]
        vs_ref[...] = zqkv_ref[:, 2 * W_D:3 * W_D]


def _inproj(cfg, l, depth, x, g, w_in, qg, kg, cos_t, sin_t, ones_bd, caches):
    n, tm, sp = cfg.n_tok, cfg.tm, cfg.s_prompt
    n_dec = n - sp
    assert sp % tm == 0 and n_dec % tm == 0
    npt = sp // tm
    row = lambda w: pl.BlockSpec((tm, w), lambda i: (i, 0))
    prompt_rows = pl.BlockSpec((None, tm * N_HEADS, LANES), lambda i: (l, jnp.minimum(i, npt - 1), 0))
    sample_rows = pl.BlockSpec((None, tm, W_D), lambda i: (l, jnp.maximum(i - npt, 0), 0))
    n_in = 8
    aliased = [] if caches is None else list(caches)
    return pl.pallas_call(
        functools.partial(_inproj_kernel, n_prompt_tiles=npt),
        grid=(n // tm,),
        in_specs=[
            row(D_MODEL),
            _const_spec((1, D_MODEL), (0, 0)),
            _const_spec((None, D_MODEL, W_PROJ), (l, 0, 0)),
            _const_spec((1, NORM_W), (0, 0)),
            _const_spec((1, NORM_W), (0, 0)),
            row(LANES),
            row(LANES),
            _const_spec((NORM_W, NORM_W), (0, 0)),
        ] + [pl.BlockSpec(memory_space=pl.ANY) for _ in aliased],
        out_specs=[row(W_ABC), row(W_D), row(W_D), row(W_D),
                   pl.BlockSpec((N_HEADS, None, LANES, tm), lambda i: (0, i, 0, 0)),
                   pl.BlockSpec((N_HEADS, None, VT_ROWS, tm), lambda i: (0, i, 0, 0)),
                   pl.BlockSpec((None, 2 * N_HEADS, HEAD_DIM, tm), lambda i: (l, 0, 0, jnp.minimum(i, npt - 1))),
                   sample_rows, prompt_rows, sample_rows],
        out_shape=[
            jax.ShapeDtypeStruct((n, W_ABC), F32),
            jax.ShapeDtypeStruct((n, W_D), BF16),
            jax.ShapeDtypeStruct((n, W_D), BF16),
            jax.ShapeDtypeStruct((n, W_D), BF16),
            jax.ShapeDtypeStruct((N_HEADS, n // tm, LANES, tm), BF16),
            jax.ShapeDtypeStruct((N_HEADS, n // tm, VT_ROWS, tm), BF16),
            jax.ShapeDtypeStruct((depth, 2 * N_HEADS, HEAD_DIM, sp), F32),
            jax.ShapeDtypeStruct((depth, n_dec, W_D), F32),
            jax.ShapeDtypeStruct((depth, sp * N_HEADS, LANES), F32),
            jax.ShapeDtypeStruct((depth, n_dec, W_D), F32),
        ],
        scratch_shapes=[pltpu.VMEM((tm, 3 * W_D), F32)],
        input_output_aliases={n_in + j: 6 + j for j in range(len(aliased))},
        compiler_params=pltpu.CompilerParams(
            dimension_semantics=("arbitrary",), vmem_limit_bytes=44 * MIB),
        name="inproj",
    )(x, g, w_in, qg, kg, cos_t, sin_t, ones_bd, *aliased)


SAMPLE_MIX_GROUP = 8
MIX_ROWS = 128
SUBLANES = 8


def _shifted_rows(ref, base, rows, depth, tmp):
    out = {}
    for r in range(SUBLANES):
        js = [j for j in range(1, depth + 1) if (-j) % SUBLANES == r]
        if not js:
            continue
        start, length = base - max(js), max(js) - min(js) + rows
        tmp[r, 0:length, :] = ref[start:start + length, :]
        for j in js:
            out[j] = tmp[r, max(js) - j:max(js) - j + rows, :]
    return out


def _mixer_compute(t, pos0, z_ref, pw_ref, ps_ref, sw_ref, cw_ref, cb_ref, lg_ref, lb_ref,
                   y_ref, pool_o, sconv_o, cconv_o, ea, eb, ec, tmp):
    u = z_ref[:, 0:W_A]
    ea[HALO:HALO + t, :] = u
    eb[HALO:HALO + t, :] = z_ref[:, W_A + W_B:W_A + 2 * W_B] * z_ref[:, W_A + 2 * W_B:W_A + 3 * W_B]
    zc = z_ref[:, W_A + 3 * W_B:W_A + 3 * W_B + W_C]
    ec[HALO:HALO + t, :] = zc * jax.nn.sigmoid(z_ref[:, W_A + 3 * W_B + W_C:W_ABC])

    rows = min(t, MIX_ROWS)
    for r0 in range(0, t, rows):
        base = HALO + r0
        lane = lax.broadcasted_iota(jnp.int32, (rows, W_A), 1)
        cur = ea[base:base + rows, :]
        back = _shifted_rows(ea, base, rows, max(POOL_WINDOWS) - 1, tmp)
        acc = cur
        sums = {}
        for j in range(1, max(POOL_WINDOWS)):
            acc = acc + back[j]
            if j + 1 in POOL_WINDOWS:
                sums[j + 1] = acc
        tot = sums[POOL_WINDOWS[-1]]
        win = jnp.full((rows, W_A), float(POOL_WINDOWS[-1]), F32)
        for gi in range(len(POOL_WINDOWS) - 2, -1, -1):
            in_group = lane < (gi + 1) * GA
            tot = jnp.where(in_group, sums[POOL_WINDOWS[gi]], tot)
            win = jnp.where(in_group, float(POOL_WINDOWS[gi]), win)
        if pos0 is None:
            cnt = win
        else:
            pos1 = (pos0 + r0 + 1 + lax.broadcasted_iota(jnp.int32, (rows, W_A), 0)).astype(F32)
            cnt = jnp.minimum(pos1, win)
        d = (tot / cnt - cur).astype(BF16)
        y_ref[r0:r0 + rows, 0:W_A] = _dot(d, pw_ref[...]) * ps_ref[...]

        conv = sw_ref[SCONV_K - 1:SCONV_K, :] * eb[base:base + rows, :]
        for j in range(SCONV_K - 1):
            off = base - (SCONV_K - 1) + j
            conv = conv + sw_ref[j:j + 1, :] * eb[off:off + rows, :]
        y_ref[r0:r0 + rows, W_A:W_A + W_B] = z_ref[r0:r0 + rows, W_A:W_A + W_B] * conv

        back = _shifted_rows(ec, base, rows, CCONV_K - 1, tmp)
        conv = cw_ref[CCONV_K - 1:CCONV_K, :] * ec[base:base + rows, :]
        for j in range(CCONV_K - 1):
            conv = conv + cw_ref[j:j + 1, :] * back[CCONV_K - 1 - j]
        conv = conv + cb_ref[...]
        mu = jnp.mean(conv, axis=-1, keepdims=True)
        cen = conv - mu
        var = jnp.mean(cen * cen, axis=-1, keepdims=True)
        ln = cen * lax.rsqrt(var + EPS) * lg_ref[...] + lb_ref[...]
        y_ref[r0:r0 + rows, W_A + W_B:W_Y] = ln * jax.nn.sigmoid(ln)

    pool_o[...] = ea[HALO + t - 16:HALO + t, :]
    sconv_o[...] = eb[HALO + t - 8:HALO + t, :]
    cconv_o[...] = ec[HALO + t - 32:HALO + t, :]


def _mixer_prompt_kernel(z_ref, pw_ref, ps_ref, sw_ref, cw_ref, cb_ref, lg_ref, lb_ref,
                         y_ref, pool_o, sconv_o, cconv_o, ea, eb, ec, tmp, *, t):
    i = pl.program_id(0)

    @pl.when(i == 0)
    def _():
        zeros = jnp.zeros((HALO, W_A), F32)
        ea[0:HALO, :] = zeros
        eb[0:HALO, :] = zeros
        ec[0:HALO, :] = zeros

    @pl.when(i > 0)
    def _():
        ea[0:HALO, :] = ea[t:t + HALO, :]
        eb[0:HALO, :] = eb[t:t + HALO, :]
        ec[0:HALO, :] = ec[t:t + HALO, :]

    _mixer_compute(t, i * t, z_ref, pw_ref, ps_ref, sw_ref, cw_ref, cb_ref, lg_ref, lb_ref,
                   y_ref, pool_o, sconv_o, cconv_o, ea, eb, ec, tmp)


def _mixer_sample_kernel(z_ref, sp_ref, ss_ref, sc_ref, pw_ref, ps_ref, sw_ref, cw_ref, cb_ref,
                         lg_ref, lb_ref, y_in_ref, y_ref, pool_o, sconv_o, cconv_o, ea, eb, ec, tmp, *, t, group):
    del y_in_ref
    for s in range(group):
        rows = pl.ds(s * t, t)
        ea[0:HALO, :] = sp_ref[s]
        eb[0:HALO, :] = ss_ref[s]
        ec[0:HALO, :] = sc_ref[s]
        _mixer_compute(t, None, z_ref.at[rows], pw_ref, ps_ref, sw_ref, cw_ref, cb_ref, lg_ref, lb_ref,
                       y_ref.at[rows], pool_o.at[s], sconv_o.at[s], cconv_o.at[s], ea, eb, ec, tmp)


def _mixer_weight_specs():
    return [
        _const_spec((W_A, W_A), (0, 0)),
        _const_spec((1, W_A), (0, 0)),
        _const_spec((SCONV_K, W_B), (0, 0)),
        _const_spec((CCONV_K, W_C), (0, 0)),
        _const_spec((1, W_C), (0, 0)),
        _const_spec((1, W_C), (0, 0)),
        _const_spec((1, W_C), (0, 0)),
    ]


def _mixer_scratch(t):
    return [pltpu.VMEM((HALO + t, W_A), F32), pltpu.VMEM((HALO + t, W_B), F32),
            pltpu.VMEM((HALO + t, W_C), F32),
            pltpu.VMEM((SUBLANES, min(t, MIX_ROWS) + HALO, W_C), F32)]


def _mixer_prompt(cfg, zabc, weights):
    t = cfg.tmix
    const_out = lambda r: pl.BlockSpec((r, W_A), lambda i: (0, 0))
    return pl.pallas_call(
        functools.partial(_mixer_prompt_kernel, t=t),
        grid=(cfg.s_prompt // t,),
        in_specs=[pl.BlockSpec((t, W_ABC), lambda i: (i, 0))] + _mixer_weight_specs(),
        out_specs=[pl.BlockSpec((t, W_Y), lambda i: (i, 0)), const_out(16), const_out(8), const_out(32)],
        out_shape=[
            jax.ShapeDtypeStruct((cfg.n_tok, W_Y), F32),
            jax.ShapeDtypeStruct((16, W_A), F32),
            jax.ShapeDtypeStruct((8, W_B), F32),
            jax.ShapeDtypeStruct((32, W_C), F32),
        ],
        scratch_shapes=_mixer_scratch(t),
        compiler_params=pltpu.CompilerParams(dimension_semantics=("arbitrary",)),
        name="mixer_prompt",
    )(zabc, *weights)


def _mixer_sample(cfg, zabc, st_pool, st_sconv, st_cconv, weights, y_abc):
    t, nb = cfg.t_dec, cfg.n_dec
    group = math.gcd(nb, SAMPLE_MIX_GROUP)
    assert cfg.s_prompt % (group * t) == 0
    row0 = cfg.s_prompt // (group * t)
    state_spec = lambda: pl.BlockSpec((group, HALO, W_A), lambda b: (b, 0, 0))
    out_state = lambda r: pl.BlockSpec((group, r, W_A), lambda b: (b, 0, 0))
    return pl.pallas_call(
        functools.partial(_mixer_sample_kernel, t=t, group=group),
        grid=(nb // group,),
        in_specs=[pl.BlockSpec((group * t, W_ABC), lambda b: (row0 + b, 0)),
                  state_spec(), state_spec(), state_spec()]
                 + _mixer_weight_specs()
                 + [pl.BlockSpec(memory_space=pl.ANY)],
        out_specs=[pl.BlockSpec((group * t, W_Y), lambda b: (row0 + b, 0)),
                   out_state(16), out_state(8), out_state(32)],
        out_shape=[
            jax.ShapeDtypeStruct((cfg.n_tok, W_Y), F32),
            jax.ShapeDtypeStruct((nb, 16, W_A), F32),
            jax.ShapeDtypeStruct((nb, 8, W_B), F32),
            jax.ShapeDtypeStruct((nb, 32, W_C), F32),
        ],
        scratch_shapes=_mixer_scratch(t),
        input_output_aliases={11: 0},
        compiler_params=pltpu.CompilerParams(dimension_semantics=("arbitrary",)),
        name="mixer_sample",
    )(zabc, st_pool, st_sconv, st_cconv, *weights, y_abc)


def _lambda(lq1, lk1, lq2, lk2, lam_init):
    s1 = jnp.sum(lq1[...] * lk1[...], axis=-1, keepdims=True)
    s2 = jnp.sum(lq2[...] * lk2[...], axis=-1, keepdims=True)
    return jnp.exp(s1) - jnp.exp(s2) + lam_init


def _diff_out(acc, l, lam, sg, lam_init, t):
    o = acc[0:t] / l[0:t] - lam * (acc[t:2 * t] / l[t:2 * t])
    return _rms(o, sg) * (1.0 - lam_init)


def _attn_prompt_kernel(qt_ref, qtn_ref, k_ref, vt_ref, lq1, lk1, lq2, lk2, sgc_ref, o_ref,
                        q2t_ref, q2tn_ref, s0_ref, s1_ref, mx0_ref, mx1_ref, m_ref, acc_ref, *, tk, lam_init):
    qi = pl.program_id(1)
    nq = 2 * tk
    chan = lax.broadcasted_iota(jnp.int32, (LANES, tk), 0)
    zero = jnp.zeros((LANES, tk), BF16)
    for src_ref, dst_ref in ((qt_ref, q2t_ref), (qtn_ref, q2tn_ref)):
        for half in range(2):
            qt = src_ref[half]
            dst_ref[:, half * tk:(half + 1) * tk] = jnp.where(chan < HEAD_DIM, qt, zero)
            dst_ref[:, nq + half * tk:nq + (half + 1) * tk] = jnp.where(chan >= HEAD_DIM, qt, zero)
    m_ref[...] = jnp.full(m_ref.shape, -jnp.inf, F32)
    acc_ref[...] = jnp.zeros(acc_ref.shape, F32)

    ncols = 2 * nq // COL_GROUPS

    def scores(j, s_ref, mx_ref, diagonal=None, group=None, queries=q2t_ref):
        cols = slice(None) if group is None else slice(group * ncols, (group + 1) * ncols)
        kb = k_ref[pl.ds(pl.multiple_of(j * tk, tk), tk), :]
        s = _dot(kb, queries[:, cols])
        if diagonal is not None:
            key = lax.broadcasted_iota(jnp.int32, s.shape, 0)
            col = lax.broadcasted_iota(jnp.int32, s.shape, 1) + (0 if group is None else group * ncols)
            key_chunk = diagonal * (tk // CHUNK) + key // CHUNK
            s = jnp.where(key_chunk <= (col & (nq - 1)) // CHUNK, s, -jnp.inf)
        s_ref[:, cols] = s
        mx_ref[:, cols] = jnp.max(s, axis=0, keepdims=True)

    def consume(j, s_ref, mx_ref, group=None):
        cols = slice(None) if group is None else slice(group * ncols, (group + 1) * ncols)
        m_prev = m_ref[:, cols]
        m_new = jnp.maximum(m_prev, mx_ref[:, cols])
        alpha = jnp.exp2(m_prev - m_new)
        p = jnp.exp2(s_ref[:, cols] - m_new).astype(BF16)
        acc_ref[:, cols] = alpha * acc_ref[:, cols] + _dot(vt_ref[j], p)
        m_ref[:, cols] = m_new

    @pl.when(qi == 0)
    def _():
        scores(0, s0_ref, mx0_ref, 0)

    def pair(i, first_diagonal):
        j = 2 * i
        for g in range(COL_GROUPS):
            scores(j + 1, s1_ref, mx1_ref, None, g)
            consume(j, s0_ref, mx0_ref, g)
        for g in range(COL_GROUPS):
            scores(j + 2, s0_ref, mx0_ref, first_diagonal, g)
            consume(j + 1, s1_ref, mx1_ref, g)

    n_plain = jnp.maximum(qi - 1, 0)

    def run_pairs(first, count):
        for n in range(count):
            pair(first + n, None)

    def trip(i, carry):
        run_pairs(PAIRS_PER_TRIP * i, PAIRS_PER_TRIP)
        return carry

    lax.fori_loop(0, n_plain // PAIRS_PER_TRIP, trip, 0)
    done = (n_plain // PAIRS_PER_TRIP) * PAIRS_PER_TRIP
    size = PAIRS_PER_TRIP // 2
    while size >= 1:
        take = ((n_plain - done) // size) % 2 == 1
        pl.when(take)(functools.partial(run_pairs, done, size))
        done = done + jnp.where(take, size, 0)
        size //= 2

    @pl.when(qi > 0)
    def _():
        pair(qi - 1, 0)

    assert tk % ncols == 0
    late = [g for g in range(COL_GROUPS) if (g * ncols) % nq >= tk]
    for g in range(COL_GROUPS):
        if g in late:
            scores(2 * qi + 1, s1_ref, mx1_ref, 1, g)
        consume(2 * qi, s0_ref, mx0_ref, g)
        scores(0, s0_ref, mx0_ref, None, g, q2tn_ref)
    for g in late:
        consume(2 * qi + 1, s1_ref, mx1_ref, g)

    lam = _lambda(lq1, lk1, lq2, lk2, lam_init)
    acc = acc_ref[0:LANES, :]
    l = acc_ref[LANES:LANES + 1, :]
    ot = acc[:, 0:nq] * (1.0 / l[:, 0:nq]) - acc[:, nq:2 * nq] * (lam / l[:, nq:2 * nq])
    ms = jnp.mean(ot * ot, axis=0, keepdims=True)
    o_ref[...] = ot * lax.rsqrt(ms + EPS) * sgc_ref[...] * (1.0 - lam_init)


def _lam_specs():
    return [_const_spec((1, HEAD_DIM), (0, 0)) for _ in range(4)] + [_const_spec((1, LANES), (0, 0))]


def _attn_prompt(cfg, qt, kb, vt, lam_w, lam_init):
    s, tk = cfg.s_prompt, cfg.tm
    tq = 2 * tk
    assert tq & (tq - 1) == 0 and tk % CHUNK == 0 and s % tq == 0
    return pl.pallas_call(
        functools.partial(_attn_prompt_kernel, tk=tk, lam_init=lam_init),
        grid=(N_HEADS, s // tq),
        in_specs=[
            pl.BlockSpec((None, 2, LANES, tk), lambda h, i: (h, i, 0, 0)),
            pl.BlockSpec((None, 2, LANES, tk), lambda h, i: (h, jnp.minimum(i + 1, s // tq - 1), 0, 0)),
            pl.BlockSpec((s, LANES), lambda h, i: (0, h)),
            pl.BlockSpec((None, s // tk, VT_ROWS, tk), lambda h, i: (h, 0, 0, 0)),
        ] + _lam_specs()[:4] + [_const_spec((LANES, 1), (0, 0))],
        out_specs=pl.BlockSpec((LANES, tq), lambda h, i: (h, i)),
        out_shape=jax.ShapeDtypeStruct((W_D, s), F32),
        scratch_shapes=[
            pltpu.VMEM((LANES, 2 * tq), BF16),
            pltpu.VMEM((LANES, 2 * tq), BF16),
            pltpu.VMEM((tk, 2 * tq), F32),
            pltpu.VMEM((tk, 2 * tq), F32),
            pltpu.VMEM((1, 2 * tq), F32),
            pltpu.VMEM((1, 2 * tq), F32),
            pltpu.VMEM((1, 2 * tq), F32),
            pltpu.VMEM((VT_ROWS, 2 * tq), F32),
        ],
        compiler_params=pltpu.CompilerParams(
            dimension_semantics=("arbitrary", "arbitrary"), vmem_limit_bytes=52 * MIB),
        name="attn_prompt",
    )(qt, qt, kb, vt, *lam_w[:4], lam_w[4].reshape(LANES, 1))


def _attn_sample_kernel(q_ref, kn_ref, vn_ref, kc_ref, vc_ref, lq1, lk1, lq2, lk2, sg_ref,
                        o_ref, *, t, lam_init):
    lam = _lambda(lq1, lk1, lq2, lk2, lam_init)
    past = vc_ref.shape[0] // N_HEADS
    for h in range(N_HEADS):
        sl = slice(h * LANES, (h + 1) * LANES)
        s_past, s_new = [], []
        for c in range(2):
            ch = slice(h * LANES + c * HEAD_DIM, h * LANES + (c + 1) * HEAD_DIM)
            q = q_ref[:, ch]
            s_past.append(_dot(q, kc_ref[2 * h + c].astype(BF16)))
            s_new.append(_dot_t(q, kn_ref[:, ch]))
        s_past = jnp.concatenate(s_past, axis=0)
        s_new = jnp.concatenate(s_new, axis=0)
        v_past = vc_ref[pl.ds(h, past, stride=N_HEADS), :]
        m = jnp.maximum(jnp.max(s_past, axis=-1, keepdims=True), jnp.max(s_new, axis=-1, keepdims=True))
        p_past = jnp.exp(s_past - m)
        p_new = jnp.exp(s_new - m)
        l = jnp.sum(p_past, axis=-1, keepdims=True) + jnp.sum(p_new, axis=-1, keepdims=True)
        acc = _dot(p_past.astype(BF16), v_past.astype(BF16)) + _dot(p_new.astype(BF16), vn_ref[:, sl])
        o_ref[:, sl] = _diff_out(acc, l, lam, sg_ref[...], lam_init, t)


def _attn_sample(cfg, l, qb, kb, vb, cache_k, cache_v, lam_w, lam_init):
    t, nb = cfg.t_dec, cfg.n_dec
    row0 = cfg.s_prompt // t
    new_rows = lambda: pl.BlockSpec((t, W_D), lambda b: (row0 + b, 0))
    cache = lambda a: pl.BlockSpec((None, None) + a.shape[2:], lambda b: (l, b) + (0,) * (a.ndim - 2))
    return pl.pallas_call(
        functools.partial(_attn_sample_kernel, t=t, lam_init=lam_init),
        grid=(nb,),
        in_specs=[new_rows(), new_rows(), new_rows(), cache(cache_k), cache(cache_v)] + _lam_specs(),
        out_specs=pl.BlockSpec((t, W_D), lambda b: (b, 0)),
        out_shape=jax.ShapeDtypeStruct((nb * t, W_D), F32),
        compiler_params=pltpu.CompilerParams(
            dimension_semantics=("arbitrary",), vmem_limit_bytes=40 * MIB),
        name="attn_sample",
    )(qb, kb, vb, cache_k, cache_v, *lam_w)


def _merge_kernel(x_ref, yabc_ref, ydt_ref, yds_ref, g_ref, wg_ref, wpa_ref, wpb_ref, wpc_ref, wpd_ref, wo_ref,
                  o_ref, *, n_prompt_tiles):
    is_prompt = pl.program_id(0) < n_prompt_tiles
    half = x_ref.shape[0] // 2
    for rows in (slice(0, half), slice(half, 2 * half)):
        yd = jnp.where(is_prompt, ydt_ref[:, rows].T, yds_ref[rows, :])
        x = x_ref[rows, :]
        h = _rms(x, g_ref[...]).astype(BF16)
        branches = (
            (yabc_ref[rows, 0:W_A], wpa_ref),
            (yabc_ref[rows, W_A:W_A + W_B], wpb_ref),
            (yabc_ref[rows, W_A + W_B:W_Y], wpc_ref),
            (yd, wpd_ref),
        )
        merged = None
        for i, (y, wp_ref) in enumerate(branches):
            gate = jax.nn.sigmoid(_dot(h, wg_ref[:, i * D_MODEL:(i + 1) * D_MODEL]))
            term = gate * _dot(y.astype(BF16), wp_ref[...])
            merged = term if merged is None else merged + term
        o_ref[rows, :] = x + _dot(merged.astype(BF16), wo_ref[...])


def _merge(cfg, l, x, y_abc, yd_t, yd_s, g, w_gate, wpa, wpb, wpc, wpd, wo):
    n, tm, sp = cfg.n_tok, cfg.tm, cfg.s_prompt
    assert sp % tm == 0 and (n - sp) % tm == 0
    npt = sp // tm
    row = lambda w: pl.BlockSpec((tm, w), lambda i: (i, 0))
    return pl.pallas_call(
        functools.partial(_merge_kernel, n_prompt_tiles=npt),
        grid=(n // tm,),
        in_specs=[
            row(D_MODEL), row(W_Y),
            pl.BlockSpec((W_D, tm), lambda i: (0, jnp.minimum(i, npt - 1))),
            pl.BlockSpec((tm, W_D), lambda i: (jnp.maximum(i - npt, 0), 0)),
            _const_spec((1, D_MODEL), (0, 0)),
            _const_spec((None, D_MODEL, 4 * D_MODEL), (l, 0, 0)),
            _const_spec((None, W_A, D_MODEL), (l, 0, 0)),
            _const_spec((None, W_B, D_MODEL), (l, 0, 0)),
            _const_spec((None, W_C, D_MODEL), (l, 0, 0)),
            _const_spec((None, W_D, D_MODEL), (l, 0, 0)),
            _const_spec((None, D_MODEL, D_MODEL), (l, 0, 0)),
        ],
        out_specs=row(D_MODEL),
        out_shape=jax.ShapeDtypeStruct((n, D_MODEL), F32),
        compiler_params=pltpu.CompilerParams(
            dimension_semantics=("arbitrary",), vmem_limit_bytes=48 * MIB),
        name="merge",
    )(x, y_abc, yd_t, yd_s, g, w_gate, wpa, wpb, wpc, wpd, wo)


def _rope_tables(cfg):
    half = HEAD_DIM // 2
    inv_freq = ROPE_THETA ** (-jnp.arange(half, dtype=F32) / half)
    pos = jnp.concatenate([jnp.arange(cfg.s_prompt), jnp.tile(cfg.past + jnp.arange(cfg.t_dec), cfg.n_dec)])
    ang = pos.astype(F32)[:, None] * inv_freq[None, :]
    cos, sin = jnp.cos(ang), jnp.sin(ang)
    reps = LANES // HEAD_DIM
    cos_t = jnp.tile(jnp.concatenate([cos, cos], axis=1), (1, reps))
    sin_t = jnp.tile(jnp.concatenate([-sin, sin], axis=1), (1, reps))
    return cos_t, sin_t


def _pad_rows_top(a, rows):
    return jnp.pad(a, ((0, 0), (0, 0), (rows - a.shape[2], 0), (0, 0)))


def _forward(cfg, x_prompt, x_sample, cache_k, cache_v, state_pool, state_sconv, state_cconv,
             g_ffn1, w1_gate, w1_up, w1_down, g_mix, w_in, pool_w, pool_scale, sconv_w,
             cconv_w, cconv_b, ln_g, ln_b, q_norm_g, k_norm_g, lam_q1, lam_k1, lam_q2, lam_k2,
             subln_g, wp_a, wp_b, wp_c, wp_d, w_out, g_ffn2, w2_gate, w2_up, w2_down):
    depth = w_in.shape[0]
    sp, nb, td = cfg.s_prompt, cfg.n_dec, cfg.t_dec
    x = (x_prompt.reshape(sp, D_MODEL), x_sample.reshape(nb * td, D_MODEL))

    bf = lambda w: w.astype(BF16)
    w1g, w1u, w1d = bf(w1_gate), bf(w1_up), bf(w1_down)
    w2g, w2u, w2d = bf(w2_gate), bf(w2_up), bf(w2_down)
    w_proj, w_gate = bf(w_in[:, :, :W_PROJ]), bf(w_in[:, :, W_PROJ:])
    wpa, wpb, wpc, wpd, wo = bf(wp_a), bf(wp_b), bf(wp_c), bf(wp_d), bf(w_out)
    cos_t, sin_t = _rope_tables(cfg)
    ones_bd = jnp.kron(jnp.eye(NORM_W // HEAD_DIM, dtype=F32), jnp.ones((HEAD_DIM, HEAD_DIM), F32)).astype(BF16)
    eye_g = jnp.eye(len(POOL_WINDOWS), dtype=F32)
    ck = jnp.transpose(cache_k, (0, 1, 3, 4, 2))
    cv = cache_v.reshape(depth, nb, cfg.past * N_HEADS, 2 * HEAD_DIM)
    st_pool = _pad_rows_top(state_pool, HALO)
    st_sconv = _pad_rows_top(state_sconv, HALO)
    st_cconv = _pad_rows_top(state_cconv, HALO)
    row = lambda a: a.reshape(1, -1)

    outs = [[] for _ in range(6)]
    caches = None
    for l in range(depth):
        lam_init = 0.8 - 0.6 * math.exp(-0.3 * l)
        x = _ffn(cfg, l, x, row(g_ffn1[l]), w1g, w1u, w1d)
        qg = row(jnp.tile(q_norm_g[l], NORM_W // HEAD_DIM))
        kg = row(jnp.tile(k_norm_g[l], NORM_W // HEAD_DIM))
        zabc, qb, kb, vb, qt, vt, *caches = _inproj(cfg, l, depth, x, row(g_mix[l]), w_proj, qg, kg, cos_t, sin_t,
                                                    ones_bd, caches)

        pw_bd = (eye_g[:, None, :, None] * pool_w[l][:, :, None, :]).reshape(W_A, W_A).astype(BF16)
        mix_w = (pw_bd, row(pool_scale[l]), sconv_w[l], cconv_w[l], row(cconv_b[l]), row(ln_g[l]), row(ln_b[l]))
        y_abc, pool_p, sconv_p, cconv_p = _mixer_prompt(cfg, zabc, mix_w)
        y_abc, pool_s, sconv_s, cconv_s = _mixer_sample(cfg, zabc, st_pool[l], st_sconv[l], st_cconv[l],
                                                        mix_w, y_abc)

        lam_w = (row(lam_q1[l]), row(lam_k1[l]), row(lam_q2[l]), row(lam_k2[l]), row(subln_g[l]))
        yd_t = _attn_prompt(cfg, qt, kb, vt, lam_w, lam_init)
        yd_s = _attn_sample(cfg, l, qb, kb, vb, ck, cv, lam_w, lam_init)

        x = _merge(cfg, l, x, y_abc, yd_t, yd_s, row(g_mix[l]), w_gate, wpa, wpb, wpc, wpd, wo)
        x = _ffn(cfg, l, x, row(g_ffn2[l]), w2g, w2u, w2d, split_out=(l == depth - 1))

        outs[0].append(pool_p[None, 16 - POOL_STATE:])
        outs[1].append(sconv_p[None, 8 - (SCONV_K - 1):])
        outs[2].append(cconv_p[None, 32 - (CCONV_K - 1):])
        outs[3].append(pool_s[:, 16 - POOL_STATE:])
        outs[4].append(sconv_s[:, 8 - (SCONV_K - 1):])
        outs[5].append(cconv_s[:, 32 - (CCONV_K - 1):])

    kt_all, ks_all, vp_all, vs_all = caches
    y_prompt = x[0].reshape(1, sp, D_MODEL)
    y_sample = x[1].reshape(nb, td, D_MODEL)
    k_prompt = jnp.transpose(kt_all.reshape(depth, 1, 2 * N_HEADS, HEAD_DIM, sp), (0, 1, 4, 2, 3))
    v_prompt = vp_all.reshape(depth, 1, sp, N_HEADS, 2 * HEAD_DIM)
    k_sample = ks_all.reshape(depth, nb, td, 2 * N_HEADS, HEAD_DIM)
    v_sample = vs_all.reshape(depth, nb, td, N_HEADS, 2 * HEAD_DIM)
    st = [jnp.stack(o) for o in outs]
    return (y_prompt, y_sample, k_prompt, v_prompt, st[0], st[1], st[2], k_sample, v_sample, st[3], st[4], st[5])


def kernel(x_prompt, x_sample, cache_k, cache_v, state_pool, state_sconv, state_cconv, g_ffn1, w1_gate, w1_up, w1_down, g_mix, w_in, pool_w, pool_scale, sconv_w, cconv_w, cconv_b, ln_g, ln_b, q_norm_g, k_norm_g, lam_q1, lam_k1, lam_q2, lam_k2, subln_g, wp_a, wp_b, wp_c, wp_d, w_out, g_ffn2, w2_gate, w2_up, w2_down):
    assert x_prompt.shape[0] == 1
    cfg = Cfg(s_prompt=x_prompt.shape[1], n_dec=x_sample.shape[0], t_dec=x_sample.shape[1],
              past=cache_k.shape[2], tm=512, tmix=512)
    return _forward(cfg, x_prompt, x_sample, cache_k, cache_v, state_pool, state_sconv, state_cconv,
                    g_ffn1, w1_gate, w1_up, w1_down, g_mix, w_in, pool_w, pool_scale, sconv_w,
                    cconv_w, cconv_b, ln_g, ln_b, q_norm_g, k_norm_g, lam_q1, lam_k1, lam_q2, lam_k2,
                    subln_g, wp_a, wp_b, wp_c, wp_d, w_out, g_ffn2, w2_gate, w2_up, w2_down)
```

```python
import functools
import math
from typing import NamedTuple

import jax
import jax.numpy as jnp
from jax import lax
from jax.experimental import pallas as pl
from jax.experimental.pallas import tpu as pltpu

F32 = jnp.float32
BF16 = jnp.bfloat16

D_MODEL = 1024
DEPTH = 4
CHUNK = 64
POOL_WINDOWS = (2, 4, 8, 16)
W_A = 256
GA = 64
POOL_STATE = 15
W_B = 256
SCONV_K = 3
W_C = 256
CCONV_K = 31
HEAD_DIM = 64
N_HEADS = 4
W_D = 512
ROPE_THETA = 10000.0
D_FF = 2816
EPS = 1e-6
W_ABC = W_A + 3 * W_B + 2 * W_C
W_PROJ = W_ABC + 3 * W_D
W_Y = W_A + W_B + W_C

LANES = 128
NORM_W = 256
PAIRS_PER_TRIP = 4
COL_GROUPS = 4
VT_ROWS = LANES + 16
LOG2E = 1.4426950408889634
HALO = 32
MIB = 1024 * 1024


class Cfg(NamedTuple):
    s_prompt: int
    n_dec: int
    t_dec: int
    past: int
    tm: int
    tmix: int

    @property
    def n_tok(self):
        return self.s_prompt + self.n_dec * self.t_dec


def _const_spec(shape, index):
    return pl.BlockSpec(shape, lambda *_: index, pipeline_mode=pl.Buffered(1))


def _rms(x, g):
    ms = jnp.mean(x * x, axis=-1, keepdims=True)
    return x * lax.rsqrt(ms + EPS) * g


def _dot(a, b):
    return jnp.dot(a, b, preferred_element_type=F32)


def _dot_t(a, b):
    return lax.dot_general(a, b, (((1,), (1,)), ((), ())), preferred_element_type=F32)


def _ffn_kernel(*refs, n_prompt_tiles, split_in, split_out):
    refs = list(refs)
    is_prompt = pl.program_id(0) < n_prompt_tiles
    if split_in:
        xp_ref, xs_ref = refs[0:2]
        del refs[0:2]
        x = jnp.where(is_prompt, xp_ref[...], xs_ref[...])
    else:
        x = refs.pop(0)[...]
    g_ref, wg_ref, wu_ref, wd_ref = refs[0:4]
    halves = []
    for xh in (x[0:x.shape[0] // 2], x[x.shape[0] // 2:]):
        h = _rms(xh, g_ref[...]).astype(BF16)
        a = _dot(h, wg_ref[...])
        u = _dot(h, wu_ref[...])
        act = (a * jax.nn.sigmoid(a) * u).astype(BF16)
        halves.append(0.5 * _dot(act, wd_ref[...]))
    half_step = jnp.concatenate(halves, axis=0)
    if split_out:
        op_ref, os_ref = refs[4:6]

        @pl.when(is_prompt)
        def _():
            op_ref[...] = x + half_step

        @pl.when(jnp.logical_not(is_prompt))
        def _():
            os_ref[...] = x + half_step
    else:
        refs[4][...] = x + half_step


def _ffn(cfg, l, x, g, wg, wu, wd, split_out=False):
    n, tm, sp = cfg.n_tok, cfg.tm, cfg.s_prompt
    assert sp % tm == 0 and (n - sp) % tm == 0
    npt = sp // tm
    split_in = isinstance(x, tuple)
    rows = pl.BlockSpec((tm, D_MODEL), lambda i: (i, 0))
    prompt_rows = pl.BlockSpec((tm, D_MODEL), lambda i: (jnp.minimum(i, npt - 1), 0))
    sample_rows = pl.BlockSpec((tm, D_MODEL), lambda i: (jnp.maximum(i - npt, 0), 0))
    whole = jax.ShapeDtypeStruct((n, D_MODEL), F32)
    halves = [jax.ShapeDtypeStruct((sp, D_MODEL), F32), jax.ShapeDtypeStruct((n - sp, D_MODEL), F32)]
    return pl.pallas_call(
        functools.partial(_ffn_kernel, n_prompt_tiles=npt, split_in=split_in, split_out=split_out),
        grid=(n // tm,),
        in_specs=([prompt_rows, sample_rows] if split_in else [rows]) + [
            _const_spec((1, D_MODEL), (0, 0)),
            _const_spec((None, D_MODEL, D_FF), (l, 0, 0)),
            _const_spec((None, D_MODEL, D_FF), (l, 0, 0)),
            _const_spec((None, D_FF, D_MODEL), (l, 0, 0)),
        ],
        out_specs=[prompt_rows, sample_rows] if split_out else rows,
        out_shape=halves if split_out else whole,
        compiler_params=pltpu.CompilerParams(
            dimension_semantics=("arbitrary",), vmem_limit_bytes=52 * MIB),
        name="ffn",
    )(*(x if split_in else (x,)), g, wg, wu, wd)


def _inproj_kernel(x_ref, g_ref, w_ref, qg_ref, kg_ref, cos_ref, sin_ref, ones_ref, *rest, n_prompt_tiles):
    zabc_ref, qb_ref, kb_ref, vb_ref, qt_ref, vt_ref, kt_ref, ks_ref, vp_ref, vs_ref, zqkv_ref = rest[-11:]
    is_prompt = pl.program_id(0) < n_prompt_tiles
    h = _rms(x_ref[...], g_ref[...]).astype(BF16)
    assert W_ABC % (2 * NORM_W) == 0 and W_ABC // (2 * NORM_W) <= W_D // LANES
    cos = cos_ref[...]
    sin = sin_ref[...]
    ones = ones_ref[...]
    lane = lax.broadcasted_iota(jnp.int32, cos.shape, 1)
    first_half = (lane & (HEAD_DIM // 2)) == 0

    def head_norm(z, g):
        ss = z * z
        hi = ss.astype(BF16)
        lo = (ss - hi.astype(F32)).astype(BF16)
        tot = _dot(hi, ones) + _dot(lo, ones)
        return z * lax.rsqrt(tot * (1.0 / HEAD_DIM) + EPS) * g

    def rope(y):
        half = HEAD_DIM // 2
        partner = jnp.where(first_half, pltpu.roll(y, LANES - half, 1), pltpu.roll(y, half, 1))
        return y * cos + partner * sin

    nw = ones.shape[0]
    abc_w = 2 * NORM_W
    n_chunks = W_D // LANES
    for g0 in range(0, W_D, nw):
        for part in range(3):
            lo = part * W_D + g0
            zqkv_ref[:, lo:lo + nw] = _dot(h, w_ref[:, W_ABC + lo:W_ABC + lo + nw])
    yq = [head_norm(zqkv_ref[:, g0:g0 + nw], qg_ref[...]) for g0 in range(0, W_D, nw)]
    yk = [head_norm(zqkv_ref[:, W_D + g0:W_D + g0 + nw], kg_ref[...]) for g0 in range(0, W_D, nw)]
    for c in range(n_chunks):
        sl = slice(c * LANES, (c + 1) * LANES)
        grp, off = divmod(c * LANES, nw)
        q = rope(yq[grp][:, off:off + LANES]) * (HEAD_DIM ** -0.5)
        qb_ref[:, sl] = q.astype(BF16)
        qt_ref[c] = (q * LOG2E).T.astype(BF16)
        k = rope(yk[grp][:, off:off + LANES])
        kb_ref[:, sl] = k.astype(BF16)
        zqkv_ref[:, W_D + c * LANES:W_D + (c + 1) * LANES] = k
        v = zqkv_ref[:, 2 * W_D + c * LANES:2 * W_D + (c + 1) * LANES]
        vb_ref[:, sl] = v.astype(BF16)
        vt_ref[c, 0:LANES, :] = v.T.astype(BF16)
        vt_ref[c, LANES:VT_ROWS, :] = jnp.ones((VT_ROWS - LANES, v.shape[0]), BF16)
        if c * abc_w < W_ABC:
            zabc_ref[:, c * abc_w:(c + 1) * abc_w] = _dot(h, w_ref[:, c * abc_w:(c + 1) * abc_w])

    @pl.when(is_prompt)
    def _():
        for c in range(W_D // LANES):
            kt = zqkv_ref[:, W_D + c * LANES:W_D + (c + 1) * LANES].T
            kt_ref[2 * c] = kt[0:HEAD_DIM]
            kt_ref[2 * c + 1] = kt[HEAD_DIM:LANES]
        for hd in range(N_HEADS):
            vp_ref[pl.ds(hd, kt_ref.shape[-1], stride=N_HEADS), :] = (
                zqkv_ref[:, 2 * W_D + hd * LANES:2 * W_D + (hd + 1) * LANES])

    @pl.when(jnp.logical_not(is_prompt))
    def _():
        ks_ref[...] = zqkv_ref[:, W_D:2 * W_D]
        vs_ref[...] = zqkv_ref[:, 2 * W_D:3 * W_D]


def _inproj(cfg, l, depth, x, g, w_in, qg, kg, cos_t, sin_t, ones_bd, caches):
    n, tm, sp = cfg.n_tok, cfg.tm, cfg.s_prompt
    n_dec = n - sp
    assert sp % tm == 0 and n_dec % tm == 0
    npt = sp // tm
    row = lambda w: pl.BlockSpec((tm, w), lambda i: (i, 0))
    prompt_rows = pl.BlockSpec((None, tm * N_HEADS, LANES), lambda i: (l, jnp.minimum(i, npt - 1), 0))
    sample_rows = pl.BlockSpec((None, tm, W_D), lambda i: (l, jnp.maximum(i - npt, 0), 0))
    n_in = 8
    aliased = [] if caches is None else list(caches)
    return pl.pallas_call(
        functools.partial(_inproj_kernel, n_prompt_tiles=npt),
        grid=(n // tm,),
        in_specs=[
            row(D_MODEL),
            _const_spec((1, D_MODEL), (0, 0)),
            _const_spec((None, D_MODEL, W_PROJ), (l, 0, 0)),
            _const_spec((1, NORM_W), (0, 0)),
            _const_spec((1, NORM_W), (0, 0)),
            row(LANES),
            row(LANES),
            _const_spec((NORM_W, NORM_W), (0, 0)),
        ] + [pl.BlockSpec(memory_space=pl.ANY) for _ in aliased],
        out_specs=[row(W_ABC), row(W_D), row(W_D), row(W_D),
                   pl.BlockSpec((N_HEADS, None, LANES, tm), lambda i: (0, i, 0, 0)),
                   pl.BlockSpec((N_HEADS, None, VT_ROWS, tm), lambda i: (0, i, 0, 0)),
                   pl.BlockSpec((None, 2 * N_HEADS, HEAD_DIM, tm), lambda i: (l, 0, 0, jnp.minimum(i, npt - 1))),
                   sample_rows, prompt_rows, sample_rows],
        out_shape=[
            jax.ShapeDtypeStruct((n, W_ABC), F32),
            jax.ShapeDtypeStruct((n, W_D), BF16),
            jax.ShapeDtypeStruct((n, W_D), BF16),
            jax.ShapeDtypeStruct((n, W_D), BF16),
            jax.ShapeDtypeStruct((N_HEADS, n // tm, LANES, tm), BF16),
            jax.ShapeDtypeStruct((N_HEADS, n // tm, VT_ROWS, tm), BF16),
            jax.ShapeDtypeStruct((depth, 2 * N_HEADS, HEAD_DIM, sp), F32),
            jax.ShapeDtypeStruct((depth, n_dec, W_D), F32),
            jax.ShapeDtypeStruct((depth, sp * N_HEADS, LANES), F32),
            jax.ShapeDtypeStruct((depth, n_dec, W_D), F32),
        ],
        scratch_shapes=[pltpu.VMEM((tm, 3 * W_D), F32)],
        input_output_aliases={n_in + j: 6 + j for j in range(len(aliased))},
        compiler_params=pltpu.CompilerParams(
            dimension_semantics=("arbitrary",), vmem_limit_bytes=44 * MIB),
        name="inproj",
    )(x, g, w_in, qg, kg, cos_t, sin_t, ones_bd, *aliased)


SAMPLE_MIX_GROUP = 8
MIX_ROWS = 128
SUBLANES = 8


def _shifted_rows(ref, base, rows, depth, tmp):
    out = {}
    for r in range(SUBLANES):
        js = [j for j in range(1, depth + 1) if (-j) % SUBLANES == r]
        if not js:
            continue
        start, length = base - max(js), max(js) - min(js) + rows
        tmp[r, 0:length, :] = ref[start:start + length, :]
        for j in js:
            out[j] = tmp[r, max(js) - j:max(js) - j + rows, :]
    return out


def _mixer_compute(t, pos0, z_ref, pw_ref, ps_ref, sw_ref, cw_ref, cb_ref, lg_ref, lb_ref,
                   y_ref, pool_o, sconv_o, cconv_o, ea, eb, ec, tmp):
    u = z_ref[:, 0:W_A]
    ea[HALO:HALO + t, :] = u
    eb[HALO:HALO + t, :] = z_ref[:, W_A + W_B:W_A + 2 * W_B] * z_ref[:, W_A + 2 * W_B:W_A + 3 * W_B]
    zc = z_ref[:, W_A + 3 * W_B:W_A + 3 * W_B + W_C]
    ec[HALO:HALO + t, :] = zc * jax.nn.sigmoid(z_ref[:, W_A + 3 * W_B + W_C:W_ABC])

    rows = min(t, MIX_ROWS)
    for r0 in range(0, t, rows):
        base = HALO + r0
        lane = lax.broadcasted_iota(jnp.int32, (rows, W_A), 1)
        cur = ea[base:base + rows, :]
        back = _shifted_rows(ea, base, rows, max(POOL_WINDOWS) - 1, tmp)
        acc = cur
        sums = {}
        for j in range(1, max(POOL_WINDOWS)):
            acc = acc + back[j]
            if j + 1 in POOL_WINDOWS:
                sums[j + 1] = acc
        tot = sums[POOL_WINDOWS[-1]]
        win = jnp.full((rows, W_A), float(POOL_WINDOWS[-1]), F32)
        for gi in range(len(POOL_WINDOWS) - 2, -1, -1):
            in_group = lane < (gi + 1) * GA
            tot = jnp.where(in_group, sums[POOL_WINDOWS[gi]], tot)
            win = jnp.where(in_group, float(POOL_WINDOWS[gi]), win)
        if pos0 is None:
            cnt = win
        else:
            pos1 = (pos0 + r0 + 1 + lax.broadcasted_iota(jnp.int32, (rows, W_A), 0)).astype(F32)
            cnt = jnp.minimum(pos1, win)
        d = (tot / cnt - cur).astype(BF16)
        y_ref[r0:r0 + rows, 0:W_A] = _dot(d, pw_ref[...]) * ps_ref[...]

        conv = sw_ref[SCONV_K - 1:SCONV_K, :] * eb[base:base + rows, :]
        for j in range(SCONV_K - 1):
            off = base - (SCONV_K - 1) + j
            conv = conv + sw_ref[j:j + 1, :] * eb[off:off + rows, :]
        y_ref[r0:r0 + rows, W_A:W_A + W_B] = z_ref[r0:r0 + rows, W_A:W_A + W_B] * conv

        back = _shifted_rows(ec, base, rows, CCONV_K - 1, tmp)
        conv = cw_ref[CCONV_K - 1:CCONV_K, :] * ec[base:base + rows, :]
        for j in range(CCONV_K - 1):
            conv = conv + cw_ref[j:j + 1, :] * back[CCONV_K - 1 - j]
        conv = conv + cb_ref[...]
        mu = jnp.mean(conv, axis=-1, keepdims=True)
        cen = conv - mu
        var = jnp.mean(cen * cen, axis=-1, keepdims=True)
        ln = cen * lax.rsqrt(var + EPS) * lg_ref[...] + lb_ref[...]
        y_ref[r0:r0 + rows, W_A + W_B:W_Y] = ln * jax.nn.sigmoid(ln)

    pool_o[...] = ea[HALO + t - 16:HALO + t, :]
    sconv_o[...] = eb[HALO + t - 8:HALO + t, :]
    cconv_o[...] = ec[HALO + t - 32:HALO + t, :]


def _mixer_prompt_kernel(z_ref, pw_ref, ps_ref, sw_ref, cw_ref, cb_ref, lg_ref, lb_ref,
                         y_ref, pool_o, sconv_o, cconv_o, ea, eb, ec, tmp, *, t):
    i = pl.program_id(0)

    @pl.when(i == 0)
    def _():
        zeros = jnp.zeros((HALO, W_A), F32)
        ea[0:HALO, :] = zeros
        eb[0:HALO, :] = zeros
        ec[0:HALO, :] = zeros

    @pl.when(i > 0)
    def _():
        ea[0:HALO, :] = ea[t:t + HALO, :]
        eb[0:HALO, :] = eb[t:t + HALO, :]
        ec[0:HALO, :] = ec[t:t + HALO, :]

    _mixer_compute(t, i * t, z_ref, pw_ref, ps_ref, sw_ref, cw_ref, cb_ref, lg_ref, lb_ref,
                   y_ref, pool_o, sconv_o, cconv_o, ea, eb, ec, tmp)


def _mixer_sample_kernel(z_ref, sp_ref, ss_ref, sc_ref, pw_ref, ps_ref, sw_ref, cw_ref, cb_ref,
                         lg_ref, lb_ref, y_in_ref, y_ref, pool_o, sconv_o, cconv_o, ea, eb, ec, tmp, *, t, group):
    del y_in_ref
    for s in range(group):
        rows = pl.ds(s * t, t)
        ea[0:HALO, :] = sp_ref[s]
        eb[0:HALO, :] = ss_ref[s]
        ec[0:HALO, :] = sc_ref[s]
        _mixer_compute(t, None, z_ref.at[rows], pw_ref, ps_ref, sw_ref, cw_ref, cb_ref, lg_ref, lb_ref,
                       y_ref.at[rows], pool_o.at[s], sconv_o.at[s], cconv_o.at[s], ea, eb, ec, tmp)


def _mixer_weight_specs():
    return [
        _const_spec((W_A, W_A), (0, 0)),
        _const_spec((1, W_A), (0, 0)),
        _const_spec((SCONV_K, W_B), (0, 0)),
        _const_spec((CCONV_K, W_C), (0, 0)),
        _const_spec((1, W_C), (0, 0)),
        _const_spec((1, W_C), (0, 0)),
        _const_spec((1, W_C), (0, 0)),
    ]


def _mixer_scratch(t):
    return [pltpu.VMEM((HALO + t, W_A), F32), pltpu.VMEM((HALO + t, W_B), F32),
            pltpu.VMEM((HALO + t, W_C), F32),
            pltpu.VMEM((SUBLANES, min(t, MIX_ROWS) + HALO, W_C), F32)]


def _mixer_prompt(cfg, zabc, weights):
    t = cfg.tmix
    const_out = lambda r: pl.BlockSpec((r, W_A), lambda i: (0, 0))
    return pl.pallas_call(
        functools.partial(_mixer_prompt_kernel, t=t),
        grid=(cfg.s_prompt // t,),
        in_specs=[pl.BlockSpec((t, W_ABC), lambda i: (i, 0))] + _mixer_weight_specs(),
        out_specs=[pl.BlockSpec((t, W_Y), lambda i: (i, 0)), const_out(16), const_out(8), const_out(32)],
        out_shape=[
            jax.ShapeDtypeStruct((cfg.n_tok, W_Y), F32),
            jax.ShapeDtypeStruct((16, W_A), F32),
            jax.ShapeDtypeStruct((8, W_B), F32),
            jax.ShapeDtypeStruct((32, W_C), F32),
        ],
        scratch_shapes=_mixer_scratch(t),
        compiler_params=pltpu.CompilerParams(dimension_semantics=("arbitrary",)),
        name="mixer_prompt",
    )(zabc, *weights)


def _mixer_sample(cfg, zabc, st_pool, st_sconv, st_cconv, weights, y_abc):
    t, nb = cfg.t_dec, cfg.n_dec
    group = math.gcd(nb, SAMPLE_MIX_GROUP)
    assert cfg.s_prompt % (group * t) == 0
    row0 = cfg.s_prompt // (group * t)
    state_spec = lambda: pl.BlockSpec((group, HALO, W_A), lambda b: (b, 0, 0))
    out_state = lambda r: pl.BlockSpec((group, r, W_A), lambda b: (b, 0, 0))
    return pl.pallas_call(
        functools.partial(_mixer_sample_kernel, t=t, group=group),
        grid=(nb // group,),
        in_specs=[pl.BlockSpec((group * t, W_ABC), lambda b: (row0 + b, 0)),
                  state_spec(), state_spec(), state_spec()]
                 + _mixer_weight_specs()
                 + [pl.BlockSpec(memory_space=pl.ANY)],
        out_specs=[pl.BlockSpec((group * t, W_Y), lambda b: (row0 + b, 0)),
                   out_state(16), out_state(8), out_state(32)],
        out_shape=[
            jax.ShapeDtypeStruct((cfg.n_tok, W_Y), F32),
            jax.ShapeDtypeStruct((nb, 16, W_A), F32),
            jax.ShapeDtypeStruct((nb, 8, W_B), F32),
            jax.ShapeDtypeStruct((nb, 32, W_C), F32),
        ],
        scratch_shapes=_mixer_scratch(t),
        input_output_aliases={11: 0},
        compiler_params=pltpu.CompilerParams(dimension_semantics=("arbitrary",)),
        name="mixer_sample",
    )(zabc, st_pool, st_sconv, st_cconv, *weights, y_abc)


def _lambda(lq1, lk1, lq2, lk2, lam_init):
    s1 = jnp.sum(lq1[...] * lk1[...], axis=-1, keepdims=True)
    s2 = jnp.sum(lq2[...] * lk2[...], axis=-1, keepdims=True)
    return jnp.exp(s1) - jnp.exp(s2) + lam_init


def _diff_out(acc, l, lam, sg, lam_init, t):
    o = acc[0:t] / l[0:t] - lam * (acc[t:2 * t] / l[t:2 * t])
    return _rms(o, sg) * (1.0 - lam_init)


def _attn_prompt_kernel(qt_ref, qtn_ref, k_ref, vt_ref, lq1, lk1, lq2, lk2, sgc_ref, o_ref,
                        q2t_ref, q2tn_ref, s0_ref, s1_ref, mx0_ref, mx1_ref, m_ref, acc_ref, *, tk, lam_init):
    qi = pl.program_id(1)
    nq = 2 * tk
    chan = lax.broadcasted_iota(jnp.int32, (LANES, tk), 0)
    zero = jnp.zeros((LANES, tk), BF16)
    for src_ref, dst_ref in ((qt_ref, q2t_ref), (qtn_ref, q2tn_ref)):
        for half in range(2):
            qt = src_ref[half]
            dst_ref[:, half * tk:(half + 1) * tk] = jnp.where(chan < HEAD_DIM, qt, zero)
            dst_ref[:, nq + half * tk:nq + (half + 1) * tk] = jnp.where(chan >= HEAD_DIM, qt, zero)
    m_ref[...] = jnp.full(m_ref.shape, -jnp.inf, F32)
    acc_ref[...] = jnp.zeros(acc_ref.shape, F32)

    ncols = 2 * nq // COL_GROUPS

    def scores(j, s_ref, mx_ref, diagonal=None, group=None, queries=q2t_ref):
        cols = slice(None) if group is None else slice(group * ncols, (group + 1) * ncols)
        kb = k_ref[pl.ds(pl.multiple_of(j * tk, tk), tk), :]
        s = _dot(kb, queries[:, cols])
        if diagonal is not None:
            key = lax.broadcasted_iota(jnp.int32, s.shape, 0)
            col = lax.broadcasted_iota(jnp.int32, s.shape, 1) + (0 if group is None else group * ncols)
            key_chunk = diagonal * (tk // CHUNK) + key // CHUNK
            s = jnp.where(key_chunk <= (col & (nq - 1)) // CHUNK, s, -jnp.inf)
        s_ref[:, cols] = s
        mx_ref[:, cols] = jnp.max(s, axis=0, keepdims=True)

    def consume(j, s_ref, mx_ref, group=None):
        cols = slice(None) if group is None else slice(group * ncols, (group + 1) * ncols)
        m_prev = m_ref[:, cols]
        m_new = jnp.maximum(m_prev, mx_ref[:, cols])
        alpha = jnp.exp2(m_prev - m_new)
        p = jnp.exp2(s_ref[:, cols] - m_new).astype(BF16)
        acc_ref[:, cols] = alpha * acc_ref[:, cols] + _dot(vt_ref[j], p)
        m_ref[:, cols] = m_new

    def pair(i, first_diagonal):
        j = 2 * i
        for g in range(COL_GROUPS):
            scores(j + 1, s1_ref, mx1_ref, None, g)
            consume(j, s0_ref, mx0_ref, g)
        for g in range(COL_GROUPS):
            scores(j + 2, s0_ref, mx0_ref, first_diagonal, g)
            consume(j + 1, s1_ref, mx1_ref, g)

    n_plain = jnp.maximum(qi - 1, 0)

    def run_pairs(first, count):
        for n in range(count):
            pair(first + n, None)

    def trip(i, carry):
        run_pairs(PAIRS_PER_TRIP * i, PAIRS_PER_TRIP)
        return carry

    lax.fori_loop(0, n_plain // PAIRS_PER_TRIP, trip, 0)
    done = (n_plain // PAIRS_PER_TRIP) * PAIRS_PER_TRIP
    size = PAIRS_PER_TRIP // 2
    while size >= 1:
        take = ((n_plain - done) // size) % 2 == 1
        pl.when(take)(functools.partial(run_pairs, done, size))
        done = done + jnp.where(take, size, 0)
        size //= 2

    def own_blocks():
        assert tk % ncols == 0
        late = [g for g in range(COL_GROUPS) if (g * ncols) % nq >= tk]
        for g in range(COL_GROUPS):
            if g in late:
                scores(2 * qi + 1, s1_ref, mx1_ref, 1, g)
            consume(2 * qi, s0_ref, mx0_ref, g)
            scores(0, s0_ref, mx0_ref, None, g, q2tn_ref)
        for g in late:
            consume(2 * qi + 1, s1_ref, mx1_ref, g)

    @pl.when(qi > 0)
    def _():
        pair(qi - 1, 0)
        own_blocks()

    @pl.when(qi == 0)
    def _():
        scores(0, s0_ref, mx0_ref, 0)
        own_blocks()

    lam = _lambda(lq1, lk1, lq2, lk2, lam_init)
    acc = acc_ref[0:LANES, :]
    l = acc_ref[LANES:LANES + 1, :]
    ot = acc[:, 0:nq] * (1.0 / l[:, 0:nq]) - acc[:, nq:2 * nq] * (lam / l[:, nq:2 * nq])
    ms = jnp.mean(ot * ot, axis=0, keepdims=True)
    o_ref[...] = ot * lax.rsqrt(ms + EPS) * sgc_ref[...] * (1.0 - lam_init)


def _lam_specs():
    return [_const_spec((1, HEAD_DIM), (0, 0)) for _ in range(4)] + [_const_spec((1, LANES), (0, 0))]


def _attn_prompt(cfg, qt, kb, vt, lam_w, lam_init):
    s, tk = cfg.s_prompt, cfg.tm
    tq = 2 * tk
    assert tq & (tq - 1) == 0 and tk % CHUNK == 0 and s % tq == 0
    return pl.pallas_call(
        functools.partial(_attn_prompt_kernel, tk=tk, lam_init=lam_init),
        grid=(N_HEADS, s // tq),
        in_specs=[
            pl.BlockSpec((None, 2, LANES, tk), lambda h, i: (h, i, 0, 0)),
            pl.BlockSpec((None, 2, LANES, tk), lambda h, i: (h, jnp.minimum(i + 1, s // tq - 1), 0, 0)),
            pl.BlockSpec((s, LANES), lambda h, i: (0, h)),
            pl.BlockSpec((None, s // tk, VT_ROWS, tk), lambda h, i: (h, 0, 0, 0)),
        ] + _lam_specs()[:4] + [_const_spec((LANES, 1), (0, 0))],
        out_specs=pl.BlockSpec((LANES, tq), lambda h, i: (h, i)),
        out_shape=jax.ShapeDtypeStruct((W_D, s), F32),
        scratch_shapes=[
            pltpu.VMEM((LANES, 2 * tq), BF16),
            pltpu.VMEM((LANES, 2 * tq), BF16),
            pltpu.VMEM((tk, 2 * tq), F32),
            pltpu.VMEM((tk, 2 * tq), F32),
            pltpu.VMEM((1, 2 * tq), F32),
            pltpu.VMEM((1, 2 * tq), F32),
            pltpu.VMEM((1, 2 * tq), F32),
            pltpu.VMEM((VT_ROWS, 2 * tq), F32),
        ],
        compiler_params=pltpu.CompilerParams(
            dimension_semantics=("arbitrary", "arbitrary"), vmem_limit_bytes=52 * MIB),
        name="attn_prompt",
    )(qt, qt, kb, vt, *lam_w[:4], lam_w[4].reshape(LANES, 1))


def _attn_sample_kernel(q_ref, kn_ref, vn_ref, kc_ref, vc_ref, lq1, lk1, lq2, lk2, sg_ref,
                        o_ref, *, t, lam_init):
    lam = _lambda(lq1, lk1, lq2, lk2, lam_init)
    past = vc_ref.shape[0] // N_HEADS
    for h in range(N_HEADS):
        sl = slice(h * LANES, (h + 1) * LANES)
        s_past, s_new = [], []
        for c in range(2):
            ch = slice(h * LANES + c * HEAD_DIM, h * LANES + (c + 1) * HEAD_DIM)
            q = q_ref[:, ch]
            s_past.append(_dot(q, kc_ref[2 * h + c].astype(BF16)))
            s_new.append(_dot_t(q, kn_ref[:, ch]))
        s_past = jnp.concatenate(s_past, axis=0)
        s_new = jnp.concatenate(s_new, axis=0)
        v_past = vc_ref[pl.ds(h, past, stride=N_HEADS), :]
        m = jnp.maximum(jnp.max(s_past, axis=-1, keepdims=True), jnp.max(s_new, axis=-1, keepdims=True))
        p_past = jnp.exp(s_past - m)
        p_new = jnp.exp(s_new - m)
        l = jnp.sum(p_past, axis=-1, keepdims=True) + jnp.sum(p_new, axis=-1, keepdims=True)
        acc = _dot(p_past.astype(BF16), v_past.astype(BF16)) + _dot(p_new.astype(BF16), vn_ref[:, sl])
        o_ref[:, sl] = _diff_out(acc, l, lam, sg_ref[...], lam_init, t)


def _attn_sample(cfg, l, qb, kb, vb, cache_k, cache_v, lam_w, lam_init):
    t, nb = cfg.t_dec, cfg.n_dec
    row0 = cfg.s_prompt // t
    new_rows = lambda: pl.BlockSpec((t, W_D), lambda b: (row0 + b, 0))
    cache = lambda a: pl.BlockSpec((None, None) + a.shape[2:], lambda b: (l, b) + (0,) * (a.ndim - 2))
    return pl.pallas_call(
        functools.partial(_attn_sample_kernel, t=t, lam_init=lam_init),
        grid=(nb,),
        in_specs=[new_rows(), new_rows(), new_rows(), cache(cache_k), cache(cache_v)] + _lam_specs(),
        out_specs=pl.BlockSpec((t, W_D), lambda b: (b, 0)),
        out_shape=jax.ShapeDtypeStruct((nb * t, W_D), F32),
        compiler_params=pltpu.CompilerParams(
            dimension_semantics=("arbitrary",), vmem_limit_bytes=40 * MIB),
        name="attn_sample",
    )(qb, kb, vb, cache_k, cache_v, *lam_w)


def _merge_kernel(x_ref, yabc_ref, ydt_ref, yds_ref, g_ref, wg_ref, wpa_ref, wpb_ref, wpc_ref, wpd_ref, wo_ref,
                  o_ref, *, n_prompt_tiles):
    is_prompt = pl.program_id(0) < n_prompt_tiles
    half = x_ref.shape[0] // 2
    for rows in (slice(0, half), slice(half, 2 * half)):
        yd = jnp.where(is_prompt, ydt_ref[:, rows].T, yds_ref[rows, :])
        x = x_ref[rows, :]
        h = _rms(x, g_ref[...]).astype(BF16)
        branches = (
            (yabc_ref[rows, 0:W_A], wpa_ref),
            (yabc_ref[rows, W_A:W_A + W_B], wpb_ref),
            (yabc_ref[rows, W_A + W_B:W_Y], wpc_ref),
            (yd, wpd_ref),
        )
        merged = None
        for i, (y, wp_ref) in enumerate(branches):
            gate = jax.nn.sigmoid(_dot(h, wg_ref[:, i * D_MODEL:(i + 1) * D_MODEL]))
            term = gate * _dot(y.astype(BF16), wp_ref[...])
            merged = term if merged is None else merged + term
        o_ref[rows, :] = x + _dot(merged.astype(BF16), wo_ref[...])


def _merge(cfg, l, x, y_abc, yd_t, yd_s, g, w_gate, wpa, wpb, wpc, wpd, wo):
    n, tm, sp = cfg.n_tok, cfg.tm, cfg.s_prompt
    assert sp % tm == 0 and (n - sp) % tm == 0
    npt = sp // tm
    row = lambda w: pl.BlockSpec((tm, w), lambda i: (i, 0))
    return pl.pallas_call(
        functools.partial(_merge_kernel, n_prompt_tiles=npt),
        grid=(n // tm,),
        in_specs=[
            row(D_MODEL), row(W_Y),
            pl.BlockSpec((W_D, tm), lambda i: (0, jnp.minimum(i, npt - 1))),
            pl.BlockSpec((tm, W_D), lambda i: (jnp.maximum(i - npt, 0), 0)),
            _const_spec((1, D_MODEL), (0, 0)),
            _const_spec((None, D_MODEL, 4 * D_MODEL), (l, 0, 0)),
            _const_spec((None, W_A, D_MODEL), (l, 0, 0)),
            _const_spec((None, W_B, D_MODEL), (l, 0, 0)),
            _const_spec((None, W_C, D_MODEL), (l, 0, 0)),
            _const_spec((None, W_D, D_MODEL), (l, 0, 0)),
            _const_spec((None, D_MODEL, D_MODEL), (l, 0, 0)),
        ],
        out_specs=row(D_MODEL),
        out_shape=jax.ShapeDtypeStruct((n, D_MODEL), F32),
        compiler_params=pltpu.CompilerParams(
            dimension_semantics=("arbitrary",), vmem_limit_bytes=48 * MIB),
        name="merge",
    )(x, y_abc, yd_t, yd_s, g, w_gate, wpa, wpb, wpc, wpd, wo)


def _rope_tables(cfg):
    half = HEAD_DIM // 2
    inv_freq = ROPE_THETA ** (-jnp.arange(half, dtype=F32) / half)
    pos = jnp.concatenate([jnp.arange(cfg.s_prompt), jnp.tile(cfg.past + jnp.arange(cfg.t_dec), cfg.n_dec)])
    ang = pos.astype(F32)[:, None] * inv_freq[None, :]
    cos, sin = jnp.cos(ang), jnp.sin(ang)
    reps = LANES // HEAD_DIM
    cos_t = jnp.tile(jnp.concatenate([cos, cos], axis=1), (1, reps))
    sin_t = jnp.tile(jnp.concatenate([-sin, sin], axis=1), (1, reps))
    return cos_t, sin_t


def _pad_rows_top(a, rows):
    return jnp.pad(a, ((0, 0), (0, 0), (rows - a.shape[2], 0), (0, 0)))


def _forward(cfg, x_prompt, x_sample, cache_k, cache_v, state_pool, state_sconv, state_cconv,
             g_ffn1, w1_gate, w1_up, w1_down, g_mix, w_in, pool_w, pool_scale, sconv_w,
             cconv_w, cconv_b, ln_g, ln_b, q_norm_g, k_norm_g, lam_q1, lam_k1, lam_q2, lam_k2,
             subln_g, wp_a, wp_b, wp_c, wp_d, w_out, g_ffn2, w2_gate, w2_up, w2_down):
    depth = w_in.shape[0]
    sp, nb, td = cfg.s_prompt, cfg.n_dec, cfg.t_dec
    x = (x_prompt.reshape(sp, D_MODEL), x_sample.reshape(nb * td, D_MODEL))

    bf = lambda w: w.astype(BF16)
    w1g, w1u, w1d = bf(w1_gate), bf(w1_up), bf(w1_down)
    w2g, w2u, w2d = bf(w2_gate), bf(w2_up), bf(w2_down)
    w_proj, w_gate = bf(w_in[:, :, :W_PROJ]), bf(w_in[:, :, W_PROJ:])
    wpa, wpb, wpc, wpd, wo = bf(wp_a), bf(wp_b), bf(wp_c), bf(wp_d), bf(w_out)
    cos_t, sin_t = _rope_tables(cfg)
    ones_bd = jnp.kron(jnp.eye(NORM_W // HEAD_DIM, dtype=F32), jnp.ones((HEAD_DIM, HEAD_DIM), F32)).astype(BF16)
    eye_g = jnp.eye(len(POOL_WINDOWS), dtype=F32)
    ck = jnp.transpose(cache_k, (0, 1, 3, 4, 2))
    cv = cache_v.reshape(depth, nb, cfg.past * N_HEADS, 2 * HEAD_DIM)
    st_pool = _pad_rows_top(state_pool, HALO)
    st_sconv = _pad_rows_top(state_sconv, HALO)
    st_cconv = _pad_rows_top(state_cconv, HALO)
    row = lambda a: a.reshape(1, -1)

    outs = [[] for _ in range(6)]
    caches = None
    for l in range(depth):
        lam_init = 0.8 - 0.6 * math.exp(-0.3 * l)
        x = _ffn(cfg, l, x, row(g_ffn1[l]), w1g, w1u, w1d)
        qg = row(jnp.tile(q_norm_g[l], NORM_W // HEAD_DIM))
        kg = row(jnp.tile(k_norm_g[l], NORM_W // HEAD_DIM))
        zabc, qb, kb, vb, qt, vt, *caches = _inproj(cfg, l, depth, x, row(g_mix[l]), w_proj, qg, kg, cos_t, sin_t,
                                                    ones_bd, caches)

        pw_bd = (eye_g[:, None, :, None] * pool_w[l][:, :, None, :]).reshape(W_A, W_A).astype(BF16)
        mix_w = (pw_bd, row(pool_scale[l]), sconv_w[l], cconv_w[l], row(cconv_b[l]), row(ln_g[l]), row(ln_b[l]))
        y_abc, pool_p, sconv_p, cconv_p = _mixer_prompt(cfg, zabc, mix_w)
        y_abc, pool_s, sconv_s, cconv_s = _mixer_sample(cfg, zabc, st_pool[l], st_sconv[l], st_cconv[l],
                                                        mix_w, y_abc)

        lam_w = (row(lam_q1[l]), row(lam_k1[l]), row(lam_q2[l]), row(lam_k2[l]), row(subln_g[l]))
        yd_t = _attn_prompt(cfg, qt, kb, vt, lam_w, lam_init)
        yd_s = _attn_sample(cfg, l, qb, kb, vb, ck, cv, lam_w, lam_init)

        x = _merge(cfg, l, x, y_abc, yd_t, yd_s, row(g_mix[l]), w_gate, wpa, wpb, wpc, wpd, wo)
        x = _ffn(cfg, l, x, row(g_ffn2[l]), w2g, w2u, w2d, split_out=(l == depth - 1))

        outs[0].append(pool_p[None, 16 - POOL_STATE:])
        outs[1].append(sconv_p[None, 8 - (SCONV_K - 1):])
        outs[2].append(cconv_p[None, 32 - (CCONV_K - 1):])
        outs[3].append(pool_s[:, 16 - POOL_STATE:])
        outs[4].append(sconv_s[:, 8 - (SCONV_K - 1):])
        outs[5].append(cconv_s[:, 32 - (CCONV_K - 1):])

    kt_all, ks_all, vp_all, vs_all = caches
    y_prompt = x[0].reshape(1, sp, D_MODEL)
    y_sample = x[1].reshape(nb, td, D_MODEL)
    k_prompt = jnp.transpose(kt_all.reshape(depth, 1, 2 * N_HEADS, HEAD_DIM, sp), (0, 1, 4, 2, 3))
    v_prompt = vp_all.reshape(depth, 1, sp, N_HEADS, 2 * HEAD_DIM)
    k_sample = ks_all.reshape(depth, nb, td, 2 * N_HEADS, HEAD_DIM)
    v_sample = vs_all.reshape(depth, nb, td, N_HEADS, 2 * HEAD_DIM)
    st = [jnp.stack(o) for o in outs]
    return (y_prompt, y_sample, k_prompt, v_prompt, st[0], st[1], st[2], k_sample, v_sample, st[3], st[4], st[5])


def kernel(x_prompt, x_sample, cache_k, cache_v, state_pool, state_sconv, state_cconv, g_ffn1, w1_gate, w1_up, w1_down, g_mix, w_in, pool_w, pool_scale, sconv_w, cconv_w, cconv_b, ln_g, ln_b, q_norm_g, k_norm_g, lam_q1, lam_k1, lam_q2, lam_k2, subln_g, wp_a, wp_b, wp_c, wp_d, w_out, g_ffn2, w2_gate, w2_up, w2_down):
    assert x_prompt.shape[0] == 1
    cfg = Cfg(s_prompt=x_prompt.shape[1], n_dec=x_sample.shape[0], t_dec=x_sample.shape[1],
              past=cache_k.shape[2], tm=512, tmix=512)
    return _forward(cfg, x_prompt, x_sample, cache_k, cache_v, state_pool, state_sconv, state_cconv,
                    g_ffn1, w1_gate, w1_up, w1_down, g_mix, w_in, pool_w, pool_scale, sconv_w,
                    cconv_w, cconv_b, ln_g, ln_b, q_norm_g, k_norm_g, lam_q1, lam_k1, lam_q2, lam_k2,
                    subln_g, wp_a, wp_b, wp_c, wp_d, w_out, g_ffn2, w2_gate, w2_up, w2_down)
```

```python
import functools
import math
from typing import NamedTuple

import jax
import jax.numpy as jnp
from jax import lax
from jax.experimental import pallas as pl
from jax.experimental.pallas import tpu as pltpu

F32 = jnp.float32
BF16 = jnp.bfloat16

D_MODEL = 1024
DEPTH = 4
CHUNK = 64
POOL_WINDOWS = (2, 4, 8, 16)
W_A = 256
GA = 64
POOL_STATE = 15
W_B = 256
SCONV_K = 3
W_C = 256
CCONV_K = 31
HEAD_DIM = 64
N_HEADS = 4
W_D = 512
ROPE_THETA = 10000.0
D_FF = 2816
EPS = 1e-6
W_ABC = W_A + 3 * W_B + 2 * W_C
W_PROJ = W_ABC + 3 * W_D
W_Y = W_A + W_B + W_C

LANES = 128
NORM_W = 256
PAIRS_PER_TRIP = 4
COL_GROUPS = 4
VT_ROWS = LANES + 16
LOG2E = 1.4426950408889634
HALO = 32
MIB = 1024 * 1024


class Cfg(NamedTuple):
    s_prompt: int
    n_dec: int
    t_dec: int
    past: int
    tm: int
    tmix: int

    @property
    def n_tok(self):
        return self.s_prompt + self.n_dec * self.t_dec


def _const_spec(shape, index):
    return pl.BlockSpec(shape, lambda *_: index, pipeline_mode=pl.Buffered(1))


def _rms(x, g):
    ms = jnp.mean(x * x, axis=-1, keepdims=True)
    return x * lax.rsqrt(ms + EPS) * g


def _dot(a, b):
    return jnp.dot(a, b, preferred_element_type=F32)


def _dot_t(a, b):
    return lax.dot_general(a, b, (((1,), (1,)), ((), ())), preferred_element_type=F32)


def _ffn_kernel(*refs, n_prompt_tiles, split_in, split_out):
    refs = list(refs)
    is_prompt = pl.program_id(0) < n_prompt_tiles
    if split_in:
        xp_ref, xs_ref = refs[0:2]
        del refs[0:2]
        x = jnp.where(is_prompt, xp_ref[...], xs_ref[...])
    else:
        x = refs.pop(0)[...]
    g_ref, wg_ref, wu_ref, wd_ref = refs[0:4]
    halves = []
    for xh in (x[0:x.shape[0] // 2], x[x.shape[0] // 2:]):
        h = _rms(xh, g_ref[...]).astype(BF16)
        a = _dot(h, wg_ref[...])
        u = _dot(h, wu_ref[...])
        act = (a * jax.nn.sigmoid(a) * u).astype(BF16)
        halves.append(0.5 * _dot(act, wd_ref[...]))
    half_step = jnp.concatenate(halves, axis=0)
    if split_out:
        op_ref, os_ref = refs[4:6]

        @pl.when(is_prompt)
        def _():
            op_ref[...] = x + half_step

        @pl.when(jnp.logical_not(is_prompt))
        def _():
            os_ref[...] = x + half_step
    else:
        refs[4][...] = x + half_step


def _ffn(cfg, l, x, g, wg, wu, wd, split_out=False):
    n, tm, sp = cfg.n_tok, cfg.tm, cfg.s_prompt
    assert sp % tm == 0 and (n - sp) % tm == 0
    npt = sp // tm
    split_in = isinstance(x, tuple)
    rows = pl.BlockSpec((tm, D_MODEL), lambda i: (i, 0))
    prompt_rows = pl.BlockSpec((tm, D_MODEL), lambda i: (jnp.minimum(i, npt - 1), 0))
    sample_rows = pl.BlockSpec((tm, D_MODEL), lambda i: (jnp.maximum(i - npt, 0), 0))
    whole = jax.ShapeDtypeStruct((n, D_MODEL), F32)
    halves = [jax.ShapeDtypeStruct((sp, D_MODEL), F32), jax.ShapeDtypeStruct((n - sp, D_MODEL), F32)]
    return pl.pallas_call(
        functools.partial(_ffn_kernel, n_prompt_tiles=npt, split_in=split_in, split_out=split_out),
        grid=(n // tm,),
        in_specs=([prompt_rows, sample_rows] if split_in else [rows]) + [
            _const_spec((1, D_MODEL), (0, 0)),
            _const_spec((None, D_MODEL, D_FF), (l, 0, 0)),
            _const_spec((None, D_MODEL, D_FF), (l, 0, 0)),
            _const_spec((None, D_FF, D_MODEL), (l, 0, 0)),
        ],
        out_specs=[prompt_rows, sample_rows] if split_out else rows,
        out_shape=halves if split_out else whole,
        compiler_params=pltpu.CompilerParams(
            dimension_semantics=("arbitrary",), vmem_limit_bytes=52 * MIB),
        name="ffn",
    )(*(x if split_in else (x,)), g, wg, wu, wd)


def _inproj_kernel(x_ref, g_ref, w_ref, qg_ref, kg_ref, cos_ref, sin_ref, ones_ref, *rest, n_prompt_tiles):
    zabc_ref, qb_ref, kb_ref, vb_ref, qt_ref, vt_ref, kt_ref, ks_ref, vp_ref, vs_ref, zqkv_ref = rest[-11:]
    is_prompt = pl.program_id(0) < n_prompt_tiles
    h = _rms(x_ref[...], g_ref[...]).astype(BF16)
    assert W_ABC % (2 * NORM_W) == 0 and W_ABC // (2 * NORM_W) <= W_D // LANES
    cos = cos_ref[...]
    sin = sin_ref[...]
    ones = ones_ref[...]
    lane = lax.broadcasted_iota(jnp.int32, cos.shape, 1)
    first_half = (lane & (HEAD_DIM // 2)) == 0

    def head_norm(z, g):
        ss = z * z
        hi = ss.astype(BF16)
        lo = (ss - hi.astype(F32)).astype(BF16)
        tot = _dot(hi, ones) + _dot(lo, ones)
        return z * lax.rsqrt(tot * (1.0 / HEAD_DIM) + EPS) * g

    def rope(y):
        half = HEAD_DIM // 2
        partner = jnp.where(first_half, pltpu.roll(y, LANES - half, 1), pltpu.roll(y, half, 1))
        return y * cos + partner * sin

    nw = ones.shape[0]
    abc_w = 2 * NORM_W
    n_chunks = W_D // LANES
    for g0 in range(0, W_D, nw):
        for part in range(3):
            lo = part * W_D + g0
            zqkv_ref[:, lo:lo + nw] = _dot(h, w_ref[:, W_ABC + lo:W_ABC + lo + nw])
    yq = [head_norm(zqkv_ref[:, g0:g0 + nw], qg_ref[...]) for g0 in range(0, W_D, nw)]
    yk = [head_norm(zqkv_ref[:, W_D + g0:W_D + g0 + nw], kg_ref[...]) for g0 in range(0, W_D, nw)]
    for c in range(n_chunks):
        sl = slice(c * LANES, (c + 1) * LANES)
        grp, off = divmod(c * LANES, nw)
        q = rope(yq[grp][:, off:off + LANES]) * (HEAD_DIM ** -0.5)
        qb_ref[:, sl] = q.astype(BF16)
        qt_ref[c] = (q * LOG2E).T.astype(BF16)
        k = rope(yk[grp][:, off:off + LANES])
        kb_ref[:, sl] = k.astype(BF16)
        zqkv_ref[:, W_D + c * LANES:W_D + (c + 1) * LANES] = k
        v = zqkv_ref[:, 2 * W_D + c * LANES:2 * W_D + (c + 1) * LANES]
        vb_ref[:, sl] = v.astype(BF16)
        vt_ref[c, 0:LANES, :] = v.T.astype(BF16)
        vt_ref[c, LANES:VT_ROWS, :] = jnp.ones((VT_ROWS - LANES, v.shape[0]), BF16)
        if c * abc_w < W_ABC:
            zabc_ref[:, c * abc_w:(c + 1) * abc_w] = _dot(h, w_ref[:, c * abc_w:(c + 1) * abc_w])

    @pl.when(is_prompt)
    def _():
        for c in range(W_D // LANES):
            kt = zqkv_ref[:, W_D + c * LANES:W_D + (c + 1) * LANES].T
            kt_ref[2 * c] = kt[0:HEAD_DIM]
            kt_ref[2 * c + 1] = kt[HEAD_DIM:LANES]
        for hd in range(N_HEADS):
            vp_ref[pl.ds(hd, kt_ref.shape[-1], stride=N_HEADS), :] = (
                zqkv_ref[:, 2 * W_D + hd * LANES:2 * W_D + (hd + 1) * LANES])

    @pl.when(jnp.logical_not(is_prompt))
    def _():
        ks_ref[...] = zqkv_ref[:, W_D:2 * W_D]
        vs_ref[...] = zqkv_ref[:, 2 * W_D:3 * W_D]


def _inproj(cfg, l, depth, x, g, w_in, qg, kg, cos_t, sin_t, ones_bd, caches):
    n, tm, sp = cfg.n_tok, cfg.tm, cfg.s_prompt
    n_dec = n - sp
    assert sp % tm == 0 and n_dec % tm == 0
    npt = sp // tm
    row = lambda w: pl.BlockSpec((tm, w), lambda i: (i, 0))
    prompt_rows = pl.BlockSpec((None, tm * N_HEADS, LANES), lambda i: (l, jnp.minimum(i, npt - 1), 0))
    sample_rows = pl.BlockSpec((None, tm, W_D), lambda i: (l, jnp.maximum(i - npt, 0), 0))
    n_in = 8
    aliased = [] if caches is None else list(caches)
    return pl.pallas_call(
        functools.partial(_inproj_kernel, n_prompt_tiles=npt),
        grid=(n // tm,),
        in_specs=[
            row(D_MODEL),
            _const_spec((1, D_MODEL), (0, 0)),
            _const_spec((None, D_MODEL, W_PROJ), (l, 0, 0)),
            _const_spec((1, NORM_W), (0, 0)),
            _const_spec((1, NORM_W), (0, 0)),
            row(LANES),
            row(LANES),
            _const_spec((NORM_W, NORM_W), (0, 0)),
        ] + [pl.BlockSpec(memory_space=pl.ANY) for _ in aliased],
        out_specs=[row(W_ABC), row(W_D), row(W_D), row(W_D),
                   pl.BlockSpec((N_HEADS, None, LANES, tm), lambda i: (0, i, 0, 0)),
                   pl.BlockSpec((N_HEADS, None, VT_ROWS, tm), lambda i: (0, i, 0, 0)),
                   pl.BlockSpec((None, 2 * N_HEADS, HEAD_DIM, tm), lambda i: (l, 0, 0, jnp.minimum(i, npt - 1))),
                   sample_rows, prompt_rows, sample_rows],
        out_shape=[
            jax.ShapeDtypeStruct((n, W_ABC), F32),
            jax.ShapeDtypeStruct((n, W_D), BF16),
            jax.ShapeDtypeStruct((n, W_D), BF16),
            jax.ShapeDtypeStruct((n, W_D), BF16),
            jax.ShapeDtypeStruct((N_HEADS, n // tm, LANES, tm), BF16),
            jax.ShapeDtypeStruct((N_HEADS, n // tm, VT_ROWS, tm), BF16),
            jax.ShapeDtypeStruct((depth, 2 * N_HEADS, HEAD_DIM, sp), F32),
            jax.ShapeDtypeStruct((depth, n_dec, W_D), F32),
            jax.ShapeDtypeStruct((depth, sp * N_HEADS, LANES), F32),
            jax.ShapeDtypeStruct((depth, n_dec, W_D), F32),
        ],
        scratch_shapes=[pltpu.VMEM((tm, 3 * W_D), F32)],
        input_output_aliases={n_in + j: 6 + j for j in range(len(aliased))},
        compiler_params=pltpu.CompilerParams(
            dimension_semantics=("arbitrary",), vmem_limit_bytes=44 * MIB),
        name="inproj",
    )(x, g, w_in, qg, kg, cos_t, sin_t, ones_bd, *aliased)


SAMPLE_ATTN_GROUP = 2
SAMPLE_MIX_GROUP = 8
MIX_ROWS = 128
SUBLANES = 8


def _shifted_rows(ref, base, rows, depth, tmp):
    out = {}
    for r in range(SUBLANES):
        js = [j for j in range(1, depth + 1) if (-j) % SUBLANES == r]
        if not js:
            continue
        start, length = base - max(js), max(js) - min(js) + rows
        tmp[r, 0:length, :] = ref[start:start + length, :]
        for j in js:
            out[j] = tmp[r, max(js) - j:max(js) - j + rows, :]
    return out


def _mixer_compute(t, pos0, z_ref, pw_ref, ps_ref, sw_ref, cw_ref, cb_ref, lg_ref, lb_ref,
                   y_ref, pool_o, sconv_o, cconv_o, ea, eb, ec, tmp):
    u = z_ref[:, 0:W_A]
    ea[HALO:HALO + t, :] = u
    eb[HALO:HALO + t, :] = z_ref[:, W_A + W_B:W_A + 2 * W_B] * z_ref[:, W_A + 2 * W_B:W_A + 3 * W_B]
    zc = z_ref[:, W_A + 3 * W_B:W_A + 3 * W_B + W_C]
    ec[HALO:HALO + t, :] = zc * jax.nn.sigmoid(z_ref[:, W_A + 3 * W_B + W_C:W_ABC])

    rows = min(t, MIX_ROWS)
    for r0 in range(0, t, rows):
        base = HALO + r0
        lane = lax.broadcasted_iota(jnp.int32, (rows, W_A), 1)
        cur = ea[base:base + rows, :]
        back = _shifted_rows(ea, base, rows, max(POOL_WINDOWS) - 1, tmp)
        acc = cur
        sums = {}
        for j in range(1, max(POOL_WINDOWS)):
            acc = acc + back[j]
            if j + 1 in POOL_WINDOWS:
                sums[j + 1] = acc
        tot = sums[POOL_WINDOWS[-1]]
        win = jnp.full((rows, W_A), float(POOL_WINDOWS[-1]), F32)
        for gi in range(len(POOL_WINDOWS) - 2, -1, -1):
            in_group = lane < (gi + 1) * GA
            tot = jnp.where(in_group, sums[POOL_WINDOWS[gi]], tot)
            win = jnp.where(in_group, float(POOL_WINDOWS[gi]), win)
        if pos0 is None:
            cnt = win
        else:
            pos1 = (pos0 + r0 + 1 + lax.broadcasted_iota(jnp.int32, (rows, W_A), 0)).astype(F32)
            cnt = jnp.minimum(pos1, win)
        d = (tot / cnt - cur).astype(BF16)
        y_ref[r0:r0 + rows, 0:W_A] = _dot(d, pw_ref[...]) * ps_ref[...]

        conv = sw_ref[SCONV_K - 1:SCONV_K, :] * eb[base:base + rows, :]
        for j in range(SCONV_K - 1):
            off = base - (SCONV_K - 1) + j
            conv = conv + sw_ref[j:j + 1, :] * eb[off:off + rows, :]
        y_ref[r0:r0 + rows, W_A:W_A + W_B] = z_ref[r0:r0 + rows, W_A:W_A + W_B] * conv

        back = _shifted_rows(ec, base, rows, CCONV_K - 1, tmp)
        conv = cw_ref[CCONV_K - 1:CCONV_K, :] * ec[base:base + rows, :]
        for j in range(CCONV_K - 1):
            conv = conv + cw_ref[j:j + 1, :] * back[CCONV_K - 1 - j]
        conv = conv + cb_ref[...]
        mu = jnp.mean(conv, axis=-1, keepdims=True)
        cen = conv - mu
        var = jnp.mean(cen * cen, axis=-1, keepdims=True)
        ln = cen * lax.rsqrt(var + EPS) * lg_ref[...] + lb_ref[...]
        y_ref[r0:r0 + rows, W_A + W_B:W_Y] = ln * jax.nn.sigmoid(ln)

    pool_o[...] = ea[HALO + t - 16:HALO + t, :]
    sconv_o[...] = eb[HALO + t - 8:HALO + t, :]
    cconv_o[...] = ec[HALO + t - 32:HALO + t, :]


def _mixer_prompt_kernel(z_ref, pw_ref, ps_ref, sw_ref, cw_ref, cb_ref, lg_ref, lb_ref,
                         y_ref, pool_o, sconv_o, cconv_o, ea, eb, ec, tmp, *, t):
    i = pl.program_id(0)

    @pl.when(i == 0)
    def _():
        zeros = jnp.zeros((HALO, W_A), F32)
        ea[0:HALO, :] = zeros
        eb[0:HALO, :] = zeros
        ec[0:HALO, :] = zeros

    @pl.when(i > 0)
    def _():
        ea[0:HALO, :] = ea[t:t + HALO, :]
        eb[0:HALO, :] = eb[t:t + HALO, :]
        ec[0:HALO, :] = ec[t:t + HALO, :]

    _mixer_compute(t, i * t, z_ref, pw_ref, ps_ref, sw_ref, cw_ref, cb_ref, lg_ref, lb_ref,
                   y_ref, pool_o, sconv_o, cconv_o, ea, eb, ec, tmp)


def _mixer_sample_kernel(z_ref, sp_ref, ss_ref, sc_ref, pw_ref, ps_ref, sw_ref, cw_ref, cb_ref,
                         lg_ref, lb_ref, y_in_ref, y_ref, pool_o, sconv_o, cconv_o, ea, eb, ec, tmp, *, t, group):
    del y_in_ref
    for s in range(group):
        rows = pl.ds(s * t, t)
        ea[0:HALO, :] = sp_ref[s]
        eb[0:HALO, :] = ss_ref[s]
        ec[0:HALO, :] = sc_ref[s]
        _mixer_compute(t, None, z_ref.at[rows], pw_ref, ps_ref, sw_ref, cw_ref, cb_ref, lg_ref, lb_ref,
                       y_ref.at[rows], pool_o.at[s], sconv_o.at[s], cconv_o.at[s], ea, eb, ec, tmp)


def _mixer_weight_specs():
    return [
        _const_spec((W_A, W_A), (0, 0)),
        _const_spec((1, W_A), (0, 0)),
        _const_spec((SCONV_K, W_B), (0, 0)),
        _const_spec((CCONV_K, W_C), (0, 0)),
        _const_spec((1, W_C), (0, 0)),
        _const_spec((1, W_C), (0, 0)),
        _const_spec((1, W_C), (0, 0)),
    ]


def _mixer_scratch(t):
    return [pltpu.VMEM((HALO + t, W_A), F32), pltpu.VMEM((HALO + t, W_B), F32),
            pltpu.VMEM((HALO + t, W_C), F32),
            pltpu.VMEM((SUBLANES, min(t, MIX_ROWS) + HALO, W_C), F32)]


def _mixer_prompt(cfg, zabc, weights):
    t = cfg.tmix
    const_out = lambda r: pl.BlockSpec((r, W_A), lambda i: (0, 0))
    return pl.pallas_call(
        functools.partial(_mixer_prompt_kernel, t=t),
        grid=(cfg.s_prompt // t,),
        in_specs=[pl.BlockSpec((t, W_ABC), lambda i: (i, 0))] + _mixer_weight_specs(),
        out_specs=[pl.BlockSpec((t, W_Y), lambda i: (i, 0)), const_out(16), const_out(8), const_out(32)],
        out_shape=[
            jax.ShapeDtypeStruct((cfg.n_tok, W_Y), F32),
            jax.ShapeDtypeStruct((16, W_A), F32),
            jax.ShapeDtypeStruct((8, W_B), F32),
            jax.ShapeDtypeStruct((32, W_C), F32),
        ],
        scratch_shapes=_mixer_scratch(t),
        compiler_params=pltpu.CompilerParams(dimension_semantics=("arbitrary",)),
        name="mixer_prompt",
    )(zabc, *weights)


def _mixer_sample(cfg, zabc, st_pool, st_sconv, st_cconv, weights, y_abc):
    t, nb = cfg.t_dec, cfg.n_dec
    group = math.gcd(nb, SAMPLE_MIX_GROUP)
    assert cfg.s_prompt % (group * t) == 0
    row0 = cfg.s_prompt // (group * t)
    state_spec = lambda: pl.BlockSpec((group, HALO, W_A), lambda b: (b, 0, 0))
    out_state = lambda r: pl.BlockSpec((group, r, W_A), lambda b: (b, 0, 0))
    return pl.pallas_call(
        functools.partial(_mixer_sample_kernel, t=t, group=group),
        grid=(nb // group,),
        in_specs=[pl.BlockSpec((group * t, W_ABC), lambda b: (row0 + b, 0)),
                  state_spec(), state_spec(), state_spec()]
                 + _mixer_weight_specs()
                 + [pl.BlockSpec(memory_space=pl.ANY)],
        out_specs=[pl.BlockSpec((group * t, W_Y), lambda b: (row0 + b, 0)),
                   out_state(16), out_state(8), out_state(32)],
        out_shape=[
            jax.ShapeDtypeStruct((cfg.n_tok, W_Y), F32),
            jax.ShapeDtypeStruct((nb, 16, W_A), F32),
            jax.ShapeDtypeStruct((nb, 8, W_B), F32),
            jax.ShapeDtypeStruct((nb, 32, W_C), F32),
        ],
        scratch_shapes=_mixer_scratch(t),
        input_output_aliases={11: 0},
        compiler_params=pltpu.CompilerParams(dimension_semantics=("arbitrary",)),
        name="mixer_sample",
    )(zabc, st_pool, st_sconv, st_cconv, *weights, y_abc)


def _lambda(lq1, lk1, lq2, lk2, lam_init):
    s1 = jnp.sum(lq1[...] * lk1[...], axis=-1, keepdims=True)
    s2 = jnp.sum(lq2[...] * lk2[...], axis=-1, keepdims=True)
    return jnp.exp(s1) - jnp.exp(s2) + lam_init


def _diff_out(acc, l, lam, sg, lam_init, t):
    o = acc[0:t] / l[0:t] - lam * (acc[t:2 * t] / l[t:2 * t])
    return _rms(o, sg) * (1.0 - lam_init)


def _attn_prompt_kernel(qt_ref, qtn_ref, k_ref, vt_ref, lq1, lk1, lq2, lk2, sgc_ref, o_ref,
                        q2t_ref, q2tn_ref, s0_ref, s1_ref, mx0_ref, mx1_ref, m_ref, acc_ref, *, tk, lam_init):
    qi = pl.program_id(1)
    nq = 2 * tk
    chan = lax.broadcasted_iota(jnp.int32, (LANES, tk), 0)
    zero = jnp.zeros((LANES, tk), BF16)
    for src_ref, dst_ref in ((qt_ref, q2t_ref), (qtn_ref, q2tn_ref)):
        for half in range(2):
            qt = src_ref[half]
            dst_ref[:, half * tk:(half + 1) * tk] = jnp.where(chan < HEAD_DIM, qt, zero)
            dst_ref[:, nq + half * tk:nq + (half + 1) * tk] = jnp.where(chan >= HEAD_DIM, qt, zero)
    m_ref[...] = jnp.full(m_ref.shape, -jnp.inf, F32)
    acc_ref[...] = jnp.zeros(acc_ref.shape, F32)

    ncols = 2 * nq // COL_GROUPS

    def scores(j, s_ref, mx_ref, diagonal=None, group=None, queries=q2t_ref):
        cols = slice(None) if group is None else slice(group * ncols, (group + 1) * ncols)
        kb = k_ref[pl.ds(pl.multiple_of(j * tk, tk), tk), :]
        s = _dot(kb, queries[:, cols])
        if diagonal is not None:
            key = lax.broadcasted_iota(jnp.int32, s.shape, 0)
            col = lax.broadcasted_iota(jnp.int32, s.shape, 1) + (0 if group is None else group * ncols)
            key_chunk = diagonal * (tk // CHUNK) + key // CHUNK
            s = jnp.where(key_chunk <= (col & (nq - 1)) // CHUNK, s, -jnp.inf)
        s_ref[:, cols] = s
        mx_ref[:, cols] = jnp.max(s, axis=0, keepdims=True)

    def consume(j, s_ref, mx_ref, group=None):
        cols = slice(None) if group is None else slice(group * ncols, (group + 1) * ncols)
        m_prev = m_ref[:, cols]
        m_new = jnp.maximum(m_prev, mx_ref[:, cols])
        alpha = jnp.exp2(m_prev - m_new)
        p = jnp.exp2(s_ref[:, cols] - m_new).astype(BF16)
        acc_ref[:, cols] = alpha * acc_ref[:, cols] + _dot(vt_ref[j], p)
        m_ref[:, cols] = m_new

    def pair(i, first_diagonal):
        j = 2 * i
        for g in range(COL_GROUPS):
            scores(j + 1, s1_ref, mx1_ref, None, g)
            consume(j, s0_ref, mx0_ref, g)
        for g in range(COL_GROUPS):
            scores(j + 2, s0_ref, mx0_ref, first_diagonal, g)
            consume(j + 1, s1_ref, mx1_ref, g)

    n_plain = jnp.maximum(qi - 1, 0)

    def run_pairs(first, count):
        for n in range(count):
            pair(first + n, None)

    def trip(i, carry):
        run_pairs(PAIRS_PER_TRIP * i, PAIRS_PER_TRIP)
        return carry

    lax.fori_loop(0, n_plain // PAIRS_PER_TRIP, trip, 0)
    done = (n_plain // PAIRS_PER_TRIP) * PAIRS_PER_TRIP
    size = PAIRS_PER_TRIP // 2
    while size >= 1:
        take = ((n_plain - done) // size) % 2 == 1
        pl.when(take)(functools.partial(run_pairs, done, size))
        done = done + jnp.where(take, size, 0)
        size //= 2

    def own_blocks():
        assert tk % ncols == 0
        late = [g for g in range(COL_GROUPS) if (g * ncols) % nq >= tk]
        for g in range(COL_GROUPS):
            if g in late:
                scores(2 * qi + 1, s1_ref, mx1_ref, 1, g)
            consume(2 * qi, s0_ref, mx0_ref, g)
            scores(0, s0_ref, mx0_ref, None, g, q2tn_ref)
        for g in late:
            consume(2 * qi + 1, s1_ref, mx1_ref, g)

    @pl.when(qi > 0)
    def _():
        pair(qi - 1, 0)
        own_blocks()

    @pl.when(qi == 0)
    def _():
        scores(0, s0_ref, mx0_ref, 0)
        own_blocks()

    lam = _lambda(lq1, lk1, lq2, lk2, lam_init)
    acc = acc_ref[0:LANES, :]
    l = acc_ref[LANES:LANES + 1, :]
    ot = acc[:, 0:nq] * (1.0 / l[:, 0:nq]) - acc[:, nq:2 * nq] * (lam / l[:, nq:2 * nq])
    ms = jnp.mean(ot * ot, axis=0, keepdims=True)
    o_ref[...] = ot * lax.rsqrt(ms + EPS) * sgc_ref[...] * (1.0 - lam_init)


def _lam_specs():
    return [_const_spec((1, HEAD_DIM), (0, 0)) for _ in range(4)] + [_const_spec((1, LANES), (0, 0))]


def _attn_prompt(cfg, qt, kb, vt, lam_w, lam_init):
    s, tk = cfg.s_prompt, cfg.tm
    tq = 2 * tk
    assert tq & (tq - 1) == 0 and tk % CHUNK == 0 and s % tq == 0
    return pl.pallas_call(
        functools.partial(_attn_prompt_kernel, tk=tk, lam_init=lam_init),
        grid=(N_HEADS, s // tq),
        in_specs=[
            pl.BlockSpec((None, 2, LANES, tk), lambda h, i: (h, i, 0, 0)),
            pl.BlockSpec((None, 2, LANES, tk), lambda h, i: (h, jnp.minimum(i + 1, s // tq - 1), 0, 0)),
            pl.BlockSpec((s, LANES), lambda h, i: (0, h)),
            pl.BlockSpec((None, s // tk, VT_ROWS, tk), lambda h, i: (h, 0, 0, 0)),
        ] + _lam_specs()[:4] + [_const_spec((LANES, 1), (0, 0))],
        out_specs=pl.BlockSpec((LANES, tq), lambda h, i: (h, i)),
        out_shape=jax.ShapeDtypeStruct((W_D, s), F32),
        scratch_shapes=[
            pltpu.VMEM((LANES, 2 * tq), BF16),
            pltpu.VMEM((LANES, 2 * tq), BF16),
            pltpu.VMEM((tk, 2 * tq), F32),
            pltpu.VMEM((tk, 2 * tq), F32),
            pltpu.VMEM((1, 2 * tq), F32),
            pltpu.VMEM((1, 2 * tq), F32),
            pltpu.VMEM((1, 2 * tq), F32),
            pltpu.VMEM((VT_ROWS, 2 * tq), F32),
        ],
        compiler_params=pltpu.CompilerParams(
            dimension_semantics=("arbitrary", "arbitrary"), vmem_limit_bytes=52 * MIB),
        name="attn_prompt",
    )(qt, qt, kb, vt, *lam_w[:4], lam_w[4].reshape(LANES, 1))


def _attn_sample_kernel(q_ref, kn_ref, vn_ref, kc_ref, vc_ref, lq1, lk1, lq2, lk2, sg_ref,
                        o_ref, *, t, lam_init, group):
    lam = _lambda(lq1, lk1, lq2, lk2, lam_init)
    past = vc_ref.shape[1] // N_HEADS
    for s in range(group):
        rows = slice(s * t, (s + 1) * t)
        for h in range(N_HEADS):
            sl = slice(h * LANES, (h + 1) * LANES)
            s_past, s_new = [], []
            for c in range(2):
                ch = slice(h * LANES + c * HEAD_DIM, h * LANES + (c + 1) * HEAD_DIM)
                q = q_ref[rows, ch]
                s_past.append(_dot(q, kc_ref[s, 2 * h + c].astype(BF16)))
                s_new.append(_dot_t(q, kn_ref[rows, ch]))
            s_past = jnp.concatenate(s_past, axis=0)
            s_new = jnp.concatenate(s_new, axis=0)
            v_past = vc_ref[s, pl.ds(h, past, stride=N_HEADS), :]
            m = jnp.maximum(jnp.max(s_past, axis=-1, keepdims=True), jnp.max(s_new, axis=-1, keepdims=True))
            p_past = jnp.exp(s_past - m)
            p_new = jnp.exp(s_new - m)
            l = jnp.sum(p_past, axis=-1, keepdims=True) + jnp.sum(p_new, axis=-1, keepdims=True)
            acc = (_dot(p_past.astype(BF16), v_past.astype(BF16))
                   + _dot(p_new.astype(BF16), vn_ref[rows, sl]))
            o_ref[rows, sl] = _diff_out(acc, l, lam, sg_ref[...], lam_init, t)


def _attn_sample(cfg, l, qb, kb, vb, cache_k, cache_v, lam_w, lam_init):
    t, nb = cfg.t_dec, cfg.n_dec
    group = math.gcd(nb, SAMPLE_ATTN_GROUP)
    assert cfg.s_prompt % (group * t) == 0
    row0 = cfg.s_prompt // (group * t)
    new_rows = lambda: pl.BlockSpec((group * t, W_D), lambda b: (row0 + b, 0))
    cache = lambda a: pl.BlockSpec((None, group) + a.shape[2:], lambda b: (l, b) + (0,) * (a.ndim - 2))
    return pl.pallas_call(
        functools.partial(_attn_sample_kernel, t=t, lam_init=lam_init, group=group),
        grid=(nb // group,),
        in_specs=[new_rows(), new_rows(), new_rows(), cache(cache_k), cache(cache_v)] + _lam_specs(),
        out_specs=pl.BlockSpec((group * t, W_D), lambda b: (b, 0)),
        out_shape=jax.ShapeDtypeStruct((nb * t, W_D), F32),
        compiler_params=pltpu.CompilerParams(
            dimension_semantics=("arbitrary",), vmem_limit_bytes=48 * MIB),
        name="attn_sample",
    )(qb, kb, vb, cache_k, cache_v, *lam_w)


def _merge_kernel(x_ref, yabc_ref, ydt_ref, yds_ref, g_ref, wg_ref, wpa_ref, wpb_ref, wpc_ref, wpd_ref, wo_ref,
                  o_ref, *, n_prompt_tiles):
    is_prompt = pl.program_id(0) < n_prompt_tiles
    half = x_ref.shape[0] // 2
    for rows in (slice(0, half), slice(half, 2 * half)):
        yd = jnp.where(is_prompt, ydt_ref[:, rows].T, yds_ref[rows, :])
        x = x_ref[rows, :]
        h = _rms(x, g_ref[...]).astype(BF16)
        branches = (
            (yabc_ref[rows, 0:W_A], wpa_ref),
            (yabc_ref[rows, W_A:W_A + W_B], wpb_ref),
            (yabc_ref[rows, W_A + W_B:W_Y], wpc_ref),
            (yd, wpd_ref),
        )
        merged = None
        for i, (y, wp_ref) in enumerate(branches):
            gate = jax.nn.sigmoid(_dot(h, wg_ref[:, i * D_MODEL:(i + 1) * D_MODEL]))
            term = gate * _dot(y.astype(BF16), wp_ref[...])
            merged = term if merged is None else merged + term
        o_ref[rows, :] = x + _dot(merged.astype(BF16), wo_ref[...])


def _merge(cfg, l, x, y_abc, yd_t, yd_s, g, w_gate, wpa, wpb, wpc, wpd, wo):
    n, tm, sp = cfg.n_tok, cfg.tm, cfg.s_prompt
    assert sp % tm == 0 and (n - sp) % tm == 0
    npt = sp // tm
    row = lambda w: pl.BlockSpec((tm, w), lambda i: (i, 0))
    return pl.pallas_call(
        functools.partial(_merge_kernel, n_prompt_tiles=npt),
        grid=(n // tm,),
        in_specs=[
            row(D_MODEL), row(W_Y),
            pl.BlockSpec((W_D, tm), lambda i: (0, jnp.minimum(i, npt - 1))),
            pl.BlockSpec((tm, W_D), lambda i: (jnp.maximum(i - npt, 0), 0)),
            _const_spec((1, D_MODEL), (0, 0)),
            _const_spec((None, D_MODEL, 4 * D_MODEL), (l, 0, 0)),
            _const_spec((None, W_A, D_MODEL), (l, 0, 0)),
            _const_spec((None, W_B, D_MODEL), (l, 0, 0)),
            _const_spec((None, W_C, D_MODEL), (l, 0, 0)),
            _const_spec((None, W_D, D_MODEL), (l, 0, 0)),
            _const_spec((None, D_MODEL, D_MODEL), (l, 0, 0)),
        ],
        out_specs=row(D_MODEL),
        out_shape=jax.ShapeDtypeStruct((n, D_MODEL), F32),
        compiler_params=pltpu.CompilerParams(
            dimension_semantics=("arbitrary",), vmem_limit_bytes=48 * MIB),
        name="merge",
    )(x, y_abc, yd_t, yd_s, g, w_gate, wpa, wpb, wpc, wpd, wo)


def _rope_tables(cfg):
    half = HEAD_DIM // 2
    inv_freq = ROPE_THETA ** (-jnp.arange(half, dtype=F32) / half)
    pos = jnp.concatenate([jnp.arange(cfg.s_prompt), jnp.tile(cfg.past + jnp.arange(cfg.t_dec), cfg.n_dec)])
    ang = pos.astype(F32)[:, None] * inv_freq[None, :]
    cos, sin = jnp.cos(ang), jnp.sin(ang)
    reps = LANES // HEAD_DIM
    cos_t = jnp.tile(jnp.concatenate([cos, cos], axis=1), (1, reps))
    sin_t = jnp.tile(jnp.concatenate([-sin, sin], axis=1), (1, reps))
    return cos_t, sin_t


def _pad_rows_top(a, rows):
    return jnp.pad(a, ((0, 0), (0, 0), (rows - a.shape[2], 0), (0, 0)))


def _forward(cfg, x_prompt, x_sample, cache_k, cache_v, state_pool, state_sconv, state_cconv,
             g_ffn1, w1_gate, w1_up, w1_down, g_mix, w_in, pool_w, pool_scale, sconv_w,
             cconv_w, cconv_b, ln_g, ln_b, q_norm_g, k_norm_g, lam_q1, lam_k1, lam_q2, lam_k2,
             subln_g, wp_a, wp_b, wp_c, wp_d, w_out, g_ffn2, w2_gate, w2_up, w2_down):
    depth = w_in.shape[0]
    sp, nb, td = cfg.s_prompt, cfg.n_dec, cfg.t_dec
    x = (x_prompt.reshape(sp, D_MODEL), x_sample.reshape(nb * td, D_MODEL))

    bf = lambda w: w.astype(BF16)
    w1g, w1u, w1d = bf(w1_gate), bf(w1_up), bf(w1_down)
    w2g, w2u, w2d = bf(w2_gate), bf(w2_up), bf(w2_down)
    w_proj, w_gate = bf(w_in[:, :, :W_PROJ]), bf(w_in[:, :, W_PROJ:])
    wpa, wpb, wpc, wpd, wo = bf(wp_a), bf(wp_b), bf(wp_c), bf(wp_d), bf(w_out)
    cos_t, sin_t = _rope_tables(cfg)
    ones_bd = jnp.kron(jnp.eye(NORM_W // HEAD_DIM, dtype=F32), jnp.ones((HEAD_DIM, HEAD_DIM), F32)).astype(BF16)
    eye_g = jnp.eye(len(POOL_WINDOWS), dtype=F32)
    ck = jnp.transpose(cache_k, (0, 1, 3, 4, 2))
    cv = cache_v.reshape(depth, nb, cfg.past * N_HEADS, 2 * HEAD_DIM)
    st_pool = _pad_rows_top(state_pool, HALO)
    st_sconv = _pad_rows_top(state_sconv, HALO)
    st_cconv = _pad_rows_top(state_cconv, HALO)
    row = lambda a: a.reshape(1, -1)

    outs = [[] for _ in range(6)]
    caches = None
    for l in range(depth):
        lam_init = 0.8 - 0.6 * math.exp(-0.3 * l)
        x = _ffn(cfg, l, x, row(g_ffn1[l]), w1g, w1u, w1d)
        qg = row(jnp.tile(q_norm_g[l], NORM_W // HEAD_DIM))
        kg = row(jnp.tile(k_norm_g[l], NORM_W // HEAD_DIM))
        zabc, qb, kb, vb, qt, vt, *caches = _inproj(cfg, l, depth, x, row(g_mix[l]), w_proj, qg, kg, cos_t, sin_t,
                                                    ones_bd, caches)

        pw_bd = (eye_g[:, None, :, None] * pool_w[l][:, :, None, :]).reshape(W_A, W_A).astype(BF16)
        mix_w = (pw_bd, row(pool_scale[l]), sconv_w[l], cconv_w[l], row(cconv_b[l]), row(ln_g[l]), row(ln_b[l]))
        y_abc, pool_p, sconv_p, cconv_p = _mixer_prompt(cfg, zabc, mix_w)
        y_abc, pool_s, sconv_s, cconv_s = _mixer_sample(cfg, zabc, st_pool[l], st_sconv[l], st_cconv[l],
                                                        mix_w, y_abc)

        lam_w = (row(lam_q1[l]), row(lam_k1[l]), row(lam_q2[l]), row(lam_k2[l]), row(subln_g[l]))
        yd_t = _attn_prompt(cfg, qt, kb, vt, lam_w, lam_init)
        yd_s = _attn_sample(cfg, l, qb, kb, vb, ck, cv, lam_w, lam_init)

        x = _merge(cfg, l, x, y_abc, yd_t, yd_s, row(g_mix[l]), w_gate, wpa, wpb, wpc, wpd, wo)
        x = _ffn(cfg, l, x, row(g_ffn2[l]), w2g, w2u, w2d, split_out=(l == depth - 1))

        outs[0].append(pool_p[None, 16 - POOL_STATE:])
        outs[1].append(sconv_p[None, 8 - (SCONV_K - 1):])
        outs[2].append(cconv_p[None, 32 - (CCONV_K - 1):])
        outs[3].append(pool_s[:, 16 - POOL_STATE:])
        outs[4].append(sconv_s[:, 8 - (SCONV_K - 1):])
        outs[5].append(cconv_s[:, 32 - (CCONV_K - 1):])

    kt_all, ks_all, vp_all, vs_all = caches
    y_prompt = x[0].reshape(1, sp, D_MODEL)
    y_sample = x[1].reshape(nb, td, D_MODEL)
    k_prompt = jnp.transpose(kt_all.reshape(depth, 1, 2 * N_HEADS, HEAD_DIM, sp), (0, 1, 4, 2, 3))
    v_prompt = vp_all.reshape(depth, 1, sp, N_HEADS, 2 * HEAD_DIM)
    k_sample = ks_all.reshape(depth, nb, td, 2 * N_HEADS, HEAD_DIM)
    v_sample = vs_all.reshape(depth, nb, td, N_HEADS, 2 * HEAD_DIM)
    st = [jnp.stack(o) for o in outs]
    return (y_prompt, y_sample, k_prompt, v_prompt, st[0], st[1], st[2], k_sample, v_sample, st[3], st[4], st[5])


def kernel(x_prompt, x_sample, cache_k, cache_v, state_pool, state_sconv, state_cconv, g_ffn1, w1_gate, w1_up, w1_down, g_mix, w_in, pool_w, pool_scale, sconv_w, cconv_w, cconv_b, ln_g, ln_b, q_norm_g, k_norm_g, lam_q1, lam_k1, lam_q2, lam_k2, subln_g, wp_a, wp_b, wp_c, wp_d, w_out, g_ffn2, w2_gate, w2_up, w2_down):
    assert x_prompt.shape[0] == 1
    cfg = Cfg(s_prompt=x_prompt.shape[1], n_dec=x_sample.shape[0], t_dec=x_sample.shape[1],
              past=cache_k.shape[2], tm=512, tmix=512)
    return _forward(cfg, x_prompt, x_sample, cache_k, cache_v, state_pool, state_sconv, state_cconv,
                    g_ffn1, w1_gate, w1_up, w1_down, g_mix, w_in, pool_w, pool_scale, sconv_w,
                    cconv_w, cconv_b, ln_g, ln_b, q_norm_g, k_norm_g, lam_q1, lam_k1, lam_q2, lam_k2,
                    subln_g, wp_a, wp_b, wp_c, wp_d, w_out, g_ffn2, w2_gate, w2_up, w2_down)
```

```python
import functools
import math
from typing import NamedTuple

import jax
import jax.numpy as jnp
from jax import lax
from jax.experimental import pallas as pl
from jax.experimental.pallas import tpu as pltpu

F32 = jnp.float32
BF16 = jnp.bfloat16

D_MODEL = 1024
DEPTH = 4
CHUNK = 64
POOL_WINDOWS = (2, 4, 8, 16)
W_A = 256
GA = 64
POOL_STATE = 15
W_B = 256
SCONV_K = 3
W_C = 256
CCONV_K = 31
HEAD_DIM = 64
N_HEADS = 4
W_D = 512
ROPE_THETA = 10000.0
D_FF = 2816
EPS = 1e-6
W_ABC = W_A + 3 * W_B + 2 * W_C
W_PROJ = W_ABC + 3 * W_D
W_Y = W_A + W_B + W_C

LANES = 128
NORM_W = 256
PAIRS_PER_TRIP = 4
COL_GROUPS = 4
VT_ROWS = LANES + 16
LOG2E = 1.4426950408889634
HALO = 32
MIB = 1024 * 1024


class Cfg(NamedTuple):
    s_prompt: int
    n_dec: int
    t_dec: int
    past: int
    tm: int
    tmix: int

    @property
    def n_tok(self):
        return self.s_prompt + self.n_dec * self.t_dec


def _const_spec(shape, index):
    return pl.BlockSpec(shape, lambda *_: index, pipeline_mode=pl.Buffered(1))


def _rms(x, g):
    ms = jnp.mean(x * x, axis=-1, keepdims=True)
    return x * lax.rsqrt(ms + EPS) * g


def _dot(a, b):
    return jnp.dot(a, b, preferred_element_type=F32)


def _dot_t(a, b):
    return lax.dot_general(a, b, (((1,), (1,)), ((), ())), preferred_element_type=F32)


def _ffn_kernel(*refs, n_prompt_tiles, split_in, split_out):
    refs = list(refs)
    is_prompt = pl.program_id(0) < n_prompt_tiles
    if split_in:
        xp_ref, xs_ref = refs[0:2]
        del refs[0:2]
        x = jnp.where(is_prompt, xp_ref[...], xs_ref[...])
    else:
        x = refs.pop(0)[...]
    g_ref, wg_ref, wu_ref, wd_ref = refs[0:4]
    halves = []
    for xh in (x[0:x.shape[0] // 2], x[x.shape[0] // 2:]):
        h = _rms(xh, g_ref[...]).astype(BF16)
        a = _dot(h, wg_ref[...])
        u = _dot(h, wu_ref[...])
        act = (a * jax.nn.sigmoid(a) * u).astype(BF16)
        halves.append(0.5 * _dot(act, wd_ref[...]))
    half_step = jnp.concatenate(halves, axis=0)
    if split_out:
        op_ref, os_ref = refs[4:6]

        @pl.when(is_prompt)
        def _():
            op_ref[...] = x + half_step

        @pl.when(jnp.logical_not(is_prompt))
        def _():
            os_ref[...] = x + half_step
    else:
        refs[4][...] = x + half_step


def _ffn(cfg, l, x, g, wg, wu, wd, split_out=False):
    n, tm, sp = cfg.n_tok, cfg.tm, cfg.s_prompt
    assert sp % tm == 0 and (n - sp) % tm == 0
    npt = sp // tm
    split_in = isinstance(x, tuple)
    rows = pl.BlockSpec((tm, D_MODEL), lambda i: (i, 0))
    prompt_rows = pl.BlockSpec((tm, D_MODEL), lambda i: (jnp.minimum(i, npt - 1), 0))
    sample_rows = pl.BlockSpec((tm, D_MODEL), lambda i: (jnp.maximum(i - npt, 0), 0))
    whole = jax.ShapeDtypeStruct((n, D_MODEL), F32)
    halves = [jax.ShapeDtypeStruct((sp, D_MODEL), F32), jax.ShapeDtypeStruct((n - sp, D_MODEL), F32)]
    return pl.pallas_call(
        functools.partial(_ffn_kernel, n_prompt_tiles=npt, split_in=split_in, split_out=split_out),
        grid=(n // tm,),
        in_specs=([prompt_rows, sample_rows] if split_in else [rows]) + [
            _const_spec((1, D_MODEL), (0, 0)),
            _const_spec((None, D_MODEL, D_FF), (l, 0, 0)),
            _const_spec((None, D_MODEL, D_FF), (l, 0, 0)),
            _const_spec((None, D_FF, D_MODEL), (l, 0, 0)),
        ],
        out_specs=[prompt_rows, sample_rows] if split_out else rows,
        out_shape=halves if split_out else whole,
        compiler_params=pltpu.CompilerParams(
            dimension_semantics=("arbitrary",), vmem_limit_bytes=52 * MIB),
        name="ffn",
    )(*(x if split_in else (x,)), g, wg, wu, wd)


def _inproj_kernel(x_ref, g_ref, w_ref, qg_ref, kg_ref, cos_ref, sin_ref, ones_ref, *rest, n_prompt_tiles):
    zabc_ref, qb_ref, kb_ref, vb_ref, qt_ref, vt_ref, kt_ref, ks_ref, vp_ref, vs_ref, zqkv_ref = rest[-11:]
    is_prompt = pl.program_id(0) < n_prompt_tiles
    h = _rms(x_ref[...], g_ref[...]).astype(BF16)
    assert W_ABC % (2 * NORM_W) == 0 and W_ABC // (2 * NORM_W) <= W_D // LANES
    cos = cos_ref[...]
    sin = sin_ref[...]
    ones = ones_ref[...]
    lane = lax.broadcasted_iota(jnp.int32, cos.shape, 1)
    first_half = (lane & (HEAD_DIM // 2)) == 0
    map_rows = lax.broadcasted_iota(jnp.int32, (LANES, x_ref.shape[0]), 0) < HEAD_DIM

    def head_norm(z, g):
        ss = z * z
        hi = ss.astype(BF16)
        lo = (ss - hi.astype(F32)).astype(BF16)
        tot = _dot(hi, ones) + _dot(lo, ones)
        return z * lax.rsqrt(tot * (1.0 / HEAD_DIM) + EPS) * g

    def rope(y):
        half = HEAD_DIM // 2
        partner = jnp.where(first_half, pltpu.roll(y, LANES - half, 1), pltpu.roll(y, half, 1))
        return y * cos + partner * sin

    nw = ones.shape[0]
    abc_w = 2 * NORM_W
    n_chunks = W_D // LANES
    for g0 in range(0, W_D, nw):
        for part in range(3):
            lo = part * W_D + g0
            zqkv_ref[:, lo:lo + nw] = _dot(h, w_ref[:, W_ABC + lo:W_ABC + lo + nw])
    yq = [head_norm(zqkv_ref[:, g0:g0 + nw], qg_ref[...]) for g0 in range(0, W_D, nw)]
    yk = [head_norm(zqkv_ref[:, W_D + g0:W_D + g0 + nw], kg_ref[...]) for g0 in range(0, W_D, nw)]
    for c in range(n_chunks):
        sl = slice(c * LANES, (c + 1) * LANES)
        grp, off = divmod(c * LANES, nw)
        q = rope(yq[grp][:, off:off + LANES]) * (HEAD_DIM ** -0.5)
        qb_ref[:, sl] = q.astype(BF16)
        qt = (q * LOG2E).T.astype(BF16)
        qt_ref[0, c] = jnp.where(map_rows, qt, jnp.zeros_like(qt))
        qt_ref[1, c] = jnp.where(map_rows, jnp.zeros_like(qt), qt)
        k = rope(yk[grp][:, off:off + LANES])
        kb_ref[:, sl] = k.astype(BF16)
        zqkv_ref[:, W_D + c * LANES:W_D + (c + 1) * LANES] = k
        v = zqkv_ref[:, 2 * W_D + c * LANES:2 * W_D + (c + 1) * LANES]
        vb_ref[:, sl] = v.astype(BF16)
        vt_ref[c, 0:LANES, :] = v.T.astype(BF16)
        vt_ref[c, LANES:VT_ROWS, :] = jnp.ones((VT_ROWS - LANES, v.shape[0]), BF16)
        if c * abc_w < W_ABC:
            zabc_ref[:, c * abc_w:(c + 1) * abc_w] = _dot(h, w_ref[:, c * abc_w:(c + 1) * abc_w])

    @pl.when(is_prompt)
    def _():
        for c in range(W_D // LANES):
            kt = zqkv_ref[:, W_D + c * LANES:W_D + (c + 1) * LANES].T
            kt_ref[2 * c] = kt[0:HEAD_DIM]
            kt_ref[2 * c + 1] = kt[HEAD_DIM:LANES]
        for hd in range(N_HEADS):
            vp_ref[pl.ds(hd, kt_ref.shape[-1], stride=N_HEADS), :] = (
                zqkv_ref[:, 2 * W_D + hd * LANES:2 * W_D + (hd + 1) * LANES])

    @pl.when(jnp.logical_not(is_prompt))
    def _():
        ks_ref[...] = zqkv_ref[:, W_D:2 * W_D]
        vs_ref[...] = zqkv_ref[:, 2 * W_D:3 * W_D]


def _inproj(cfg, l, depth, x, g, w_in, qg, kg, cos_t, sin_t, ones_bd, caches):
    n, tm, sp = cfg.n_tok, cfg.tm, cfg.s_prompt
    n_dec = n - sp
    assert sp % tm == 0 and n_dec % tm == 0
    npt = sp // tm
    row = lambda w: pl.BlockSpec((tm, w), lambda i: (i, 0))
    prompt_rows = pl.BlockSpec((None, tm * N_HEADS, LANES), lambda i: (l, jnp.minimum(i, npt - 1), 0))
    sample_rows = pl.BlockSpec((None, tm, W_D), lambda i: (l, jnp.maximum(i - npt, 0), 0))
    n_in = 8
    aliased = [] if caches is None else list(caches)
    return pl.pallas_call(
        functools.partial(_inproj_kernel, n_prompt_tiles=npt),
        grid=(n // tm,),
        in_specs=[
            row(D_MODEL),
            _const_spec((1, D_MODEL), (0, 0)),
            _const_spec((None, D_MODEL, W_PROJ), (l, 0, 0)),
            _const_spec((1, NORM_W), (0, 0)),
            _const_spec((1, NORM_W), (0, 0)),
            row(LANES),
            row(LANES),
            _const_spec((NORM_W, NORM_W), (0, 0)),
        ] + [pl.BlockSpec(memory_space=pl.ANY) for _ in aliased],
        out_specs=[row(W_ABC), row(W_D), row(W_D), row(W_D),
                   pl.BlockSpec((2, N_HEADS, None, LANES, tm), lambda i: (0, 0, i, 0, 0)),
                   pl.BlockSpec((N_HEADS, None, VT_ROWS, tm), lambda i: (0, i, 0, 0)),
                   pl.BlockSpec((None, 2 * N_HEADS, HEAD_DIM, tm), lambda i: (l, 0, 0, jnp.minimum(i, npt - 1))),
                   sample_rows, prompt_rows, sample_rows],
        out_shape=[
            jax.ShapeDtypeStruct((n, W_ABC), F32),
            jax.ShapeDtypeStruct((n, W_D), BF16),
            jax.ShapeDtypeStruct((n, W_D), BF16),
            jax.ShapeDtypeStruct((n, W_D), BF16),
            jax.ShapeDtypeStruct((2, N_HEADS, n // tm, LANES, tm), BF16),
            jax.ShapeDtypeStruct((N_HEADS, n // tm, VT_ROWS, tm), BF16),
            jax.ShapeDtypeStruct((depth, 2 * N_HEADS, HEAD_DIM, sp), F32),
            jax.ShapeDtypeStruct((depth, n_dec, W_D), F32),
            jax.ShapeDtypeStruct((depth, sp * N_HEADS, LANES), F32),
            jax.ShapeDtypeStruct((depth, n_dec, W_D), F32),
        ],
        scratch_shapes=[pltpu.VMEM((tm, 3 * W_D), F32)],
        input_output_aliases={n_in + j: 6 + j for j in range(len(aliased))},
        compiler_params=pltpu.CompilerParams(
            dimension_semantics=("arbitrary",), vmem_limit_bytes=44 * MIB),
        name="inproj",
    )(x, g, w_in, qg, kg, cos_t, sin_t, ones_bd, *aliased)


SAMPLE_ATTN_GROUP = 2
SAMPLE_MIX_GROUP = 8
MIX_ROWS = 128
SUBLANES = 8


def _shifted_rows(ref, base, rows, depth, tmp):
    out = {}
    for r in range(SUBLANES):
        js = [j for j in range(1, depth + 1) if (-j) % SUBLANES == r]
        if not js:
            continue
        start, length = base - max(js), max(js) - min(js) + rows
        tmp[r, 0:length, :] = ref[start:start + length, :]
        for j in js:
            out[j] = tmp[r, max(js) - j:max(js) - j + rows, :]
    return out


def _mixer_compute(t, pos0, z_ref, pw_ref, ps_ref, sw_ref, cw_ref, cb_ref, lg_ref, lb_ref,
                   y_ref, pool_o, sconv_o, cconv_o, ea, eb, ec, tmp):
    u = z_ref[:, 0:W_A]
    ea[HALO:HALO + t, :] = u
    eb[HALO:HALO + t, :] = z_ref[:, W_A + W_B:W_A + 2 * W_B] * z_ref[:, W_A + 2 * W_B:W_A + 3 * W_B]
    zc = z_ref[:, W_A + 3 * W_B:W_A + 3 * W_B + W_C]
    ec[HALO:HALO + t, :] = zc * jax.nn.sigmoid(z_ref[:, W_A + 3 * W_B + W_C:W_ABC])

    rows = min(t, MIX_ROWS)
    for r0 in range(0, t, rows):
        base = HALO + r0
        lane = lax.broadcasted_iota(jnp.int32, (rows, W_A), 1)
        cur = ea[base:base + rows, :]
        back = _shifted_rows(ea, base, rows, max(POOL_WINDOWS) - 1, tmp)
        acc = cur
        sums = {}
        for j in range(1, max(POOL_WINDOWS)):
            acc = acc + back[j]
            if j + 1 in POOL_WINDOWS:
                sums[j + 1] = acc
        tot = sums[POOL_WINDOWS[-1]]
        win = jnp.full((rows, W_A), float(POOL_WINDOWS[-1]), F32)
        for gi in range(len(POOL_WINDOWS) - 2, -1, -1):
            in_group = lane < (gi + 1) * GA
            tot = jnp.where(in_group, sums[POOL_WINDOWS[gi]], tot)
            win = jnp.where(in_group, float(POOL_WINDOWS[gi]), win)
        if pos0 is None:
            cnt = win
        else:
            pos1 = (pos0 + r0 + 1 + lax.broadcasted_iota(jnp.int32, (rows, W_A), 0)).astype(F32)
            cnt = jnp.minimum(pos1, win)
        d = (tot / cnt - cur).astype(BF16)
        y_ref[r0:r0 + rows, 0:W_A] = _dot(d, pw_ref[...]) * ps_ref[...]

        conv = sw_ref[SCONV_K - 1:SCONV_K, :] * eb[base:base + rows, :]
        for j in range(SCONV_K - 1):
            off = base - (SCONV_K - 1) + j
            conv = conv + sw_ref[j:j + 1, :] * eb[off:off + rows, :]
        y_ref[r0:r0 + rows, W_A:W_A + W_B] = z_ref[r0:r0 + rows, W_A:W_A + W_B] * conv

        back = _shifted_rows(ec, base, rows, CCONV_K - 1, tmp)
        conv = cw_ref[CCONV_K - 1:CCONV_K, :] * ec[base:base + rows, :]
        for j in range(CCONV_K - 1):
            conv = conv + cw_ref[j:j + 1, :] * back[CCONV_K - 1 - j]
        conv = conv + cb_ref[...]
        mu = jnp.mean(conv, axis=-1, keepdims=True)
        cen = conv - mu
        var = jnp.mean(cen * cen, axis=-1, keepdims=True)
        ln = cen * lax.rsqrt(var + EPS) * lg_ref[...] + lb_ref[...]
        y_ref[r0:r0 + rows, W_A + W_B:W_Y] = ln * jax.nn.sigmoid(ln)

    pool_o[...] = ea[HALO + t - 16:HALO + t, :]
    sconv_o[...] = eb[HALO + t - 8:HALO + t, :]
    cconv_o[...] = ec[HALO + t - 32:HALO + t, :]


def _mixer_prompt_kernel(z_ref, pw_ref, ps_ref, sw_ref, cw_ref, cb_ref, lg_ref, lb_ref,
                         y_ref, pool_o, sconv_o, cconv_o, ea, eb, ec, tmp, *, t):
    i = pl.program_id(0)

    @pl.when(i == 0)
    def _():
        zeros = jnp.zeros((HALO, W_A), F32)
        ea[0:HALO, :] = zeros
        eb[0:HALO, :] = zeros
        ec[0:HALO, :] = zeros

    @pl.when(i > 0)
    def _():
        ea[0:HALO, :] = ea[t:t + HALO, :]
        eb[0:HALO, :] = eb[t:t + HALO, :]
        ec[0:HALO, :] = ec[t:t + HALO, :]

    _mixer_compute(t, i * t, z_ref, pw_ref, ps_ref, sw_ref, cw_ref, cb_ref, lg_ref, lb_ref,
                   y_ref, pool_o, sconv_o, cconv_o, ea, eb, ec, tmp)


def _mixer_sample_kernel(z_ref, sp_ref, ss_ref, sc_ref, pw_ref, ps_ref, sw_ref, cw_ref, cb_ref,
                         lg_ref, lb_ref, y_in_ref, y_ref, pool_o, sconv_o, cconv_o, ea, eb, ec, tmp, *, t, group):
    del y_in_ref
    for s in range(group):
        rows = pl.ds(s * t, t)
        ea[0:HALO, :] = sp_ref[s]
        eb[0:HALO, :] = ss_ref[s]
        ec[0:HALO, :] = sc_ref[s]
        _mixer_compute(t, None, z_ref.at[rows], pw_ref, ps_ref, sw_ref, cw_ref, cb_ref, lg_ref, lb_ref,
                       y_ref.at[rows], pool_o.at[s], sconv_o.at[s], cconv_o.at[s], ea, eb, ec, tmp)


def _mixer_weight_specs():
    return [
        _const_spec((W_A, W_A), (0, 0)),
        _const_spec((1, W_A), (0, 0)),
        _const_spec((SCONV_K, W_B), (0, 0)),
        _const_spec((CCONV_K, W_C), (0, 0)),
        _const_spec((1, W_C), (0, 0)),
        _const_spec((1, W_C), (0, 0)),
        _const_spec((1, W_C), (0, 0)),
    ]


def _mixer_scratch(t):
    return [pltpu.VMEM((HALO + t, W_A), F32), pltpu.VMEM((HALO + t, W_B), F32),
            pltpu.VMEM((HALO + t, W_C), F32),
            pltpu.VMEM((SUBLANES, min(t, MIX_ROWS) + HALO, W_C), F32)]


def _mixer_prompt(cfg, zabc, weights):
    t = cfg.tmix
    const_out = lambda r: pl.BlockSpec((r, W_A), lambda i: (0, 0))
    return pl.pallas_call(
        functools.partial(_mixer_prompt_kernel, t=t),
        grid=(cfg.s_prompt // t,),
        in_specs=[pl.BlockSpec((t, W_ABC), lambda i: (i, 0))] + _mixer_weight_specs(),
        out_specs=[pl.BlockSpec((t, W_Y), lambda i: (i, 0)), const_out(16), const_out(8), const_out(32)],
        out_shape=[
            jax.ShapeDtypeStruct((cfg.n_tok, W_Y), F32),
            jax.ShapeDtypeStruct((16, W_A), F32),
            jax.ShapeDtypeStruct((8, W_B), F32),
            jax.ShapeDtypeStruct((32, W_C), F32),
        ],
        scratch_shapes=_mixer_scratch(t),
        compiler_params=pltpu.CompilerParams(dimension_semantics=("arbitrary",)),
        name="mixer_prompt",
    )(zabc, *weights)


def _mixer_sample(cfg, zabc, st_pool, st_sconv, st_cconv, weights, y_abc):
    t, nb = cfg.t_dec, cfg.n_dec
    group = math.gcd(nb, SAMPLE_MIX_GROUP)
    assert cfg.s_prompt % (group * t) == 0
    row0 = cfg.s_prompt // (group * t)
    state_spec = lambda: pl.BlockSpec((group, HALO, W_A), lambda b: (b, 0, 0))
    out_state = lambda r: pl.BlockSpec((group, r, W_A), lambda b: (b, 0, 0))
    return pl.pallas_call(
        functools.partial(_mixer_sample_kernel, t=t, group=group),
        grid=(nb // group,),
        in_specs=[pl.BlockSpec((group * t, W_ABC), lambda b: (row0 + b, 0)),
                  state_spec(), state_spec(), state_spec()]
                 + _mixer_weight_specs()
                 + [pl.BlockSpec(memory_space=pl.ANY)],
        out_specs=[pl.BlockSpec((group * t, W_Y), lambda b: (row0 + b, 0)),
                   out_state(16), out_state(8), out_state(32)],
        out_shape=[
            jax.ShapeDtypeStruct((cfg.n_tok, W_Y), F32),
            jax.ShapeDtypeStruct((nb, 16, W_A), F32),
            jax.ShapeDtypeStruct((nb, 8, W_B), F32),
            jax.ShapeDtypeStruct((nb, 32, W_C), F32),
        ],
        scratch_shapes=_mixer_scratch(t),
        input_output_aliases={11: 0},
        compiler_params=pltpu.CompilerParams(dimension_semantics=("arbitrary",)),
        name="mixer_sample",
    )(zabc, st_pool, st_sconv, st_cconv, *weights, y_abc)


def _lambda(lq1, lk1, lq2, lk2, lam_init):
    s1 = jnp.sum(lq1[...] * lk1[...], axis=-1, keepdims=True)
    s2 = jnp.sum(lq2[...] * lk2[...], axis=-1, keepdims=True)
    return jnp.exp(s1) - jnp.exp(s2) + lam_init


def _diff_out(acc, l, lam, sg, lam_init, t):
    o = acc[0:t] / l[0:t] - lam * (acc[t:2 * t] / l[t:2 * t])
    return _rms(o, sg) * (1.0 - lam_init)


def _attn_prompt_kernel(qt_ref, qtn_ref, k_ref, vt_ref, lq1, lk1, lq2, lk2, sgc_ref, o_ref,
                        s0_ref, s1_ref, mx0_ref, mx1_ref, m_ref, acc_ref, *, tk, lam_init):
    qi = pl.program_id(1)
    nq = 2 * tk
    m_ref[...] = jnp.full(m_ref.shape, -jnp.inf, F32)
    acc_ref[...] = jnp.zeros(acc_ref.shape, F32)

    ncols = 2 * nq // COL_GROUPS

    assert ncols == tk

    def scores(j, s_ref, mx_ref, diagonal, group, queries=qt_ref):
        cols = slice(group * ncols, (group + 1) * ncols)
        kb = k_ref[pl.ds(pl.multiple_of(j * tk, tk), tk), :]
        s = _dot(kb, queries[group // 2, group % 2])
        if diagonal is not None:
            key = lax.broadcasted_iota(jnp.int32, s.shape, 0)
            col = lax.broadcasted_iota(jnp.int32, s.shape, 1) + group * ncols
            key_chunk = diagonal * (tk // CHUNK) + key // CHUNK
            s = jnp.where(key_chunk <= (col & (nq - 1)) // CHUNK, s, -jnp.inf)
        s_ref[:, cols] = s
        mx_ref[:, cols] = jnp.max(s, axis=0, keepdims=True)

    def consume(j, s_ref, mx_ref, group=None):
        cols = slice(None) if group is None else slice(group * ncols, (group + 1) * ncols)
        m_prev = m_ref[:, cols]
        m_new = jnp.maximum(m_prev, mx_ref[:, cols])
        alpha = jnp.exp2(m_prev - m_new)
        p = jnp.exp2(s_ref[:, cols] - m_new).astype(BF16)
        acc_ref[:, cols] = alpha * acc_ref[:, cols] + _dot(vt_ref[j], p)
        m_ref[:, cols] = m_new

    def pair(i, first_diagonal):
        j = 2 * i
        for g in range(COL_GROUPS):
            scores(j + 1, s1_ref, mx1_ref, None, g)
            consume(j, s0_ref, mx0_ref, g)
        for g in range(COL_GROUPS):
            scores(j + 2, s0_ref, mx0_ref, first_diagonal, g)
            consume(j + 1, s1_ref, mx1_ref, g)

    n_plain = jnp.maximum(qi - 1, 0)

    def run_pairs(first, count):
        for n in range(count):
            pair(first + n, None)

    def trip(i, carry):
        run_pairs(PAIRS_PER_TRIP * i, PAIRS_PER_TRIP)
        return carry

    lax.fori_loop(0, n_plain // PAIRS_PER_TRIP, trip, 0)
    done = (n_plain // PAIRS_PER_TRIP) * PAIRS_PER_TRIP
    size = PAIRS_PER_TRIP // 2
    while size >= 1:
        take = ((n_plain - done) // size) % 2 == 1
        pl.when(take)(functools.partial(run_pairs, done, size))
        done = done + jnp.where(take, size, 0)
        size //= 2

    def own_blocks():
        assert tk % ncols == 0
        late = [g for g in range(COL_GROUPS) if (g * ncols) % nq >= tk]
        for g in range(COL_GROUPS):
            if g in late:
                scores(2 * qi + 1, s1_ref, mx1_ref, 1, g)
            consume(2 * qi, s0_ref, mx0_ref, g)
            scores(0, s0_ref, mx0_ref, None, g, qtn_ref)
        for g in late:
            consume(2 * qi + 1, s1_ref, mx1_ref, g)

    @pl.when(qi > 0)
    def _():
        pair(qi - 1, 0)
        own_blocks()

    @pl.when(qi == 0)
    def _():
        for g in range(COL_GROUPS):
            scores(0, s0_ref, mx0_ref, 0, g)
        own_blocks()

    lam = _lambda(lq1, lk1, lq2, lk2, lam_init)
    acc = acc_ref[0:LANES, :]
    l = acc_ref[LANES:LANES + 1, :]
    ot = acc[:, 0:nq] * (1.0 / l[:, 0:nq]) - acc[:, nq:2 * nq] * (lam / l[:, nq:2 * nq])
    ms = jnp.mean(ot * ot, axis=0, keepdims=True)
    o_ref[...] = ot * lax.rsqrt(ms + EPS) * sgc_ref[...] * (1.0 - lam_init)


def _lam_specs():
    return [_const_spec((1, HEAD_DIM), (0, 0)) for _ in range(4)] + [_const_spec((1, LANES), (0, 0))]


def _attn_prompt(cfg, qt, kb, vt, lam_w, lam_init):
    s, tk = cfg.s_prompt, cfg.tm
    tq = 2 * tk
    assert tq & (tq - 1) == 0 and tk % CHUNK == 0 and s % tq == 0
    return pl.pallas_call(
        functools.partial(_attn_prompt_kernel, tk=tk, lam_init=lam_init),
        grid=(N_HEADS, s // tq),
        in_specs=[
            pl.BlockSpec((2, None, 2, LANES, tk), lambda h, i: (0, h, i, 0, 0)),
            pl.BlockSpec((2, None, 2, LANES, tk), lambda h, i: (0, h, jnp.minimum(i + 1, s // tq - 1), 0, 0)),
            pl.BlockSpec((s, LANES), lambda h, i: (0, h)),
            pl.BlockSpec((None, s // tk, VT_ROWS, tk), lambda h, i: (h, 0, 0, 0)),
        ] + _lam_specs()[:4] + [_const_spec((LANES, 1), (0, 0))],
        out_specs=pl.BlockSpec((LANES, tq), lambda h, i: (h, i)),
        out_shape=jax.ShapeDtypeStruct((W_D, s), F32),
        scratch_shapes=[
            pltpu.VMEM((tk, 2 * tq), F32),
            pltpu.VMEM((tk, 2 * tq), F32),
            pltpu.VMEM((1, 2 * tq), F32),
            pltpu.VMEM((1, 2 * tq), F32),
            pltpu.VMEM((1, 2 * tq), F32),
            pltpu.VMEM((VT_ROWS, 2 * tq), F32),
        ],
        compiler_params=pltpu.CompilerParams(
            dimension_semantics=("arbitrary", "arbitrary"), vmem_limit_bytes=52 * MIB),
        name="attn_prompt",
    )(qt, qt, kb, vt, *lam_w[:4], lam_w[4].reshape(LANES, 1))


def _attn_sample_kernel(q_ref, kn_ref, vn_ref, kc_ref, vc_ref, lq1, lk1, lq2, lk2, sg_ref,
                        o_ref, *, t, lam_init, group):
    lam = _lambda(lq1, lk1, lq2, lk2, lam_init)
    past = vc_ref.shape[1] // N_HEADS
    for s in range(group):
        rows = slice(s * t, (s + 1) * t)
        for h in range(N_HEADS):
            sl = slice(h * LANES, (h + 1) * LANES)
            s_past, s_new = [], []
            for c in range(2):
                ch = slice(h * LANES + c * HEAD_DIM, h * LANES + (c + 1) * HEAD_DIM)
                q = q_ref[rows, ch]
                s_past.append(_dot(q, kc_ref[s, 2 * h + c].astype(BF16)))
                s_new.append(_dot_t(q, kn_ref[rows, ch]))
            s_past = jnp.concatenate(s_past, axis=0)
            s_new = jnp.concatenate(s_new, axis=0)
            v_past = vc_ref[s, pl.ds(h, past, stride=N_HEADS), :]
            m = jnp.maximum(jnp.max(s_past, axis=-1, keepdims=True), jnp.max(s_new, axis=-1, keepdims=True))
            p_past = jnp.exp(s_past - m)
            p_new = jnp.exp(s_new - m)
            l = jnp.sum(p_past, axis=-1, keepdims=True) + jnp.sum(p_new, axis=-1, keepdims=True)
            acc = (_dot(p_past.astype(BF16), v_past.astype(BF16))
                   + _dot(p_new.astype(BF16), vn_ref[rows, sl]))
            o_ref[rows, sl] = _diff_out(acc, l, lam, sg_ref[...], lam_init, t)


def _attn_sample(cfg, l, qb, kb, vb, cache_k, cache_v, lam_w, lam_init):
    t, nb = cfg.t_dec, cfg.n_dec
    group = math.gcd(nb, SAMPLE_ATTN_GROUP)
    assert cfg.s_prompt % (group * t) == 0
    row0 = cfg.s_prompt // (group * t)
    new_rows = lambda: pl.BlockSpec((group * t, W_D), lambda b: (row0 + b, 0))
    cache = lambda a: pl.BlockSpec((None, group) + a.shape[2:], lambda b: (l, b) + (0,) * (a.ndim - 2))
    return pl.pallas_call(
        functools.partial(_attn_sample_kernel, t=t, lam_init=lam_init, group=group),
        grid=(nb // group,),
        in_specs=[new_rows(), new_rows(), new_rows(), cache(cache_k), cache(cache_v)] + _lam_specs(),
        out_specs=pl.BlockSpec((group * t, W_D), lambda b: (b, 0)),
        out_shape=jax.ShapeDtypeStruct((nb * t, W_D), F32),
        compiler_params=pltpu.CompilerParams(
            dimension_semantics=("arbitrary",), vmem_limit_bytes=48 * MIB),
        name="attn_sample",
    )(qb, kb, vb, cache_k, cache_v, *lam_w)


def _merge_kernel(x_ref, yabc_ref, ydt_ref, yds_ref, g_ref, wg_ref, wpa_ref, wpb_ref, wpc_ref, wpd_ref, wo_ref,
                  o_ref, *, n_prompt_tiles):
    is_prompt = pl.program_id(0) < n_prompt_tiles
    half = x_ref.shape[0] // 2
    for rows in (slice(0, half), slice(half, 2 * half)):
        yd = jnp.where(is_prompt, ydt_ref[:, rows].T, yds_ref[rows, :])
        x = x_ref[rows, :]
        h = _rms(x, g_ref[...]).astype(BF16)
        branches = (
            (yabc_ref[rows, 0:W_A], wpa_ref),
            (yabc_ref[rows, W_A:W_A + W_B], wpb_ref),
            (yabc_ref[rows, W_A + W_B:W_Y], wpc_ref),
            (yd, wpd_ref),
        )
        merged = None
        for i, (y, wp_ref) in enumerate(branches):
            gate = jax.nn.sigmoid(_dot(h, wg_ref[:, i * D_MODEL:(i + 1) * D_MODEL]))
            term = gate * _dot(y.astype(BF16), wp_ref[...])
            merged = term if merged is None else merged + term
        o_ref[rows, :] = x + _dot(merged.astype(BF16), wo_ref[...])


def _merge(cfg, l, x, y_abc, yd_t, yd_s, g, w_gate, wpa, wpb, wpc, wpd, wo):
    n, tm, sp = cfg.n_tok, cfg.tm, cfg.s_prompt
    assert sp % tm == 0 and (n - sp) % tm == 0
    npt = sp // tm
    row = lambda w: pl.BlockSpec((tm, w), lambda i: (i, 0))
    return pl.pallas_call(
        functools.partial(_merge_kernel, n_prompt_tiles=npt),
        grid=(n // tm,),
        in_specs=[
            row(D_MODEL), row(W_Y),
            pl.BlockSpec((W_D, tm), lambda i: (0, jnp.minimum(i, npt - 1))),
            pl.BlockSpec((tm, W_D), lambda i: (jnp.maximum(i - npt, 0), 0)),
            _const_spec((1, D_MODEL), (0, 0)),
            _const_spec((None, D_MODEL, 4 * D_MODEL), (l, 0, 0)),
            _const_spec((None, W_A, D_MODEL), (l, 0, 0)),
            _const_spec((None, W_B, D_MODEL), (l, 0, 0)),
            _const_spec((None, W_C, D_MODEL), (l, 0, 0)),
            _const_spec((None, W_D, D_MODEL), (l, 0, 0)),
            _const_spec((None, D_MODEL, D_MODEL), (l, 0, 0)),
        ],
        out_specs=row(D_MODEL),
        out_shape=jax.ShapeDtypeStruct((n, D_MODEL), F32),
        compiler_params=pltpu.CompilerParams(
            dimension_semantics=("arbitrary",), vmem_limit_bytes=48 * MIB),
        name="merge",
    )(x, y_abc, yd_t, yd_s, g, w_gate, wpa, wpb, wpc, wpd, wo)


def _rope_tables(cfg):
    half = HEAD_DIM // 2
    inv_freq = ROPE_THETA ** (-jnp.arange(half, dtype=F32) / half)
    pos = jnp.concatenate([jnp.arange(cfg.s_prompt), jnp.tile(cfg.past + jnp.arange(cfg.t_dec), cfg.n_dec)])
    ang = pos.astype(F32)[:, None] * inv_freq[None, :]
    cos, sin = jnp.cos(ang), jnp.sin(ang)
    reps = LANES // HEAD_DIM
    cos_t = jnp.tile(jnp.concatenate([cos, cos], axis=1), (1, reps))
    sin_t = jnp.tile(jnp.concatenate([-sin, sin], axis=1), (1, reps))
    return cos_t, sin_t


def _pad_rows_top(a, rows):
    return jnp.pad(a, ((0, 0), (0, 0), (rows - a.shape[2], 0), (0, 0)))


def _forward(cfg, x_prompt, x_sample, cache_k, cache_v, state_pool, state_sconv, state_cconv,
             g_ffn1, w1_gate, w1_up, w1_down, g_mix, w_in, pool_w, pool_scale, sconv_w,
             cconv_w, cconv_b, ln_g, ln_b, q_norm_g, k_norm_g, lam_q1, lam_k1, lam_q2, lam_k2,
             subln_g, wp_a, wp_b, wp_c, wp_d, w_out, g_ffn2, w2_gate, w2_up, w2_down):
    depth = w_in.shape[0]
    sp, nb, td = cfg.s_prompt, cfg.n_dec, cfg.t_dec
    x = (x_prompt.reshape(sp, D_MODEL), x_sample.reshape(nb * td, D_MODEL))

    bf = lambda w: w.astype(BF16)
    w1g, w1u, w1d = bf(w1_gate), bf(w1_up), bf(w1_down)
    w2g, w2u, w2d = bf(w2_gate), bf(w2_up), bf(w2_down)
    w_proj, w_gate = bf(w_in[:, :, :W_PROJ]), bf(w_in[:, :, W_PROJ:])
    wpa, wpb, wpc, wpd, wo = bf(wp_a), bf(wp_b), bf(wp_c), bf(wp_d), bf(w_out)
    cos_t, sin_t = _rope_tables(cfg)
    ones_bd = jnp.kron(jnp.eye(NORM_W // HEAD_DIM, dtype=F32), jnp.ones((HEAD_DIM, HEAD_DIM), F32)).astype(BF16)
    eye_g = jnp.eye(len(POOL_WINDOWS), dtype=F32)
    ck = jnp.transpose(cache_k, (0, 1, 3, 4, 2))
    cv = cache_v.reshape(depth, nb, cfg.past * N_HEADS, 2 * HEAD_DIM)
    st_pool = _pad_rows_top(state_pool, HALO)
    st_sconv = _pad_rows_top(state_sconv, HALO)
    st_cconv = _pad_rows_top(state_cconv, HALO)
    row = lambda a: a.reshape(1, -1)

    outs = [[] for _ in range(6)]
    caches = None
    for l in range(depth):
        lam_init = 0.8 - 0.6 * math.exp(-0.3 * l)
        x = _ffn(cfg, l, x, row(g_ffn1[l]), w1g, w1u, w1d)
        qg = row(jnp.tile(q_norm_g[l], NORM_W // HEAD_DIM))
        kg = row(jnp.tile(k_norm_g[l], NORM_W // HEAD_DIM))
        zabc, qb, kb, vb, qt, vt, *caches = _inproj(cfg, l, depth, x, row(g_mix[l]), w_proj, qg, kg, cos_t, sin_t,
                                                    ones_bd, caches)

        pw_bd = (eye_g[:, None, :, None] * pool_w[l][:, :, None, :]).reshape(W_A, W_A).astype(BF16)
        mix_w = (pw_bd, row(pool_scale[l]), sconv_w[l], cconv_w[l], row(cconv_b[l]), row(ln_g[l]), row(ln_b[l]))
        y_abc, pool_p, sconv_p, cconv_p = _mixer_prompt(cfg, zabc, mix_w)
        y_abc, pool_s, sconv_s, cconv_s = _mixer_sample(cfg, zabc, st_pool[l], st_sconv[l], st_cconv[l],
                                                        mix_w, y_abc)

        lam_w = (row(lam_q1[l]), row(lam_k1[l]), row(lam_q2[l]), row(lam_k2[l]), row(subln_g[l]))
        yd_t = _attn_prompt(cfg, qt, kb, vt, lam_w, lam_init)
        yd_s = _attn_sample(cfg, l, qb, kb, vb, ck, cv, lam_w, lam_init)

        x = _merge(cfg, l, x, y_abc, yd_t, yd_s, row(g_mix[l]), w_gate, wpa, wpb, wpc, wpd, wo)
        x = _ffn(cfg, l, x, row(g_ffn2[l]), w2g, w2u, w2d, split_out=(l == depth - 1))

        outs[0].append(pool_p[None, 16 - POOL_STATE:])
        outs[1].append(sconv_p[None, 8 - (SCONV_K - 1):])
        outs[2].append(cconv_p[None, 32 - (CCONV_K - 1):])
        outs[3].append(pool_s[:, 16 - POOL_STATE:])
        outs[4].append(sconv_s[:, 8 - (SCONV_K - 1):])
        outs[5].append(cconv_s[:, 32 - (CCONV_K - 1):])

    kt_all, ks_all, vp_all, vs_all = caches
    y_prompt = x[0].reshape(1, sp, D_MODEL)
    y_sample = x[1].reshape(nb, td, D_MODEL)
    k_prompt = jnp.transpose(kt_all.reshape(depth, 1, 2 * N_HEADS, HEAD_DIM, sp), (0, 1, 4, 2, 3))
    v_prompt = vp_all.reshape(depth, 1, sp, N_HEADS, 2 * HEAD_DIM)
    k_sample = ks_all.reshape(depth, nb, td, 2 * N_HEADS, HEAD_DIM)
    v_sample = vs_all.reshape(depth, nb, td, N_HEADS, 2 * HEAD_DIM)
    st = [jnp.stack(o) for o in outs]
    return (y_prompt, y_sample, k_prompt, v_prompt, st[0], st[1], st[2], k_sample, v_sample, st[3], st[4], st[5])


def kernel(x_prompt, x_sample, cache_k, cache_v, state_pool, state_sconv, state_cconv, g_ffn1, w1_gate, w1_up, w1_down, g_mix, w_in, pool_w, pool_scale, sconv_w, cconv_w, cconv_b, ln_g, ln_b, q_norm_g, k_norm_g, lam_q1, lam_k1, lam_q2, lam_k2, subln_g, wp_a, wp_b, wp_c, wp_d, w_out, g_ffn2, w2_gate, w2_up, w2_down):
    assert x_prompt.shape[0] == 1
    cfg = Cfg(s_prompt=x_prompt.shape[1], n_dec=x_sample.shape[0], t_dec=x_sample.shape[1],
              past=cache_k.shape[2], tm=512, tmix=512)
    return _forward(cfg, x_prompt, x_sample, cache_k, cache_v, state_pool, state_sconv, state_cconv,
                    g_ffn1, w1_gate, w1_up, w1_down, g_mix, w_in, pool_w, pool_scale, sconv_w,
                    cconv_w, cconv_b, ln_g, ln_b, q_norm_g, k_norm_g, lam_q1, lam_k1, lam_q2, lam_k2,
                    subln_g, wp_a, wp_b, wp_c, wp_d, w_out, g_ffn2, w2_gate, w2_up, w2_down)
```
